```python
import math
import jax, jax.numpy as jnp
from jax import lax
import numpy as np

D_MODEL = 1024
BATCH = 2
SEQ = 8192
DEPTH = 4
DEC_BATCH = 128
DEC_SEQ = 1
PAST_LEN = 8192
PAGE_SIZE = 128

N_MIXERS = 3
POOL_WINDOWS = (2, 4, 8, 16)
N_POOL_GROUPS = len(POOL_WINDOWS)
POOL_WIDTH = D_MODEL
POOL_GROUP = POOL_WIDTH // N_POOL_GROUPS
POOL_STATE = max(POOL_WINDOWS) - 1
HEAD_DIM = 64
N_HEADS = D_MODEL // HEAD_DIM
N_KV_HEADS = 4
GROUP = N_HEADS // N_KV_HEADS
ATT_WIDTH = N_HEADS * HEAD_DIM
KV_WIDTH = N_KV_HEADS * HEAD_DIM
SCALE = HEAD_DIM ** -0.5
WINDOW = 128
BLOCK = 128
N_BUCKETS = 32
MAX_DISTANCE = 128
EPS = 1e-6
NEG = -1e30
N_A = (DEPTH + 2) // N_MIXERS
N_B = (DEPTH + 1) // N_MIXERS
N_C = DEPTH // N_MIXERS
IN_A = 2 * POOL_WIDTH
IN_B = 2 * ATT_WIDTH + 2 * KV_WIDTH
IN_C = 2 * ATT_WIDTH + 2 * KV_WIDTH + N_HEADS

kernel_name = 'pool_swa_fox_hybrid_step'


def rmsnorm(x, g):
    xf = x.astype(jnp.float32)
    y = xf * lax.rsqrt(jnp.mean(xf * xf, axis=-1, keepdims=True) + EPS) * g.astype(jnp.float32)
    return y.astype(x.dtype)


def t5_bucket(dist):
    n = jnp.maximum(dist, 0)
    max_exact = N_BUCKETS // 2
    nf = jnp.maximum(n, 1).astype(jnp.float32)
    large = max_exact + (jnp.log(nf / max_exact) / math.log(MAX_DISTANCE / max_exact) * (N_BUCKETS - max_exact)).astype(jnp.int32)
    large = jnp.minimum(large, N_BUCKETS - 1)
    return jnp.where(n < max_exact, n, large)


def heads_first(c):
    b, l, _ = c.shape
    return c.reshape(b, l, N_KV_HEADS, GROUP).transpose(0, 2, 3, 1)


def multiscale_pool(u_ext, n_prefix):
    b, l, _ = u_ext.shape
    uf = u_ext.astype(jnp.float32)
    cs = jnp.concatenate([jnp.zeros((b, 1, POOL_WIDTH), jnp.float32), jnp.cumsum(uf, axis=1)], axis=1)
    rows = jnp.arange(n_prefix, l)
    hi = cs[:, rows + 1]
    outs = []
    for g, w in enumerate(POOL_WINDOWS):
        lo = jnp.maximum(rows + 1 - w, 0)
        cnt = (rows + 1 - lo).astype(jnp.float32)[None, :, None]
        c0, c1 = g * POOL_GROUP, (g + 1) * POOL_GROUP
        outs.append((hi[:, :, c0:c1] - cs[:, lo, c0:c1]) / cnt)
    return jnp.concatenate(outs, axis=-1) - uf[:, n_prefix:]


def pool_branch(xn, u_prefix, w_in, pool_mix, pool_scale, w_out):
    b, t, _ = xn.shape
    u, gate = jnp.split(xn @ w_in, 2, axis=-1)
    u_ext = jnp.concatenate([u_prefix.astype(u.dtype), u], axis=1)
    p = multiscale_pool(u_ext, u_prefix.shape[1]).astype(xn.dtype)
    p = jnp.einsum('btgc,gcd->btgd', p.reshape(b, t, N_POOL_GROUPS, POOL_GROUP), pool_mix).reshape(b, t, POOL_WIDTH)
    o = p * pool_scale * jax.nn.silu(gate)
    return o @ w_out, u_ext[:, -POOL_STATE:]


def swa_attend(q, k, v, dist, valid, sinks, rel_bias):
    nq, nk = dist.shape
    bias = rel_bias.astype(jnp.float32)[t5_bucket(dist)]
    bias = jnp.transpose(bias, (2, 0, 1)).reshape(N_KV_HEADS, GROUP, nq, nk)
    logits = jnp.einsum('bnqkgd,bnskd->bnkgqs', q, k, preferred_element_type=jnp.float32) * SCALE + bias
    logits = jnp.where(valid[None, :, None, None], logits, NEG)
    sink = sinks.astype(jnp.float32).reshape(N_KV_HEADS, GROUP, 1, 1)
    m = jnp.maximum(jnp.max(logits, axis=-1, keepdims=True), sink)
    p = jnp.exp(logits - m)
    denom = jnp.sum(p, axis=-1, keepdims=True) + jnp.exp(sink - m)
    return jnp.einsum('bnkgqs,bnskd->bnqkgd', (p / denom).astype(v.dtype), v)


def swa_branch(xn, k_buf, v_buf, w_in, sinks, rel_bias, w_out):
    b, t, _ = xn.shape
    q, k, v, gate = jnp.split(xn @ w_in, [ATT_WIDTH, ATT_WIDTH + KV_WIDTH, ATT_WIDTH + 2 * KV_WIDTH], axis=-1)
    q = q.reshape(b, t, N_KV_HEADS, GROUP, HEAD_DIM)
    k = k.reshape(b, t, N_KV_HEADS, HEAD_DIM)
    v = v.reshape(b, t, N_KV_HEADS, HEAD_DIM)
    if k_buf is None:
        nb = t // BLOCK
        qb = q.reshape(b, nb, BLOCK, N_KV_HEADS, GROUP, HEAD_DIM)

        def band(a):
            ap = jnp.concatenate([jnp.zeros_like(a[:, :BLOCK]), a], axis=1).reshape(b, nb + 1, BLOCK, N_KV_HEADS, HEAD_DIM)
            return jnp.concatenate([ap[:, :-1], ap[:, 1:]], axis=2)

        qi = jnp.arange(BLOCK)[:, None]
        ki = jnp.arange(2 * BLOCK)[None, :]
        dist = qi + BLOCK - ki
        key_pos = jnp.arange(nb)[:, None, None] * BLOCK - BLOCK + ki[None]
        valid = (dist >= 0) & (dist < WINDOW) & (key_pos >= 0)
        o = swa_attend(qb, band(k), band(v), dist, valid, sinks, rel_bias)
        k_all, v_all = k, v
    else:
        k_all = jnp.concatenate([k_buf.astype(k.dtype), k], axis=1)
        v_all = jnp.concatenate([v_buf.astype(v.dtype), v], axis=1)
        dist = jnp.arange(t)[:, None] + WINDOW - jnp.arange(WINDOW + t)[None, :]
        valid = ((dist >= 0) & (dist < WINDOW))[None]
        o = swa_attend(q[:, None], k_all[:, None], v_all[:, None], dist, valid, sinks, rel_bias)
    o = o.reshape(b, t, ATT_WIDTH) * jax.nn.silu(gate)
    return o @ w_out, k_all[:, -WINDOW:], v_all[:, -WINDOW:]


def fox_prompt(q, k, v, logf):
    b, s = q.shape[:2]
    nb = s // BLOCK
    c = heads_first(jnp.cumsum(logf, axis=1))
    key_pos = jnp.arange(s)

    def one_block(i):
        q_i = lax.dynamic_slice_in_dim(q, i * BLOCK, BLOCK, axis=1)
        c_i = lax.dynamic_slice_in_dim(c, i * BLOCK, BLOCK, axis=3)
        logits = jnp.einsum('bqkgd,bskd->bkgqs', q_i, k, preferred_element_type=jnp.float32) * SCALE
        logits = logits + c_i[..., :, None] - c[..., None, :]
        qpos = i * BLOCK + jnp.arange(BLOCK)
        logits = jnp.where(key_pos[None, :] <= qpos[:, None], logits, NEG)
        p = jax.nn.softmax(logits, axis=-1)
        return jnp.einsum('bkgqs,bskd->bqkgd', p.astype(v.dtype), v)

    out = lax.map(one_block, jnp.arange(nb))
    return out.transpose(1, 0, 2, 3, 4, 5).reshape(b, s, ATT_WIDTH)


def fox_sample(q, k, v, logf, k_past, v_past, logf_past):
    b, t = q.shape[:2]
    n_past = k_past.shape[1]
    c_past = jnp.cumsum(logf_past.astype(jnp.float32), axis=1)
    c_new = c_past[:, -1:] + jnp.cumsum(logf, axis=1)
    cp, cn = heads_first(c_past), heads_first(c_new)
    lp = jnp.einsum('bqkgd,bskd->bkgqs', q, k_past.astype(q.dtype), preferred_element_type=jnp.float32) * SCALE
    lp = lp + cn[..., :, None] - cp[..., None, :]
    ln = jnp.einsum('bqkgd,bskd->bkgqs', q, k, preferred_element_type=jnp.float32) * SCALE
    ln = ln + cn[..., :, None] - cn[..., None, :]
    ln = jnp.where(jnp.tril(jnp.ones((t, t), bool)), ln, NEG)
    p = jax.nn.softmax(jnp.concatenate([lp, ln], axis=-1), axis=-1).astype(v.dtype)
    o = jnp.einsum('bkgqs,bskd->bqkgd', p[..., :n_past], v_past.astype(v.dtype)) + jnp.einsum('bkgqs,bskd->bqkgd', p[..., n_past:], v)
    return o.reshape(b, t, ATT_WIDTH)


def fox_branch(xn, past, w_in, f_bias, w_out):
    b, t, _ = xn.shape
    q, k, v, f, gate = jnp.split(xn @ w_in, [ATT_WIDTH, ATT_WIDTH + KV_WIDTH, ATT_WIDTH + 2 * KV_WIDTH, ATT_WIDTH + 2 * KV_WIDTH + N_HEADS], axis=-1)
    q = q.reshape(b, t, N_KV_HEADS, GROUP, HEAD_DIM)
    k = k.reshape(b, t, N_KV_HEADS, HEAD_DIM)
    v = v.reshape(b, t, N_KV_HEADS, HEAD_DIM)
    logf = jax.nn.log_sigmoid(f.astype(jnp.float32) + f_bias.astype(jnp.float32))
    if past is None:
        o = fox_prompt(q, k, v, logf)
    else:
        o = fox_sample(q, k, v, logf, *past)
    o = o * jax.nn.silu(gate)
    return o @ w_out, k, v, logf.astype(xn.dtype)


def setup_inputs(seed: int = 0) -> dict:
    key = jax.random.key(seed)
    ks = jax.random.split(key, 24)
    f32 = jnp.float32
    nrm = jax.random.normal
    n_pages = PAST_LEN // PAGE_SIZE
    n_used = DEC_BATCH * n_pages
    n_pool = (5 * n_used) // 4
    page_table = jax.random.permutation(ks[0], n_pool)[:n_used].reshape(DEC_BATCH, n_pages).astype(jnp.int32)
    return dict(
        x_prompt=nrm(ks[1], (BATCH, SEQ, D_MODEL), f32),
        x_sample=nrm(ks[2], (DEC_BATCH, DEC_SEQ, D_MODEL), f32),
        state_pool=nrm(ks[3], (N_A, DEC_BATCH, POOL_STATE, POOL_WIDTH), f32),
        cache_win_k=nrm(ks[4], (N_B, DEC_BATCH, WINDOW, N_KV_HEADS, HEAD_DIM), f32),
        cache_win_v=nrm(ks[5], (N_B, DEC_BATCH, WINDOW, N_KV_HEADS, HEAD_DIM), f32),
        cache_fox_k=nrm(ks[6], (N_C, n_pool, PAGE_SIZE, N_KV_HEADS, HEAD_DIM), f32),
        cache_fox_v=nrm(ks[7], (N_C, n_pool, PAGE_SIZE, N_KV_HEADS, HEAD_DIM), f32),
        cache_fox_logf=jax.nn.log_sigmoid(3.0 + nrm(ks[8], (N_C, n_pool, PAGE_SIZE, N_HEADS), f32)),
        page_table=page_table,
        norm_g=1.0 + 0.02 * nrm(ks[9], (DEPTH, D_MODEL), f32),
        final_norm_g=1.0 + 0.02 * nrm(ks[10], (D_MODEL,), f32),
        rel_bias=0.1 * nrm(ks[11], (N_BUCKETS, N_HEADS), f32),
        pool_w_in=nrm(ks[12], (N_A, D_MODEL, IN_A), f32) * D_MODEL ** -0.5,
        pool_mix=nrm(ks[13], (N_A, N_POOL_GROUPS, POOL_GROUP, POOL_GROUP), f32) * POOL_GROUP ** -0.5,
        pool_scale=1.0 + 0.1 * nrm(ks[14], (N_A, POOL_WIDTH), f32),
        pool_w_out=nrm(ks[15], (N_A, POOL_WIDTH, D_MODEL), f32) * POOL_WIDTH ** -0.5,
        swa_w_in=nrm(ks[16], (N_B, D_MODEL, IN_B), f32) * D_MODEL ** -0.5,
        swa_sinks=0.5 * nrm(ks[17], (N_B, N_HEADS), f32),
        swa_w_out=nrm(ks[18], (N_B, ATT_WIDTH, D_MODEL), f32) * ATT_WIDTH ** -0.5,
        fox_w_in=nrm(ks[19], (N_C, D_MODEL, IN_C), f32) * D_MODEL ** -0.5,
        fox_f_bias=jax.random.uniform(ks[20], (N_C, N_HEADS), f32, 1.0, 6.0),
        fox_w_out=nrm(ks[21], (N_C, ATT_WIDTH, D_MODEL), f32) * ATT_WIDTH ** -0.5,
    )


def reference(x_prompt, x_sample, state_pool, cache_win_k, cache_win_v, cache_fox_k, cache_fox_v, cache_fox_logf, page_table,
              norm_g, final_norm_g, rel_bias, pool_w_in, pool_mix, pool_scale, pool_w_out,
              swa_w_in, swa_sinks, swa_w_out, fox_w_in, fox_f_bias, fox_w_out):
    xp, xs = x_prompt, x_sample
    db = x_sample.shape[0]
    pool_p, pool_s = [], []
    wk_p, wv_p, wk_s, wv_s = [], [], [], []
    fk_p, fv_p, fl_p, fk_s, fv_s, fl_s = [], [], [], [], [], []
    for i in range(DEPTH):
        kind, j = i % N_MIXERS, i // N_MIXERS
        hp = rmsnorm(xp, norm_g[i])
        hs = rmsnorm(xs, norm_g[i])
        if kind == 0:
            w = (pool_w_in[j], pool_mix[j], pool_scale[j], pool_w_out[j])
            yp, st_p = pool_branch(hp, jnp.zeros((hp.shape[0], 0, POOL_WIDTH), hp.dtype), *w)
            ys, st_s = pool_branch(hs, state_pool[j], *w)
            pool_p.append(st_p)
            pool_s.append(st_s)
        elif kind == 1:
            w = (swa_w_in[j], swa_sinks[j], rel_bias, swa_w_out[j])
            yp, kp, vp = swa_branch(hp, None, None, *w)
            ys, ks_, vs_ = swa_branch(hs, cache_win_k[j], cache_win_v[j], *w)
            wk_p.append(kp)
            wv_p.append(vp)
            wk_s.append(ks_)
            wv_s.append(vs_)
        else:
            w = (fox_w_in[j], fox_f_bias[j], fox_w_out[j])
            past = (cache_fox_k[j, page_table].reshape(db, -1, N_KV_HEADS, HEAD_DIM),
                    cache_fox_v[j, page_table].reshape(db, -1, N_KV_HEADS, HEAD_DIM),
                    cache_fox_logf[j, page_table].reshape(db, -1, N_HEADS))
            yp, kp, vp, lp = fox_branch(hp, None, *w)
            ys, ks_, vs_, ls_ = fox_branch(hs, past, *w)
            fk_p.append(kp)
            fv_p.append(vp)
            fl_p.append(lp)
            fk_s.append(ks_)
            fv_s.append(vs_)
            fl_s.append(ls_)
        xp = xp + yp
        xs = xs + ys
    y_prompt = rmsnorm(xp, final_norm_g)
    y_sample = rmsnorm(xs, final_norm_g)
    return (y_prompt, y_sample, jnp.stack(pool_p), jnp.stack(pool_s), jnp.stack(wk_p), jnp.stack(wv_p), jnp.stack(wk_s), jnp.stack(wv_s), jnp.stack(fk_p), jnp.stack(fv_p), jnp.stack(fl_p), jnp.stack(fk_s), jnp.stack(fv_s), jnp.stack(fl_s))
```

```python
import functools
import math

import numpy as np
import jax
import jax.numpy as jnp
from jax import lax
from jax.experimental import pallas as pl
from jax.experimental.pallas import tpu as pltpu

D_MODEL = 1024
HEAD_DIM = 64
N_HEADS = 16
N_KV = 4
GROUP = 4
KV_WIDTH = N_KV * HEAD_DIM
POOL_WINDOWS = (2, 4, 8, 16)
POOL_GROUP = 256
POOL_STATE = 15
WINDOW = 128
N_BUCKETS = 32
MAX_DISTANCE = 128
SCALE = HEAD_DIM ** -0.5
EPS = 1e-6
NEG = -1e30
PAGE = 128
LANES = 128

BF16 = jnp.bfloat16
F32 = jnp.float32

VMEM_LIMIT = 56 * 1024 * 1024


def _cparams(sem):
    return pltpu.CompilerParams(dimension_semantics=sem, vmem_limit_bytes=VMEM_LIMIT)


def _rms_bf16(x, g):
    ms = jnp.mean(x * x, axis=-1, keepdims=True)
    return (x * lax.rsqrt(ms + EPS) * g).astype(BF16)


def _silu(x):
    return x * jax.nn.sigmoid(x)


def _dot(a, b):
    return jnp.dot(a, b, preferred_element_type=F32)


def _dot_nt(a, b):
    return lax.dot_general(a, b, (((1,), (1,)), ((), ())), preferred_element_type=F32)


def _bf16_round(x):
    return x.astype(BF16).astype(F32)


def _full(shape):
    n = len(shape)
    return pl.BlockSpec(shape, lambda *_: (0,) * n)


def _pool_proj_kernel(x_ref, g_ref, w_ref, u_ref, gate_ref):
    hb = _rms_bf16(x_ref[...], g_ref[...])
    u_ref[...] = _dot(hb, w_ref[:, :D_MODEL])
    gate_ref[...] = _dot(hb, w_ref[:, D_MODEL:])


def _pool_proj(x, g, w, tm):
    m = x.shape[0]
    row = lambda i: (i, 0)
    return pl.pallas_call(
        _pool_proj_kernel,
        grid=(m // tm,),
        in_specs=[pl.BlockSpec((tm, D_MODEL), row), _full((1, D_MODEL)), _full(w.shape)],
        out_specs=[pl.BlockSpec((tm, D_MODEL), row), pl.BlockSpec((tm, D_MODEL), row)],
        out_shape=[jax.ShapeDtypeStruct((m, D_MODEL), F32)] * 2,
        compiler_params=_cparams(("parallel",)),
        name="pool_proj",
    )(x, g, w)


def _swa_proj_kernel(x_ref, g_ref, w_ref, q_ref, k_ref, v_ref, kt_ref, vt_ref, gate_ref):
    hb = _rms_bf16(x_ref[...], g_ref[...])
    q_ref[...] = (_dot(hb, w_ref[:, :D_MODEL]) * SCALE).astype(BF16)
    k = _dot(hb, w_ref[:, D_MODEL:D_MODEL + KV_WIDTH])
    v = _dot(hb, w_ref[:, D_MODEL + KV_WIDTH:D_MODEL + 2 * KV_WIDTH])
    k_ref[...] = k
    v_ref[...] = v
    kt_ref[...] = k.T
    vt_ref[...] = v.T
    gate_ref[...] = _dot(hb, w_ref[:, D_MODEL + 2 * KV_WIDTH:])


def _swa_proj(x, g, w, tm, batch, seq):
    m = x.shape[0]
    ns = seq // tm
    row = lambda i: (i, 0)
    tcol = pl.BlockSpec((None, KV_WIDTH, tm), lambda i: (i // ns, 0, i % ns))
    return pl.pallas_call(
        _swa_proj_kernel,
        grid=(m // tm,),
        in_specs=[pl.BlockSpec((tm, D_MODEL), row), _full((1, D_MODEL)), _full(w.shape)],
        out_specs=[pl.BlockSpec((tm, D_MODEL), row), pl.BlockSpec((tm, KV_WIDTH), row),
                   pl.BlockSpec((tm, KV_WIDTH), row), tcol, tcol, pl.BlockSpec((tm, D_MODEL), row)],
        out_shape=[jax.ShapeDtypeStruct((m, D_MODEL), BF16),
                   jax.ShapeDtypeStruct((m, KV_WIDTH), F32),
                   jax.ShapeDtypeStruct((m, KV_WIDTH), F32),
                   jax.ShapeDtypeStruct((batch, KV_WIDTH, seq), F32),
                   jax.ShapeDtypeStruct((batch, KV_WIDTH, seq), F32),
                   jax.ShapeDtypeStruct((m, D_MODEL), F32)],
        compiler_params=_cparams(("parallel",)),
        name="swa_proj",
    )(x, g, w)


def _log_sigmoid(x):
    return -(jnp.maximum(-x, 0.0) + jnp.log1p(jnp.exp(-jnp.abs(x))))


def _fox_proj_kernel(x_ref, g_ref, w_ref, wf_ref, fb_ref, *out_refs, prompt):
    if prompt:
        qh_ref, kt_ref, vt_ref, ktb_ref, vh_ref, logft_ref, gate_ref = out_refs
    else:
        q_ref, k_ref, v_ref, kt_ref, vt_ref, logf_ref, logft_ref, gate_ref = out_refs
    hb = _rms_bf16(x_ref[...], g_ref[...])
    q = _dot(hb, w_ref[:, :D_MODEL]) * SCALE
    k = _dot(hb, w_ref[:, D_MODEL:D_MODEL + KV_WIDTH])
    v = _dot(hb, w_ref[:, D_MODEL + KV_WIDTH:D_MODEL + 2 * KV_WIDTH])
    kt = k.T
    kt_ref[...] = kt
    vt_ref[...] = v.T
    logf = _log_sigmoid(_dot(hb, wf_ref[...]) + fb_ref[...])
    logft_ref[...] = logf.T[:N_HEADS, :]
    if prompt:
        for hd in range(N_HEADS):
            qh_ref[hd] = q[:, hd * HEAD_DIM:(hd + 1) * HEAD_DIM].astype(BF16)
        ktb_ref[...] = kt.astype(BF16)
        for h in range(N_KV):
            vh_ref[h] = v[:, h * HEAD_DIM:(h + 1) * HEAD_DIM].astype(BF16)
    else:
        q_ref[...] = q.astype(BF16)
        k_ref[...] = k
        v_ref[...] = v
        logf_ref[...] = logf[:, :N_HEADS]
    gate_ref[...] = _dot(hb, w_ref[:, D_MODEL + 2 * KV_WIDTH:])


def _fox_proj(x, g, w, wf, fb, tm, batch, seq, prompt):
    m = x.shape[0]
    ns = seq // tm
    row = lambda i: (i, 0)
    hrow = lambda i: (0, i, 0)
    tcol = lambda r: pl.BlockSpec((None, r, tm), lambda i: (i // ns, 0, i % ns))
    sds = jax.ShapeDtypeStruct
    kv_t = [tcol(KV_WIDTH), tcol(KV_WIDTH)]
    kv_t_shape = [sds((batch, KV_WIDTH, seq), F32)] * 2
    if prompt:
        out_specs = ([pl.BlockSpec((N_HEADS, tm, HEAD_DIM), hrow)] + kv_t
                     + [tcol(KV_WIDTH), pl.BlockSpec((N_KV, tm, HEAD_DIM), hrow), tcol(N_HEADS)])
        out_shape = ([sds((N_HEADS, m, HEAD_DIM), BF16)] + kv_t_shape
                     + [sds((batch, KV_WIDTH, seq), BF16), sds((N_KV, m, HEAD_DIM), BF16),
                        sds((batch, N_HEADS, seq), F32)])
    else:
        out_specs = ([pl.BlockSpec((tm, D_MODEL), row), pl.BlockSpec((tm, KV_WIDTH), row),
                      pl.BlockSpec((tm, KV_WIDTH), row)] + kv_t
                     + [pl.BlockSpec((tm, N_HEADS), row), tcol(N_HEADS)])
        out_shape = ([sds((m, D_MODEL), BF16), sds((m, KV_WIDTH), F32), sds((m, KV_WIDTH), F32)]
                     + kv_t_shape + [sds((m, N_HEADS), F32), sds((batch, N_HEADS, seq), F32)])
    out_specs.append(pl.BlockSpec((tm, D_MODEL), row))
    out_shape.append(sds((m, D_MODEL), F32))
    return pl.pallas_call(
        functools.partial(_fox_proj_kernel, prompt=prompt),
        grid=(m // tm,),
        in_specs=[pl.BlockSpec((tm, D_MODEL), row), _full((1, D_MODEL)), _full(w.shape),
                  _full(wf.shape), _full(fb.shape)],
        out_specs=out_specs,
        out_shape=out_shape,
        compiler_params=_cparams(("parallel",)),
        name="fox_proj",
    )(x, g, w, wf, fb)


def _out_proj_kernel(*refs, gated):
    if gated:
        o_ref, gate_ref, w_ref, x_ref, y_ref = refs
        ob = (o_ref[...] * _silu(gate_ref[...])).astype(BF16)
    else:
        o_ref, w_ref, x_ref, y_ref = refs
        ob = o_ref[...]
    y_ref[...] = x_ref[...] + _dot(ob, w_ref[...])


def _out_proj(o, gate, w, x, tm):
    m = x.shape[0]
    row = lambda i: (i, 0)
    tile = pl.BlockSpec((tm, D_MODEL), row)
    gated = gate is not None
    ins = [o, gate, w, x] if gated else [o, w, x]
    in_specs = [tile, tile, _full(w.shape), tile] if gated else [tile, _full(w.shape), tile]
    return pl.pallas_call(
        functools.partial(_out_proj_kernel, gated=gated),
        grid=(m // tm,),
        in_specs=in_specs,
        out_specs=tile,
        out_shape=jax.ShapeDtypeStruct((m, D_MODEL), F32),
        compiler_params=_cparams(("parallel",)),
        name="out_proj",
    )(*ins)


def _final_norm_kernel(x_ref, g_ref, y_ref):
    x = x_ref[...]
    ms = jnp.mean(x * x, axis=-1, keepdims=True)
    y_ref[...] = x * lax.rsqrt(ms + EPS) * g_ref[...]


def _final_norm(x, g, tm):
    m = x.shape[0]
    row = lambda i: (i, 0)
    return pl.pallas_call(
        _final_norm_kernel,
        grid=(m // tm,),
        in_specs=[pl.BlockSpec((tm, D_MODEL), row), _full((1, D_MODEL))],
        out_specs=pl.BlockSpec((tm, D_MODEL), row),
        out_shape=jax.ShapeDtypeStruct((m, D_MODEL), F32),
        compiler_params=_cparams(("parallel",)),
        name="final_norm",
    )(x, g)


HALO = 16


def _pool_finish(pooled_groups, u, gate, mix_ref, scale_ref, o_ref):
    pieces = []
    for g in range(len(POOL_WINDOWS)):
        c0 = g * POOL_GROUP
        p = (pooled_groups[g] - u[:, c0:c0 + POOL_GROUP]).astype(BF16)
        pieces.append(_dot(p, mix_ref[g]))
    pm = jnp.concatenate(pieces, axis=1)
    o_ref[...] = (pm * scale_ref[...] * _silu(gate)).astype(BF16)


def _pool_prompt_kernel(u_ref, gate_ref, mix_ref, scale_ref, o_ref, ubuf, *, tp):
    i = pl.program_id(1)

    @pl.when(i == 0)
    def _():
        ubuf[0:HALO, :] = jnp.zeros((HALO, D_MODEL), F32)

    u = u_ref[...]
    ubuf[HALO:HALO + tp, :] = u
    pos = i * tp + lax.broadcasted_iota(jnp.int32, (tp, 1), 0)
    pooled = []
    for g, w in enumerate(POOL_WINDOWS):
        c0 = g * POOL_GROUP
        acc = u[:, c0:c0 + POOL_GROUP]
        for k in range(1, w):
            acc = acc + ubuf[HALO - k:HALO - k + tp, c0:c0 + POOL_GROUP]
        cnt = jnp.minimum(pos + 1, w).astype(F32)
        pooled.append(acc / cnt)
    _pool_finish(pooled, u, gate_ref[...], mix_ref, scale_ref, o_ref)
    ubuf[0:HALO, :] = u[tp - HALO:tp, :]


def _pool_prompt(u, gate, mix, scale, batch, seq, tp):
    ns = seq // tp
    tile = pl.BlockSpec((tp, D_MODEL), lambda b, i: (b * ns + i, 0))
    return pl.pallas_call(
        functools.partial(_pool_prompt_kernel, tp=tp),
        grid=(batch, ns),
        in_specs=[tile, tile, _full(mix.shape), _full(scale.shape)],
        out_specs=tile,
        out_shape=jax.ShapeDtypeStruct((batch * seq, D_MODEL), BF16),
        scratch_shapes=[pltpu.VMEM((HALO + tp, D_MODEL), F32)],
        compiler_params=_cparams(("parallel", "arbitrary")),
        name="pool_prompt",
    )(u, gate, mix, scale)


def _pool_sample_kernel(st_ref, u_ref, gate_ref, mix_ref, scale_ref, o_ref):
    u = u_ref[...]
    pooled = []
    for g, w in enumerate(POOL_WINDOWS):
        c0 = g * POOL_GROUP
        acc = u[:, c0:c0 + POOL_GROUP]
        for k in range(1, w):
            acc = acc + st_ref[POOL_STATE - k, :, c0:c0 + POOL_GROUP]
        pooled.append(acc / float(w))
    _pool_finish(pooled, u, gate_ref[...], mix_ref, scale_ref, o_ref)


def _pool_sample(state_t, u, gate, mix, scale):
    m = u.shape[0]
    return pl.pallas_call(
        _pool_sample_kernel,
        grid=(1,),
        in_specs=[_full(state_t.shape), _full(u.shape), _full(gate.shape), _full(mix.shape),
                  _full(scale.shape)],
        out_specs=_full((m, D_MODEL)),
        out_shape=jax.ShapeDtypeStruct((m, D_MODEL), BF16),
        compiler_params=_cparams(("arbitrary",)),
        name="pool_sample",
    )(state_t, u, gate, mix, scale)


def _t5_bucket_np(dist):
    n = np.maximum(dist, 0)
    max_exact = N_BUCKETS // 2
    nf = np.maximum(n, 1).astype(np.float32)
    large = max_exact + (np.log(nf / max_exact) / math.log(MAX_DISTANCE / max_exact)
                         * (N_BUCKETS - max_exact)).astype(np.int32)
    large = np.minimum(large, N_BUCKETS - 1)
    return np.where(n < max_exact, n, large)


def _swa_prompt_kernel(sink_ref, q_ref, kp_ref, kc_ref, vp_ref, vc_ref, bias_ref, o_ref):
    j = pl.program_id(1)
    kb = jnp.concatenate([kp_ref[...], kc_ref[...]], axis=1).astype(BF16)
    vb = jnp.concatenate([vp_ref[...], vc_ref[...]], axis=1).astype(BF16)
    col = lax.broadcasted_iota(jnp.int32, (WINDOW, 2 * WINDOW), 1)
    no_prev = jnp.logical_and(j == 0, col < WINDOW)
    q = q_ref[...]
    outs = []
    for h in range(N_KV):
        kh = kb[h * HEAD_DIM:(h + 1) * HEAD_DIM, :]
        vh = vb[h * HEAD_DIM:(h + 1) * HEAD_DIM, :]
        for g in range(GROUP):
            hd = h * GROUP + g
            s = _dot(q[:, hd * HEAD_DIM:(hd + 1) * HEAD_DIM], kh) + bias_ref[hd]
            s = jnp.where(no_prev, NEG, s)
            sink = sink_ref[hd]
            m = jnp.maximum(jnp.max(s, axis=-1, keepdims=True), sink)
            p = jnp.exp(s - m)
            denom = jnp.sum(p, axis=-1, keepdims=True) + jnp.exp(sink - m)
            outs.append(_dot_nt((p / denom).astype(BF16), vh))
    o_ref[...] = jnp.concatenate(outs, axis=1)


def _swa_prompt(q, kt, vt, bias_band, sinks, batch, seq):
    nb = seq // WINDOW
    rows = lambda b, j: (b * nb + j, 0)
    cur = lambda b, j: (b, 0, j)
    prev = lambda b, j: (b, 0, jnp.maximum(j - 1, 0))
    kv = lambda im: pl.BlockSpec((None, KV_WIDTH, WINDOW), im)
    return pl.pallas_call(
        _swa_prompt_kernel,
        grid=(batch, nb),
        in_specs=[pl.BlockSpec(memory_space=pltpu.SMEM),
                  pl.BlockSpec((WINDOW, D_MODEL), rows),
                  kv(prev), kv(cur), kv(prev), kv(cur),
                  _full(bias_band.shape)],
        out_specs=pl.BlockSpec((WINDOW, D_MODEL), rows),
        out_shape=jax.ShapeDtypeStruct((batch * seq, D_MODEL), F32),
        compiler_params=_cparams(("parallel", "arbitrary")),
        name="swa_prompt",
    )(sinks, q, kt, kt, vt, vt, bias_band)


def _head_diag(o_full):
    out = jnp.zeros((N_HEADS, HEAD_DIM), F32)
    row_kv = lax.broadcasted_iota(jnp.int32, (N_HEADS, HEAD_DIM), 0) // GROUP
    for h in range(N_KV):
        out = out + jnp.where(row_kv == h, o_full[:, h * HEAD_DIM:(h + 1) * HEAD_DIM], 0.0)
    return out


def _swa_sample_kernel(qm_ref, kc_ref, vc_ref, kn_ref, vn_ref, knt_ref, vnt_ref, bias_ref, bias0_ref,
                       sink_ref, o_ref, wk_ref, wv_ref, *, bt):
    i = pl.program_id(0)
    lane = lax.broadcasted_iota(jnp.int32, (KV_WIDTH, WINDOW), 1)
    sink = sink_ref[...]
    for e in range(bt):
        b = i * bt + e
        kc = kc_ref[e]
        vc = vc_ref[e]
        qm = qm_ref[e]
        s = _dot(qm, kc.astype(BF16)) + bias_ref[...]
        s_n = jnp.sum(qm.astype(F32) * _bf16_round(kn_ref[e]), axis=1, keepdims=True) + bias0_ref[...]
        m = jnp.maximum(jnp.maximum(jnp.max(s, axis=1, keepdims=True), s_n), sink)
        p = jnp.exp(s - m)
        p_n = jnp.exp(s_n - m)
        denom = jnp.sum(p, axis=1, keepdims=True) + p_n + jnp.exp(sink - m)
        o_full = _dot_nt((p / denom).astype(BF16), vc.astype(BF16))
        o_full = o_full + _bf16_round(p_n / denom) * _bf16_round(vn_ref[e])
        o_ref[e] = _head_diag(o_full)
        kcol = jnp.sum(jnp.where(lane == b, knt_ref[...], 0.0), axis=1, keepdims=True)
        vcol = jnp.sum(jnp.where(lane == b, vnt_ref[...], 0.0), axis=1, keepdims=True)
        wk_ref[e] = jnp.where(lane == WINDOW - 1, kcol, pltpu.roll(kc, WINDOW - 1, 1))
        wv_ref[e] = jnp.where(lane == WINDOW - 1, vcol, pltpu.roll(vc, WINDOW - 1, 1))


def _swa_sample(qm, kc, vc, kn, vn, knt, vnt, bias_keys, bias0, sinks, bt):
    m = qm.shape[0]
    blk3 = lambda s1, s2: pl.BlockSpec((bt, s1, s2), lambda i: (i, 0, 0))
    cache = jax.ShapeDtypeStruct((m, KV_WIDTH, WINDOW), F32)
    return pl.pallas_call(
        functools.partial(_swa_sample_kernel, bt=bt),
        grid=(m // bt,),
        in_specs=[blk3(N_HEADS, KV_WIDTH), blk3(KV_WIDTH, WINDOW), blk3(KV_WIDTH, WINDOW),
                  blk3(1, KV_WIDTH), blk3(1, KV_WIDTH), _full(knt.shape), _full(vnt.shape),
                  _full(bias_keys.shape), _full(bias0.shape), _full(sinks.shape)],
        out_specs=[blk3(N_HEADS, HEAD_DIM), blk3(KV_WIDTH, WINDOW), blk3(KV_WIDTH, WINDOW)],
        out_shape=[jax.ShapeDtypeStruct((m, N_HEADS, HEAD_DIM), F32), cache, cache],
        compiler_params=_cparams(("parallel",)),
        name="swa_sample",
    )(qm, kc, vc, kn, vn, knt, vnt, bias_keys, bias0, sinks)


def _split3(x):
    hi = x.astype(BF16)
    r = x - hi.astype(F32)
    mid = r.astype(BF16)
    lo = (r - mid.astype(F32)).astype(BF16)
    return hi, mid, lo


def _cumsum_lanes_blocks(x, blk):
    n = x.shape[0]
    hi, mid, lo = _split3(x)
    r_i = lax.broadcasted_iota(jnp.int32, (blk, blk), 0)
    c_i = lax.broadcasted_iota(jnp.int32, (blk, blk), 1)
    upper = jnp.where(r_i <= c_i, 1.0, 0.0).astype(BF16)
    r = _dot(jnp.concatenate([hi, mid, lo], axis=0), upper)
    return r[0:n] + r[n:2 * n] + r[2 * n:]


def _cumsum_lanes_kernel(x_ref, c_ref, carry, *, blk):
    i = pl.program_id(1)

    @pl.when(i == 0)
    def _():
        carry[...] = jnp.zeros_like(carry)

    c = _cumsum_lanes_blocks(x_ref[...], blk) + carry[:, 0:1]
    c_ref[...] = c
    carry[...] = jnp.broadcast_to(c[:, blk - 1:blk], carry.shape)


def _cumsum_lanes(x, blk):
    b, h, s = x.shape
    spec = pl.BlockSpec((None, h, blk), lambda bi, i: (bi, 0, i))
    return pl.pallas_call(
        functools.partial(_cumsum_lanes_kernel, blk=blk),
        grid=(b, s // blk),
        in_specs=[spec],
        out_specs=spec,
        out_shape=jax.ShapeDtypeStruct(x.shape, F32),
        scratch_shapes=[pltpu.VMEM((h, LANES), F32)],
        compiler_params=_cparams(("parallel", "arbitrary")),
        name="fox_cumsum",
    )(x)


def _fox_prompt_kernel(q_ref, kt_ref, v_ref, c_ref, o_ref, m_ref, l_ref, acc_ref, *, tq, tk):
    qi = pl.program_id(2)
    ki = pl.program_id(3)

    @pl.when(ki == 0)
    def _():
        m_ref[...] = jnp.full(m_ref.shape, NEG, F32)
        l_ref[...] = jnp.zeros(l_ref.shape, F32)
        acc_ref[...] = jnp.zeros(acc_ref.shape, F32)

    @pl.when(ki <= qi)
    def _():
        q = q_ref[...].reshape(GROUP * tq, HEAD_DIM)
        s = _dot(q, kt_ref[...]).reshape(GROUP, tq, tk)
        s = s - c_ref[...][:, None, :]
        qpos = qi * tq + lax.broadcasted_iota(jnp.int32, (tq, tk), 0)
        kpos = ki * tk + lax.broadcasted_iota(jnp.int32, (tq, tk), 1)
        s = jnp.where((kpos <= qpos)[None], s, NEG).reshape(GROUP * tq, tk)
        m_old = m_ref[...]
        m_new = jnp.maximum(m_old, jnp.max(s, axis=-1, keepdims=True))
        alpha = jnp.exp(m_old - m_new)
        p = jnp.exp(s - m_new)
        l_ref[...] = alpha * l_ref[...] + jnp.sum(p, axis=-1, keepdims=True)
        acc_ref[...] = alpha * acc_ref[...] + _dot(p.astype(BF16), v_ref[...])
        m_ref[...] = m_new

    @pl.when(ki == qi)
    def _():
        o = (acc_ref[...] / l_ref[...]).reshape(GROUP, tq, HEAD_DIM)
        o_ref[...] = jnp.concatenate([o[g] for g in range(GROUP)], axis=1)


def _fox_prompt(qh, ktb, vh, ct, batch, seq, tq, tk):
    nq, nk = seq // tq, seq // tk
    last = lambda qi: (qi * tq + tq - 1) // tk
    kblk = lambda qi, ki: jnp.minimum(ki, last(qi))
    return pl.pallas_call(
        functools.partial(_fox_prompt_kernel, tq=tq, tk=tk),
        grid=(batch, N_KV, nq, nk),
        in_specs=[pl.BlockSpec((GROUP, tq, HEAD_DIM), lambda b, h, qi, ki: (h, b * nq + qi, 0)),
                  pl.BlockSpec((None, HEAD_DIM, tk), lambda b, h, qi, ki: (b, h, kblk(qi, ki))),
                  pl.BlockSpec((None, tk, HEAD_DIM), lambda b, h, qi, ki: (h, b * nk + kblk(qi, ki), 0)),
                  pl.BlockSpec((None, None, GROUP, tk), lambda b, h, qi, ki: (b, h, 0, kblk(qi, ki)))],
        out_specs=pl.BlockSpec((tq, GROUP * HEAD_DIM), lambda b, h, qi, ki: (b * nq + qi, h)),
        out_shape=jax.ShapeDtypeStruct((batch * seq, D_MODEL), F32),
        scratch_shapes=[pltpu.VMEM((GROUP * tq, 1), F32), pltpu.VMEM((GROUP * tq, 1), F32),
                        pltpu.VMEM((GROUP * tq, HEAD_DIM), F32)],
        compiler_params=_cparams(("parallel", "parallel", "parallel", "arbitrary")),
        name="fox_prompt",
    )(qh, ktb, vh, ct)


def _fox_sample_kernel(pt_ref, qm_ref, kn_ref, vn_ref, fn_ref, ck_hbm, cv_hbm, cf_hbm, o_ref,
                       kbuf, vbuf, fbuf, sems, m_ref, l_ref, acc_ref, carry_ref, *, cp, nchunk):
    b = pl.program_id(0)
    c = pl.program_id(1)
    nb = pl.num_programs(0)
    step = b * nchunk + c
    slot = step % 2

    def copies(bb, cc, sl, p):
        page = pt_ref[bb, cc * cp + p]
        return (pltpu.make_async_copy(ck_hbm.at[page], kbuf.at[sl, p], sems.at[sl, 0]),
                pltpu.make_async_copy(cv_hbm.at[page], vbuf.at[sl, p], sems.at[sl, 1]),
                pltpu.make_async_copy(cf_hbm.at[page], fbuf.at[sl, p], sems.at[sl, 2]))

    def issue(bb, cc, sl):
        def body(p, carry):
            for cpy in copies(bb, cc, sl, p):
                cpy.start()
            return carry
        lax.fori_loop(0, cp, body, 0)

    @pl.when(step == 0)
    def _():
        issue(b, c, slot)

    @pl.when(step + 1 < nb * nchunk)
    def _():
        nxt = step + 1
        issue(nxt // nchunk, nxt % nchunk, 1 - slot)

    def wait_body(p, carry):
        for cpy in copies(b, c, slot, p):
            cpy.wait()
        return carry
    lax.fori_loop(0, cp, wait_body, 0)

    @pl.when(c == 0)
    def _():
        m_ref[...] = jnp.full(m_ref.shape, NEG, F32)
        l_ref[...] = jnp.zeros(l_ref.shape, F32)
        acc_ref[...] = jnp.zeros(acc_ref.shape, F32)
        carry_ref[...] = jnp.zeros(carry_ref.shape, F32)

    qm = qm_ref[...]
    cin = _cumsum_lanes_blocks(fbuf[slot].reshape(cp * N_HEADS, PAGE), PAGE).reshape(cp, N_HEADS, PAGE)
    carry = carry_ref[...]
    cs = []
    for p in range(cp):
        cs.append(cin[p] + carry)
        carry = carry + cin[p][:, PAGE - 1:PAGE]
    carry_ref[...] = carry
    kcat = jnp.concatenate([kbuf[slot, p] for p in range(cp)], axis=1).astype(BF16)
    t = _dot(qm, kcat) - jnp.concatenate(cs, axis=1)
    m_old = m_ref[...]
    m_new = jnp.maximum(m_old, jnp.max(t, axis=1, keepdims=True))
    alpha = jnp.exp(m_old - m_new)
    p = jnp.exp(t - m_new)
    l_new = alpha * l_ref[...] + jnp.sum(p, axis=1, keepdims=True)
    vcat = jnp.concatenate([vbuf[slot, p_] for p_ in range(cp)], axis=1).astype(BF16)
    acc_new = alpha * acc_ref[...] + _dot_nt(p.astype(BF16), vcat)
    m_ref[...] = m_new
    l_ref[...] = l_new
    acc_ref[...] = acc_new

    @pl.when(c == nchunk - 1)
    def _():
        s_n = jnp.sum(qm.astype(F32) * _bf16_round(kn_ref[...]), axis=1, keepdims=True)
        t_n = s_n - (carry + fn_ref[...])
        m_f = jnp.maximum(m_new, t_n)
        a = jnp.exp(m_new - m_f)
        p_n = jnp.exp(t_n - m_f)
        l_f = a * l_new + p_n
        acc = a * acc_new + _bf16_round(p_n) * _bf16_round(vn_ref[...])
        o_ref[...] = _head_diag(acc / l_f)


def _fox_sample(page_table, qm, kn, vn, fn, ck, cv, cf, cp):
    m, n_pages = page_table.shape
    nchunk = n_pages // cp
    per_b = lambda s1, s2: pl.BlockSpec((None, s1, s2), lambda b, c, pt: (b, 0, 0))
    any_spec = pl.BlockSpec(memory_space=pl.ANY)
    return pl.pallas_call(
        functools.partial(_fox_sample_kernel, cp=cp, nchunk=nchunk),
        grid_spec=pltpu.PrefetchScalarGridSpec(
            num_scalar_prefetch=1,
            grid=(m, nchunk),
            in_specs=[per_b(N_HEADS, KV_WIDTH), per_b(1, KV_WIDTH), per_b(1, KV_WIDTH), per_b(N_HEADS, 1),
                      any_spec, any_spec, any_spec],
            out_specs=per_b(N_HEADS, HEAD_DIM),
            scratch_shapes=[pltpu.VMEM((2, cp, KV_WIDTH, PAGE), F32),
                            pltpu.VMEM((2, cp, KV_WIDTH, PAGE), F32),
                            pltpu.VMEM((2, cp, N_HEADS, PAGE), F32),
                            pltpu.SemaphoreType.DMA((2, 3)),
                            pltpu.VMEM((N_HEADS, 1), F32), pltpu.VMEM((N_HEADS, 1), F32),
                            pltpu.VMEM((N_HEADS, KV_WIDTH), F32), pltpu.VMEM((N_HEADS, 1), F32)],
        ),
        out_shape=jax.ShapeDtypeStruct((m, N_HEADS, HEAD_DIM), F32),
        compiler_params=_cparams(("arbitrary", "arbitrary")),
        name="fox_sample",
    )(page_table, qm, kn, vn, fn, ck, cv, cf)


def _head_rows_q(q_rows):
    m = q_rows.shape[0]
    q4 = q_rows.reshape(m, N_KV, GROUP, HEAD_DIM)
    eye = jnp.eye(N_KV, dtype=q_rows.dtype)
    return jnp.einsum("bkgd,kj->bkgjd", q4, eye).reshape(m, N_HEADS, KV_WIDTH)


def _tokens_last(x):
    lead = x.shape[:-3]
    n = len(lead)
    xt = jnp.transpose(x, tuple(range(n)) + (n + 1, n + 2, n))
    return xt.reshape(lead + (KV_WIDTH, x.shape[-3]))


def _tokens_first(xt):
    lead = xt.shape[:-2]
    n = len(lead)
    x4 = xt.reshape(lead + (N_KV, HEAD_DIM, xt.shape[-1]))
    return jnp.transpose(x4, tuple(range(n)) + (n + 2, n, n + 1))


def kernel(x_prompt, x_sample, state_pool, cache_win_k, cache_win_v, cache_fox_k, cache_fox_v,
           cache_fox_logf, page_table, norm_g, final_norm_g, rel_bias, pool_w_in, pool_mix,
           pool_scale, pool_w_out, swa_w_in, swa_sinks, swa_w_out, fox_w_in, fox_f_bias, fox_w_out):
    batch, seq, _ = x_prompt.shape
    db = x_sample.shape[0]
    depth = norm_g.shape[0]
    mp = batch * seq
    tm_p, tm_s = 512, db

    xp = x_prompt.reshape(mp, D_MODEL)
    xs = x_sample.reshape(db, D_MODEL)

    dist_band = np.arange(WINDOW)[:, None] + WINDOW - np.arange(2 * WINDOW)[None, :]
    in_window = (dist_band >= 0) & (dist_band < WINDOW)
    rb = rel_bias.astype(F32)
    bias_band = jnp.where(in_window[None], jnp.transpose(rb[_t5_bucket_np(dist_band)], (2, 0, 1)), NEG)
    dist_keys = WINDOW - np.arange(WINDOW)
    bias_keys = jnp.where((dist_keys < WINDOW)[None, :], rb[_t5_bucket_np(dist_keys)].T, NEG)
    bias0 = rb[0].reshape(N_HEADS, 1)

    pool_p, pool_s = [], []
    wk_p, wv_p, wk_s, wv_s = [], [], [], []
    fk_p, fv_p, fl_p, fk_s, fv_s, fl_s = [], [], [], [], [], []
    for i in range(depth):
        kind, j = i % 3, i // 3
        g = norm_g[i].reshape(1, D_MODEL)
        if kind == 0:
            w_in = pool_w_in[j].astype(BF16)
            mix = pool_mix[j].astype(BF16)
            scale = pool_scale[j].reshape(1, D_MODEL)
            w_out = pool_w_out[j].astype(BF16)
            u_p, gate_p = _pool_proj(xp, g, w_in, tm_p)
            o_p = _pool_prompt(u_p, gate_p, mix, scale, batch, seq, tm_p)
            xp = _out_proj(o_p, None, w_out, xp, tm_p)
            pool_p.append(u_p.reshape(batch, seq, D_MODEL)[:, seq - POOL_STATE:])
            u_s, gate_s = _pool_proj(xs, g, w_in, tm_s)
            o_s = _pool_sample(jnp.transpose(state_pool[j], (1, 0, 2)), u_s, gate_s, mix, scale)
            xs = _out_proj(o_s, None, w_out, xs, tm_s)
            pool_s.append(jnp.concatenate([state_pool[j][:, 1:], u_s[:, None, :]], axis=1))
        elif kind == 1:
            w_in = swa_w_in[j].astype(BF16)
            w_out = swa_w_out[j].astype(BF16)
            sinks = swa_sinks[j].astype(F32)
            q_p, _, _, kt_p, vt_p, gate_p = _swa_proj(xp, g, w_in, tm_p, batch, seq)
            o_p = _swa_prompt(q_p, kt_p, vt_p, bias_band, sinks, batch, seq)
            xp = _out_proj(o_p, gate_p, w_out, xp, tm_p)
            wk_p.append(_tokens_first(kt_p[:, :, seq - WINDOW:]))
            wv_p.append(_tokens_first(vt_p[:, :, seq - WINDOW:]))
            q_s, k_s, v_s, kt_s, vt_s, gate_s = _swa_proj(xs, g, w_in, tm_s, 1, db)
            o_s, wk, wv = _swa_sample(_head_rows_q(q_s), _tokens_last(cache_win_k[j]),
                                      _tokens_last(cache_win_v[j]), k_s[:, None, :], v_s[:, None, :],
                                      kt_s[0], vt_s[0], bias_keys, bias0, sinks.reshape(N_HEADS, 1), 8)
            xs = _out_proj(o_s.reshape(db, D_MODEL), gate_s, w_out, xs, tm_s)
            wk_s.append(_tokens_first(wk))
            wv_s.append(_tokens_first(wv))
        else:
            w_full = fox_w_in[j]
            nqkv = D_MODEL + 2 * KV_WIDTH
            w_in = jnp.concatenate([w_full[:, :nqkv], w_full[:, nqkv + N_HEADS:]], axis=1).astype(BF16)
            wf = jnp.pad(w_full[:, nqkv:nqkv + N_HEADS], ((0, 0), (0, LANES - N_HEADS))).astype(BF16)
            fb = jnp.pad(fox_f_bias[j].astype(F32), (0, LANES - N_HEADS)).reshape(1, LANES)
            w_out = fox_w_out[j].astype(BF16)
            qh, kt_p, vt_p, ktb, vh, logft_p, gate_p = _fox_proj(xp, g, w_in, wf, fb, tm_p, batch, seq, True)
            ct = _cumsum_lanes(logft_p, 512).reshape(batch, N_KV, GROUP, seq)
            o_p = _fox_prompt(qh, ktb, vh, ct, batch, seq, 512, 512)
            xp = _out_proj(o_p, gate_p, w_out, xp, tm_p)
            fk_p.append(_tokens_first(kt_p))
            fv_p.append(_tokens_first(vt_p))
            fl_p.append(jnp.transpose(logft_p, (0, 2, 1)))
            q_s, k_s, v_s, kt_s, vt_s, logf_s, logft_s, gate_s = _fox_proj(
                xs, g, w_in, wf, fb, tm_s, 1, db, False)
            o_s = _fox_sample(page_table, _head_rows_q(q_s), k_s[:, None, :], v_s[:, None, :],
                              logf_s[:, :, None], _tokens_last(cache_fox_k[j]),
                              _tokens_last(cache_fox_v[j]), jnp.transpose(cache_fox_logf[j], (0, 2, 1)), 32)
            xs = _out_proj(o_s.reshape(db, D_MODEL), gate_s, w_out, xs, tm_s)
            fk_s.append(_tokens_first(kt_s[0])[:, None])
            fv_s.append(_tokens_first(vt_s[0])[:, None])
            fl_s.append(logft_s[0].T[:, None, :])

    fg = final_norm_g.reshape(1, D_MODEL)
    y_prompt = _final_norm(xp, fg, tm_p).reshape(batch, seq, D_MODEL)
    y_sample = _final_norm(xs, fg, tm_s).reshape(db, 1, D_MODEL)
    return (y_prompt, y_sample, jnp.stack(pool_p), jnp.stack(pool_s), jnp.stack(wk_p), jnp.stack(wv_p),
            jnp.stack(wk_s), jnp.stack(wv_s), jnp.stack(fk_p), jnp.stack(fv_p), jnp.stack(fl_p),
            jnp.stack(fk_s), jnp.stack(fv_s), jnp.stack(fl_s))
```

```python
import functools
import math

import numpy as np
import jax
import jax.numpy as jnp
from jax import lax
from jax.experimental import pallas as pl
from jax.experimental.pallas import tpu as pltpu

D_MODEL = 1024
HEAD_DIM = 64
N_HEADS = 16
N_KV = 4
GROUP = 4
KV_WIDTH = N_KV * HEAD_DIM
POOL_WINDOWS = (2, 4, 8, 16)
POOL_GROUP = 256
POOL_STATE = 15
WINDOW = 128
N_BUCKETS = 32
MAX_DISTANCE = 128
SCALE = HEAD_DIM ** -0.5
EPS = 1e-6
NEG = -1e30
PAGE = 128
LANES = 128

BF16 = jnp.bfloat16
F32 = jnp.float32

VMEM_LIMIT = 56 * 1024 * 1024


def _cparams(sem):
    return pltpu.CompilerParams(dimension_semantics=sem, vmem_limit_bytes=VMEM_LIMIT)


def _rms_bf16(x, g):
    ms = jnp.mean(x * x, axis=-1, keepdims=True)
    return (x * lax.rsqrt(ms + EPS) * g).astype(BF16)


def _silu(x):
    return x * jax.nn.sigmoid(x)


def _dot(a, b):
    return jnp.dot(a, b, preferred_element_type=F32)


def _dot_nt(a, b):
    return lax.dot_general(a, b, (((1,), (1,)), ((), ())), preferred_element_type=F32)


def _bf16_round(x):
    return x.astype(BF16).astype(F32)


def _full(shape):
    n = len(shape)
    return pl.BlockSpec(shape, lambda *_: (0,) * n)


def _pool_proj_kernel(x_ref, g_ref, w_ref, u_ref, gate_ref):
    hb = _rms_bf16(x_ref[...], g_ref[...])
    u_ref[...] = _dot(hb, w_ref[:, :D_MODEL])
    gate_ref[...] = _dot(hb, w_ref[:, D_MODEL:])


def _pool_proj(x, g, w, tm):
    m = x.shape[0]
    row = lambda i: (i, 0)
    return pl.pallas_call(
        _pool_proj_kernel,
        grid=(m // tm,),
        in_specs=[pl.BlockSpec((tm, D_MODEL), row), _full((1, D_MODEL)), _full(w.shape)],
        out_specs=[pl.BlockSpec((tm, D_MODEL), row), pl.BlockSpec((tm, D_MODEL), row)],
        out_shape=[jax.ShapeDtypeStruct((m, D_MODEL), F32)] * 2,
        compiler_params=_cparams(("parallel",)),
        name="pool_proj",
    )(x, g, w)


def _swa_proj_kernel(x_ref, g_ref, w_ref, q_ref, k_ref, v_ref, kt_ref, vt_ref, gate_ref):
    hb = _rms_bf16(x_ref[...], g_ref[...])
    q_ref[...] = (_dot(hb, w_ref[:, :D_MODEL]) * SCALE).astype(BF16)
    k = _dot(hb, w_ref[:, D_MODEL:D_MODEL + KV_WIDTH])
    v = _dot(hb, w_ref[:, D_MODEL + KV_WIDTH:D_MODEL + 2 * KV_WIDTH])
    k_ref[...] = k
    v_ref[...] = v
    kt_ref[...] = k.T
    vt_ref[...] = v.T
    gate_ref[...] = _dot(hb, w_ref[:, D_MODEL + 2 * KV_WIDTH:])


def _swa_proj(x, g, w, tm, batch, seq):
    m = x.shape[0]
    ns = seq // tm
    row = lambda i: (i, 0)
    tcol = pl.BlockSpec((None, KV_WIDTH, tm), lambda i: (i // ns, 0, i % ns))
    return pl.pallas_call(
        _swa_proj_kernel,
        grid=(m // tm,),
        in_specs=[pl.BlockSpec((tm, D_MODEL), row), _full((1, D_MODEL)), _full(w.shape)],
        out_specs=[pl.BlockSpec((tm, D_MODEL), row), pl.BlockSpec((tm, KV_WIDTH), row),
                   pl.BlockSpec((tm, KV_WIDTH), row), tcol, tcol, pl.BlockSpec((tm, D_MODEL), row)],
        out_shape=[jax.ShapeDtypeStruct((m, D_MODEL), BF16),
                   jax.ShapeDtypeStruct((m, KV_WIDTH), F32),
                   jax.ShapeDtypeStruct((m, KV_WIDTH), F32),
                   jax.ShapeDtypeStruct((batch, KV_WIDTH, seq), F32),
                   jax.ShapeDtypeStruct((batch, KV_WIDTH, seq), F32),
                   jax.ShapeDtypeStruct((m, D_MODEL), F32)],
        compiler_params=_cparams(("parallel",)),
        name="swa_proj",
    )(x, g, w)


def _log_sigmoid(x):
    return -(jnp.maximum(-x, 0.0) + jnp.log1p(jnp.exp(-jnp.abs(x))))


LOG2E = 1.4426950408889634
AUG = 128
N_PIECES = 3
FOX_TILE = 512
FOX_KEYS = 512
FOX_QUERIES = 256
GROUP_SHIFT = GROUP.bit_length() - 1


def _fox_proj_kernel(x_ref, g_ref, w_ref, wf_ref, fb_ref, *out_refs, prompt):
    if prompt:
        qt_ref, kt_ref, vt_ref, kaug_ref, vtb_ref, logft_ref, gate_ref = out_refs
    else:
        q_ref, k_ref, v_ref, kt_ref, vt_ref, logf_ref, logft_ref, gate_ref = out_refs
    hb = _rms_bf16(x_ref[...], g_ref[...])
    q = _dot(hb, w_ref[:, :D_MODEL])
    k = _dot(hb, w_ref[:, D_MODEL:D_MODEL + KV_WIDTH])
    v = _dot(hb, w_ref[:, D_MODEL + KV_WIDTH:D_MODEL + 2 * KV_WIDTH])
    vt = v.T
    kt_ref[...] = k.T
    vt_ref[...] = vt
    logf = _log_sigmoid(_dot(hb, wf_ref[...]) + fb_ref[...])
    logft_ref[...] = logf.T[:N_HEADS, :]
    if prompt:
        tm = q.shape[0]
        qt = (q * (SCALE * LOG2E)).T.astype(BF16)
        r = lax.broadcasted_iota(jnp.int32, (AUG - HEAD_DIM, tm), 0)
        for hd in range(N_HEADS):
            h, g = divmod(hd, GROUP)
            cols = slice(g * tm, (g + 1) * tm)
            qt_ref[h, 0:HEAD_DIM, cols] = qt[hd * HEAD_DIM:(hd + 1) * HEAD_DIM, :]
            pick = jnp.logical_and(r < N_PIECES * GROUP, (r & (GROUP - 1)) == g)
            qt_ref[h, HEAD_DIM:AUG, cols] = jnp.where(pick, -1.0, 0.0).astype(BF16)
        zeros = jnp.zeros((tm, AUG - HEAD_DIM), F32)
        for h in range(N_KV):
            kaug_ref[h] = jnp.concatenate([k[:, h * HEAD_DIM:(h + 1) * HEAD_DIM], zeros], axis=1).astype(BF16)
            vth = vt[h * HEAD_DIM:(h + 1) * HEAD_DIM, :].astype(BF16)
            for kb in range(tm // FOX_KEYS):
                vtb_ref[h, kb] = vth[:, kb * FOX_KEYS:(kb + 1) * FOX_KEYS]
    else:
        q_ref[...] = (q * SCALE).astype(BF16)
        k_ref[...] = k
        v_ref[...] = v
        logf_ref[...] = logf[:, :N_HEADS]
    gate_ref[...] = _dot(hb, w_ref[:, D_MODEL + 2 * KV_WIDTH:])


def _fox_proj(x, g, w, wf, fb, tm, batch, seq, prompt):
    m = x.shape[0]
    ns = seq // tm
    row = lambda i: (i, 0)
    tcol = lambda r: pl.BlockSpec((None, r, tm), lambda i: (i // ns, 0, i % ns))
    sds = jax.ShapeDtypeStruct
    kv_t = [tcol(KV_WIDTH), tcol(KV_WIDTH)]
    kv_t_shape = [sds((batch, KV_WIDTH, seq), F32)] * 2
    if prompt:
        out_specs = ([pl.BlockSpec((None, N_KV, None, AUG, GROUP * tm), lambda i: (i // ns, 0, i % ns, 0, 0))] + kv_t
                     + [pl.BlockSpec((N_KV, tm, AUG), lambda i: (0, i, 0)),
                        pl.BlockSpec((None, N_KV, tm // FOX_KEYS, HEAD_DIM, FOX_KEYS),
                                     lambda i: (i // ns, 0, i % ns, 0, 0)),
                        tcol(N_HEADS)])
        out_shape = ([sds((batch, N_KV, ns, AUG, GROUP * tm), BF16)] + kv_t_shape
                     + [sds((N_KV, m, AUG), BF16),
                        sds((batch, N_KV, seq // FOX_KEYS, HEAD_DIM, FOX_KEYS), BF16),
                        sds((batch, N_HEADS, seq), F32)])
    else:
        out_specs = ([pl.BlockSpec((tm, D_MODEL), row), pl.BlockSpec((tm, KV_WIDTH), row),
                      pl.BlockSpec((tm, KV_WIDTH), row)] + kv_t
                     + [pl.BlockSpec((tm, N_HEADS), row), tcol(N_HEADS)])
        out_shape = ([sds((m, D_MODEL), BF16), sds((m, KV_WIDTH), F32), sds((m, KV_WIDTH), F32)]
                     + kv_t_shape + [sds((m, N_HEADS), F32), sds((batch, N_HEADS, seq), F32)])
    out_specs.append(pl.BlockSpec((tm, D_MODEL), row))
    out_shape.append(sds((m, D_MODEL), F32))
    return pl.pallas_call(
        functools.partial(_fox_proj_kernel, prompt=prompt),
        grid=(m // tm,),
        in_specs=[pl.BlockSpec((tm, D_MODEL), row), _full((1, D_MODEL)), _full(w.shape),
                  _full(wf.shape), _full(fb.shape)],
        out_specs=out_specs,
        out_shape=out_shape,
        compiler_params=_cparams(("parallel",)),
        name="fox_proj",
    )(x, g, w, wf, fb)


def _out_proj_kernel(*refs, gated):
    if gated:
        o_ref, gate_ref, w_ref, x_ref, y_ref = refs
        ob = (o_ref[...] * _silu(gate_ref[...])).astype(BF16)
    else:
        o_ref, w_ref, x_ref, y_ref = refs
        ob = o_ref[...]
    y_ref[...] = x_ref[...] + _dot(ob, w_ref[...])


def _out_proj(o, gate, w, x, tm):
    m = x.shape[0]
    row = lambda i: (i, 0)
    tile = pl.BlockSpec((tm, D_MODEL), row)
    gated = gate is not None
    ins = [o, gate, w, x] if gated else [o, w, x]
    in_specs = [tile, tile, _full(w.shape), tile] if gated else [tile, _full(w.shape), tile]
    return pl.pallas_call(
        functools.partial(_out_proj_kernel, gated=gated),
        grid=(m // tm,),
        in_specs=in_specs,
        out_specs=tile,
        out_shape=jax.ShapeDtypeStruct((m, D_MODEL), F32),
        compiler_params=_cparams(("parallel",)),
        name="out_proj",
    )(*ins)


def _final_norm_kernel(x_ref, g_ref, y_ref):
    x = x_ref[...]
    ms = jnp.mean(x * x, axis=-1, keepdims=True)
    y_ref[...] = x * lax.rsqrt(ms + EPS) * g_ref[...]


def _final_norm(x, g, tm):
    m = x.shape[0]
    row = lambda i: (i, 0)
    return pl.pallas_call(
        _final_norm_kernel,
        grid=(m // tm,),
        in_specs=[pl.BlockSpec((tm, D_MODEL), row), _full((1, D_MODEL))],
        out_specs=pl.BlockSpec((tm, D_MODEL), row),
        out_shape=jax.ShapeDtypeStruct((m, D_MODEL), F32),
        compiler_params=_cparams(("parallel",)),
        name="final_norm",
    )(x, g)


HALO = 16


def _pool_finish(pooled_groups, u, gate, mix_ref, scale_ref, o_ref):
    pieces = []
    for g in range(len(POOL_WINDOWS)):
        c0 = g * POOL_GROUP
        p = (pooled_groups[g] - u[:, c0:c0 + POOL_GROUP]).astype(BF16)
        pieces.append(_dot(p, mix_ref[g]))
    pm = jnp.concatenate(pieces, axis=1)
    o_ref[...] = (pm * scale_ref[...] * _silu(gate)).astype(BF16)


def _pool_prompt_kernel(u_ref, gate_ref, mix_ref, scale_ref, o_ref, ubuf, *, tp):
    i = pl.program_id(1)

    @pl.when(i == 0)
    def _():
        ubuf[0:HALO, :] = jnp.zeros((HALO, D_MODEL), F32)

    u = u_ref[...]
    ubuf[HALO:HALO + tp, :] = u
    pos = i * tp + lax.broadcasted_iota(jnp.int32, (tp, 1), 0)
    pooled = []
    for g, w in enumerate(POOL_WINDOWS):
        c0 = g * POOL_GROUP
        acc = u[:, c0:c0 + POOL_GROUP]
        for k in range(1, w):
            acc = acc + ubuf[HALO - k:HALO - k + tp, c0:c0 + POOL_GROUP]
        cnt = jnp.minimum(pos + 1, w).astype(F32)
        pooled.append(acc / cnt)
    _pool_finish(pooled, u, gate_ref[...], mix_ref, scale_ref, o_ref)
    ubuf[0:HALO, :] = u[tp - HALO:tp, :]


def _pool_prompt(u, gate, mix, scale, batch, seq, tp):
    ns = seq // tp
    tile = pl.BlockSpec((tp, D_MODEL), lambda b, i: (b * ns + i, 0))
    return pl.pallas_call(
        functools.partial(_pool_prompt_kernel, tp=tp),
        grid=(batch, ns),
        in_specs=[tile, tile, _full(mix.shape), _full(scale.shape)],
        out_specs=tile,
        out_shape=jax.ShapeDtypeStruct((batch * seq, D_MODEL), BF16),
        scratch_shapes=[pltpu.VMEM((HALO + tp, D_MODEL), F32)],
        compiler_params=_cparams(("parallel", "arbitrary")),
        name="pool_prompt",
    )(u, gate, mix, scale)


def _pool_sample_kernel(st_ref, u_ref, gate_ref, mix_ref, scale_ref, o_ref):
    u = u_ref[...]
    pooled = []
    for g, w in enumerate(POOL_WINDOWS):
        c0 = g * POOL_GROUP
        acc = u[:, c0:c0 + POOL_GROUP]
        for k in range(1, w):
            acc = acc + st_ref[POOL_STATE - k, :, c0:c0 + POOL_GROUP]
        pooled.append(acc / float(w))
    _pool_finish(pooled, u, gate_ref[...], mix_ref, scale_ref, o_ref)


def _pool_sample(state_t, u, gate, mix, scale):
    m = u.shape[0]
    return pl.pallas_call(
        _pool_sample_kernel,
        grid=(1,),
        in_specs=[_full(state_t.shape), _full(u.shape), _full(gate.shape), _full(mix.shape),
                  _full(scale.shape)],
        out_specs=_full((m, D_MODEL)),
        out_shape=jax.ShapeDtypeStruct((m, D_MODEL), BF16),
        compiler_params=_cparams(("arbitrary",)),
        name="pool_sample",
    )(state_t, u, gate, mix, scale)


def _t5_bucket_np(dist):
    n = np.maximum(dist, 0)
    max_exact = N_BUCKETS // 2
    nf = np.maximum(n, 1).astype(np.float32)
    large = max_exact + (np.log(nf / max_exact) / math.log(MAX_DISTANCE / max_exact)
                         * (N_BUCKETS - max_exact)).astype(np.int32)
    large = np.minimum(large, N_BUCKETS - 1)
    return np.where(n < max_exact, n, large)


def _swa_prompt_kernel(sink_ref, rb_ref, q_ref, kp_ref, kc_ref, vp_ref, vc_ref, bucket_ref, o_ref, bias_ref):
    j = pl.program_id(1)

    @pl.when(j == 0)
    def _():
        bucket = bucket_ref[...]
        hits = [bucket == bk for bk in range(N_BUCKETS)]
        for hd in range(N_HEADS):
            b = jnp.full((WINDOW, 2 * WINDOW), NEG, F32)
            for bk in range(N_BUCKETS):
                b = jnp.where(hits[bk], rb_ref[bk, hd], b)
            bias_ref[hd] = b

    kb = jnp.concatenate([kp_ref[...], kc_ref[...]], axis=1).astype(BF16)
    vb = jnp.concatenate([vp_ref[...], vc_ref[...]], axis=1).astype(BF16)
    col = lax.broadcasted_iota(jnp.int32, (WINDOW, 2 * WINDOW), 1)
    no_prev = jnp.logical_and(j == 0, col < WINDOW)
    q = q_ref[...]
    outs = []
    for h in range(N_KV):
        kh = kb[h * HEAD_DIM:(h + 1) * HEAD_DIM, :]
        vh = vb[h * HEAD_DIM:(h + 1) * HEAD_DIM, :]
        for g in range(GROUP):
            hd = h * GROUP + g
            s = _dot(q[:, hd * HEAD_DIM:(hd + 1) * HEAD_DIM], kh) + bias_ref[hd]
            s = jnp.where(no_prev, NEG, s)
            sink = sink_ref[hd]
            m = jnp.maximum(jnp.max(s, axis=-1, keepdims=True), sink)
            p = jnp.exp(s - m)
            denom = jnp.sum(p, axis=-1, keepdims=True) + jnp.exp(sink - m)
            outs.append(_dot_nt((p / denom).astype(BF16), vh))
    o_ref[...] = jnp.concatenate(outs, axis=1)


def _swa_prompt(q, kt, vt, rel_bias, sinks, batch, seq):
    nb = seq // WINDOW
    dist = np.arange(WINDOW)[:, None] + WINDOW - np.arange(2 * WINDOW)[None, :]
    bucket = np.where((dist >= 0) & (dist < WINDOW), _t5_bucket_np(dist), -1).astype(np.int32)
    rows = lambda b, j: (b * nb + j, 0)
    cur = lambda b, j: (b, 0, j)
    prev = lambda b, j: (b, 0, jnp.maximum(j - 1, 0))
    kv = lambda im: pl.BlockSpec((None, KV_WIDTH, WINDOW), im)
    smem = pl.BlockSpec(memory_space=pltpu.SMEM)
    return pl.pallas_call(
        _swa_prompt_kernel,
        grid=(batch, nb),
        in_specs=[smem, smem,
                  pl.BlockSpec((WINDOW, D_MODEL), rows),
                  kv(prev), kv(cur), kv(prev), kv(cur),
                  _full(bucket.shape)],
        out_specs=pl.BlockSpec((WINDOW, D_MODEL), rows),
        out_shape=jax.ShapeDtypeStruct((batch * seq, D_MODEL), F32),
        scratch_shapes=[pltpu.VMEM((N_HEADS, WINDOW, 2 * WINDOW), F32)],
        compiler_params=_cparams(("parallel", "arbitrary")),
        name="swa_prompt",
    )(sinks, rel_bias, q, kt, kt, vt, vt, jnp.asarray(bucket))


def _head_diag(o_full):
    out = jnp.zeros((N_HEADS, HEAD_DIM), F32)
    row_kv = lax.broadcasted_iota(jnp.int32, (N_HEADS, HEAD_DIM), 0) // GROUP
    for h in range(N_KV):
        out = out + jnp.where(row_kv == h, o_full[:, h * HEAD_DIM:(h + 1) * HEAD_DIM], 0.0)
    return out


def _swa_sample_kernel(qm_ref, kc_ref, vc_ref, kn_ref, vn_ref, knt_ref, vnt_ref, bias_ref, bias0_ref,
                       sink_ref, o_ref, wk_ref, wv_ref, *, bt):
    i = pl.program_id(0)
    lane = lax.broadcasted_iota(jnp.int32, (KV_WIDTH, WINDOW), 1)
    sink = sink_ref[...]
    for e in range(bt):
        b = i * bt + e
        kc = kc_ref[e]
        vc = vc_ref[e]
        qm = qm_ref[e]
        s = _dot(qm, kc.astype(BF16)) + bias_ref[...]
        s_n = jnp.sum(qm.astype(F32) * _bf16_round(kn_ref[e]), axis=1, keepdims=True) + bias0_ref[...]
        m = jnp.maximum(jnp.maximum(jnp.max(s, axis=1, keepdims=True), s_n), sink)
        p = jnp.exp(s - m)
        p_n = jnp.exp(s_n - m)
        denom = jnp.sum(p, axis=1, keepdims=True) + p_n + jnp.exp(sink - m)
        o_full = _dot_nt((p / denom).astype(BF16), vc.astype(BF16))
        o_full = o_full + _bf16_round(p_n / denom) * _bf16_round(vn_ref[e])
        o_ref[e] = _head_diag(o_full)
        kcol = jnp.sum(jnp.where(lane == b, knt_ref[...], 0.0), axis=1, keepdims=True)
        vcol = jnp.sum(jnp.where(lane == b, vnt_ref[...], 0.0), axis=1, keepdims=True)
        wk_ref[e] = jnp.where(lane == WINDOW - 1, kcol, pltpu.roll(kc, WINDOW - 1, 1))
        wv_ref[e] = jnp.where(lane == WINDOW - 1, vcol, pltpu.roll(vc, WINDOW - 1, 1))


def _swa_sample(qm, kc, vc, kn, vn, knt, vnt, bias_keys, bias0, sinks, bt):
    m = qm.shape[0]
    blk3 = lambda s1, s2: pl.BlockSpec((bt, s1, s2), lambda i: (i, 0, 0))
    cache = jax.ShapeDtypeStruct((m, KV_WIDTH, WINDOW), F32)
    return pl.pallas_call(
        functools.partial(_swa_sample_kernel, bt=bt),
        grid=(m // bt,),
        in_specs=[blk3(N_HEADS, KV_WIDTH), blk3(KV_WIDTH, WINDOW), blk3(KV_WIDTH, WINDOW),
                  blk3(1, KV_WIDTH), blk3(1, KV_WIDTH), _full(knt.shape), _full(vnt.shape),
                  _full(bias_keys.shape), _full(bias0.shape), _full(sinks.shape)],
        out_specs=[blk3(N_HEADS, HEAD_DIM), blk3(KV_WIDTH, WINDOW), blk3(KV_WIDTH, WINDOW)],
        out_shape=[jax.ShapeDtypeStruct((m, N_HEADS, HEAD_DIM), F32), cache, cache],
        compiler_params=_cparams(("parallel",)),
        name="swa_sample",
    )(qm, kc, vc, kn, vn, knt, vnt, bias_keys, bias0, sinks)


def _split3(x):
    hi = x.astype(BF16)
    r = x - hi.astype(F32)
    mid = r.astype(BF16)
    lo = (r - mid.astype(F32)).astype(BF16)
    return hi, mid, lo


def _cumsum_lanes_blocks(x, blk):
    n = x.shape[0]
    hi, mid, lo = _split3(x)
    r_i = lax.broadcasted_iota(jnp.int32, (blk, blk), 0)
    c_i = lax.broadcasted_iota(jnp.int32, (blk, blk), 1)
    upper = jnp.where(r_i <= c_i, 1.0, 0.0).astype(BF16)
    r = _dot(jnp.concatenate([hi, mid, lo], axis=0), upper)
    return r[0:n] + r[n:2 * n] + r[2 * n:]


def _fox_decay_kernel(x_ref, kin_ref, kout_ref, carry, *, blk):
    i = pl.program_id(1)

    @pl.when(i == 0)
    def _():
        carry[...] = jnp.zeros_like(carry)

    c = _cumsum_lanes_blocks(x_ref[...], blk) + carry[:, 0:1]
    carry[...] = jnp.broadcast_to(c[:, blk - 1:blk], carry.shape)
    hi, mid, lo = _split3(c * LOG2E)
    pieces = jnp.concatenate([hi, mid, lo], axis=0).astype(F32)
    pad = jnp.zeros((LANES - N_PIECES * N_HEADS, blk), F32)
    pieces_t = jnp.concatenate([pieces, pad], axis=0).T.astype(BF16)
    src = lax.broadcasted_iota(jnp.int32, (LANES, AUG), 0)
    dst = lax.broadcasted_iota(jnp.int32, (LANES, AUG), 1) - HEAD_DIM
    in_aug = jnp.logical_and(dst >= 0, dst < N_PIECES * GROUP)
    for h in range(N_KV):
        want = (dst >> GROUP_SHIFT) * N_HEADS + h * GROUP + (dst & (GROUP - 1))
        place = jnp.where(jnp.logical_and(in_aug, src == want), 1.0, 0.0).astype(BF16)
        kout_ref[h] = kin_ref[h] + _dot(pieces_t, place).astype(BF16)


def _fox_decay(logft, kaug, blk):
    b, h, s = logft.shape
    ns = s // blk
    kspec = pl.BlockSpec((N_KV, blk, AUG), lambda bi, i: (0, bi * ns + i, 0))
    return pl.pallas_call(
        functools.partial(_fox_decay_kernel, blk=blk),
        grid=(b, ns),
        in_specs=[pl.BlockSpec((None, h, blk), lambda bi, i: (bi, 0, i)), kspec],
        out_specs=kspec,
        out_shape=jax.ShapeDtypeStruct(kaug.shape, kaug.dtype),
        scratch_shapes=[pltpu.VMEM((h, LANES), F32)],
        input_output_aliases={1: 0},
        compiler_params=_cparams(("parallel", "arbitrary")),
        name="fox_decay",
    )(logft, kaug)


def _fox_prompt_kernel(qt_ref, k_ref, vt_ref, o_ref, *scratch, tq, tk):
    assert tq == tk
    qi = pl.program_id(2)
    m_ref, l_ref, acc_ref, sa_ref, sb_ref, ca_ref, cb_ref = scratch
    cols = GROUP * tq
    m_ref[...] = jnp.full(m_ref.shape, NEG, F32)
    l_ref[...] = jnp.zeros(l_ref.shape, F32)
    acc_ref[...] = jnp.zeros(acc_ref.shape, F32)

    def logits(kb, s_ref, c_ref):
        k0 = pl.multiple_of(kb * tk, tk)
        s = _dot(k_ref[pl.ds(k0, tk), :], qt_ref[...])
        s_ref[...] = s
        c_ref[...] = jnp.max(s, axis=0, keepdims=True)

    def accumulate(kb, s, cmax):
        m_old = m_ref[...]
        m_new = jnp.maximum(m_old, cmax)
        alpha = jnp.exp2(m_old - m_new)
        p = jnp.exp2(s - m_new)
        l_ref[...] = alpha * l_ref[...] + jnp.sum(p, axis=0, keepdims=True)
        acc_ref[...] = alpha * acc_ref[...] + _dot(vt_ref[kb], p.astype(BF16))
        m_ref[...] = m_new

    def diagonal(s_ref):
        kpos = lax.broadcasted_iota(jnp.int32, (tk, cols), 0)
        qpos = lax.broadcasted_iota(jnp.int32, (tk, cols), 1) & (tq - 1)
        s = jnp.where(kpos <= qpos, s_ref[...], NEG)
        accumulate(qi, s, jnp.max(s, axis=0, keepdims=True))

    logits(0, sa_ref, ca_ref)

    def two_blocks(j, carry):
        kb = 2 * j
        logits(kb + 1, sb_ref, cb_ref)
        accumulate(kb, sa_ref[...], ca_ref[...])
        logits(kb + 2, sa_ref, ca_ref)
        accumulate(kb + 1, sb_ref[...], cb_ref[...])
        return carry

    lax.fori_loop(0, qi // 2, two_blocks, 0)

    @pl.when(qi % 2 == 1)
    def _():
        logits(qi, sb_ref, cb_ref)
        accumulate(qi - 1, sa_ref[...], ca_ref[...])
        diagonal(sb_ref)

    @pl.when(qi % 2 == 0)
    def _():
        diagonal(sa_ref)

    o = acc_ref[...] / l_ref[...]
    o_ref[...] = jnp.concatenate([o[:, g * tq:(g + 1) * tq] for g in range(GROUP)], axis=0).T


def _fox_prompt(qt, kaug, vtb, batch, seq, tq):
    nq = seq // tq
    nk, tk = vtb.shape[2], vtb.shape[4]
    cols = GROUP * tq
    return pl.pallas_call(
        functools.partial(_fox_prompt_kernel, tq=tq, tk=tk),
        grid=(batch, N_KV, nq),
        in_specs=[pl.BlockSpec((None, None, None, AUG, cols), lambda b, h, qi: (b, h, qi, 0, 0)),
                  pl.BlockSpec((None, seq, AUG), lambda b, h, qi: (h, b, 0)),
                  pl.BlockSpec((None, None, nk, HEAD_DIM, tk), lambda b, h, qi: (b, h, 0, 0, 0))],
        out_specs=pl.BlockSpec((tq, GROUP * HEAD_DIM), lambda b, h, qi: (b * nq + qi, h)),
        out_shape=jax.ShapeDtypeStruct((batch * seq, D_MODEL), F32),
        scratch_shapes=[pltpu.VMEM((1, cols), F32), pltpu.VMEM((1, cols), F32),
                        pltpu.VMEM((HEAD_DIM, cols), F32),
                        pltpu.VMEM((tk, cols), F32), pltpu.VMEM((tk, cols), F32),
                        pltpu.VMEM((1, cols), F32), pltpu.VMEM((1, cols), F32)],
        compiler_params=_cparams(("parallel", "parallel", "arbitrary")),
        name="fox_prompt",
    )(qt, kaug, vtb)


def _fox_sample_kernel(pt_ref, qm_ref, kn_ref, vn_ref, fn_ref, ck_hbm, cv_hbm, cf_hbm, o_ref,
                       kbuf, vbuf, fbuf, sems, m_ref, l_ref, acc_ref, carry_ref, *, cp, nchunk):
    b = pl.program_id(0)
    c = pl.program_id(1)
    nb = pl.num_programs(0)
    step = b * nchunk + c
    slot = step % 2

    def copies(bb, cc, sl, p):
        page = pt_ref[bb, cc * cp + p]
        return (pltpu.make_async_copy(ck_hbm.at[page], kbuf.at[sl, p], sems.at[sl, 0]),
                pltpu.make_async_copy(cv_hbm.at[page], vbuf.at[sl, p], sems.at[sl, 1]),
                pltpu.make_async_copy(cf_hbm.at[page], fbuf.at[sl, p], sems.at[sl, 2]))

    def issue(bb, cc, sl):
        def body(p, carry):
            for cpy in copies(bb, cc, sl, p):
                cpy.start()
            return carry
        lax.fori_loop(0, cp, body, 0)

    @pl.when(step == 0)
    def _():
        issue(b, c, slot)

    @pl.when(step + 1 < nb * nchunk)
    def _():
        nxt = step + 1
        issue(nxt // nchunk, nxt % nchunk, 1 - slot)

    def wait_body(p, carry):
        for cpy in copies(b, c, slot, p):
            cpy.wait()
        return carry
    lax.fori_loop(0, cp, wait_body, 0)

    @pl.when(c == 0)
    def _():
        m_ref[...] = jnp.full(m_ref.shape, NEG, F32)
        l_ref[...] = jnp.zeros(l_ref.shape, F32)
        acc_ref[...] = jnp.zeros(acc_ref.shape, F32)
        carry_ref[...] = jnp.zeros(carry_ref.shape, F32)

    qm = qm_ref[...]
    cin = _cumsum_lanes_blocks(fbuf[slot].reshape(cp * N_HEADS, PAGE), PAGE).reshape(cp, N_HEADS, PAGE)
    carry = carry_ref[...]
    cs = []
    for p in range(cp):
        cs.append(cin[p] + carry)
        carry = carry + cin[p][:, PAGE - 1:PAGE]
    carry_ref[...] = carry
    kcat = jnp.concatenate([kbuf[slot, p] for p in range(cp)], axis=1).astype(BF16)
    t = _dot(qm, kcat) - jnp.concatenate(cs, axis=1)
    m_old = m_ref[...]
    m_new = jnp.maximum(m_old, jnp.max(t, axis=1, keepdims=True))
    alpha = jnp.exp(m_old - m_new)
    p = jnp.exp(t - m_new)
    l_new = alpha * l_ref[...] + jnp.sum(p, axis=1, keepdims=True)
    vcat = jnp.concatenate([vbuf[slot, p_] for p_ in range(cp)], axis=1).astype(BF16)
    acc_new = alpha * acc_ref[...] + _dot_nt(p.astype(BF16), vcat)
    m_ref[...] = m_new
    l_ref[...] = l_new
    acc_ref[...] = acc_new

    @pl.when(c == nchunk - 1)
    def _():
        s_n = jnp.sum(qm.astype(F32) * _bf16_round(kn_ref[...]), axis=1, keepdims=True)
        t_n = s_n - (carry + fn_ref[...])
        m_f = jnp.maximum(m_new, t_n)
        a = jnp.exp(m_new - m_f)
        p_n = jnp.exp(t_n - m_f)
        l_f = a * l_new + p_n
        acc = a * acc_new + _bf16_round(p_n) * _bf16_round(vn_ref[...])
        o_ref[...] = _head_diag(acc / l_f)


def _fox_sample(page_table, qm, kn, vn, fn, ck, cv, cf, cp):
    m, n_pages = page_table.shape
    nchunk = n_pages // cp
    per_b = lambda s1, s2: pl.BlockSpec((None, s1, s2), lambda b, c, pt: (b, 0, 0))
    any_spec = pl.BlockSpec(memory_space=pl.ANY)
    return pl.pallas_call(
        functools.partial(_fox_sample_kernel, cp=cp, nchunk=nchunk),
        grid_spec=pltpu.PrefetchScalarGridSpec(
            num_scalar_prefetch=1,
            grid=(m, nchunk),
            in_specs=[per_b(N_HEADS, KV_WIDTH), per_b(1, KV_WIDTH), per_b(1, KV_WIDTH), per_b(N_HEADS, 1),
                      any_spec, any_spec, any_spec],
            out_specs=per_b(N_HEADS, HEAD_DIM),
            scratch_shapes=[pltpu.VMEM((2, cp, KV_WIDTH, PAGE), F32),
                            pltpu.VMEM((2, cp, KV_WIDTH, PAGE), F32),
                            pltpu.VMEM((2, cp, N_HEADS, PAGE), F32),
                            pltpu.SemaphoreType.DMA((2, 3)),
                            pltpu.VMEM((N_HEADS, 1), F32), pltpu.VMEM((N_HEADS, 1), F32),
                            pltpu.VMEM((N_HEADS, KV_WIDTH), F32), pltpu.VMEM((N_HEADS, 1), F32)],
        ),
        out_shape=jax.ShapeDtypeStruct((m, N_HEADS, HEAD_DIM), F32),
        compiler_params=_cparams(("arbitrary", "arbitrary")),
        name="fox_sample",
    )(page_table, qm, kn, vn, fn, ck, cv, cf)


def _head_rows_q(q_rows):
    m = q_rows.shape[0]
    q4 = q_rows.reshape(m, N_KV, GROUP, HEAD_DIM)
    eye = jnp.eye(N_KV, dtype=q_rows.dtype)
    return (q4[:, :, :, None, :] * eye[None, :, None, :, None]).reshape(m, N_HEADS, KV_WIDTH)


def _tokens_last(x):
    lead = x.shape[:-3]
    n = len(lead)
    xt = jnp.transpose(x, tuple(range(n)) + (n + 1, n + 2, n))
    return xt.reshape(lead + (KV_WIDTH, x.shape[-3]))


def _tokens_first(xt):
    lead = xt.shape[:-2]
    n = len(lead)
    x4 = xt.reshape(lead + (N_KV, HEAD_DIM, xt.shape[-1]))
    return jnp.transpose(x4, tuple(range(n)) + (n + 2, n, n + 1))


def kernel(x_prompt, x_sample, state_pool, cache_win_k, cache_win_v, cache_fox_k, cache_fox_v,
           cache_fox_logf, page_table, norm_g, final_norm_g, rel_bias, pool_w_in, pool_mix,
           pool_scale, pool_w_out, swa_w_in, swa_sinks, swa_w_out, fox_w_in, fox_f_bias, fox_w_out):
    batch, seq, _ = x_prompt.shape
    db = x_sample.shape[0]
    depth = norm_g.shape[0]
    mp = batch * seq
    tm_p, tm_s = 512, db

    xp = x_prompt.reshape(mp, D_MODEL)
    xs = x_sample.reshape(db, D_MODEL)

    rb = rel_bias.astype(F32)
    dist_keys = WINDOW - np.arange(WINDOW)
    bias_keys = jnp.where((dist_keys < WINDOW)[None, :], rb[_t5_bucket_np(dist_keys)].T, NEG)
    bias0 = rb[0].reshape(N_HEADS, 1)

    pool_p, pool_s = [], []
    wk_p, wv_p, wk_s, wv_s = [], [], [], []
    fk_p, fv_p, fl_p, fk_s, fv_s, fl_s = [], [], [], [], [], []
    for i in range(depth):
        kind, j = i % 3, i // 3
        g = norm_g[i].reshape(1, D_MODEL)
        if kind == 0:
            w_in = pool_w_in[j].astype(BF16)
            mix = pool_mix[j].astype(BF16)
            scale = pool_scale[j].reshape(1, D_MODEL)
            w_out = pool_w_out[j].astype(BF16)
            u_p, gate_p = _pool_proj(xp, g, w_in, tm_p)
            o_p = _pool_prompt(u_p, gate_p, mix, scale, batch, seq, tm_p)
            xp = _out_proj(o_p, None, w_out, xp, tm_p)
            pool_p.append(u_p.reshape(batch, seq, D_MODEL)[:, seq - POOL_STATE:])
            u_s, gate_s = _pool_proj(xs, g, w_in, tm_s)
            o_s = _pool_sample(jnp.transpose(state_pool[j], (1, 0, 2)), u_s, gate_s, mix, scale)
            xs = _out_proj(o_s, None, w_out, xs, tm_s)
            pool_s.append(jnp.concatenate([state_pool[j][:, 1:], u_s[:, None, :]], axis=1))
        elif kind == 1:
            w_in = swa_w_in[j].astype(BF16)
            w_out = swa_w_out[j].astype(BF16)
            sinks = swa_sinks[j].astype(F32)
            q_p, _, _, kt_p, vt_p, gate_p = _swa_proj(xp, g, w_in, tm_p, batch, seq)
            o_p = _swa_prompt(q_p, kt_p, vt_p, rb, sinks, batch, seq)
            xp = _out_proj(o_p, gate_p, w_out, xp, tm_p)
            wk_p.append(_tokens_first(kt_p[:, :, seq - WINDOW:]))
            wv_p.append(_tokens_first(vt_p[:, :, seq - WINDOW:]))
            q_s, k_s, v_s, kt_s, vt_s, gate_s = _swa_proj(xs, g, w_in, tm_s, 1, db)
            o_s, wk, wv = _swa_sample(_head_rows_q(q_s), _tokens_last(cache_win_k[j]),
                                      _tokens_last(cache_win_v[j]), k_s[:, None, :], v_s[:, None, :],
                                      kt_s[0], vt_s[0], bias_keys, bias0, sinks.reshape(N_HEADS, 1), 8)
            xs = _out_proj(o_s.reshape(db, D_MODEL), gate_s, w_out, xs, tm_s)
            wk_s.append(_tokens_first(wk))
            wv_s.append(_tokens_first(wv))
        else:
            w_full = fox_w_in[j]
            nqkv = D_MODEL + 2 * KV_WIDTH
            w_in = jnp.concatenate([w_full[:, :nqkv], w_full[:, nqkv + N_HEADS:]], axis=1).astype(BF16)
            wf = jnp.pad(w_full[:, nqkv:nqkv + N_HEADS], ((0, 0), (0, LANES - N_HEADS))).astype(BF16)
            fb = jnp.pad(fox_f_bias[j].astype(F32), (0, LANES - N_HEADS)).reshape(1, LANES)
            w_out = fox_w_out[j].astype(BF16)
            qt, kt_p, vt_p, kaug, vtb, logft_p, gate_p = _fox_proj(xp, g, w_in, wf, fb, FOX_TILE, batch, seq, True)
            kaug = _fox_decay(logft_p, kaug, FOX_TILE)
            o_p = _fox_prompt(qt, kaug, vtb, batch, seq, FOX_TILE)
            xp = _out_proj(o_p, gate_p, w_out, xp, tm_p)
            fk_p.append(_tokens_first(kt_p))
            fv_p.append(_tokens_first(vt_p))
            fl_p.append(jnp.transpose(logft_p, (0, 2, 1)))
            q_s, k_s, v_s, kt_s, vt_s, logf_s, logft_s, gate_s = _fox_proj(
                xs, g, w_in, wf, fb, tm_s, 1, db, False)
            o_s = _fox_sample(page_table, _head_rows_q(q_s), k_s[:, None, :], v_s[:, None, :],
                              logf_s[:, :, None], _tokens_last(cache_fox_k[j]),
                              _tokens_last(cache_fox_v[j]), jnp.transpose(cache_fox_logf[j], (0, 2, 1)), 32)
            xs = _out_proj(o_s.reshape(db, D_MODEL), gate_s, w_out, xs, tm_s)
            fk_s.append(_tokens_first(kt_s[0])[:, None])
            fv_s.append(_tokens_first(vt_s[0])[:, None])
            fl_s.append(logft_s[0].T[:, None, :])

    fg = final_norm_g.reshape(1, D_MODEL)
    y_prompt = _final_norm(xp, fg, tm_p).reshape(batch, seq, D_MODEL)
    y_sample = _final_norm(xs, fg, tm_s).reshape(db, 1, D_MODEL)
    return (y_prompt, y_sample, jnp.stack(pool_p), jnp.stack(pool_s), jnp.stack(wk_p), jnp.stack(wv_p),
            jnp.stack(wk_s), jnp.stack(wv_s), jnp.stack(fk_p), jnp.stack(fv_p), jnp.stack(fl_p),
            jnp.stack(fk_s), jnp.stack(fv_s), jnp.stack(fl_s))
```

```python
import functools
import math

import numpy as np
import jax
import jax.numpy as jnp
from jax import lax
from jax.experimental import pallas as pl
from jax.experimental.pallas import tpu as pltpu

D_MODEL = 1024
HEAD_DIM = 64
N_HEADS = 16
N_KV = 4
GROUP = 4
KV_WIDTH = N_KV * HEAD_DIM
POOL_WINDOWS = (2, 4, 8, 16)
POOL_GROUP = 256
POOL_STATE = 15
WINDOW = 128
N_BUCKETS = 32
MAX_DISTANCE = 128
SCALE = HEAD_DIM ** -0.5
EPS = 1e-6
NEG = -1e30
PAGE = 128
LANES = 128

BF16 = jnp.bfloat16
F32 = jnp.float32

VMEM_LIMIT = 56 * 1024 * 1024


def _cparams(sem):
    return pltpu.CompilerParams(dimension_semantics=sem, vmem_limit_bytes=VMEM_LIMIT)


def _rms_bf16(x, g):
    ms = jnp.mean(x * x, axis=-1, keepdims=True)
    return (x * lax.rsqrt(ms + EPS) * g).astype(BF16)


def _silu(x):
    return x * jax.nn.sigmoid(x)


def _dot(a, b):
    return jnp.dot(a, b, preferred_element_type=F32)


def _dot_nt(a, b):
    return lax.dot_general(a, b, (((1,), (1,)), ((), ())), preferred_element_type=F32)


def _bf16_round(x):
    return x.astype(BF16).astype(F32)


def _full(shape):
    n = len(shape)
    return pl.BlockSpec(shape, lambda *_: (0,) * n)


LOG2E = 1.4426950408889634


def _swa_proj_kernel(x_ref, g_ref, w_ref, *out_refs, prompt):
    if prompt:
        qt_ref, kh_ref, kt_ref, vt_ref, vtb_ref, gate_ref = out_refs
    else:
        q_ref, k_ref, v_ref, kt_ref, vt_ref, gate_ref = out_refs
    hb = _rms_bf16(x_ref[...], g_ref[...])
    q = _dot(hb, w_ref[:, :D_MODEL])
    k = _dot(hb, w_ref[:, D_MODEL:D_MODEL + KV_WIDTH])
    v = _dot(hb, w_ref[:, D_MODEL + KV_WIDTH:D_MODEL + 2 * KV_WIDTH])
    vt = v.T
    kt_ref[...] = k.T
    vt_ref[...] = vt
    if prompt:
        tm = q.shape[0]
        qt = (q * (SCALE * LOG2E)).T.astype(BF16)
        for hd in range(N_HEADS):
            h, g = divmod(hd, GROUP)
            for jb in range(tm // WINDOW):
                qt_ref[h, jb, :, g * WINDOW:(g + 1) * WINDOW] = (
                    qt[hd * HEAD_DIM:(hd + 1) * HEAD_DIM, jb * WINDOW:(jb + 1) * WINDOW])
        for h in range(N_KV):
            kh_ref[h] = k[:, h * HEAD_DIM:(h + 1) * HEAD_DIM].astype(BF16)
            vth = vt[h * HEAD_DIM:(h + 1) * HEAD_DIM, :].astype(BF16)
            for jb in range(tm // WINDOW):
                vtb_ref[h, jb] = vth[:, jb * WINDOW:(jb + 1) * WINDOW]
    else:
        q_ref[...] = (q * SCALE).astype(BF16)
        k_ref[...] = k
        v_ref[...] = v
    gate_ref[...] = _dot(hb, w_ref[:, D_MODEL + 2 * KV_WIDTH:])


def _swa_proj(x, g, w, tm, batch, seq, prompt):
    m = x.shape[0]
    ns = seq // tm
    nb = tm // WINDOW
    row = lambda i: (i, 0)
    sds = jax.ShapeDtypeStruct
    tcol = pl.BlockSpec((None, KV_WIDTH, tm), lambda i: (i // ns, 0, i % ns))
    blocks = lambda last: pl.BlockSpec((None, N_KV, nb, HEAD_DIM, last), lambda i: (i // ns, 0, i % ns, 0, 0))
    kv_t_shape = [sds((batch, KV_WIDTH, seq), F32)] * 2
    if prompt:
        out_specs = [blocks(GROUP * WINDOW), pl.BlockSpec((N_KV, tm, HEAD_DIM), lambda i: (0, i, 0)),
                     tcol, tcol, blocks(WINDOW)]
        out_shape = ([sds((batch, N_KV, seq // WINDOW, HEAD_DIM, GROUP * WINDOW), BF16),
                      sds((N_KV, m, HEAD_DIM), BF16)] + kv_t_shape
                     + [sds((batch, N_KV, seq // WINDOW, HEAD_DIM, WINDOW), BF16)])
    else:
        out_specs = [pl.BlockSpec((tm, D_MODEL), row), pl.BlockSpec((tm, KV_WIDTH), row),
                     pl.BlockSpec((tm, KV_WIDTH), row), tcol, tcol]
        out_shape = [sds((m, D_MODEL), BF16), sds((m, KV_WIDTH), F32), sds((m, KV_WIDTH), F32)] + kv_t_shape
    out_specs.append(pl.BlockSpec((tm, D_MODEL), row))
    out_shape.append(sds((m, D_MODEL), F32))
    return pl.pallas_call(
        functools.partial(_swa_proj_kernel, prompt=prompt),
        grid=(m // tm,),
        in_specs=[pl.BlockSpec((tm, D_MODEL), row), _full((1, D_MODEL)), _full(w.shape)],
        out_specs=out_specs,
        out_shape=out_shape,
        compiler_params=_cparams(("parallel",)),
        name="swa_proj",
    )(x, g, w)


def _log_sigmoid(x):
    return -(jnp.maximum(-x, 0.0) + jnp.log1p(jnp.exp(-jnp.abs(x))))


AUG = 128
N_PIECES = 3
FOX_TILE = 512
FOX_KEYS = 512
FOX_QUERIES = 256
GROUP_SHIFT = GROUP.bit_length() - 1


def _fox_proj_kernel(x_ref, g_ref, w_ref, wf_ref, fb_ref, *out_refs, prompt):
    if prompt:
        qt_ref, kt_ref, vt_ref, kaug_ref, vtb_ref, logft_ref, gate_ref = out_refs
    else:
        q_ref, k_ref, v_ref, kt_ref, vt_ref, logf_ref, logft_ref, gate_ref = out_refs
    hb = _rms_bf16(x_ref[...], g_ref[...])
    q = _dot(hb, w_ref[:, :D_MODEL])
    k = _dot(hb, w_ref[:, D_MODEL:D_MODEL + KV_WIDTH])
    v = _dot(hb, w_ref[:, D_MODEL + KV_WIDTH:D_MODEL + 2 * KV_WIDTH])
    vt = v.T
    kt_ref[...] = k.T
    vt_ref[...] = vt
    logf = _log_sigmoid(_dot(hb, wf_ref[...]) + fb_ref[...])
    logft_ref[...] = logf.T[:N_HEADS, :]
    if prompt:
        tm = q.shape[0]
        qt = (q * (SCALE * LOG2E)).T.astype(BF16)
        r = lax.broadcasted_iota(jnp.int32, (AUG - HEAD_DIM, tm), 0)
        for hd in range(N_HEADS):
            h, g = divmod(hd, GROUP)
            cols = slice(g * tm, (g + 1) * tm)
            qt_ref[h, 0:HEAD_DIM, cols] = qt[hd * HEAD_DIM:(hd + 1) * HEAD_DIM, :]
            pick = jnp.logical_and(r < N_PIECES * GROUP, (r & (GROUP - 1)) == g)
            qt_ref[h, HEAD_DIM:AUG, cols] = jnp.where(pick, -1.0, 0.0).astype(BF16)
        zeros = jnp.zeros((tm, AUG - HEAD_DIM), F32)
        for h in range(N_KV):
            kaug_ref[h] = jnp.concatenate([k[:, h * HEAD_DIM:(h + 1) * HEAD_DIM], zeros], axis=1).astype(BF16)
            vth = vt[h * HEAD_DIM:(h + 1) * HEAD_DIM, :].astype(BF16)
            for kb in range(tm // FOX_KEYS):
                vtb_ref[h, kb] = vth[:, kb * FOX_KEYS:(kb + 1) * FOX_KEYS]
    else:
        q_ref[...] = (q * SCALE).astype(BF16)
        k_ref[...] = k
        v_ref[...] = v
        logf_ref[...] = logf[:, :N_HEADS]
    gate_ref[...] = _dot(hb, w_ref[:, D_MODEL + 2 * KV_WIDTH:])


def _fox_proj(x, g, w, wf, fb, tm, batch, seq, prompt):
    m = x.shape[0]
    ns = seq // tm
    row = lambda i: (i, 0)
    tcol = lambda r: pl.BlockSpec((None, r, tm), lambda i: (i // ns, 0, i % ns))
    sds = jax.ShapeDtypeStruct
    kv_t = [tcol(KV_WIDTH), tcol(KV_WIDTH)]
    kv_t_shape = [sds((batch, KV_WIDTH, seq), F32)] * 2
    if prompt:
        out_specs = ([pl.BlockSpec((None, N_KV, None, AUG, GROUP * tm), lambda i: (i // ns, 0, i % ns, 0, 0))] + kv_t
                     + [pl.BlockSpec((N_KV, tm, AUG), lambda i: (0, i, 0)),
                        pl.BlockSpec((None, N_KV, tm // FOX_KEYS, HEAD_DIM, FOX_KEYS),
                                     lambda i: (i // ns, 0, i % ns, 0, 0)),
                        tcol(N_HEADS)])
        out_shape = ([sds((batch, N_KV, ns, AUG, GROUP * tm), BF16)] + kv_t_shape
                     + [sds((N_KV, m, AUG), BF16),
                        sds((batch, N_KV, seq // FOX_KEYS, HEAD_DIM, FOX_KEYS), BF16),
                        sds((batch, N_HEADS, seq), F32)])
    else:
        out_specs = ([pl.BlockSpec((tm, D_MODEL), row), pl.BlockSpec((tm, KV_WIDTH), row),
                      pl.BlockSpec((tm, KV_WIDTH), row)] + kv_t
                     + [pl.BlockSpec((tm, N_HEADS), row), tcol(N_HEADS)])
        out_shape = ([sds((m, D_MODEL), BF16), sds((m, KV_WIDTH), F32), sds((m, KV_WIDTH), F32)]
                     + kv_t_shape + [sds((m, N_HEADS), F32), sds((batch, N_HEADS, seq), F32)])
    out_specs.append(pl.BlockSpec((tm, D_MODEL), row))
    out_shape.append(sds((m, D_MODEL), F32))
    return pl.pallas_call(
        functools.partial(_fox_proj_kernel, prompt=prompt),
        grid=(m // tm,),
        in_specs=[pl.BlockSpec((tm, D_MODEL), row), _full((1, D_MODEL)), _full(w.shape),
                  _full(wf.shape), _full(fb.shape)],
        out_specs=out_specs,
        out_shape=out_shape,
        compiler_params=_cparams(("parallel",)),
        name="fox_proj",
    )(x, g, w, wf, fb)


def _out_proj_kernel(*refs, gated):
    if gated:
        o_ref, gate_ref, w_ref, x_ref, y_ref = refs
        ob = (o_ref[...] * _silu(gate_ref[...])).astype(BF16)
    else:
        o_ref, w_ref, x_ref, y_ref = refs
        ob = o_ref[...]
    y_ref[...] = x_ref[...] + _dot(ob, w_ref[...])


def _out_proj(o, gate, w, x, tm):
    m = x.shape[0]
    row = lambda i: (i, 0)
    tile = pl.BlockSpec((tm, D_MODEL), row)
    gated = gate is not None
    ins = [o, gate, w, x] if gated else [o, w, x]
    in_specs = [tile, tile, _full(w.shape), tile] if gated else [tile, _full(w.shape), tile]
    return pl.pallas_call(
        functools.partial(_out_proj_kernel, gated=gated),
        grid=(m // tm,),
        in_specs=in_specs,
        out_specs=tile,
        out_shape=jax.ShapeDtypeStruct((m, D_MODEL), F32),
        compiler_params=_cparams(("parallel",)),
        name="out_proj",
    )(*ins)


def _final_norm_kernel(x_ref, g_ref, y_ref):
    x = x_ref[...]
    ms = jnp.mean(x * x, axis=-1, keepdims=True)
    y_ref[...] = x * lax.rsqrt(ms + EPS) * g_ref[...]


def _final_norm(x, g, tm):
    m = x.shape[0]
    row = lambda i: (i, 0)
    return pl.pallas_call(
        _final_norm_kernel,
        grid=(m // tm,),
        in_specs=[pl.BlockSpec((tm, D_MODEL), row), _full((1, D_MODEL))],
        out_specs=pl.BlockSpec((tm, D_MODEL), row),
        out_shape=jax.ShapeDtypeStruct((m, D_MODEL), F32),
        compiler_params=_cparams(("parallel",)),
        name="final_norm",
    )(x, g)


HALO = 16


def _pool_layer_tail(x, u, gate, pooled_groups, mix_ref, scale_ref, wout_ref, fg_ref, y_ref):
    pieces = []
    for g in range(len(POOL_WINDOWS)):
        c0 = g * POOL_GROUP
        p = (pooled_groups[g] - u[:, c0:c0 + POOL_GROUP]).astype(BF16)
        pieces.append(_dot(p, mix_ref[g]))
    pm = jnp.concatenate(pieces, axis=1)
    o = (pm * scale_ref[...] * _silu(gate)).astype(BF16)
    y = x + _dot(o, wout_ref[...])
    if fg_ref is not None:
        ms = jnp.mean(y * y, axis=-1, keepdims=True)
        y = y * lax.rsqrt(ms + EPS) * fg_ref[...]
    y_ref[...] = y


def _pool_prompt_kernel(*refs, tp, final):
    if final:
        x_ref, g_ref, win_ref, mix_ref, scale_ref, wout_ref, fg_ref, y_ref, tail_ref, ubuf = refs
    else:
        x_ref, g_ref, win_ref, mix_ref, scale_ref, wout_ref, y_ref, tail_ref, ubuf = refs
        fg_ref = None
    i = pl.program_id(1)

    @pl.when(i == 0)
    def _():
        ubuf[0:HALO, :] = jnp.zeros((HALO, D_MODEL), F32)

    x = x_ref[...]
    hb = _rms_bf16(x, g_ref[...])
    u = _dot(hb, win_ref[:, :D_MODEL])
    gate = _dot(hb, win_ref[:, D_MODEL:])
    ubuf[HALO:HALO + tp, :] = u
    pos = i * tp + lax.broadcasted_iota(jnp.int32, (tp, 1), 0)
    pooled = []
    for g, w in enumerate(POOL_WINDOWS):
        c0 = g * POOL_GROUP
        acc = u[:, c0:c0 + POOL_GROUP]
        for k in range(1, w):
            acc = acc + ubuf[HALO - k:HALO - k + tp, c0:c0 + POOL_GROUP]
        cnt = jnp.minimum(pos + 1, w).astype(F32)
        pooled.append(acc / cnt)
    _pool_layer_tail(x, u, gate, pooled, mix_ref, scale_ref, wout_ref, fg_ref, y_ref)
    ubuf[0:HALO, :] = u[tp - HALO:tp, :]

    @pl.when(i == pl.num_programs(1) - 1)
    def _():
        tail_ref[...] = u[tp - HALO:tp, :]


def _pool_prompt(x, g, w_in, mix, scale, w_out, fg, batch, seq, tp):
    ns = seq // tp
    final = fg is not None
    tile = pl.BlockSpec((tp, D_MODEL), lambda b, i: (b * ns + i, 0))
    ins = [x, g, w_in, mix, scale, w_out] + ([fg] if final else [])
    in_specs = [tile] + [_full(a.shape) for a in ins[1:]]
    return pl.pallas_call(
        functools.partial(_pool_prompt_kernel, tp=tp, final=final),
        grid=(batch, ns),
        in_specs=in_specs,
        out_specs=[tile, pl.BlockSpec((None, HALO, D_MODEL), lambda b, i: (b, 0, 0))],
        out_shape=[jax.ShapeDtypeStruct((batch * seq, D_MODEL), F32),
                   jax.ShapeDtypeStruct((batch, HALO, D_MODEL), F32)],
        scratch_shapes=[pltpu.VMEM((HALO + tp, D_MODEL), F32)],
        compiler_params=_cparams(("parallel", "arbitrary")),
        name="pool_prompt",
    )(*ins)


def _pool_sample_kernel(*refs, final):
    if final:
        x_ref, st_ref, g_ref, win_ref, mix_ref, scale_ref, wout_ref, fg_ref, y_ref, u_ref = refs
    else:
        x_ref, st_ref, g_ref, win_ref, mix_ref, scale_ref, wout_ref, y_ref, u_ref = refs
        fg_ref = None
    x = x_ref[...]
    hb = _rms_bf16(x, g_ref[...])
    u = _dot(hb, win_ref[:, :D_MODEL])
    gate = _dot(hb, win_ref[:, D_MODEL:])
    u_ref[...] = u
    pooled = []
    for g, w in enumerate(POOL_WINDOWS):
        c0 = g * POOL_GROUP
        acc = u[:, c0:c0 + POOL_GROUP]
        for k in range(1, w):
            acc = acc + st_ref[POOL_STATE - k, :, c0:c0 + POOL_GROUP]
        pooled.append(acc / float(w))
    _pool_layer_tail(x, u, gate, pooled, mix_ref, scale_ref, wout_ref, fg_ref, y_ref)


def _pool_sample(x, state_t, g, w_in, mix, scale, w_out, fg):
    m = x.shape[0]
    final = fg is not None
    ins = [x, state_t, g, w_in, mix, scale, w_out] + ([fg] if final else [])
    return pl.pallas_call(
        functools.partial(_pool_sample_kernel, final=final),
        grid=(1,),
        in_specs=[_full(a.shape) for a in ins],
        out_specs=[_full((m, D_MODEL)), _full((m, D_MODEL))],
        out_shape=[jax.ShapeDtypeStruct((m, D_MODEL), F32)] * 2,
        compiler_params=_cparams(("arbitrary",)),
        name="pool_sample",
    )(*ins)


def _t5_bucket_np(dist):
    n = np.maximum(dist, 0)
    max_exact = N_BUCKETS // 2
    nf = np.maximum(n, 1).astype(np.float32)
    large = max_exact + (np.log(nf / max_exact) / math.log(MAX_DISTANCE / max_exact)
                         * (N_BUCKETS - max_exact)).astype(np.int32)
    large = np.minimum(large, N_BUCKETS - 1)
    return np.where(n < max_exact, n, large)


def _swa_prompt_kernel(sink_ref, rb_ref, qt_ref, kp_ref, kc_ref, vp_ref, vc_ref, bucket_ref, gate_ref,
                       wout_ref, x_ref, y_ref, bias_ref, o_buf, *, nb):
    i = pl.program_id(1)
    cols = GROUP * WINDOW

    @pl.when(i == 0)
    def _():
        bucket = bucket_ref[...]
        hits = [bucket == bk for bk in range(N_BUCKETS)]
        for hd in range(N_HEADS):
            h, g = divmod(hd, GROUP)
            b = jnp.full((2 * WINDOW, WINDOW), NEG, F32)
            for bk in range(N_BUCKETS):
                b = jnp.where(hits[bk], rb_ref[bk, hd] * LOG2E, b)
            bias_ref[h, :, g * WINDOW:(g + 1) * WINDOW] = b

    key_row = lax.broadcasted_iota(jnp.int32, (2 * WINDOW, cols), 0)
    no_prev = jnp.logical_and(i == 0, key_row < WINDOW)
    sinks = [jnp.concatenate([jnp.full((1, WINDOW), sink_ref[h * GROUP + g] * LOG2E, F32) for g in range(GROUP)],
                             axis=1) for h in range(N_KV)]

    def logits(jb, h):
        if jb == 0:
            kband = jnp.concatenate([kp_ref[h], kc_ref[h, 0:WINDOW, :]], axis=0)
        else:
            kband = kc_ref[h, (jb - 1) * WINDOW:(jb + 1) * WINDOW, :]
        s = _dot(kband, qt_ref[h, jb]) + bias_ref[h]
        return jnp.where(no_prev, NEG, s) if jb == 0 else s

    def attend(jb, h, s):
        vprev = vp_ref[h] if jb == 0 else vc_ref[h, jb - 1]
        vband = jnp.concatenate([vprev, vc_ref[h, jb]], axis=1)
        m = jnp.maximum(jnp.max(s, axis=0, keepdims=True), sinks[h])
        p = jnp.exp2(s - m)
        denom = jnp.sum(p, axis=0, keepdims=True) + jnp.exp2(sinks[h] - m)
        ot = _dot(vband, p.astype(BF16)) * (1.0 / denom)
        o_heads = jnp.concatenate([ot[:, g * WINDOW:(g + 1) * WINDOW] for g in range(GROUP)], axis=0)
        o_buf[jb * WINDOW:(jb + 1) * WINDOW, h * GROUP * HEAD_DIM:(h + 1) * GROUP * HEAD_DIM] = o_heads.T

    units = [(jb, h) for jb in range(nb) for h in range(N_KV)]
    s = logits(*units[0])
    for u, unit in enumerate(units):
        s_next = logits(*units[u + 1]) if u + 1 < len(units) else None
        attend(*unit, s)
        s = s_next
    ob = (o_buf[...] * _silu(gate_ref[...])).astype(BF16)
    y_ref[...] = x_ref[...] + _dot(ob, wout_ref[...])


def _swa_prompt(qt, kh, vtb, gate, w_out, x, rel_bias, sinks, batch, seq, tm):
    ns = seq // tm
    nb = tm // WINDOW
    dist = np.arange(WINDOW)[None, :] + WINDOW - np.arange(2 * WINDOW)[:, None]
    bucket = np.where((dist >= 0) & (dist < WINDOW), _t5_bucket_np(dist), -1).astype(np.int32)
    tile = pl.BlockSpec((tm, D_MODEL), lambda b, i: (b * ns + i, 0))
    smem = pl.BlockSpec(memory_space=pltpu.SMEM)
    prev_blk = lambda i: jnp.maximum(i * nb - 1, 0)
    return pl.pallas_call(
        functools.partial(_swa_prompt_kernel, nb=nb),
        grid=(batch, ns),
        in_specs=[smem, smem,
                  pl.BlockSpec((None, N_KV, nb, HEAD_DIM, GROUP * WINDOW), lambda b, i: (b, 0, i, 0, 0)),
                  pl.BlockSpec((N_KV, WINDOW, HEAD_DIM), lambda b, i: (0, b * ns * nb + prev_blk(i), 0)),
                  pl.BlockSpec((N_KV, tm, HEAD_DIM), lambda b, i: (0, b * ns + i, 0)),
                  pl.BlockSpec((None, N_KV, None, HEAD_DIM, WINDOW), lambda b, i: (b, 0, prev_blk(i), 0, 0)),
                  pl.BlockSpec((None, N_KV, nb, HEAD_DIM, WINDOW), lambda b, i: (b, 0, i, 0, 0)),
                  _full(bucket.shape), tile, _full(w_out.shape), tile],
        out_specs=tile,
        out_shape=jax.ShapeDtypeStruct((batch * seq, D_MODEL), F32),
        scratch_shapes=[pltpu.VMEM((N_KV, 2 * WINDOW, GROUP * WINDOW), F32),
                        pltpu.VMEM((tm, D_MODEL), F32)],
        compiler_params=_cparams(("parallel", "arbitrary")),
        name="swa_prompt",
    )(sinks, rel_bias, qt, kh, kh, vtb, vtb, jnp.asarray(bucket), gate, w_out, x)


def _head_diag(o_full):
    out = jnp.zeros((N_HEADS, HEAD_DIM), F32)
    row_kv = lax.broadcasted_iota(jnp.int32, (N_HEADS, HEAD_DIM), 0) // GROUP
    for h in range(N_KV):
        out = out + jnp.where(row_kv == h, o_full[:, h * HEAD_DIM:(h + 1) * HEAD_DIM], 0.0)
    return out


def _swa_sample_kernel(qm_ref, kc_ref, vc_ref, kn_ref, vn_ref, knt_ref, vnt_ref, bias_ref, bias0_ref,
                       sink_ref, o_ref, wk_ref, wv_ref, *, bt):
    i = pl.program_id(0)
    lane = lax.broadcasted_iota(jnp.int32, (KV_WIDTH, WINDOW), 1)
    sink = sink_ref[...]
    for e in range(bt):
        b = i * bt + e
        kc = kc_ref[e]
        vc = vc_ref[e]
        qm = qm_ref[e]
        s = _dot(qm, kc.astype(BF16)) + bias_ref[...]
        s_n = jnp.sum(qm.astype(F32) * _bf16_round(kn_ref[e]), axis=1, keepdims=True) + bias0_ref[...]
        m = jnp.maximum(jnp.maximum(jnp.max(s, axis=1, keepdims=True), s_n), sink)
        p = jnp.exp(s - m)
        p_n = jnp.exp(s_n - m)
        denom = jnp.sum(p, axis=1, keepdims=True) + p_n + jnp.exp(sink - m)
        o_full = _dot_nt((p / denom).astype(BF16), vc.astype(BF16))
        o_full = o_full + _bf16_round(p_n / denom) * _bf16_round(vn_ref[e])
        o_ref[e] = _head_diag(o_full)
        kcol = jnp.sum(jnp.where(lane == b, knt_ref[...], 0.0), axis=1, keepdims=True)
        vcol = jnp.sum(jnp.where(lane == b, vnt_ref[...], 0.0), axis=1, keepdims=True)
        wk_ref[e] = jnp.where(lane == WINDOW - 1, kcol, pltpu.roll(kc, WINDOW - 1, 1))
        wv_ref[e] = jnp.where(lane == WINDOW - 1, vcol, pltpu.roll(vc, WINDOW - 1, 1))


def _swa_sample(qm, kc, vc, kn, vn, knt, vnt, bias_keys, bias0, sinks, bt):
    m = qm.shape[0]
    blk3 = lambda s1, s2: pl.BlockSpec((bt, s1, s2), lambda i: (i, 0, 0))
    cache = jax.ShapeDtypeStruct((m, KV_WIDTH, WINDOW), F32)
    return pl.pallas_call(
        functools.partial(_swa_sample_kernel, bt=bt),
        grid=(m // bt,),
        in_specs=[blk3(N_HEADS, KV_WIDTH), blk3(KV_WIDTH, WINDOW), blk3(KV_WIDTH, WINDOW),
                  blk3(1, KV_WIDTH), blk3(1, KV_WIDTH), _full(knt.shape), _full(vnt.shape),
                  _full(bias_keys.shape), _full(bias0.shape), _full(sinks.shape)],
        out_specs=[blk3(N_HEADS, HEAD_DIM), blk3(KV_WIDTH, WINDOW), blk3(KV_WIDTH, WINDOW)],
        out_shape=[jax.ShapeDtypeStruct((m, N_HEADS, HEAD_DIM), F32), cache, cache],
        compiler_params=_cparams(("parallel",)),
        name="swa_sample",
    )(qm, kc, vc, kn, vn, knt, vnt, bias_keys, bias0, sinks)


def _split3(x):
    hi = x.astype(BF16)
    r = x - hi.astype(F32)
    mid = r.astype(BF16)
    lo = (r - mid.astype(F32)).astype(BF16)
    return hi, mid, lo


def _cumsum_lanes_blocks(x, blk):
    n = x.shape[0]
    hi, mid, lo = _split3(x)
    r_i = lax.broadcasted_iota(jnp.int32, (blk, blk), 0)
    c_i = lax.broadcasted_iota(jnp.int32, (blk, blk), 1)
    upper = jnp.where(r_i <= c_i, 1.0, 0.0).astype(BF16)
    r = _dot(jnp.concatenate([hi, mid, lo], axis=0), upper)
    return r[0:n] + r[n:2 * n] + r[2 * n:]


def _fox_decay_kernel(x_ref, kin_ref, kout_ref, carry, *, blk):
    i = pl.program_id(1)

    @pl.when(i == 0)
    def _():
        carry[...] = jnp.zeros_like(carry)

    c = _cumsum_lanes_blocks(x_ref[...], blk) + carry[:, 0:1]
    carry[...] = jnp.broadcast_to(c[:, blk - 1:blk], carry.shape)
    hi, mid, lo = _split3(c * LOG2E)
    pieces = jnp.concatenate([hi, mid, lo], axis=0).astype(F32)
    pad = jnp.zeros((LANES - N_PIECES * N_HEADS, blk), F32)
    pieces_t = jnp.concatenate([pieces, pad], axis=0).T.astype(BF16)
    src = lax.broadcasted_iota(jnp.int32, (LANES, AUG), 0)
    dst = lax.broadcasted_iota(jnp.int32, (LANES, AUG), 1) - HEAD_DIM
    in_aug = jnp.logical_and(dst >= 0, dst < N_PIECES * GROUP)
    for h in range(N_KV):
        want = (dst >> GROUP_SHIFT) * N_HEADS + h * GROUP + (dst & (GROUP - 1))
        place = jnp.where(jnp.logical_and(in_aug, src == want), 1.0, 0.0).astype(BF16)
        kout_ref[h] = kin_ref[h] + _dot(pieces_t, place).astype(BF16)


def _fox_decay(logft, kaug, blk):
    b, h, s = logft.shape
    ns = s // blk
    kspec = pl.BlockSpec((N_KV, blk, AUG), lambda bi, i: (0, bi * ns + i, 0))
    return pl.pallas_call(
        functools.partial(_fox_decay_kernel, blk=blk),
        grid=(b, ns),
        in_specs=[pl.BlockSpec((None, h, blk), lambda bi, i: (bi, 0, i)), kspec],
        out_specs=kspec,
        out_shape=jax.ShapeDtypeStruct(kaug.shape, kaug.dtype),
        scratch_shapes=[pltpu.VMEM((h, LANES), F32)],
        input_output_aliases={1: 0},
        compiler_params=_cparams(("parallel", "arbitrary")),
        name="fox_decay",
    )(logft, kaug)


def _fox_prompt_kernel(qt_ref, k_ref, vt_ref, o_ref, *scratch, tq, tk):
    assert tq == tk
    qi = pl.program_id(2)
    m_ref, l_ref, acc_ref, sa_ref, sb_ref, ca_ref, cb_ref = scratch
    cols = GROUP * tq
    m_ref[...] = jnp.full(m_ref.shape, NEG, F32)
    l_ref[...] = jnp.zeros(l_ref.shape, F32)
    acc_ref[...] = jnp.zeros(acc_ref.shape, F32)

    def logits(kb, s_ref, c_ref):
        k0 = pl.multiple_of(kb * tk, tk)
        s = _dot(k_ref[pl.ds(k0, tk), :], qt_ref[...])
        s_ref[...] = s
        c_ref[...] = jnp.max(s, axis=0, keepdims=True)

    def accumulate(kb, s, cmax):
        m_old = m_ref[...]
        m_new = jnp.maximum(m_old, cmax)
        alpha = jnp.exp2(m_old - m_new)
        p = jnp.exp2(s - m_new)
        l_ref[...] = alpha * l_ref[...] + jnp.sum(p, axis=0, keepdims=True)
        acc_ref[...] = alpha * acc_ref[...] + _dot(vt_ref[kb], p.astype(BF16))
        m_ref[...] = m_new

    def diagonal(s_ref):
        kpos = lax.broadcasted_iota(jnp.int32, (tk, cols), 0)
        qpos = lax.broadcasted_iota(jnp.int32, (tk, cols), 1) & (tq - 1)
        s = jnp.where(kpos <= qpos, s_ref[...], NEG)
        accumulate(qi, s, jnp.max(s, axis=0, keepdims=True))

    logits(0, sa_ref, ca_ref)

    def two_blocks(j, carry):
        kb = 2 * j
        logits(kb + 1, sb_ref, cb_ref)
        accumulate(kb, sa_ref[...], ca_ref[...])
        logits(kb + 2, sa_ref, ca_ref)
        accumulate(kb + 1, sb_ref[...], cb_ref[...])
        return carry

    lax.fori_loop(0, qi // 2, two_blocks, 0)

    @pl.when(qi % 2 == 1)
    def _():
        logits(qi, sb_ref, cb_ref)
        accumulate(qi - 1, sa_ref[...], ca_ref[...])
        diagonal(sb_ref)

    @pl.when(qi % 2 == 0)
    def _():
        diagonal(sa_ref)

    o = acc_ref[...] / l_ref[...]
    o_ref[...] = jnp.concatenate([o[:, g * tq:(g + 1) * tq] for g in range(GROUP)], axis=0).T


def _fox_prompt(qt, kaug, vtb, batch, seq, tq):
    nq = seq // tq
    nk, tk = vtb.shape[2], vtb.shape[4]
    cols = GROUP * tq
    return pl.pallas_call(
        functools.partial(_fox_prompt_kernel, tq=tq, tk=tk),
        grid=(batch, N_KV, nq),
        in_specs=[pl.BlockSpec((None, None, None, AUG, cols), lambda b, h, qi: (b, h, qi, 0, 0)),
                  pl.BlockSpec((None, seq, AUG), lambda b, h, qi: (h, b, 0)),
                  pl.BlockSpec((None, None, nk, HEAD_DIM, tk), lambda b, h, qi: (b, h, 0, 0, 0))],
        out_specs=pl.BlockSpec((tq, GROUP * HEAD_DIM), lambda b, h, qi: (b * nq + qi, h)),
        out_shape=jax.ShapeDtypeStruct((batch * seq, D_MODEL), F32),
        scratch_shapes=[pltpu.VMEM((1, cols), F32), pltpu.VMEM((1, cols), F32),
                        pltpu.VMEM((HEAD_DIM, cols), F32),
                        pltpu.VMEM((tk, cols), F32), pltpu.VMEM((tk, cols), F32),
                        pltpu.VMEM((1, cols), F32), pltpu.VMEM((1, cols), F32)],
        compiler_params=_cparams(("parallel", "parallel", "arbitrary")),
        name="fox_prompt",
    )(qt, kaug, vtb)


def _fox_sample_kernel(pt_ref, qm_ref, kn_ref, vn_ref, fn_ref, ck_hbm, cv_hbm, cf_hbm, o_ref,
                       kbuf, vbuf, fbuf, sems, m_ref, l_ref, acc_ref, carry_ref, *, cp, nchunk):
    b = pl.program_id(0)
    c = pl.program_id(1)
    nb = pl.num_programs(0)
    step = b * nchunk + c
    slot = step % 2

    def copies(bb, cc, sl, p):
        page = pt_ref[bb, cc * cp + p]
        return (pltpu.make_async_copy(ck_hbm.at[page], kbuf.at[sl, p], sems.at[sl, 0]),
                pltpu.make_async_copy(cv_hbm.at[page], vbuf.at[sl, p], sems.at[sl, 1]),
                pltpu.make_async_copy(cf_hbm.at[page], fbuf.at[sl, p], sems.at[sl, 2]))

    def issue(bb, cc, sl):
        def body(p, carry):
            for cpy in copies(bb, cc, sl, p):
                cpy.start()
            return carry
        lax.fori_loop(0, cp, body, 0)

    @pl.when(step == 0)
    def _():
        issue(b, c, slot)

    @pl.when(step + 1 < nb * nchunk)
    def _():
        nxt = step + 1
        issue(nxt // nchunk, nxt % nchunk, 1 - slot)

    def wait_body(p, carry):
        for cpy in copies(b, c, slot, p):
            cpy.wait()
        return carry
    lax.fori_loop(0, cp, wait_body, 0)

    @pl.when(c == 0)
    def _():
        m_ref[...] = jnp.full(m_ref.shape, NEG, F32)
        l_ref[...] = jnp.zeros(l_ref.shape, F32)
        acc_ref[...] = jnp.zeros(acc_ref.shape, F32)
        carry_ref[...] = jnp.zeros(carry_ref.shape, F32)

    qm = qm_ref[...]
    cin = _cumsum_lanes_blocks(fbuf[slot].reshape(cp * N_HEADS, PAGE), PAGE).reshape(cp, N_HEADS, PAGE)
    carry = carry_ref[...]
    cs = []
    for p in range(cp):
        cs.append(cin[p] + carry)
        carry = carry + cin[p][:, PAGE - 1:PAGE]
    carry_ref[...] = carry
    kcat = jnp.concatenate([kbuf[slot, p] for p in range(cp)], axis=1).astype(BF16)
    t = _dot(qm, kcat) - jnp.concatenate(cs, axis=1)
    m_old = m_ref[...]
    m_new = jnp.maximum(m_old, jnp.max(t, axis=1, keepdims=True))
    alpha = jnp.exp(m_old - m_new)
    p = jnp.exp(t - m_new)
    l_new = alpha * l_ref[...] + jnp.sum(p, axis=1, keepdims=True)
    vcat = jnp.concatenate([vbuf[slot, p_] for p_ in range(cp)], axis=1).astype(BF16)
    acc_new = alpha * acc_ref[...] + _dot_nt(p.astype(BF16), vcat)
    m_ref[...] = m_new
    l_ref[...] = l_new
    acc_ref[...] = acc_new

    @pl.when(c == nchunk - 1)
    def _():
        s_n = jnp.sum(qm.astype(F32) * _bf16_round(kn_ref[...]), axis=1, keepdims=True)
        t_n = s_n - (carry + fn_ref[...])
        m_f = jnp.maximum(m_new, t_n)
        a = jnp.exp(m_new - m_f)
        p_n = jnp.exp(t_n - m_f)
        l_f = a * l_new + p_n
        acc = a * acc_new + _bf16_round(p_n) * _bf16_round(vn_ref[...])
        o_ref[...] = _head_diag(acc / l_f)


def _fox_sample(page_table, qm, kn, vn, fn, ck, cv, cf, cp):
    m, n_pages = page_table.shape
    nchunk = n_pages // cp
    per_b = lambda s1, s2: pl.BlockSpec((None, s1, s2), lambda b, c, pt: (b, 0, 0))
    any_spec = pl.BlockSpec(memory_space=pl.ANY)
    return pl.pallas_call(
        functools.partial(_fox_sample_kernel, cp=cp, nchunk=nchunk),
        grid_spec=pltpu.PrefetchScalarGridSpec(
            num_scalar_prefetch=1,
            grid=(m, nchunk),
            in_specs=[per_b(N_HEADS, KV_WIDTH), per_b(1, KV_WIDTH), per_b(1, KV_WIDTH), per_b(N_HEADS, 1),
                      any_spec, any_spec, any_spec],
            out_specs=per_b(N_HEADS, HEAD_DIM),
            scratch_shapes=[pltpu.VMEM((2, cp, KV_WIDTH, PAGE), F32),
                            pltpu.VMEM((2, cp, KV_WIDTH, PAGE), F32),
                            pltpu.VMEM((2, cp, N_HEADS, PAGE), F32),
                            pltpu.SemaphoreType.DMA((2, 3)),
                            pltpu.VMEM((N_HEADS, 1), F32), pltpu.VMEM((N_HEADS, 1), F32),
                            pltpu.VMEM((N_HEADS, KV_WIDTH), F32), pltpu.VMEM((N_HEADS, 1), F32)],
        ),
        out_shape=jax.ShapeDtypeStruct((m, N_HEADS, HEAD_DIM), F32),
        compiler_params=_cparams(("arbitrary", "arbitrary")),
        name="fox_sample",
    )(page_table, qm, kn, vn, fn, ck, cv, cf)


def _head_rows_q(q_rows):
    m = q_rows.shape[0]
    q4 = q_rows.reshape(m, N_KV, GROUP, HEAD_DIM)
    eye = jnp.eye(N_KV, dtype=q_rows.dtype)
    return (q4[:, :, :, None, :] * eye[None, :, None, :, None]).reshape(m, N_HEADS, KV_WIDTH)


def _tokens_last(x):
    lead = x.shape[:-3]
    n = len(lead)
    xt = jnp.transpose(x, tuple(range(n)) + (n + 1, n + 2, n))
    return xt.reshape(lead + (KV_WIDTH, x.shape[-3]))


def _tokens_first(xt):
    lead = xt.shape[:-2]
    n = len(lead)
    x4 = xt.reshape(lead + (N_KV, HEAD_DIM, xt.shape[-1]))
    return jnp.transpose(x4, tuple(range(n)) + (n + 2, n, n + 1))


def kernel(x_prompt, x_sample, state_pool, cache_win_k, cache_win_v, cache_fox_k, cache_fox_v,
           cache_fox_logf, page_table, norm_g, final_norm_g, rel_bias, pool_w_in, pool_mix,
           pool_scale, pool_w_out, swa_w_in, swa_sinks, swa_w_out, fox_w_in, fox_f_bias, fox_w_out):
    batch, seq, _ = x_prompt.shape
    db = x_sample.shape[0]
    depth = norm_g.shape[0]
    mp = batch * seq
    tm_p, tm_s = 512, db

    xp = x_prompt.reshape(mp, D_MODEL)
    xs = x_sample.reshape(db, D_MODEL)

    rb = rel_bias.astype(F32)
    dist_keys = WINDOW - np.arange(WINDOW)
    bias_keys = jnp.where((dist_keys < WINDOW)[None, :], rb[_t5_bucket_np(dist_keys)].T, NEG)
    bias0 = rb[0].reshape(N_HEADS, 1)
    fg = final_norm_g.reshape(1, D_MODEL)

    pool_p, pool_s = [], []
    wk_p, wv_p, wk_s, wv_s = [], [], [], []
    fk_p, fv_p, fl_p, fk_s, fv_s, fl_s = [], [], [], [], [], []
    for i in range(depth):
        kind, j = i % 3, i // 3
        g = norm_g[i].reshape(1, D_MODEL)
        if kind == 0:
            w_in = pool_w_in[j].astype(BF16)
            mix = pool_mix[j].astype(BF16)
            scale = pool_scale[j].reshape(1, D_MODEL)
            w_out = pool_w_out[j].astype(BF16)
            fg_l = fg if i == depth - 1 else None
            xp, u_tail = _pool_prompt(xp, g, w_in, mix, scale, w_out, fg_l, batch, seq, tm_p)
            pool_p.append(u_tail[:, HALO - POOL_STATE:])
            xs, u_s = _pool_sample(xs, jnp.transpose(state_pool[j], (1, 0, 2)), g, w_in, mix, scale, w_out, fg_l)
            pool_s.append(jnp.concatenate([state_pool[j][:, 1:], u_s[:, None, :]], axis=1))
        elif kind == 1:
            w_in = swa_w_in[j].astype(BF16)
            w_out = swa_w_out[j].astype(BF16)
            sinks = swa_sinks[j].astype(F32)
            qt_p, kh_p, kt_p, vt_p, vtb_p, gate_p = _swa_proj(xp, g, w_in, tm_p, batch, seq, True)
            xp = _swa_prompt(qt_p, kh_p, vtb_p, gate_p, w_out, xp, rb, sinks, batch, seq, tm_p)
            wk_p.append(_tokens_first(kt_p[:, :, seq - WINDOW:]))
            wv_p.append(_tokens_first(vt_p[:, :, seq - WINDOW:]))
            q_s, k_s, v_s, kt_s, vt_s, gate_s = _swa_proj(xs, g, w_in, tm_s, 1, db, False)
            o_s, wk, wv = _swa_sample(_head_rows_q(q_s), _tokens_last(cache_win_k[j]),
                                      _tokens_last(cache_win_v[j]), k_s[:, None, :], v_s[:, None, :],
                                      kt_s[0], vt_s[0], bias_keys, bias0, sinks.reshape(N_HEADS, 1), 8)
            xs = _out_proj(o_s.reshape(db, D_MODEL), gate_s, w_out, xs, tm_s)
            wk_s.append(_tokens_first(wk))
            wv_s.append(_tokens_first(wv))
        else:
            w_full = fox_w_in[j]
            nqkv = D_MODEL + 2 * KV_WIDTH
            w_in = jnp.concatenate([w_full[:, :nqkv], w_full[:, nqkv + N_HEADS:]], axis=1).astype(BF16)
            wf = jnp.pad(w_full[:, nqkv:nqkv + N_HEADS], ((0, 0), (0, LANES - N_HEADS))).astype(BF16)
            fb = jnp.pad(fox_f_bias[j].astype(F32), (0, LANES - N_HEADS)).reshape(1, LANES)
            w_out = fox_w_out[j].astype(BF16)
            qt, kt_p, vt_p, kaug, vtb, logft_p, gate_p = _fox_proj(xp, g, w_in, wf, fb, FOX_TILE, batch, seq, True)
            kaug = _fox_decay(logft_p, kaug, FOX_TILE)
            o_p = _fox_prompt(qt, kaug, vtb, batch, seq, FOX_TILE)
            xp = _out_proj(o_p, gate_p, w_out, xp, tm_p)
            fk_p.append(_tokens_first(kt_p))
            fv_p.append(_tokens_first(vt_p))
            fl_p.append(jnp.transpose(logft_p, (0, 2, 1)))
            q_s, k_s, v_s, kt_s, vt_s, logf_s, logft_s, gate_s = _fox_proj(
                xs, g, w_in, wf, fb, tm_s, 1, db, False)
            o_s = _fox_sample(page_table, _head_rows_q(q_s), k_s[:, None, :], v_s[:, None, :],
                              logf_s[:, :, None], _tokens_last(cache_fox_k[j]),
                              _tokens_last(cache_fox_v[j]), jnp.transpose(cache_fox_logf[j], (0, 2, 1)), 32)
            xs = _out_proj(o_s.reshape(db, D_MODEL), gate_s, w_out, xs, tm_s)
            fk_s.append(_tokens_first(kt_s[0])[:, None])
            fv_s.append(_tokens_first(vt_s[0])[:, None])
            fl_s.append(logft_s[0].T[:, None, :])

    if (depth - 1) % 3 != 0:
        xp, xs = _final_norm(xp, fg, tm_p), _final_norm(xs, fg, tm_s)
    y_prompt = xp.reshape(batch, seq, D_MODEL)
    y_sample = xs.reshape(db, 1, D_MODEL)
    return (y_prompt, y_sample, jnp.stack(pool_p), jnp.stack(pool_s), jnp.stack(wk_p), jnp.stack(wv_p),
            jnp.stack(wk_s), jnp.stack(wv_s), jnp.stack(fk_p), jnp.stack(fv_p), jnp.stack(fl_p),
            jnp.stack(fk_s), jnp.stack(fv_s), jnp.stack(fl_s))
```

```python
import functools
import math

import numpy as np
import jax
import jax.numpy as jnp
from jax import lax
from jax.experimental import pallas as pl
from jax.experimental.pallas import tpu as pltpu

D_MODEL = 1024
HEAD_DIM = 64
N_HEADS = 16
N_KV = 4
GROUP = 4
KV_WIDTH = N_KV * HEAD_DIM
POOL_WINDOWS = (2, 4, 8, 16)
POOL_GROUP = 256
POOL_STATE = 15
WINDOW = 128
N_BUCKETS = 32
MAX_DISTANCE = 128
SCALE = HEAD_DIM ** -0.5
EPS = 1e-6
NEG = -1e30
PAGE = 128
LANES = 128

BF16 = jnp.bfloat16
F32 = jnp.float32

VMEM_LIMIT = 56 * 1024 * 1024


def _cparams(sem):
    return pltpu.CompilerParams(dimension_semantics=sem, vmem_limit_bytes=VMEM_LIMIT)


def _rms_bf16(x, g):
    ms = jnp.mean(x * x, axis=-1, keepdims=True)
    return (x * lax.rsqrt(ms + EPS) * g).astype(BF16)


def _silu(x):
    return x * jax.nn.sigmoid(x)


def _dot(a, b):
    return jnp.dot(a, b, preferred_element_type=F32)


def _dot_nt(a, b):
    return lax.dot_general(a, b, (((1,), (1,)), ((), ())), preferred_element_type=F32)


def _bf16_round(x):
    return x.astype(BF16).astype(F32)


def _full(shape):
    n = len(shape)
    return pl.BlockSpec(shape, lambda *_: (0,) * n)


LOG2E = 1.4426950408889634


def _swa_proj_kernel(x_ref, g_ref, w_ref, *out_refs, prompt):
    if prompt:
        qt_ref, kh_ref, kt_ref, vt_ref, vtb_ref, gate_ref = out_refs
    else:
        q_ref, k_ref, v_ref, kt_ref, vt_ref, gate_ref = out_refs
    hb = _rms_bf16(x_ref[...], g_ref[...])
    q = _dot(hb, w_ref[:, :D_MODEL])
    k = _dot(hb, w_ref[:, D_MODEL:D_MODEL + KV_WIDTH])
    v = _dot(hb, w_ref[:, D_MODEL + KV_WIDTH:D_MODEL + 2 * KV_WIDTH])
    vt = v.T
    kt_ref[...] = k.T
    vt_ref[...] = vt
    if prompt:
        tm = q.shape[0]
        qt = (q * (SCALE * LOG2E)).T.astype(BF16)
        for hd in range(N_HEADS):
            h, g = divmod(hd, GROUP)
            for jb in range(tm // WINDOW):
                qt_ref[h, jb, :, g * WINDOW:(g + 1) * WINDOW] = (
                    qt[hd * HEAD_DIM:(hd + 1) * HEAD_DIM, jb * WINDOW:(jb + 1) * WINDOW])
        for h in range(N_KV):
            kh_ref[h] = k[:, h * HEAD_DIM:(h + 1) * HEAD_DIM].astype(BF16)
            vth = vt[h * HEAD_DIM:(h + 1) * HEAD_DIM, :].astype(BF16)
            for jb in range(tm // WINDOW):
                vtb_ref[h, jb] = vth[:, jb * WINDOW:(jb + 1) * WINDOW]
    else:
        q_ref[...] = (q * SCALE).astype(BF16)
        k_ref[...] = k
        v_ref[...] = v
    gate_ref[...] = _dot(hb, w_ref[:, D_MODEL + 2 * KV_WIDTH:])


def _swa_proj(x, g, w, tm, batch, seq, prompt):
    m = x.shape[0]
    ns = seq // tm
    nb = tm // WINDOW
    row = lambda i: (i, 0)
    sds = jax.ShapeDtypeStruct
    tcol = pl.BlockSpec((None, KV_WIDTH, tm), lambda i: (i // ns, 0, i % ns))
    blocks = lambda last: pl.BlockSpec((None, N_KV, nb, HEAD_DIM, last), lambda i: (i // ns, 0, i % ns, 0, 0))
    kv_t_shape = [sds((batch, KV_WIDTH, seq), F32)] * 2
    if prompt:
        out_specs = [blocks(GROUP * WINDOW), pl.BlockSpec((N_KV, tm, HEAD_DIM), lambda i: (0, i, 0)),
                     tcol, tcol, blocks(WINDOW)]
        out_shape = ([sds((batch, N_KV, seq // WINDOW, HEAD_DIM, GROUP * WINDOW), BF16),
                      sds((N_KV, m, HEAD_DIM), BF16)] + kv_t_shape
                     + [sds((batch, N_KV, seq // WINDOW, HEAD_DIM, WINDOW), BF16)])
    else:
        out_specs = [pl.BlockSpec((tm, D_MODEL), row), pl.BlockSpec((tm, KV_WIDTH), row),
                     pl.BlockSpec((tm, KV_WIDTH), row), tcol, tcol]
        out_shape = [sds((m, D_MODEL), BF16), sds((m, KV_WIDTH), F32), sds((m, KV_WIDTH), F32)] + kv_t_shape
    out_specs.append(pl.BlockSpec((tm, D_MODEL), row))
    out_shape.append(sds((m, D_MODEL), F32))
    return pl.pallas_call(
        functools.partial(_swa_proj_kernel, prompt=prompt),
        grid=(m // tm,),
        in_specs=[pl.BlockSpec((tm, D_MODEL), row), _full((1, D_MODEL)), _full(w.shape)],
        out_specs=out_specs,
        out_shape=out_shape,
        compiler_params=_cparams(("parallel",)),
        name="swa_proj",
    )(x, g, w)


def _log_sigmoid(x):
    return -(jnp.maximum(-x, 0.0) + jnp.log1p(jnp.exp(-jnp.abs(x))))


AUG = 128
N_PIECES = 3
V_ROWS = HEAD_DIM + 16
FOX_PAGES_PER_STEP = 64
FOX_TILE = 512
FOX_KEYS = 512
FOX_QUERIES = 256
GROUP_SHIFT = GROUP.bit_length() - 1


def _fox_proj_kernel(x_ref, g_ref, w_ref, wf_ref, fb_ref, *out_refs, prompt):
    if prompt:
        qt_ref, kt_ref, vt_ref, kaug_ref, vtb_ref, logft_ref, gate_ref = out_refs
    else:
        q_ref, k_ref, v_ref, kt_ref, vt_ref, logf_ref, logft_ref, gate_ref = out_refs
    hb = _rms_bf16(x_ref[...], g_ref[...])
    q = _dot(hb, w_ref[:, :D_MODEL])
    k = _dot(hb, w_ref[:, D_MODEL:D_MODEL + KV_WIDTH])
    v = _dot(hb, w_ref[:, D_MODEL + KV_WIDTH:D_MODEL + 2 * KV_WIDTH])
    vt = v.T
    kt_ref[...] = k.T
    vt_ref[...] = vt
    logf = _log_sigmoid(_dot(hb, wf_ref[...]) + fb_ref[...])
    logft_ref[...] = logf.T[:N_HEADS, :]
    if prompt:
        tm = q.shape[0]
        qt = (q * (SCALE * LOG2E)).T.astype(BF16)
        r = lax.broadcasted_iota(jnp.int32, (AUG - HEAD_DIM, tm), 0)
        for hd in range(N_HEADS):
            h, g = divmod(hd, GROUP)
            cols = slice(g * tm, (g + 1) * tm)
            qt_ref[h, 0:HEAD_DIM, cols] = qt[hd * HEAD_DIM:(hd + 1) * HEAD_DIM, :]
            pick = jnp.logical_and(r < N_PIECES * GROUP, (r & (GROUP - 1)) == g)
            qt_ref[h, HEAD_DIM:AUG, cols] = jnp.where(pick, -1.0, 0.0).astype(BF16)
        zeros = jnp.zeros((tm, AUG - HEAD_DIM), F32)
        for h in range(N_KV):
            kaug_ref[h] = jnp.concatenate([k[:, h * HEAD_DIM:(h + 1) * HEAD_DIM], zeros], axis=1).astype(BF16)
            ones_row = lax.broadcasted_iota(jnp.int32, (V_ROWS - HEAD_DIM, tm), 0) == 0
            vth = jnp.concatenate([vt[h * HEAD_DIM:(h + 1) * HEAD_DIM, :],
                                   jnp.where(ones_row, 1.0, 0.0)], axis=0).astype(BF16)
            for kb in range(tm // FOX_KEYS):
                vtb_ref[h, kb] = vth[:, kb * FOX_KEYS:(kb + 1) * FOX_KEYS]
    else:
        q_ref[...] = (q * SCALE).astype(BF16)
        k_ref[...] = k
        v_ref[...] = v
        logf_ref[...] = logf[:, :N_HEADS]
    gate_ref[...] = _dot(hb, w_ref[:, D_MODEL + 2 * KV_WIDTH:])


def _fox_proj(x, g, w, wf, fb, tm, batch, seq, prompt):
    m = x.shape[0]
    ns = seq // tm
    row = lambda i: (i, 0)
    tcol = lambda r: pl.BlockSpec((None, r, tm), lambda i: (i // ns, 0, i % ns))
    sds = jax.ShapeDtypeStruct
    kv_t = [tcol(KV_WIDTH), tcol(KV_WIDTH)]
    kv_t_shape = [sds((batch, KV_WIDTH, seq), F32)] * 2
    if prompt:
        out_specs = ([pl.BlockSpec((None, N_KV, None, AUG, GROUP * tm), lambda i: (i // ns, 0, i % ns, 0, 0))] + kv_t
                     + [pl.BlockSpec((N_KV, tm, AUG), lambda i: (0, i, 0)),
                        pl.BlockSpec((None, N_KV, tm // FOX_KEYS, V_ROWS, FOX_KEYS),
                                     lambda i: (i // ns, 0, i % ns, 0, 0)),
                        tcol(N_HEADS)])
        out_shape = ([sds((batch, N_KV, ns, AUG, GROUP * tm), BF16)] + kv_t_shape
                     + [sds((N_KV, m, AUG), BF16),
                        sds((batch, N_KV, seq // FOX_KEYS, V_ROWS, FOX_KEYS), BF16),
                        sds((batch, N_HEADS, seq), F32)])
    else:
        out_specs = ([pl.BlockSpec((tm, D_MODEL), row), pl.BlockSpec((tm, KV_WIDTH), row),
                      pl.BlockSpec((tm, KV_WIDTH), row)] + kv_t
                     + [pl.BlockSpec((tm, N_HEADS), row), tcol(N_HEADS)])
        out_shape = ([sds((m, D_MODEL), BF16), sds((m, KV_WIDTH), F32), sds((m, KV_WIDTH), F32)]
                     + kv_t_shape + [sds((m, N_HEADS), F32), sds((batch, N_HEADS, seq), F32)])
    out_specs.append(pl.BlockSpec((tm, D_MODEL), row))
    out_shape.append(sds((m, D_MODEL), F32))
    return pl.pallas_call(
        functools.partial(_fox_proj_kernel, prompt=prompt),
        grid=(m // tm,),
        in_specs=[pl.BlockSpec((tm, D_MODEL), row), _full((1, D_MODEL)), _full(w.shape),
                  _full(wf.shape), _full(fb.shape)],
        out_specs=out_specs,
        out_shape=out_shape,
        compiler_params=_cparams(("parallel",)),
        name="fox_proj",
    )(x, g, w, wf, fb)


def _out_proj_kernel(*refs, gated):
    if gated:
        o_ref, gate_ref, w_ref, x_ref, y_ref = refs
        ob = (o_ref[...] * _silu(gate_ref[...])).astype(BF16)
    else:
        o_ref, w_ref, x_ref, y_ref = refs
        ob = o_ref[...]
    y_ref[...] = x_ref[...] + _dot(ob, w_ref[...])


def _out_proj(o, gate, w, x, tm):
    m = x.shape[0]
    row = lambda i: (i, 0)
    tile = pl.BlockSpec((tm, D_MODEL), row)
    gated = gate is not None
    ins = [o, gate, w, x] if gated else [o, w, x]
    in_specs = [tile, tile, _full(w.shape), tile] if gated else [tile, _full(w.shape), tile]
    return pl.pallas_call(
        functools.partial(_out_proj_kernel, gated=gated),
        grid=(m // tm,),
        in_specs=in_specs,
        out_specs=tile,
        out_shape=jax.ShapeDtypeStruct((m, D_MODEL), F32),
        compiler_params=_cparams(("parallel",)),
        name="out_proj",
    )(*ins)


def _final_norm_kernel(x_ref, g_ref, y_ref):
    x = x_ref[...]
    ms = jnp.mean(x * x, axis=-1, keepdims=True)
    y_ref[...] = x * lax.rsqrt(ms + EPS) * g_ref[...]


def _final_norm(x, g, tm):
    m = x.shape[0]
    row = lambda i: (i, 0)
    return pl.pallas_call(
        _final_norm_kernel,
        grid=(m // tm,),
        in_specs=[pl.BlockSpec((tm, D_MODEL), row), _full((1, D_MODEL))],
        out_specs=pl.BlockSpec((tm, D_MODEL), row),
        out_shape=jax.ShapeDtypeStruct((m, D_MODEL), F32),
        compiler_params=_cparams(("parallel",)),
        name="final_norm",
    )(x, g)


HALO = 16


def _pool_layer_tail(x, u, gate, pooled_groups, mix_ref, scale_ref, wout_ref, fg_ref, y_ref):
    pieces = []
    for g in range(len(POOL_WINDOWS)):
        c0 = g * POOL_GROUP
        p = (pooled_groups[g] - u[:, c0:c0 + POOL_GROUP]).astype(BF16)
        pieces.append(_dot(p, mix_ref[g]))
    pm = jnp.concatenate(pieces, axis=1)
    o = (pm * scale_ref[...] * _silu(gate)).astype(BF16)
    y = x + _dot(o, wout_ref[...])
    if fg_ref is not None:
        ms = jnp.mean(y * y, axis=-1, keepdims=True)
        y = y * lax.rsqrt(ms + EPS) * fg_ref[...]
    y_ref[...] = y


def _pool_prompt_kernel(*refs, tp, final):
    if final:
        x_ref, g_ref, win_ref, mix_ref, scale_ref, wout_ref, fg_ref, y_ref, tail_ref, ubuf = refs
    else:
        x_ref, g_ref, win_ref, mix_ref, scale_ref, wout_ref, y_ref, tail_ref, ubuf = refs
        fg_ref = None
    i = pl.program_id(1)

    @pl.when(i == 0)
    def _():
        ubuf[0:HALO, :] = jnp.zeros((HALO, D_MODEL), F32)

    x = x_ref[...]
    hb = _rms_bf16(x, g_ref[...])
    u = _dot(hb, win_ref[:, :D_MODEL])
    gate = _dot(hb, win_ref[:, D_MODEL:])
    ubuf[HALO:HALO + tp, :] = u
    pos = i * tp + lax.broadcasted_iota(jnp.int32, (tp, 1), 0)
    pooled = []
    for g, w in enumerate(POOL_WINDOWS):
        c0 = g * POOL_GROUP
        acc = u[:, c0:c0 + POOL_GROUP]
        for k in range(1, w):
            acc = acc + ubuf[HALO - k:HALO - k + tp, c0:c0 + POOL_GROUP]
        cnt = jnp.minimum(pos + 1, w).astype(F32)
        pooled.append(acc / cnt)
    _pool_layer_tail(x, u, gate, pooled, mix_ref, scale_ref, wout_ref, fg_ref, y_ref)
    ubuf[0:HALO, :] = u[tp - HALO:tp, :]

    @pl.when(i == pl.num_programs(1) - 1)
    def _():
        tail_ref[...] = u[tp - HALO:tp, :]


def _pool_prompt(x, g, w_in, mix, scale, w_out, fg, batch, seq, tp):
    ns = seq // tp
    final = fg is not None
    tile = pl.BlockSpec((tp, D_MODEL), lambda b, i: (b * ns + i, 0))
    ins = [x, g, w_in, mix, scale, w_out] + ([fg] if final else [])
    in_specs = [tile] + [_full(a.shape) for a in ins[1:]]
    return pl.pallas_call(
        functools.partial(_pool_prompt_kernel, tp=tp, final=final),
        grid=(batch, ns),
        in_specs=in_specs,
        out_specs=[tile, pl.BlockSpec((None, HALO, D_MODEL), lambda b, i: (b, 0, 0))],
        out_shape=[jax.ShapeDtypeStruct((batch * seq, D_MODEL), F32),
                   jax.ShapeDtypeStruct((batch, HALO, D_MODEL), F32)],
        scratch_shapes=[pltpu.VMEM((HALO + tp, D_MODEL), F32)],
        compiler_params=_cparams(("parallel", "arbitrary")),
        name="pool_prompt",
    )(*ins)


def _pool_sample_kernel(*refs, final):
    if final:
        x_ref, st_ref, g_ref, win_ref, mix_ref, scale_ref, wout_ref, fg_ref, y_ref, u_ref = refs
    else:
        x_ref, st_ref, g_ref, win_ref, mix_ref, scale_ref, wout_ref, y_ref, u_ref = refs
        fg_ref = None
    x = x_ref[...]
    hb = _rms_bf16(x, g_ref[...])
    u = _dot(hb, win_ref[:, :D_MODEL])
    gate = _dot(hb, win_ref[:, D_MODEL:])
    u_ref[...] = u
    pooled = []
    for g, w in enumerate(POOL_WINDOWS):
        c0 = g * POOL_GROUP
        acc = u[:, c0:c0 + POOL_GROUP]
        for k in range(1, w):
            acc = acc + st_ref[POOL_STATE - k, :, c0:c0 + POOL_GROUP]
        pooled.append(acc / float(w))
    _pool_layer_tail(x, u, gate, pooled, mix_ref, scale_ref, wout_ref, fg_ref, y_ref)


def _pool_sample(x, state_t, g, w_in, mix, scale, w_out, fg):
    m = x.shape[0]
    final = fg is not None
    ins = [x, state_t, g, w_in, mix, scale, w_out] + ([fg] if final else [])
    return pl.pallas_call(
        functools.partial(_pool_sample_kernel, final=final),
        grid=(1,),
        in_specs=[_full(a.shape) for a in ins],
        out_specs=[_full((m, D_MODEL)), _full((m, D_MODEL))],
        out_shape=[jax.ShapeDtypeStruct((m, D_MODEL), F32)] * 2,
        compiler_params=_cparams(("arbitrary",)),
        name="pool_sample",
    )(*ins)


def _t5_bucket_np(dist):
    n = np.maximum(dist, 0)
    max_exact = N_BUCKETS // 2
    nf = np.maximum(n, 1).astype(np.float32)
    large = max_exact + (np.log(nf / max_exact) / math.log(MAX_DISTANCE / max_exact)
                         * (N_BUCKETS - max_exact)).astype(np.int32)
    large = np.minimum(large, N_BUCKETS - 1)
    return np.where(n < max_exact, n, large)


def _swa_prompt_kernel(sink_ref, rb_ref, qt_ref, kp_ref, kc_ref, vp_ref, vc_ref, bucket_ref, gate_ref,
                       wout_ref, x_ref, y_ref, bias_ref, o_buf, *, nb):
    i = pl.program_id(1)
    cols = GROUP * WINDOW

    @pl.when(i == 0)
    def _():
        bucket = bucket_ref[...]
        hits = [bucket == bk for bk in range(N_BUCKETS)]
        for hd in range(N_HEADS):
            h, g = divmod(hd, GROUP)
            b = jnp.full((2 * WINDOW, WINDOW), NEG, F32)
            for bk in range(N_BUCKETS):
                b = jnp.where(hits[bk], rb_ref[bk, hd] * LOG2E, b)
            bias_ref[h, :, g * WINDOW:(g + 1) * WINDOW] = b

    key_row = lax.broadcasted_iota(jnp.int32, (2 * WINDOW, cols), 0)
    no_prev = jnp.logical_and(i == 0, key_row < WINDOW)
    sinks = [jnp.concatenate([jnp.full((1, WINDOW), sink_ref[h * GROUP + g] * LOG2E, F32) for g in range(GROUP)],
                             axis=1) for h in range(N_KV)]

    def logits(jb, h):
        if jb == 0:
            kband = jnp.concatenate([kp_ref[h], kc_ref[h, 0:WINDOW, :]], axis=0)
        else:
            kband = kc_ref[h, (jb - 1) * WINDOW:(jb + 1) * WINDOW, :]
        s = _dot(kband, qt_ref[h, jb]) + bias_ref[h]
        return jnp.where(no_prev, NEG, s) if jb == 0 else s

    def attend(jb, h, s):
        vprev = vp_ref[h] if jb == 0 else vc_ref[h, jb - 1]
        vband = jnp.concatenate([vprev, vc_ref[h, jb]], axis=1)
        m = jnp.maximum(jnp.max(s, axis=0, keepdims=True), sinks[h])
        p = jnp.exp2(s - m)
        denom = jnp.sum(p, axis=0, keepdims=True) + jnp.exp2(sinks[h] - m)
        ot = _dot(vband, p.astype(BF16)) * (1.0 / denom)
        o_heads = jnp.concatenate([ot[:, g * WINDOW:(g + 1) * WINDOW] for g in range(GROUP)], axis=0)
        o_buf[jb * WINDOW:(jb + 1) * WINDOW, h * GROUP * HEAD_DIM:(h + 1) * GROUP * HEAD_DIM] = o_heads.T

    units = [(jb, h) for jb in range(nb) for h in range(N_KV)]
    s = logits(*units[0])
    for u, unit in enumerate(units):
        s_next = logits(*units[u + 1]) if u + 1 < len(units) else None
        attend(*unit, s)
        s = s_next
    ob = (o_buf[...] * _silu(gate_ref[...])).astype(BF16)
    y_ref[...] = x_ref[...] + _dot(ob, wout_ref[...])


def _swa_prompt(qt, kh, vtb, gate, w_out, x, rel_bias, sinks, batch, seq, tm):
    ns = seq // tm
    nb = tm // WINDOW
    dist = np.arange(WINDOW)[None, :] + WINDOW - np.arange(2 * WINDOW)[:, None]
    bucket = np.where((dist >= 0) & (dist < WINDOW), _t5_bucket_np(dist), -1).astype(np.int32)
    tile = pl.BlockSpec((tm, D_MODEL), lambda b, i: (b * ns + i, 0))
    smem = pl.BlockSpec(memory_space=pltpu.SMEM)
    prev_blk = lambda i: jnp.maximum(i * nb - 1, 0)
    return pl.pallas_call(
        functools.partial(_swa_prompt_kernel, nb=nb),
        grid=(batch, ns),
        in_specs=[smem, smem,
                  pl.BlockSpec((None, N_KV, nb, HEAD_DIM, GROUP * WINDOW), lambda b, i: (b, 0, i, 0, 0)),
                  pl.BlockSpec((N_KV, WINDOW, HEAD_DIM), lambda b, i: (0, b * ns * nb + prev_blk(i), 0)),
                  pl.BlockSpec((N_KV, tm, HEAD_DIM), lambda b, i: (0, b * ns + i, 0)),
                  pl.BlockSpec((None, N_KV, None, HEAD_DIM, WINDOW), lambda b, i: (b, 0, prev_blk(i), 0, 0)),
                  pl.BlockSpec((None, N_KV, nb, HEAD_DIM, WINDOW), lambda b, i: (b, 0, i, 0, 0)),
                  _full(bucket.shape), tile, _full(w_out.shape), tile],
        out_specs=tile,
        out_shape=jax.ShapeDtypeStruct((batch * seq, D_MODEL), F32),
        scratch_shapes=[pltpu.VMEM((N_KV, 2 * WINDOW, GROUP * WINDOW), F32),
                        pltpu.VMEM((tm, D_MODEL), F32)],
        compiler_params=_cparams(("parallel", "arbitrary")),
        name="swa_prompt",
    )(sinks, rel_bias, qt, kh, kh, vtb, vtb, jnp.asarray(bucket), gate, w_out, x)


def _head_diag(o_full):
    out = jnp.zeros((N_HEADS, HEAD_DIM), F32)
    row_kv = lax.broadcasted_iota(jnp.int32, (N_HEADS, HEAD_DIM), 0) // GROUP
    for h in range(N_KV):
        out = out + jnp.where(row_kv == h, o_full[:, h * HEAD_DIM:(h + 1) * HEAD_DIM], 0.0)
    return out


def _swa_sample_kernel(qm_ref, kc_ref, vc_ref, kn_ref, vn_ref, knt_ref, vnt_ref, bias_ref, bias0_ref,
                       sink_ref, o_ref, wk_ref, wv_ref, *, bt):
    i = pl.program_id(0)
    lane = lax.broadcasted_iota(jnp.int32, (KV_WIDTH, WINDOW), 1)
    sink = sink_ref[...]
    for e in range(bt):
        b = i * bt + e
        kc = kc_ref[e]
        vc = vc_ref[e]
        qm = qm_ref[e]
        s = _dot(qm, kc.astype(BF16)) + bias_ref[...]
        s_n = jnp.sum(qm.astype(F32) * _bf16_round(kn_ref[e]), axis=1, keepdims=True) + bias0_ref[...]
        m = jnp.maximum(jnp.maximum(jnp.max(s, axis=1, keepdims=True), s_n), sink)
        p = jnp.exp(s - m)
        p_n = jnp.exp(s_n - m)
        denom = jnp.sum(p, axis=1, keepdims=True) + p_n + jnp.exp(sink - m)
        o_full = _dot_nt((p / denom).astype(BF16), vc.astype(BF16))
        o_full = o_full + _bf16_round(p_n / denom) * _bf16_round(vn_ref[e])
        o_ref[e] = _head_diag(o_full)
        kcol = jnp.sum(jnp.where(lane == b, knt_ref[...], 0.0), axis=1, keepdims=True)
        vcol = jnp.sum(jnp.where(lane == b, vnt_ref[...], 0.0), axis=1, keepdims=True)
        wk_ref[e] = jnp.where(lane == WINDOW - 1, kcol, pltpu.roll(kc, WINDOW - 1, 1))
        wv_ref[e] = jnp.where(lane == WINDOW - 1, vcol, pltpu.roll(vc, WINDOW - 1, 1))


def _swa_sample(qm, kc, vc, kn, vn, knt, vnt, bias_keys, bias0, sinks, bt):
    m = qm.shape[0]
    blk3 = lambda s1, s2: pl.BlockSpec((bt, s1, s2), lambda i: (i, 0, 0))
    cache = jax.ShapeDtypeStruct((m, KV_WIDTH, WINDOW), F32)
    return pl.pallas_call(
        functools.partial(_swa_sample_kernel, bt=bt),
        grid=(m // bt,),
        in_specs=[blk3(N_HEADS, KV_WIDTH), blk3(KV_WIDTH, WINDOW), blk3(KV_WIDTH, WINDOW),
                  blk3(1, KV_WIDTH), blk3(1, KV_WIDTH), _full(knt.shape), _full(vnt.shape),
                  _full(bias_keys.shape), _full(bias0.shape), _full(sinks.shape)],
        out_specs=[blk3(N_HEADS, HEAD_DIM), blk3(KV_WIDTH, WINDOW), blk3(KV_WIDTH, WINDOW)],
        out_shape=[jax.ShapeDtypeStruct((m, N_HEADS, HEAD_DIM), F32), cache, cache],
        compiler_params=_cparams(("parallel",)),
        name="swa_sample",
    )(qm, kc, vc, kn, vn, knt, vnt, bias_keys, bias0, sinks)


def _split3(x):
    hi = x.astype(BF16)
    r = x - hi.astype(F32)
    mid = r.astype(BF16)
    lo = (r - mid.astype(F32)).astype(BF16)
    return hi, mid, lo


def _cumsum_lanes_blocks(x, blk):
    n = x.shape[0]
    hi, mid, lo = _split3(x)
    r_i = lax.broadcasted_iota(jnp.int32, (blk, blk), 0)
    c_i = lax.broadcasted_iota(jnp.int32, (blk, blk), 1)
    upper = jnp.where(r_i <= c_i, 1.0, 0.0).astype(BF16)
    r = _dot(jnp.concatenate([hi, mid, lo], axis=0), upper)
    return r[0:n] + r[n:2 * n] + r[2 * n:]


def _fox_decay_kernel(x_ref, kin_ref, kout_ref, carry, *, blk):
    i = pl.program_id(1)

    @pl.when(i == 0)
    def _():
        carry[...] = jnp.zeros_like(carry)

    c = _cumsum_lanes_blocks(x_ref[...], blk) + carry[:, 0:1]
    carry[...] = jnp.broadcast_to(c[:, blk - 1:blk], carry.shape)
    hi, mid, lo = _split3(c * LOG2E)
    pieces = jnp.concatenate([hi, mid, lo], axis=0).astype(F32)
    pad = jnp.zeros((LANES - N_PIECES * N_HEADS, blk), F32)
    pieces_t = jnp.concatenate([pieces, pad], axis=0).T.astype(BF16)
    src = lax.broadcasted_iota(jnp.int32, (LANES, AUG), 0)
    dst = lax.broadcasted_iota(jnp.int32, (LANES, AUG), 1) - HEAD_DIM
    in_aug = jnp.logical_and(dst >= 0, dst < N_PIECES * GROUP)
    for h in range(N_KV):
        want = (dst >> GROUP_SHIFT) * N_HEADS + h * GROUP + (dst & (GROUP - 1))
        place = jnp.where(jnp.logical_and(in_aug, src == want), 1.0, 0.0).astype(BF16)
        kout_ref[h] = kin_ref[h] + _dot(pieces_t, place).astype(BF16)


def _fox_decay(logft, kaug, blk):
    b, h, s = logft.shape
    ns = s // blk
    kspec = pl.BlockSpec((N_KV, blk, AUG), lambda bi, i: (0, bi * ns + i, 0))
    return pl.pallas_call(
        functools.partial(_fox_decay_kernel, blk=blk),
        grid=(b, ns),
        in_specs=[pl.BlockSpec((None, h, blk), lambda bi, i: (bi, 0, i)), kspec],
        out_specs=kspec,
        out_shape=jax.ShapeDtypeStruct(kaug.shape, kaug.dtype),
        scratch_shapes=[pltpu.VMEM((h, LANES), F32)],
        input_output_aliases={1: 0},
        compiler_params=_cparams(("parallel", "arbitrary")),
        name="fox_decay",
    )(logft, kaug)


def _fox_prompt_kernel(qt_ref, k_ref, vt_ref, o_ref, *scratch, tq, tk):
    assert tq == tk
    qi = pl.program_id(2)
    m_ref, acc_ref, sa_ref, sb_ref, ca_ref, cb_ref = scratch
    cols = GROUP * tq
    m_ref[...] = jnp.full(m_ref.shape, NEG, F32)
    acc_ref[...] = jnp.zeros(acc_ref.shape, F32)

    def logits(kb, s_ref, c_ref):
        k0 = pl.multiple_of(kb * tk, tk)
        s = _dot(k_ref[pl.ds(k0, tk), :], qt_ref[...])
        s_ref[...] = s
        c_ref[...] = jnp.max(s, axis=0, keepdims=True)

    def accumulate(kb, s, cmax):
        m_old = m_ref[...]
        m_new = jnp.maximum(m_old, cmax)
        alpha = jnp.exp2(m_old - m_new)
        p = jnp.exp2(s - m_new)
        acc_ref[...] = alpha * acc_ref[...] + _dot(vt_ref[kb], p.astype(BF16))
        m_ref[...] = m_new

    def diagonal(s_ref):
        kpos = lax.broadcasted_iota(jnp.int32, (tk, cols), 0)
        qpos = lax.broadcasted_iota(jnp.int32, (tk, cols), 1) & (tq - 1)
        s = jnp.where(kpos <= qpos, s_ref[...], NEG)
        accumulate(qi, s, jnp.max(s, axis=0, keepdims=True))

    logits(0, sa_ref, ca_ref)

    def two_blocks(j, carry):
        kb = 2 * j
        logits(kb + 1, sb_ref, cb_ref)
        accumulate(kb, sa_ref[...], ca_ref[...])
        logits(kb + 2, sa_ref, ca_ref)
        accumulate(kb + 1, sb_ref[...], cb_ref[...])
        return carry

    lax.fori_loop(0, qi // 2, two_blocks, 0)

    @pl.when(qi % 2 == 1)
    def _():
        logits(qi, sb_ref, cb_ref)
        accumulate(qi - 1, sa_ref[...], ca_ref[...])
        diagonal(sb_ref)

    @pl.when(qi % 2 == 0)
    def _():
        diagonal(sa_ref)

    o = acc_ref[0:HEAD_DIM, :] / acc_ref[HEAD_DIM:HEAD_DIM + 1, :]
    o_ref[...] = jnp.concatenate([o[:, g * tq:(g + 1) * tq] for g in range(GROUP)], axis=0).T


def _fox_prompt(qt, kaug, vtb, batch, seq, tq):
    nq = seq // tq
    nk, tk = vtb.shape[2], vtb.shape[4]
    cols = GROUP * tq
    return pl.pallas_call(
        functools.partial(_fox_prompt_kernel, tq=tq, tk=tk),
        grid=(batch, N_KV, nq),
        in_specs=[pl.BlockSpec((None, None, None, AUG, cols), lambda b, h, qi: (b, h, qi, 0, 0)),
                  pl.BlockSpec((None, seq, AUG), lambda b, h, qi: (h, b, 0)),
                  pl.BlockSpec((None, None, nk, V_ROWS, tk), lambda b, h, qi: (b, h, 0, 0, 0))],
        out_specs=pl.BlockSpec((tq, GROUP * HEAD_DIM), lambda b, h, qi: (b * nq + qi, h)),
        out_shape=jax.ShapeDtypeStruct((batch * seq, D_MODEL), F32),
        scratch_shapes=[pltpu.VMEM((1, cols), F32),
                        pltpu.VMEM((V_ROWS, cols), F32),
                        pltpu.VMEM((tk, cols), F32), pltpu.VMEM((tk, cols), F32),
                        pltpu.VMEM((1, cols), F32), pltpu.VMEM((1, cols), F32)],
        compiler_params=_cparams(("parallel", "parallel", "arbitrary")),
        name="fox_prompt",
    )(qt, kaug, vtb)


def _fox_sample_kernel(pt_ref, qm_ref, kn_ref, vn_ref, fn_ref, ck_hbm, cv_hbm, cf_hbm, o_ref,
                       kbuf, vbuf, fbuf, sems, m_ref, l_ref, acc_ref, carry_ref, *, cp, nchunk):
    b = pl.program_id(0)
    c = pl.program_id(1)
    nb = pl.num_programs(0)
    step = b * nchunk + c
    slot = step % 2

    def copies(bb, cc, sl, p):
        page = pt_ref[bb, cc * cp + p]
        return (pltpu.make_async_copy(ck_hbm.at[page], kbuf.at[sl, p], sems.at[sl, 0]),
                pltpu.make_async_copy(cv_hbm.at[page], vbuf.at[sl, p], sems.at[sl, 1]),
                pltpu.make_async_copy(cf_hbm.at[page], fbuf.at[sl, p], sems.at[sl, 2]))

    def issue(bb, cc, sl):
        def body(p, carry):
            for cpy in copies(bb, cc, sl, p):
                cpy.start()
            return carry
        lax.fori_loop(0, cp, body, 0)

    @pl.when(step == 0)
    def _():
        issue(b, c, slot)

    @pl.when(step + 1 < nb * nchunk)
    def _():
        nxt = step + 1
        issue(nxt // nchunk, nxt % nchunk, 1 - slot)

    def wait_body(p, carry):
        for cpy in copies(b, c, slot, p):
            cpy.wait()
        return carry
    lax.fori_loop(0, cp, wait_body, 0)

    @pl.when(c == 0)
    def _():
        m_ref[...] = jnp.full(m_ref.shape, NEG, F32)
        l_ref[...] = jnp.zeros(l_ref.shape, F32)
        acc_ref[...] = jnp.zeros(acc_ref.shape, F32)
        carry_ref[...] = jnp.zeros(carry_ref.shape, F32)

    qm = qm_ref[...]
    cin = _cumsum_lanes_blocks(fbuf[slot].reshape(cp * N_HEADS, PAGE), PAGE).reshape(cp, N_HEADS, PAGE)
    carry = carry_ref[...]
    cs = []
    for p in range(cp):
        cs.append(cin[p] + carry)
        carry = carry + cin[p][:, PAGE - 1:PAGE]
    carry_ref[...] = carry
    kcat = jnp.concatenate([kbuf[slot, p] for p in range(cp)], axis=1).astype(BF16)
    t = _dot(qm, kcat) - jnp.concatenate(cs, axis=1)
    m_old = m_ref[...]
    m_new = jnp.maximum(m_old, jnp.max(t, axis=1, keepdims=True))
    alpha = jnp.exp(m_old - m_new)
    p = jnp.exp(t - m_new)
    l_new = alpha * l_ref[...] + jnp.sum(p, axis=1, keepdims=True)
    vcat = jnp.concatenate([vbuf[slot, p_] for p_ in range(cp)], axis=1).astype(BF16)
    acc_new = alpha * acc_ref[...] + _dot_nt(p.astype(BF16), vcat)
    m_ref[...] = m_new
    l_ref[...] = l_new
    acc_ref[...] = acc_new

    @pl.when(c == nchunk - 1)
    def _():
        s_n = jnp.sum(qm.astype(F32) * _bf16_round(kn_ref[...]), axis=1, keepdims=True)
        t_n = s_n - (carry + fn_ref[...])
        m_f = jnp.maximum(m_new, t_n)
        a = jnp.exp(m_new - m_f)
        p_n = jnp.exp(t_n - m_f)
        l_f = a * l_new + p_n
        acc = a * acc_new + _bf16_round(p_n) * _bf16_round(vn_ref[...])
        o_ref[...] = _head_diag(acc / l_f)


def _fox_sample(page_table, qm, kn, vn, fn, ck, cv, cf, cp):
    m, n_pages = page_table.shape
    nchunk = n_pages // cp
    per_b = lambda s1, s2: pl.BlockSpec((None, s1, s2), lambda b, c, pt: (b, 0, 0))
    any_spec = pl.BlockSpec(memory_space=pl.ANY)
    return pl.pallas_call(
        functools.partial(_fox_sample_kernel, cp=cp, nchunk=nchunk),
        grid_spec=pltpu.PrefetchScalarGridSpec(
            num_scalar_prefetch=1,
            grid=(m, nchunk),
            in_specs=[per_b(N_HEADS, KV_WIDTH), per_b(1, KV_WIDTH), per_b(1, KV_WIDTH), per_b(N_HEADS, 1),
                      any_spec, any_spec, any_spec],
            out_specs=per_b(N_HEADS, HEAD_DIM),
            scratch_shapes=[pltpu.VMEM((2, cp, KV_WIDTH, PAGE), F32),
                            pltpu.VMEM((2, cp, KV_WIDTH, PAGE), F32),
                            pltpu.VMEM((2, cp, N_HEADS, PAGE), F32),
                            pltpu.SemaphoreType.DMA((2, 3)),
                            pltpu.VMEM((N_HEADS, 1), F32), pltpu.VMEM((N_HEADS, 1), F32),
                            pltpu.VMEM((N_HEADS, KV_WIDTH), F32), pltpu.VMEM((N_HEADS, 1), F32)],
        ),
        out_shape=jax.ShapeDtypeStruct((m, N_HEADS, HEAD_DIM), F32),
        compiler_params=_cparams(("arbitrary", "arbitrary")),
        name="fox_sample",
    )(page_table, qm, kn, vn, fn, ck, cv, cf)


def _head_rows_q(q_rows):
    m = q_rows.shape[0]
    q4 = q_rows.reshape(m, N_KV, GROUP, HEAD_DIM)
    eye = jnp.eye(N_KV, dtype=q_rows.dtype)
    return (q4[:, :, :, None, :] * eye[None, :, None, :, None]).reshape(m, N_HEADS, KV_WIDTH)


def _tokens_last(x):
    lead = x.shape[:-3]
    n = len(lead)
    xt = jnp.transpose(x, tuple(range(n)) + (n + 1, n + 2, n))
    return xt.reshape(lead + (KV_WIDTH, x.shape[-3]))


def _tokens_first(xt):
    lead = xt.shape[:-2]
    n = len(lead)
    x4 = xt.reshape(lead + (N_KV, HEAD_DIM, xt.shape[-1]))
    return jnp.transpose(x4, tuple(range(n)) + (n + 2, n, n + 1))


def kernel(x_prompt, x_sample, state_pool, cache_win_k, cache_win_v, cache_fox_k, cache_fox_v,
           cache_fox_logf, page_table, norm_g, final_norm_g, rel_bias, pool_w_in, pool_mix,
           pool_scale, pool_w_out, swa_w_in, swa_sinks, swa_w_out, fox_w_in, fox_f_bias, fox_w_out):
    batch, seq, _ = x_prompt.shape
    db = x_sample.shape[0]
    depth = norm_g.shape[0]
    mp = batch * seq
    tm_p, tm_s = 512, db

    xp = x_prompt.reshape(mp, D_MODEL)
    xs = x_sample.reshape(db, D_MODEL)

    rb = rel_bias.astype(F32)
    dist_keys = WINDOW - np.arange(WINDOW)
    bias_keys = jnp.where((dist_keys < WINDOW)[None, :], rb[_t5_bucket_np(dist_keys)].T, NEG)
    bias0 = rb[0].reshape(N_HEADS, 1)
    fg = final_norm_g.reshape(1, D_MODEL)

    pool_p, pool_s = [], []
    wk_p, wv_p, wk_s, wv_s = [], [], [], []
    fk_p, fv_p, fl_p, fk_s, fv_s, fl_s = [], [], [], [], [], []
    for i in range(depth):
        kind, j = i % 3, i // 3
        g = norm_g[i].reshape(1, D_MODEL)
        if kind == 0:
            w_in = pool_w_in[j].astype(BF16)
            mix = pool_mix[j].astype(BF16)
            scale = pool_scale[j].reshape(1, D_MODEL)
            w_out = pool_w_out[j].astype(BF16)
            fg_l = fg if i == depth - 1 else None
            xp, u_tail = _pool_prompt(xp, g, w_in, mix, scale, w_out, fg_l, batch, seq, tm_p)
            pool_p.append(u_tail[:, HALO - POOL_STATE:])
            xs, u_s = _pool_sample(xs, jnp.transpose(state_pool[j], (1, 0, 2)), g, w_in, mix, scale, w_out, fg_l)
            pool_s.append(jnp.concatenate([state_pool[j][:, 1:], u_s[:, None, :]], axis=1))
        elif kind == 1:
            w_in = swa_w_in[j].astype(BF16)
            w_out = swa_w_out[j].astype(BF16)
            sinks = swa_sinks[j].astype(F32)
            qt_p, kh_p, kt_p, vt_p, vtb_p, gate_p = _swa_proj(xp, g, w_in, tm_p, batch, seq, True)
            xp = _swa_prompt(qt_p, kh_p, vtb_p, gate_p, w_out, xp, rb, sinks, batch, seq, tm_p)
            wk_p.append(_tokens_first(kt_p[:, :, seq - WINDOW:]))
            wv_p.append(_tokens_first(vt_p[:, :, seq - WINDOW:]))
            q_s, k_s, v_s, kt_s, vt_s, gate_s = _swa_proj(xs, g, w_in, tm_s, 1, db, False)
            o_s, wk, wv = _swa_sample(_head_rows_q(q_s), _tokens_last(cache_win_k[j]),
                                      _tokens_last(cache_win_v[j]), k_s[:, None, :], v_s[:, None, :],
                                      kt_s[0], vt_s[0], bias_keys, bias0, sinks.reshape(N_HEADS, 1), 8)
            xs = _out_proj(o_s.reshape(db, D_MODEL), gate_s, w_out, xs, tm_s)
            wk_s.append(_tokens_first(wk))
            wv_s.append(_tokens_first(wv))
        else:
            w_full = fox_w_in[j]
            nqkv = D_MODEL + 2 * KV_WIDTH
            w_in = jnp.concatenate([w_full[:, :nqkv], w_full[:, nqkv + N_HEADS:]], axis=1).astype(BF16)
            wf = jnp.pad(w_full[:, nqkv:nqkv + N_HEADS], ((0, 0), (0, LANES - N_HEADS))).astype(BF16)
            fb = jnp.pad(fox_f_bias[j].astype(F32), (0, LANES - N_HEADS)).reshape(1, LANES)
            w_out = fox_w_out[j].astype(BF16)
            qt, kt_p, vt_p, kaug, vtb, logft_p, gate_p = _fox_proj(xp, g, w_in, wf, fb, FOX_TILE, batch, seq, True)
            kaug = _fox_decay(logft_p, kaug, FOX_TILE)
            o_p = _fox_prompt(qt, kaug, vtb, batch, seq, FOX_TILE)
            xp = _out_proj(o_p, gate_p, w_out, xp, tm_p)
            fk_p.append(_tokens_first(kt_p))
            fv_p.append(_tokens_first(vt_p))
            fl_p.append(jnp.transpose(logft_p, (0, 2, 1)))
            q_s, k_s, v_s, kt_s, vt_s, logf_s, logft_s, gate_s = _fox_proj(
                xs, g, w_in, wf, fb, tm_s, 1, db, False)
            o_s = _fox_sample(page_table, _head_rows_q(q_s), k_s[:, None, :], v_s[:, None, :],
                              logf_s[:, :, None], _tokens_last(cache_fox_k[j]),
                              _tokens_last(cache_fox_v[j]), jnp.transpose(cache_fox_logf[j], (0, 2, 1)),
                              FOX_PAGES_PER_STEP)
            xs = _out_proj(o_s.reshape(db, D_MODEL), gate_s, w_out, xs, tm_s)
            fk_s.append(_tokens_first(kt_s[0])[:, None])
            fv_s.append(_tokens_first(vt_s[0])[:, None])
            fl_s.append(logft_s[0].T[:, None, :])

    if (depth - 1) % 3 != 0:
        xp, xs = _final_norm(xp, fg, tm_p), _final_norm(xs, fg, tm_s)
    y_prompt = xp.reshape(batch, seq, D_MODEL)
    y_sample = xs.reshape(db, 1, D_MODEL)
    return (y_prompt, y_sample, jnp.stack(pool_p), jnp.stack(pool_s), jnp.stack(wk_p), jnp.stack(wv_p),
            jnp.stack(wk_s), jnp.stack(wv_s), jnp.stack(fk_p), jnp.stack(fv_p), jnp.stack(fl_p),
            jnp.stack(fk_s), jnp.stack(fv_s), jnp.stack(fl_s))
```

```python
import functools
import math

import numpy as np
import jax
import jax.numpy as jnp
from jax import lax
from jax.experimental import pallas as pl
from jax.experimental.pallas import tpu as pltpu

D_MODEL = 1024
HEAD_DIM = 64
N_HEADS = 16
N_KV = 4
GROUP = 4
KV_WIDTH = N_KV * HEAD_DIM
POOL_WINDOWS = (2, 4, 8, 16)
POOL_GROUP = 256
POOL_STATE = 15
WINDOW = 128
N_BUCKETS = 32
MAX_DISTANCE = 128
SCALE = HEAD_DIM ** -0.5
EPS = 1e-6
NEG = -1e30
PAGE = 128
LANES = 128

BF16 = jnp.bfloat16
F32 = jnp.float32

VMEM_LIMIT = 56 * 1024 * 1024


def _cparams(sem):
    return pltpu.CompilerParams(dimension_semantics=sem, vmem_limit_bytes=VMEM_LIMIT)


def _rms_bf16(x, g):
    ms = jnp.mean(x * x, axis=-1, keepdims=True)
    return (x * lax.rsqrt(ms + EPS) * g).astype(BF16)


def _silu(x):
    return x * jax.nn.sigmoid(x)


def _dot(a, b):
    return jnp.dot(a, b, preferred_element_type=F32)


def _dot_nt(a, b):
    return lax.dot_general(a, b, (((1,), (1,)), ((), ())), preferred_element_type=F32)


def _bf16_round(x):
    return x.astype(BF16).astype(F32)


def _full(shape):
    n = len(shape)
    return pl.BlockSpec(shape, lambda *_: (0,) * n)


LOG2E = 1.4426950408889634


def _swa_proj_kernel(x_ref, g_ref, w_ref, *out_refs, prompt):
    if prompt:
        qt_ref, kh_ref, kt_ref, vt_ref, vtb_ref, gate_ref = out_refs
    else:
        q_ref, k_ref, v_ref, kt_ref, vt_ref, gate_ref = out_refs
    hb = _rms_bf16(x_ref[...], g_ref[...])
    q = _dot(hb, w_ref[:, :D_MODEL])
    k = _dot(hb, w_ref[:, D_MODEL:D_MODEL + KV_WIDTH])
    v = _dot(hb, w_ref[:, D_MODEL + KV_WIDTH:D_MODEL + 2 * KV_WIDTH])
    vt = v.T
    kt_ref[...] = k.T
    vt_ref[...] = vt
    if prompt:
        tm = q.shape[0]
        qt = (q * (SCALE * LOG2E)).T.astype(BF16)
        for hd in range(N_HEADS):
            h, g = divmod(hd, GROUP)
            for jb in range(tm // WINDOW):
                qt_ref[h, jb, :, g * WINDOW:(g + 1) * WINDOW] = (
                    qt[hd * HEAD_DIM:(hd + 1) * HEAD_DIM, jb * WINDOW:(jb + 1) * WINDOW])
        for h in range(N_KV):
            kh_ref[h] = k[:, h * HEAD_DIM:(h + 1) * HEAD_DIM].astype(BF16)
            vth = vt[h * HEAD_DIM:(h + 1) * HEAD_DIM, :].astype(BF16)
            for jb in range(tm // WINDOW):
                vtb_ref[h, jb] = vth[:, jb * WINDOW:(jb + 1) * WINDOW]
    else:
        q_ref[...] = (q * SCALE).astype(BF16)
        k_ref[...] = k
        v_ref[...] = v
    gate_ref[...] = _dot(hb, w_ref[:, D_MODEL + 2 * KV_WIDTH:])


def _swa_proj(x, g, w, tm, batch, seq, prompt):
    m = x.shape[0]
    ns = seq // tm
    nb = tm // WINDOW
    row = lambda i: (i, 0)
    sds = jax.ShapeDtypeStruct
    tcol = pl.BlockSpec((None, KV_WIDTH, tm), lambda i: (i // ns, 0, i % ns))
    blocks = lambda last: pl.BlockSpec((None, N_KV, nb, HEAD_DIM, last), lambda i: (i // ns, 0, i % ns, 0, 0))
    kv_t_shape = [sds((batch, KV_WIDTH, seq), F32)] * 2
    if prompt:
        out_specs = [blocks(GROUP * WINDOW), pl.BlockSpec((N_KV, tm, HEAD_DIM), lambda i: (0, i, 0)),
                     tcol, tcol, blocks(WINDOW)]
        out_shape = ([sds((batch, N_KV, seq // WINDOW, HEAD_DIM, GROUP * WINDOW), BF16),
                      sds((N_KV, m, HEAD_DIM), BF16)] + kv_t_shape
                     + [sds((batch, N_KV, seq // WINDOW, HEAD_DIM, WINDOW), BF16)])
    else:
        out_specs = [pl.BlockSpec((tm, D_MODEL), row), pl.BlockSpec((tm, KV_WIDTH), row),
                     pl.BlockSpec((tm, KV_WIDTH), row), tcol, tcol]
        out_shape = [sds((m, D_MODEL), BF16), sds((m, KV_WIDTH), F32), sds((m, KV_WIDTH), F32)] + kv_t_shape
    out_specs.append(pl.BlockSpec((tm, D_MODEL), row))
    out_shape.append(sds((m, D_MODEL), F32))
    return pl.pallas_call(
        functools.partial(_swa_proj_kernel, prompt=prompt),
        grid=(m // tm,),
        in_specs=[pl.BlockSpec((tm, D_MODEL), row), _full((1, D_MODEL)), _full(w.shape)],
        out_specs=out_specs,
        out_shape=out_shape,
        compiler_params=_cparams(("parallel",)),
        name="swa_proj",
    )(x, g, w)


def _log_sigmoid(x):
    return -(jnp.maximum(-x, 0.0) + jnp.log1p(jnp.exp(-jnp.abs(x))))


AUG = 128
N_PIECES = 3
V_ROWS = HEAD_DIM + 16
FOX_PAGES_PER_STEP = 64
FOX_TILE = 512
FOX_KEYS = 512
FOX_QUERIES = 256
GROUP_SHIFT = GROUP.bit_length() - 1


def _fox_proj_kernel(x_ref, g_ref, w_ref, wf_ref, fb_ref, *out_refs, prompt):
    if prompt:
        qt_ref, kt_ref, vt_ref, kaug_ref, vtb_ref, logft_ref, gate_ref = out_refs
    else:
        q_ref, k_ref, v_ref, kt_ref, vt_ref, logf_ref, logft_ref, gate_ref = out_refs
    hb = _rms_bf16(x_ref[...], g_ref[...])
    q = _dot(hb, w_ref[:, :D_MODEL])
    k = _dot(hb, w_ref[:, D_MODEL:D_MODEL + KV_WIDTH])
    v = _dot(hb, w_ref[:, D_MODEL + KV_WIDTH:D_MODEL + 2 * KV_WIDTH])
    vt = v.T
    kt_ref[...] = k.T
    vt_ref[...] = vt
    logf = _log_sigmoid(_dot(hb, wf_ref[...]) + fb_ref[...])
    logft_ref[...] = logf.T[:N_HEADS, :]
    if prompt:
        tm = q.shape[0]
        qt = (q * (SCALE * LOG2E)).T.astype(BF16)
        r = lax.broadcasted_iota(jnp.int32, (AUG - HEAD_DIM, tm), 0)
        for hd in range(N_HEADS):
            h, g = divmod(hd, GROUP)
            cols = slice(g * tm, (g + 1) * tm)
            qt_ref[h, 0:HEAD_DIM, cols] = qt[hd * HEAD_DIM:(hd + 1) * HEAD_DIM, :]
            pick = jnp.logical_and(r < N_PIECES * GROUP, (r & (GROUP - 1)) == g)
            qt_ref[h, HEAD_DIM:AUG, cols] = jnp.where(pick, -1.0, 0.0).astype(BF16)
        zeros = jnp.zeros((tm, AUG - HEAD_DIM), F32)
        for h in range(N_KV):
            kaug_ref[h] = jnp.concatenate([k[:, h * HEAD_DIM:(h + 1) * HEAD_DIM], zeros], axis=1).astype(BF16)
            ones_row = lax.broadcasted_iota(jnp.int32, (V_ROWS - HEAD_DIM, tm), 0) == 0
            vth = jnp.concatenate([vt[h * HEAD_DIM:(h + 1) * HEAD_DIM, :],
                                   jnp.where(ones_row, 1.0, 0.0)], axis=0).astype(BF16)
            for kb in range(tm // FOX_KEYS):
                vtb_ref[h, kb] = vth[:, kb * FOX_KEYS:(kb + 1) * FOX_KEYS]
    else:
        q_ref[...] = (q * SCALE).astype(BF16)
        k_ref[...] = k
        v_ref[...] = v
        logf_ref[...] = logf[:, :N_HEADS]
    gate_ref[...] = _dot(hb, w_ref[:, D_MODEL + 2 * KV_WIDTH:])


def _fox_proj(x, g, w, wf, fb, tm, batch, seq, prompt):
    m = x.shape[0]
    ns = seq // tm
    row = lambda i: (i, 0)
    tcol = lambda r: pl.BlockSpec((None, r, tm), lambda i: (i // ns, 0, i % ns))
    sds = jax.ShapeDtypeStruct
    kv_t = [tcol(KV_WIDTH), tcol(KV_WIDTH)]
    kv_t_shape = [sds((batch, KV_WIDTH, seq), F32)] * 2
    if prompt:
        out_specs = ([pl.BlockSpec((None, N_KV, None, AUG, GROUP * tm), lambda i: (i // ns, 0, i % ns, 0, 0))] + kv_t
                     + [pl.BlockSpec((N_KV, tm, AUG), lambda i: (0, i, 0)),
                        pl.BlockSpec((None, N_KV, tm // FOX_KEYS, V_ROWS, FOX_KEYS),
                                     lambda i: (i // ns, 0, i % ns, 0, 0)),
                        tcol(N_HEADS)])
        out_shape = ([sds((batch, N_KV, ns, AUG, GROUP * tm), BF16)] + kv_t_shape
                     + [sds((N_KV, m, AUG), BF16),
                        sds((batch, N_KV, seq // FOX_KEYS, V_ROWS, FOX_KEYS), BF16),
                        sds((batch, N_HEADS, seq), F32)])
    else:
        out_specs = ([pl.BlockSpec((tm, D_MODEL), row), pl.BlockSpec((tm, KV_WIDTH), row),
                      pl.BlockSpec((tm, KV_WIDTH), row)] + kv_t
                     + [pl.BlockSpec((tm, N_HEADS), row), tcol(N_HEADS)])
        out_shape = ([sds((m, D_MODEL), BF16), sds((m, KV_WIDTH), F32), sds((m, KV_WIDTH), F32)]
                     + kv_t_shape + [sds((m, N_HEADS), F32), sds((batch, N_HEADS, seq), F32)])
    out_specs.append(pl.BlockSpec((tm, D_MODEL), row))
    out_shape.append(sds((m, D_MODEL), F32))
    return pl.pallas_call(
        functools.partial(_fox_proj_kernel, prompt=prompt),
        grid=(m // tm,),
        in_specs=[pl.BlockSpec((tm, D_MODEL), row), _full((1, D_MODEL)), _full(w.shape),
                  _full(wf.shape), _full(fb.shape)],
        out_specs=out_specs,
        out_shape=out_shape,
        compiler_params=_cparams(("parallel",)),
        name="fox_proj",
    )(x, g, w, wf, fb)


def _out_proj_kernel(*refs, gated):
    if gated:
        o_ref, gate_ref, w_ref, x_ref, y_ref = refs
        ob = (o_ref[...] * _silu(gate_ref[...])).astype(BF16)
    else:
        o_ref, w_ref, x_ref, y_ref = refs
        ob = o_ref[...]
    y_ref[...] = x_ref[...] + _dot(ob, w_ref[...])


def _out_proj(o, gate, w, x, tm):
    m = x.shape[0]
    row = lambda i: (i, 0)
    tile = pl.BlockSpec((tm, D_MODEL), row)
    gated = gate is not None
    ins = [o, gate, w, x] if gated else [o, w, x]
    in_specs = [tile, tile, _full(w.shape), tile] if gated else [tile, _full(w.shape), tile]
    return pl.pallas_call(
        functools.partial(_out_proj_kernel, gated=gated),
        grid=(m // tm,),
        in_specs=in_specs,
        out_specs=tile,
        out_shape=jax.ShapeDtypeStruct((m, D_MODEL), F32),
        compiler_params=_cparams(("parallel",)),
        name="out_proj",
    )(*ins)


def _final_norm_kernel(x_ref, g_ref, y_ref):
    x = x_ref[...]
    ms = jnp.mean(x * x, axis=-1, keepdims=True)
    y_ref[...] = x * lax.rsqrt(ms + EPS) * g_ref[...]


def _final_norm(x, g, tm):
    m = x.shape[0]
    row = lambda i: (i, 0)
    return pl.pallas_call(
        _final_norm_kernel,
        grid=(m // tm,),
        in_specs=[pl.BlockSpec((tm, D_MODEL), row), _full((1, D_MODEL))],
        out_specs=pl.BlockSpec((tm, D_MODEL), row),
        out_shape=jax.ShapeDtypeStruct((m, D_MODEL), F32),
        compiler_params=_cparams(("parallel",)),
        name="final_norm",
    )(x, g)


HALO = 16
PAD = 8
assert all(w == 2 << g for g, w in enumerate(POOL_WINDOWS)) and HALO >= max(POOL_WINDOWS) and PAD >= HALO // 2


def _pool_layer_tail(x, u, gate, pooled_groups, mix_ref, scale_ref, wout_ref, fg_ref, y_ref):
    pieces = []
    for g in range(len(POOL_WINDOWS)):
        c0 = g * POOL_GROUP
        p = (pooled_groups[g] - u[:, c0:c0 + POOL_GROUP]).astype(BF16)
        pieces.append(_dot(p, mix_ref[g]))
    pm = jnp.concatenate(pieces, axis=1)
    o = (pm * scale_ref[...] * _silu(gate)).astype(BF16)
    y = x + _dot(o, wout_ref[...])
    if fg_ref is not None:
        ms = jnp.mean(y * y, axis=-1, keepdims=True)
        y = y * lax.rsqrt(ms + EPS) * fg_ref[...]
    y_ref[...] = y


def _pool_prompt_kernel(*refs, tp, final, pending):
    refs = list(refs)
    x_ref = refs.pop(0)
    prev = [refs.pop(0) for _ in range(3)] if pending else None
    g_ref, win_ref, mix_ref, scale_ref, wout_ref = [refs.pop(0) for _ in range(5)]
    fg_ref = refs.pop(0) if final else None
    y_ref, tail_ref, *bufs = refs
    i = pl.program_id(1)
    hist = slice(PAD, PAD + HALO)
    ext = slice(PAD, PAD + HALO + tp)
    tile = slice(PAD + HALO, PAD + HALO + tp)

    def back(rows, k):
        return slice(rows.start - k, rows.stop - k)

    @pl.when(i == 0)
    def _():
        for buf in bufs:
            buf[0:PAD + HALO, :] = jnp.zeros((PAD + HALO, buf.shape[1]), F32)

    x = x_ref[...]
    if pending:
        po_ref, pgate_ref, pw_ref = prev
        x = x + _dot((po_ref[...] * _silu(pgate_ref[...])).astype(BF16), pw_ref[...])
    hb = _rms_bf16(x, g_ref[...])
    u = _dot(hb, win_ref[:, :D_MODEL])
    gate = _dot(hb, win_ref[:, D_MODEL:])
    bufs[0][tile, :] = u
    pos = i * tp + lax.broadcasted_iota(jnp.int32, (tp, 1), 0)
    pooled = []
    for g, w in enumerate(POOL_WINDOWS):
        src, span = bufs[g], 1 << g
        if g + 1 < len(bufs):
            nxt = bufs[g + 1]
            nxt[ext, :] = src[ext, POOL_GROUP:] + src[back(ext, span), POOL_GROUP:]
        acc = src[tile, 0:POOL_GROUP] + src[back(tile, span), 0:POOL_GROUP]
        inv_cnt = 1.0 / jnp.minimum(pos + 1, w).astype(F32)
        pooled.append(acc * inv_cnt)
    _pool_layer_tail(x, u, gate, pooled, mix_ref, scale_ref, wout_ref, fg_ref, y_ref)
    bufs[0][hist, :] = u[tp - HALO:tp, :]

    @pl.when(i == pl.num_programs(1) - 1)
    def _():
        tail_ref[...] = u[tp - HALO:tp, :]


def _pool_prompt(x, pending, g, w_in, mix, scale, w_out, fg, batch, seq, tp):
    ns = seq // tp
    final = fg is not None
    tile = pl.BlockSpec((tp, D_MODEL), lambda b, i: (b * ns + i, 0))
    ins = [x] + (list(pending) if pending else []) + [g, w_in, mix, scale, w_out] + ([fg] if final else [])
    in_specs = [tile] + ([tile, tile, _full(pending[2].shape)] if pending else [])
    in_specs += [_full(a.shape) for a in ins[len(in_specs):]]
    return pl.pallas_call(
        functools.partial(_pool_prompt_kernel, tp=tp, final=final, pending=bool(pending)),
        grid=(batch, ns),
        in_specs=in_specs,
        out_specs=[tile, pl.BlockSpec((None, HALO, D_MODEL), lambda b, i: (b, 0, 0))],
        out_shape=[jax.ShapeDtypeStruct((batch * seq, D_MODEL), F32),
                   jax.ShapeDtypeStruct((batch, HALO, D_MODEL), F32)],
        scratch_shapes=[pltpu.VMEM((PAD + HALO + tp, D_MODEL - g * POOL_GROUP), F32)
                        for g in range(len(POOL_WINDOWS))],
        compiler_params=_cparams(("parallel", "arbitrary")),
        name="pool_prompt",
    )(*ins)


def _pool_sample_kernel(*refs, final):
    if final:
        x_ref, st_ref, g_ref, win_ref, mix_ref, scale_ref, wout_ref, fg_ref, y_ref, u_ref = refs
    else:
        x_ref, st_ref, g_ref, win_ref, mix_ref, scale_ref, wout_ref, y_ref, u_ref = refs
        fg_ref = None
    x = x_ref[...]
    hb = _rms_bf16(x, g_ref[...])
    u = _dot(hb, win_ref[:, :D_MODEL])
    gate = _dot(hb, win_ref[:, D_MODEL:])
    u_ref[...] = u
    pooled = []
    for g, w in enumerate(POOL_WINDOWS):
        c0 = g * POOL_GROUP
        acc = u[:, c0:c0 + POOL_GROUP]
        for k in range(1, w):
            acc = acc + st_ref[POOL_STATE - k, :, c0:c0 + POOL_GROUP]
        pooled.append(acc / float(w))
    _pool_layer_tail(x, u, gate, pooled, mix_ref, scale_ref, wout_ref, fg_ref, y_ref)


def _pool_sample(x, state_t, g, w_in, mix, scale, w_out, fg):
    m = x.shape[0]
    final = fg is not None
    ins = [x, state_t, g, w_in, mix, scale, w_out] + ([fg] if final else [])
    return pl.pallas_call(
        functools.partial(_pool_sample_kernel, final=final),
        grid=(1,),
        in_specs=[_full(a.shape) for a in ins],
        out_specs=[_full((m, D_MODEL)), _full((m, D_MODEL))],
        out_shape=[jax.ShapeDtypeStruct((m, D_MODEL), F32)] * 2,
        compiler_params=_cparams(("arbitrary",)),
        name="pool_sample",
    )(*ins)


def _t5_bucket_np(dist):
    n = np.maximum(dist, 0)
    max_exact = N_BUCKETS // 2
    nf = np.maximum(n, 1).astype(np.float32)
    large = max_exact + (np.log(nf / max_exact) / math.log(MAX_DISTANCE / max_exact)
                         * (N_BUCKETS - max_exact)).astype(np.int32)
    large = np.minimum(large, N_BUCKETS - 1)
    return np.where(n < max_exact, n, large)


def _swa_prompt_kernel(sink_ref, rb_ref, qt_ref, kp_ref, kc_ref, vp_ref, vc_ref, bucket_ref, gate_ref,
                       wout_ref, x_ref, y_ref, bias_ref, o_buf, *, nb):
    i = pl.program_id(1)
    cols = GROUP * WINDOW

    @pl.when(i == 0)
    def _():
        bucket = bucket_ref[...]
        hits = [bucket == bk for bk in range(N_BUCKETS)]
        for hd in range(N_HEADS):
            h, g = divmod(hd, GROUP)
            b = jnp.full((2 * WINDOW, WINDOW), NEG, F32)
            for bk in range(N_BUCKETS):
                b = jnp.where(hits[bk], rb_ref[bk, hd] * LOG2E, b)
            bias_ref[h, :, g * WINDOW:(g + 1) * WINDOW] = b

    key_row = lax.broadcasted_iota(jnp.int32, (2 * WINDOW, cols), 0)
    no_prev = jnp.logical_and(i == 0, key_row < WINDOW)
    sinks = [jnp.concatenate([jnp.full((1, WINDOW), sink_ref[h * GROUP + g] * LOG2E, F32) for g in range(GROUP)],
                             axis=1) for h in range(N_KV)]

    def logits(jb, h):
        if jb == 0:
            kband = jnp.concatenate([kp_ref[h], kc_ref[h, 0:WINDOW, :]], axis=0)
        else:
            kband = kc_ref[h, (jb - 1) * WINDOW:(jb + 1) * WINDOW, :]
        s = _dot(kband, qt_ref[h, jb]) + bias_ref[h]
        return jnp.where(no_prev, NEG, s) if jb == 0 else s

    def attend(jb, h, s):
        vprev = vp_ref[h] if jb == 0 else vc_ref[h, jb - 1]
        vband = jnp.concatenate([vprev, vc_ref[h, jb]], axis=1)
        m = jnp.maximum(jnp.max(s, axis=0, keepdims=True), sinks[h])
        p = jnp.exp2(s - m)
        denom = jnp.sum(p, axis=0, keepdims=True) + jnp.exp2(sinks[h] - m)
        ot = _dot(vband, p.astype(BF16)) * (1.0 / denom)
        o_heads = jnp.concatenate([ot[:, g * WINDOW:(g + 1) * WINDOW] for g in range(GROUP)], axis=0)
        o_buf[jb * WINDOW:(jb + 1) * WINDOW, h * GROUP * HEAD_DIM:(h + 1) * GROUP * HEAD_DIM] = o_heads.T

    units = [(jb, h) for jb in range(nb) for h in range(N_KV)]
    s = logits(*units[0])
    for u, unit in enumerate(units):
        s_next = logits(*units[u + 1]) if u + 1 < len(units) else None
        attend(*unit, s)
        s = s_next
    ob = (o_buf[...] * _silu(gate_ref[...])).astype(BF16)
    y_ref[...] = x_ref[...] + _dot(ob, wout_ref[...])


def _swa_prompt(qt, kh, vtb, gate, w_out, x, rel_bias, sinks, batch, seq, tm):
    ns = seq // tm
    nb = tm // WINDOW
    dist = np.arange(WINDOW)[None, :] + WINDOW - np.arange(2 * WINDOW)[:, None]
    bucket = np.where((dist >= 0) & (dist < WINDOW), _t5_bucket_np(dist), -1).astype(np.int32)
    tile = pl.BlockSpec((tm, D_MODEL), lambda b, i: (b * ns + i, 0))
    smem = pl.BlockSpec(memory_space=pltpu.SMEM)
    prev_blk = lambda i: jnp.maximum(i * nb - 1, 0)
    return pl.pallas_call(
        functools.partial(_swa_prompt_kernel, nb=nb),
        grid=(batch, ns),
        in_specs=[smem, smem,
                  pl.BlockSpec((None, N_KV, nb, HEAD_DIM, GROUP * WINDOW), lambda b, i: (b, 0, i, 0, 0)),
                  pl.BlockSpec((N_KV, WINDOW, HEAD_DIM), lambda b, i: (0, b * ns * nb + prev_blk(i), 0)),
                  pl.BlockSpec((N_KV, tm, HEAD_DIM), lambda b, i: (0, b * ns + i, 0)),
                  pl.BlockSpec((None, N_KV, None, HEAD_DIM, WINDOW), lambda b, i: (b, 0, prev_blk(i), 0, 0)),
                  pl.BlockSpec((None, N_KV, nb, HEAD_DIM, WINDOW), lambda b, i: (b, 0, i, 0, 0)),
                  _full(bucket.shape), tile, _full(w_out.shape), tile],
        out_specs=tile,
        out_shape=jax.ShapeDtypeStruct((batch * seq, D_MODEL), F32),
        scratch_shapes=[pltpu.VMEM((N_KV, 2 * WINDOW, GROUP * WINDOW), F32),
                        pltpu.VMEM((tm, D_MODEL), F32)],
        compiler_params=_cparams(("parallel", "arbitrary")),
        name="swa_prompt",
    )(sinks, rel_bias, qt, kh, kh, vtb, vtb, jnp.asarray(bucket), gate, w_out, x)


def _head_diag(o_full):
    out = jnp.zeros((N_HEADS, HEAD_DIM), F32)
    row_kv = lax.broadcasted_iota(jnp.int32, (N_HEADS, HEAD_DIM), 0) // GROUP
    for h in range(N_KV):
        out = out + jnp.where(row_kv == h, o_full[:, h * HEAD_DIM:(h + 1) * HEAD_DIM], 0.0)
    return out


def _swa_sample_kernel(qm_ref, kc_ref, vc_ref, kn_ref, vn_ref, knt_ref, vnt_ref, bias_ref, bias0_ref,
                       sink_ref, o_ref, wk_ref, wv_ref, *, bt):
    i = pl.program_id(0)
    lane = lax.broadcasted_iota(jnp.int32, (KV_WIDTH, WINDOW), 1)
    sink = sink_ref[...]
    for e in range(bt):
        b = i * bt + e
        kc = kc_ref[e]
        vc = vc_ref[e]
        qm = qm_ref[e]
        s = _dot(qm, kc.astype(BF16)) + bias_ref[...]
        s_n = jnp.sum(qm.astype(F32) * _bf16_round(kn_ref[e]), axis=1, keepdims=True) + bias0_ref[...]
        m = jnp.maximum(jnp.maximum(jnp.max(s, axis=1, keepdims=True), s_n), sink)
        p = jnp.exp(s - m)
        p_n = jnp.exp(s_n - m)
        denom = jnp.sum(p, axis=1, keepdims=True) + p_n + jnp.exp(sink - m)
        o_full = _dot_nt((p / denom).astype(BF16), vc.astype(BF16))
        o_full = o_full + _bf16_round(p_n / denom) * _bf16_round(vn_ref[e])
        o_ref[e] = _head_diag(o_full)
        kcol = jnp.sum(jnp.where(lane == b, knt_ref[...], 0.0), axis=1, keepdims=True)
        vcol = jnp.sum(jnp.where(lane == b, vnt_ref[...], 0.0), axis=1, keepdims=True)
        wk_ref[e] = jnp.where(lane == WINDOW - 1, kcol, pltpu.roll(kc, WINDOW - 1, 1))
        wv_ref[e] = jnp.where(lane == WINDOW - 1, vcol, pltpu.roll(vc, WINDOW - 1, 1))


def _swa_sample(qm, kc, vc, kn, vn, knt, vnt, bias_keys, bias0, sinks, bt):
    m = qm.shape[0]
    blk3 = lambda s1, s2: pl.BlockSpec((bt, s1, s2), lambda i: (i, 0, 0))
    cache = jax.ShapeDtypeStruct((m, KV_WIDTH, WINDOW), F32)
    return pl.pallas_call(
        functools.partial(_swa_sample_kernel, bt=bt),
        grid=(m // bt,),
        in_specs=[blk3(N_HEADS, KV_WIDTH), blk3(KV_WIDTH, WINDOW), blk3(KV_WIDTH, WINDOW),
                  blk3(1, KV_WIDTH), blk3(1, KV_WIDTH), _full(knt.shape), _full(vnt.shape),
                  _full(bias_keys.shape), _full(bias0.shape), _full(sinks.shape)],
        out_specs=[blk3(N_HEADS, HEAD_DIM), blk3(KV_WIDTH, WINDOW), blk3(KV_WIDTH, WINDOW)],
        out_shape=[jax.ShapeDtypeStruct((m, N_HEADS, HEAD_DIM), F32), cache, cache],
        compiler_params=_cparams(("parallel",)),
        name="swa_sample",
    )(qm, kc, vc, kn, vn, knt, vnt, bias_keys, bias0, sinks)


def _split3(x):
    hi = x.astype(BF16)
    r = x - hi.astype(F32)
    mid = r.astype(BF16)
    lo = (r - mid.astype(F32)).astype(BF16)
    return hi, mid, lo


def _cumsum_lanes_blocks(x, blk):
    n = x.shape[0]
    hi, mid, lo = _split3(x)
    r_i = lax.broadcasted_iota(jnp.int32, (blk, blk), 0)
    c_i = lax.broadcasted_iota(jnp.int32, (blk, blk), 1)
    upper = jnp.where(r_i <= c_i, 1.0, 0.0).astype(BF16)
    r = _dot(jnp.concatenate([hi, mid, lo], axis=0), upper)
    return r[0:n] + r[n:2 * n] + r[2 * n:]


def _fox_decay_kernel(x_ref, kin_ref, kout_ref, carry, *, blk):
    i = pl.program_id(1)

    @pl.when(i == 0)
    def _():
        carry[...] = jnp.zeros_like(carry)

    c = _cumsum_lanes_blocks(x_ref[...], blk) + carry[:, 0:1]
    carry[...] = jnp.broadcast_to(c[:, blk - 1:blk], carry.shape)
    hi, mid, lo = _split3(c * LOG2E)
    pieces = jnp.concatenate([hi, mid, lo], axis=0).astype(F32)
    pad = jnp.zeros((LANES - N_PIECES * N_HEADS, blk), F32)
    pieces_t = jnp.concatenate([pieces, pad], axis=0).T.astype(BF16)
    src = lax.broadcasted_iota(jnp.int32, (LANES, AUG), 0)
    dst = lax.broadcasted_iota(jnp.int32, (LANES, AUG), 1) - HEAD_DIM
    in_aug = jnp.logical_and(dst >= 0, dst < N_PIECES * GROUP)
    for h in range(N_KV):
        want = (dst >> GROUP_SHIFT) * N_HEADS + h * GROUP + (dst & (GROUP - 1))
        place = jnp.where(jnp.logical_and(in_aug, src == want), 1.0, 0.0).astype(BF16)
        kout_ref[h] = kin_ref[h] + _dot(pieces_t, place).astype(BF16)


def _fox_decay(logft, kaug, blk):
    b, h, s = logft.shape
    ns = s // blk
    kspec = pl.BlockSpec((N_KV, blk, AUG), lambda bi, i: (0, bi * ns + i, 0))
    return pl.pallas_call(
        functools.partial(_fox_decay_kernel, blk=blk),
        grid=(b, ns),
        in_specs=[pl.BlockSpec((None, h, blk), lambda bi, i: (bi, 0, i)), kspec],
        out_specs=kspec,
        out_shape=jax.ShapeDtypeStruct(kaug.shape, kaug.dtype),
        scratch_shapes=[pltpu.VMEM((h, LANES), F32)],
        input_output_aliases={1: 0},
        compiler_params=_cparams(("parallel", "arbitrary")),
        name="fox_decay",
    )(logft, kaug)


def _fox_prompt_kernel(qt_ref, k_ref, vt_ref, o_ref, *scratch, tq, tk):
    assert tq == tk
    qi = pl.program_id(2)
    m_ref, acc_ref, sa_ref, sb_ref, ca_ref, cb_ref = scratch
    cols = GROUP * tq
    m_ref[...] = jnp.full(m_ref.shape, NEG, F32)
    acc_ref[...] = jnp.zeros(acc_ref.shape, F32)

    def logits(kb, s_ref, c_ref):
        k0 = pl.multiple_of(kb * tk, tk)
        s = _dot(k_ref[pl.ds(k0, tk), :], qt_ref[...])
        s_ref[...] = s
        c_ref[...] = jnp.max(s, axis=0, keepdims=True)

    def accumulate(kb, s, cmax):
        m_old = m_ref[...]
        m_new = jnp.maximum(m_old, cmax)
        alpha = jnp.exp2(m_old - m_new)
        p = jnp.exp2(s - m_new)
        acc_ref[...] = alpha * acc_ref[...] + _dot(vt_ref[kb], p.astype(BF16))
        m_ref[...] = m_new

    def diagonal(s_ref):
        kpos = lax.broadcasted_iota(jnp.int32, (tk, cols), 0)
        qpos = lax.broadcasted_iota(jnp.int32, (tk, cols), 1) & (tq - 1)
        s = jnp.where(kpos <= qpos, s_ref[...], NEG)
        accumulate(qi, s, jnp.max(s, axis=0, keepdims=True))

    logits(0, sa_ref, ca_ref)

    def two_blocks(j, carry):
        kb = 2 * j
        logits(kb + 1, sb_ref, cb_ref)
        accumulate(kb, sa_ref[...], ca_ref[...])
        logits(kb + 2, sa_ref, ca_ref)
        accumulate(kb + 1, sb_ref[...], cb_ref[...])
        return carry

    lax.fori_loop(0, qi // 2, two_blocks, 0)

    @pl.when(qi % 2 == 1)
    def _():
        logits(qi, sb_ref, cb_ref)
        accumulate(qi - 1, sa_ref[...], ca_ref[...])
        diagonal(sb_ref)

    @pl.when(qi % 2 == 0)
    def _():
        diagonal(sa_ref)

    o = acc_ref[0:HEAD_DIM, :] / acc_ref[HEAD_DIM:HEAD_DIM + 1, :]
    o_ref[...] = jnp.concatenate([o[:, g * tq:(g + 1) * tq] for g in range(GROUP)], axis=0).T


def _fox_prompt(qt, kaug, vtb, batch, seq, tq):
    nq = seq // tq
    nk, tk = vtb.shape[2], vtb.shape[4]
    cols = GROUP * tq
    return pl.pallas_call(
        functools.partial(_fox_prompt_kernel, tq=tq, tk=tk),
        grid=(batch, N_KV, nq),
        in_specs=[pl.BlockSpec((None, None, None, AUG, cols), lambda b, h, qi: (b, h, qi, 0, 0)),
                  pl.BlockSpec((None, seq, AUG), lambda b, h, qi: (h, b, 0)),
                  pl.BlockSpec((None, None, nk, V_ROWS, tk), lambda b, h, qi: (b, h, 0, 0, 0))],
        out_specs=pl.BlockSpec((tq, GROUP * HEAD_DIM), lambda b, h, qi: (b * nq + qi, h)),
        out_shape=jax.ShapeDtypeStruct((batch * seq, D_MODEL), F32),
        scratch_shapes=[pltpu.VMEM((1, cols), F32),
                        pltpu.VMEM((V_ROWS, cols), F32),
                        pltpu.VMEM((tk, cols), F32), pltpu.VMEM((tk, cols), F32),
                        pltpu.VMEM((1, cols), F32), pltpu.VMEM((1, cols), F32)],
        compiler_params=_cparams(("parallel", "parallel", "arbitrary")),
        name="fox_prompt",
    )(qt, kaug, vtb)


def _fox_sample_kernel(pt_ref, qm_ref, kn_ref, vn_ref, fn_ref, ck_hbm, cv_hbm, cf_hbm, o_ref,
                       kbuf, vbuf, fbuf, sems, m_ref, l_ref, acc_ref, carry_ref, *, cp, nchunk):
    b = pl.program_id(0)
    c = pl.program_id(1)
    nb = pl.num_programs(0)
    step = b * nchunk + c
    slot = step % 2

    def copies(bb, cc, sl, p):
        page = pt_ref[bb, cc * cp + p]
        return (pltpu.make_async_copy(ck_hbm.at[page], kbuf.at[sl, p], sems.at[sl, 0]),
                pltpu.make_async_copy(cv_hbm.at[page], vbuf.at[sl, p], sems.at[sl, 1]),
                pltpu.make_async_copy(cf_hbm.at[page], fbuf.at[sl, p], sems.at[sl, 2]))

    def issue(bb, cc, sl):
        def body(p, carry):
            for cpy in copies(bb, cc, sl, p):
                cpy.start()
            return carry
        lax.fori_loop(0, cp, body, 0)

    @pl.when(step == 0)
    def _():
        issue(b, c, slot)

    @pl.when(step + 1 < nb * nchunk)
    def _():
        nxt = step + 1
        issue(nxt // nchunk, nxt % nchunk, 1 - slot)

    def wait_body(p, carry):
        for cpy in copies(b, c, slot, p):
            cpy.wait()
        return carry
    lax.fori_loop(0, cp, wait_body, 0)

    @pl.when(c == 0)
    def _():
        m_ref[...] = jnp.full(m_ref.shape, NEG, F32)
        l_ref[...] = jnp.zeros(l_ref.shape, F32)
        acc_ref[...] = jnp.zeros(acc_ref.shape, F32)
        carry_ref[...] = jnp.zeros(carry_ref.shape, F32)

    qm = qm_ref[...]
    cin = _cumsum_lanes_blocks(fbuf[slot].reshape(cp * N_HEADS, PAGE), PAGE).reshape(cp, N_HEADS, PAGE)
    carry = carry_ref[...]
    cs = []
    for p in range(cp):
        cs.append(cin[p] + carry)
        carry = carry + cin[p][:, PAGE - 1:PAGE]
    carry_ref[...] = carry
    kcat = jnp.concatenate([kbuf[slot, p] for p in range(cp)], axis=1).astype(BF16)
    t = _dot(qm, kcat) - jnp.concatenate(cs, axis=1)
    m_old = m_ref[...]
    m_new = jnp.maximum(m_old, jnp.max(t, axis=1, keepdims=True))
    alpha = jnp.exp(m_old - m_new)
    p = jnp.exp(t - m_new)
    l_new = alpha * l_ref[...] + jnp.sum(p, axis=1, keepdims=True)
    vcat = jnp.concatenate([vbuf[slot, p_] for p_ in range(cp)], axis=1).astype(BF16)
    acc_new = alpha * acc_ref[...] + _dot_nt(p.astype(BF16), vcat)
    m_ref[...] = m_new
    l_ref[...] = l_new
    acc_ref[...] = acc_new

    @pl.when(c == nchunk - 1)
    def _():
        s_n = jnp.sum(qm.astype(F32) * _bf16_round(kn_ref[...]), axis=1, keepdims=True)
        t_n = s_n - (carry + fn_ref[...])
        m_f = jnp.maximum(m_new, t_n)
        a = jnp.exp(m_new - m_f)
        p_n = jnp.exp(t_n - m_f)
        l_f = a * l_new + p_n
        acc = a * acc_new + _bf16_round(p_n) * _bf16_round(vn_ref[...])
        o_ref[...] = _head_diag(acc / l_f)


def _fox_sample(page_table, qm, kn, vn, fn, ck, cv, cf, cp):
    m, n_pages = page_table.shape
    nchunk = n_pages // cp
    per_b = lambda s1, s2: pl.BlockSpec((None, s1, s2), lambda b, c, pt: (b, 0, 0))
    any_spec = pl.BlockSpec(memory_space=pl.ANY)
    return pl.pallas_call(
        functools.partial(_fox_sample_kernel, cp=cp, nchunk=nchunk),
        grid_spec=pltpu.PrefetchScalarGridSpec(
            num_scalar_prefetch=1,
            grid=(m, nchunk),
            in_specs=[per_b(N_HEADS, KV_WIDTH), per_b(1, KV_WIDTH), per_b(1, KV_WIDTH), per_b(N_HEADS, 1),
                      any_spec, any_spec, any_spec],
            out_specs=per_b(N_HEADS, HEAD_DIM),
            scratch_shapes=[pltpu.VMEM((2, cp, KV_WIDTH, PAGE), F32),
                            pltpu.VMEM((2, cp, KV_WIDTH, PAGE), F32),
                            pltpu.VMEM((2, cp, N_HEADS, PAGE), F32),
                            pltpu.SemaphoreType.DMA((2, 3)),
                            pltpu.VMEM((N_HEADS, 1), F32), pltpu.VMEM((N_HEADS, 1), F32),
                            pltpu.VMEM((N_HEADS, KV_WIDTH), F32), pltpu.VMEM((N_HEADS, 1), F32)],
        ),
        out_shape=jax.ShapeDtypeStruct((m, N_HEADS, HEAD_DIM), F32),
        compiler_params=_cparams(("arbitrary", "arbitrary")),
        name="fox_sample",
    )(page_table, qm, kn, vn, fn, ck, cv, cf)


def _head_rows_q(q_rows):
    m = q_rows.shape[0]
    q4 = q_rows.reshape(m, N_KV, GROUP, HEAD_DIM)
    eye = jnp.eye(N_KV, dtype=q_rows.dtype)
    return (q4[:, :, :, None, :] * eye[None, :, None, :, None]).reshape(m, N_HEADS, KV_WIDTH)


def _tokens_last(x):
    lead = x.shape[:-3]
    n = len(lead)
    xt = jnp.transpose(x, tuple(range(n)) + (n + 1, n + 2, n))
    return xt.reshape(lead + (KV_WIDTH, x.shape[-3]))


def _tokens_first(xt):
    lead = xt.shape[:-2]
    n = len(lead)
    x4 = xt.reshape(lead + (N_KV, HEAD_DIM, xt.shape[-1]))
    return jnp.transpose(x4, tuple(range(n)) + (n + 2, n, n + 1))


def kernel(x_prompt, x_sample, state_pool, cache_win_k, cache_win_v, cache_fox_k, cache_fox_v,
           cache_fox_logf, page_table, norm_g, final_norm_g, rel_bias, pool_w_in, pool_mix,
           pool_scale, pool_w_out, swa_w_in, swa_sinks, swa_w_out, fox_w_in, fox_f_bias, fox_w_out):
    batch, seq, _ = x_prompt.shape
    db = x_sample.shape[0]
    depth = norm_g.shape[0]
    mp = batch * seq
    tm_p, tm_s = 512, db

    xp = x_prompt.reshape(mp, D_MODEL)
    xs = x_sample.reshape(db, D_MODEL)

    rb = rel_bias.astype(F32)
    dist_keys = WINDOW - np.arange(WINDOW)
    bias_keys = jnp.where((dist_keys < WINDOW)[None, :], rb[_t5_bucket_np(dist_keys)].T, NEG)
    bias0 = rb[0].reshape(N_HEADS, 1)
    fg = final_norm_g.reshape(1, D_MODEL)

    pool_p, pool_s = [], []
    wk_p, wv_p, wk_s, wv_s = [], [], [], []
    fk_p, fv_p, fl_p, fk_s, fv_s, fl_s = [], [], [], [], [], []
    pending = None
    for i in range(depth):
        kind, j = i % 3, i // 3
        g = norm_g[i].reshape(1, D_MODEL)
        if kind == 0:
            w_in = pool_w_in[j].astype(BF16)
            mix = pool_mix[j].astype(BF16)
            scale = pool_scale[j].reshape(1, D_MODEL)
            w_out = pool_w_out[j].astype(BF16)
            fg_l = fg if i == depth - 1 else None
            xp, u_tail = _pool_prompt(xp, pending, g, w_in, mix, scale, w_out, fg_l, batch, seq, tm_p)
            pending = None
            pool_p.append(u_tail[:, HALO - POOL_STATE:])
            xs, u_s = _pool_sample(xs, jnp.transpose(state_pool[j], (1, 0, 2)), g, w_in, mix, scale, w_out, fg_l)
            pool_s.append(jnp.concatenate([state_pool[j][:, 1:], u_s[:, None, :]], axis=1))
        elif kind == 1:
            w_in = swa_w_in[j].astype(BF16)
            w_out = swa_w_out[j].astype(BF16)
            sinks = swa_sinks[j].astype(F32)
            qt_p, kh_p, kt_p, vt_p, vtb_p, gate_p = _swa_proj(xp, g, w_in, tm_p, batch, seq, True)
            xp = _swa_prompt(qt_p, kh_p, vtb_p, gate_p, w_out, xp, rb, sinks, batch, seq, tm_p)
            wk_p.append(_tokens_first(kt_p[:, :, seq - WINDOW:]))
            wv_p.append(_tokens_first(vt_p[:, :, seq - WINDOW:]))
            q_s, k_s, v_s, kt_s, vt_s, gate_s = _swa_proj(xs, g, w_in, tm_s, 1, db, False)
            o_s, wk, wv = _swa_sample(_head_rows_q(q_s), _tokens_last(cache_win_k[j]),
                                      _tokens_last(cache_win_v[j]), k_s[:, None, :], v_s[:, None, :],
                                      kt_s[0], vt_s[0], bias_keys, bias0, sinks.reshape(N_HEADS, 1), 8)
            xs = _out_proj(o_s.reshape(db, D_MODEL), gate_s, w_out, xs, tm_s)
            wk_s.append(_tokens_first(wk))
            wv_s.append(_tokens_first(wv))
        else:
            w_full = fox_w_in[j]
            nqkv = D_MODEL + 2 * KV_WIDTH
            w_in = jnp.concatenate([w_full[:, :nqkv], w_full[:, nqkv + N_HEADS:]], axis=1).astype(BF16)
            wf = jnp.pad(w_full[:, nqkv:nqkv + N_HEADS], ((0, 0), (0, LANES - N_HEADS))).astype(BF16)
            fb = jnp.pad(fox_f_bias[j].astype(F32), (0, LANES - N_HEADS)).reshape(1, LANES)
            w_out = fox_w_out[j].astype(BF16)
            qt, kt_p, vt_p, kaug, vtb, logft_p, gate_p = _fox_proj(xp, g, w_in, wf, fb, FOX_TILE, batch, seq, True)
            kaug = _fox_decay(logft_p, kaug, FOX_TILE)
            o_p = _fox_prompt(qt, kaug, vtb, batch, seq, FOX_TILE)
            if i + 1 < depth and (i + 1) % 3 == 0:
                pending = (o_p, gate_p, w_out)
            else:
                xp = _out_proj(o_p, gate_p, w_out, xp, tm_p)
            fk_p.append(_tokens_first(kt_p))
            fv_p.append(_tokens_first(vt_p))
            fl_p.append(jnp.transpose(logft_p, (0, 2, 1)))
            q_s, k_s, v_s, kt_s, vt_s, logf_s, logft_s, gate_s = _fox_proj(
                xs, g, w_in, wf, fb, tm_s, 1, db, False)
            o_s = _fox_sample(page_table, _head_rows_q(q_s), k_s[:, None, :], v_s[:, None, :],
                              logf_s[:, :, None], _tokens_last(cache_fox_k[j]),
                              _tokens_last(cache_fox_v[j]), jnp.transpose(cache_fox_logf[j], (0, 2, 1)),
                              FOX_PAGES_PER_STEP)
            xs = _out_proj(o_s.reshape(db, D_MODEL), gate_s, w_out, xs, tm_s)
            fk_s.append(_tokens_first(kt_s[0])[:, None])
            fv_s.append(_tokens_first(vt_s[0])[:, None])
            fl_s.append(logft_s[0].T[:, None, :])

    if (depth - 1) % 3 != 0:
        xp, xs = _final_norm(xp, fg, tm_p), _final_norm(xs, fg, tm_s)
    y_prompt = xp.reshape(batch, seq, D_MODEL)
    y_sample = xs.reshape(db, 1, D_MODEL)
    return (y_prompt, y_sample, jnp.stack(pool_p), jnp.stack(pool_s), jnp.stack(wk_p), jnp.stack(wv_p),
            jnp.stack(wk_s), jnp.stack(wv_s), jnp.stack(fk_p), jnp.stack(fv_p), jnp.stack(fl_p),
            jnp.stack(fk_s), jnp.stack(fv_s), jnp.stack(fl_s))
```

```python
import functools
import math

import numpy as np
import jax
import jax.numpy as jnp
from jax import lax
from jax.experimental import pallas as pl
from jax.experimental.pallas import tpu as pltpu

D_MODEL = 1024
HEAD_DIM = 64
N_HEADS = 16
N_KV = 4
GROUP = 4
KV_WIDTH = N_KV * HEAD_DIM
POOL_WINDOWS = (2, 4, 8, 16)
POOL_GROUP = 256
POOL_STATE = 15
WINDOW = 128
N_BUCKETS = 32
MAX_DISTANCE = 128
SCALE = HEAD_DIM ** -0.5
EPS = 1e-6
NEG = -1e30
PAGE = 128
LANES = 128

BF16 = jnp.bfloat16
F32 = jnp.float32

VMEM_LIMIT = 56 * 1024 * 1024


def _cparams(sem):
    return pltpu.CompilerParams(dimension_semantics=sem, vmem_limit_bytes=VMEM_LIMIT)


def _rms_bf16(x, g):
    ms = jnp.mean(x * x, axis=-1, keepdims=True)
    return (x * lax.rsqrt(ms + EPS) * g).astype(BF16)


def _silu(x):
    return x * jax.nn.sigmoid(x)


def _dot(a, b):
    return jnp.dot(a, b, preferred_element_type=F32)


def _dot_nt(a, b):
    return lax.dot_general(a, b, (((1,), (1,)), ((), ())), preferred_element_type=F32)


def _bf16_round(x):
    return x.astype(BF16).astype(F32)


def _full(shape):
    n = len(shape)
    return pl.BlockSpec(shape, lambda *_: (0,) * n)


LOG2E = 1.4426950408889634


def _swa_proj_kernel(x_ref, g_ref, w_ref, *out_refs, prompt):
    if prompt:
        qt_ref, kh_ref, kt_ref, vt_ref, vtb_ref, gate_ref = out_refs
    else:
        q_ref, k_ref, v_ref, kt_ref, vt_ref, gate_ref = out_refs
    hb = _rms_bf16(x_ref[...], g_ref[...])
    q = _dot(hb, w_ref[:, :D_MODEL])
    k = _dot(hb, w_ref[:, D_MODEL:D_MODEL + KV_WIDTH])
    v = _dot(hb, w_ref[:, D_MODEL + KV_WIDTH:D_MODEL + 2 * KV_WIDTH])
    vt = v.T
    kt_ref[...] = k.T
    vt_ref[...] = vt
    if prompt:
        tm = q.shape[0]
        qt = (q * (SCALE * LOG2E)).T.astype(BF16)
        for hd in range(N_HEADS):
            h, g = divmod(hd, GROUP)
            for jb in range(tm // WINDOW):
                qt_ref[h, jb, :, g * WINDOW:(g + 1) * WINDOW] = (
                    qt[hd * HEAD_DIM:(hd + 1) * HEAD_DIM, jb * WINDOW:(jb + 1) * WINDOW])
        for h in range(N_KV):
            kh_ref[h] = k[:, h * HEAD_DIM:(h + 1) * HEAD_DIM].astype(BF16)
            vth = vt[h * HEAD_DIM:(h + 1) * HEAD_DIM, :].astype(BF16)
            for jb in range(tm // WINDOW):
                vtb_ref[h, jb] = vth[:, jb * WINDOW:(jb + 1) * WINDOW]
    else:
        q_ref[...] = (q * SCALE).astype(BF16)
        k_ref[...] = k
        v_ref[...] = v
    gate_ref[...] = _dot(hb, w_ref[:, D_MODEL + 2 * KV_WIDTH:])


def _swa_proj(x, g, w, tm, batch, seq, prompt):
    m = x.shape[0]
    ns = seq // tm
    nb = tm // WINDOW
    row = lambda i: (i, 0)
    sds = jax.ShapeDtypeStruct
    tcol = pl.BlockSpec((None, KV_WIDTH, tm), lambda i: (i // ns, 0, i % ns))
    blocks = lambda last: pl.BlockSpec((None, N_KV, nb, HEAD_DIM, last), lambda i: (i // ns, 0, i % ns, 0, 0))
    kv_t_shape = [sds((batch, KV_WIDTH, seq), F32)] * 2
    if prompt:
        out_specs = [blocks(GROUP * WINDOW), pl.BlockSpec((N_KV, tm, HEAD_DIM), lambda i: (0, i, 0)),
                     tcol, tcol, blocks(WINDOW)]
        out_shape = ([sds((batch, N_KV, seq // WINDOW, HEAD_DIM, GROUP * WINDOW), BF16),
                      sds((N_KV, m, HEAD_DIM), BF16)] + kv_t_shape
                     + [sds((batch, N_KV, seq // WINDOW, HEAD_DIM, WINDOW), BF16)])
    else:
        out_specs = [pl.BlockSpec((tm, D_MODEL), row), pl.BlockSpec((tm, KV_WIDTH), row),
                     pl.BlockSpec((tm, KV_WIDTH), row), tcol, tcol]
        out_shape = [sds((m, D_MODEL), BF16), sds((m, KV_WIDTH), F32), sds((m, KV_WIDTH), F32)] + kv_t_shape
    out_specs.append(pl.BlockSpec((tm, D_MODEL), row))
    out_shape.append(sds((m, D_MODEL), F32))
    return pl.pallas_call(
        functools.partial(_swa_proj_kernel, prompt=prompt),
        grid=(m // tm,),
        in_specs=[pl.BlockSpec((tm, D_MODEL), row), _full((1, D_MODEL)), _full(w.shape)],
        out_specs=out_specs,
        out_shape=out_shape,
        compiler_params=_cparams(("parallel",)),
        name="swa_proj",
    )(x, g, w)


def _log_sigmoid(x):
    return -(jnp.maximum(-x, 0.0) + jnp.log1p(jnp.exp(-jnp.abs(x))))


AUG = 128
N_PIECES = 3
SWA_ROWS_PER_STEP = 16
V_ROWS = HEAD_DIM + 16
FOX_PAGES_PER_STEP = 64
FOX_TILE = 512
FOX_KEYS = 512
FOX_QUERIES = 256
GROUP_SHIFT = GROUP.bit_length() - 1


def _fox_proj_kernel(x_ref, g_ref, w_ref, wf_ref, fb_ref, *out_refs, prompt):
    if prompt:
        qt_ref, kt_ref, vt_ref, kaug_ref, vtb_ref, logft_ref, gate_ref = out_refs
    else:
        q_ref, k_ref, v_ref, kt_ref, vt_ref, logf_ref, logft_ref, gate_ref = out_refs
    hb = _rms_bf16(x_ref[...], g_ref[...])
    q = _dot(hb, w_ref[:, :D_MODEL])
    k = _dot(hb, w_ref[:, D_MODEL:D_MODEL + KV_WIDTH])
    v = _dot(hb, w_ref[:, D_MODEL + KV_WIDTH:D_MODEL + 2 * KV_WIDTH])
    vt = v.T
    kt_ref[...] = k.T
    vt_ref[...] = vt
    logf = _log_sigmoid(_dot(hb, wf_ref[...]) + fb_ref[...])
    logft_ref[...] = logf.T[:N_HEADS, :]
    if prompt:
        tm = q.shape[0]
        qt = (q * (SCALE * LOG2E)).T.astype(BF16)
        r = lax.broadcasted_iota(jnp.int32, (AUG - HEAD_DIM, tm), 0)
        for hd in range(N_HEADS):
            h, g = divmod(hd, GROUP)
            cols = slice(g * tm, (g + 1) * tm)
            qt_ref[h, 0:HEAD_DIM, cols] = qt[hd * HEAD_DIM:(hd + 1) * HEAD_DIM, :]
            pick = jnp.logical_and(r < N_PIECES * GROUP, (r & (GROUP - 1)) == g)
            qt_ref[h, HEAD_DIM:AUG, cols] = jnp.where(pick, -1.0, 0.0).astype(BF16)
        zeros = jnp.zeros((tm, AUG - HEAD_DIM), F32)
        for h in range(N_KV):
            kaug_ref[h] = jnp.concatenate([k[:, h * HEAD_DIM:(h + 1) * HEAD_DIM], zeros], axis=1).astype(BF16)
            ones_row = lax.broadcasted_iota(jnp.int32, (V_ROWS - HEAD_DIM, tm), 0) == 0
            vth = jnp.concatenate([vt[h * HEAD_DIM:(h + 1) * HEAD_DIM, :],
                                   jnp.where(ones_row, 1.0, 0.0)], axis=0).astype(BF16)
            for kb in range(tm // FOX_KEYS):
                vtb_ref[h, kb] = vth[:, kb * FOX_KEYS:(kb + 1) * FOX_KEYS]
    else:
        q_ref[...] = (q * SCALE).astype(BF16)
        k_ref[...] = k
        v_ref[...] = v
        logf_ref[...] = logf[:, :N_HEADS]
    gate_ref[...] = _dot(hb, w_ref[:, D_MODEL + 2 * KV_WIDTH:])


def _fox_proj(x, g, w, wf, fb, tm, batch, seq, prompt):
    m = x.shape[0]
    ns = seq // tm
    row = lambda i: (i, 0)
    tcol = lambda r: pl.BlockSpec((None, r, tm), lambda i: (i // ns, 0, i % ns))
    sds = jax.ShapeDtypeStruct
    kv_t = [tcol(KV_WIDTH), tcol(KV_WIDTH)]
    kv_t_shape = [sds((batch, KV_WIDTH, seq), F32)] * 2
    if prompt:
        out_specs = ([pl.BlockSpec((None, N_KV, None, AUG, GROUP * tm), lambda i: (i // ns, 0, i % ns, 0, 0))] + kv_t
                     + [pl.BlockSpec((N_KV, tm, AUG), lambda i: (0, i, 0)),
                        pl.BlockSpec((None, N_KV, tm // FOX_KEYS, V_ROWS, FOX_KEYS),
                                     lambda i: (i // ns, 0, i % ns, 0, 0)),
                        tcol(N_HEADS)])
        out_shape = ([sds((batch, N_KV, ns, AUG, GROUP * tm), BF16)] + kv_t_shape
                     + [sds((N_KV, m, AUG), BF16),
                        sds((batch, N_KV, seq // FOX_KEYS, V_ROWS, FOX_KEYS), BF16),
                        sds((batch, N_HEADS, seq), F32)])
    else:
        out_specs = ([pl.BlockSpec((tm, D_MODEL), row), pl.BlockSpec((tm, KV_WIDTH), row),
                      pl.BlockSpec((tm, KV_WIDTH), row)] + kv_t
                     + [pl.BlockSpec((tm, N_HEADS), row), tcol(N_HEADS)])
        out_shape = ([sds((m, D_MODEL), BF16), sds((m, KV_WIDTH), F32), sds((m, KV_WIDTH), F32)]
                     + kv_t_shape + [sds((m, N_HEADS), F32), sds((batch, N_HEADS, seq), F32)])
    out_specs.append(pl.BlockSpec((tm, D_MODEL), row))
    out_shape.append(sds((m, D_MODEL), F32))
    return pl.pallas_call(
        functools.partial(_fox_proj_kernel, prompt=prompt),
        grid=(m // tm,),
        in_specs=[pl.BlockSpec((tm, D_MODEL), row), _full((1, D_MODEL)), _full(w.shape),
                  _full(wf.shape), _full(fb.shape)],
        out_specs=out_specs,
        out_shape=out_shape,
        compiler_params=_cparams(("parallel",)),
        name="fox_proj",
    )(x, g, w, wf, fb)


def _out_proj_kernel(*refs, gated):
    if gated:
        o_ref, gate_ref, w_ref, x_ref, y_ref = refs
        ob = (o_ref[...] * _silu(gate_ref[...])).astype(BF16)
    else:
        o_ref, w_ref, x_ref, y_ref = refs
        ob = o_ref[...]
    y_ref[...] = x_ref[...] + _dot(ob, w_ref[...])


def _out_proj(o, gate, w, x, tm):
    m = x.shape[0]
    row = lambda i: (i, 0)
    tile = pl.BlockSpec((tm, D_MODEL), row)
    gated = gate is not None
    ins = [o, gate, w, x] if gated else [o, w, x]
    in_specs = [tile, tile, _full(w.shape), tile] if gated else [tile, _full(w.shape), tile]
    return pl.pallas_call(
        functools.partial(_out_proj_kernel, gated=gated),
        grid=(m // tm,),
        in_specs=in_specs,
        out_specs=tile,
        out_shape=jax.ShapeDtypeStruct((m, D_MODEL), F32),
        compiler_params=_cparams(("parallel",)),
        name="out_proj",
    )(*ins)


def _final_norm_kernel(x_ref, g_ref, y_ref):
    x = x_ref[...]
    ms = jnp.mean(x * x, axis=-1, keepdims=True)
    y_ref[...] = x * lax.rsqrt(ms + EPS) * g_ref[...]


def _final_norm(x, g, tm):
    m = x.shape[0]
    row = lambda i: (i, 0)
    return pl.pallas_call(
        _final_norm_kernel,
        grid=(m // tm,),
        in_specs=[pl.BlockSpec((tm, D_MODEL), row), _full((1, D_MODEL))],
        out_specs=pl.BlockSpec((tm, D_MODEL), row),
        out_shape=jax.ShapeDtypeStruct((m, D_MODEL), F32),
        compiler_params=_cparams(("parallel",)),
        name="final_norm",
    )(x, g)


HALO = 16
PAD = 8
assert all(w == 2 << g for g, w in enumerate(POOL_WINDOWS)) and HALO >= max(POOL_WINDOWS) and PAD >= HALO // 2


def _pool_layer_tail(x, u, gate, pooled_groups, mix_ref, scale_ref, wout_ref, fg_ref, y_ref):
    pieces = []
    for g in range(len(POOL_WINDOWS)):
        c0 = g * POOL_GROUP
        p = (pooled_groups[g] - u[:, c0:c0 + POOL_GROUP]).astype(BF16)
        pieces.append(_dot(p, mix_ref[g]))
    pm = jnp.concatenate(pieces, axis=1)
    o = (pm * scale_ref[...] * _silu(gate)).astype(BF16)
    y = x + _dot(o, wout_ref[...])
    if fg_ref is not None:
        ms = jnp.mean(y * y, axis=-1, keepdims=True)
        y = y * lax.rsqrt(ms + EPS) * fg_ref[...]
    y_ref[...] = y


def _pool_prompt_kernel(*refs, tp, final, pending):
    refs = list(refs)
    x_ref = refs.pop(0)
    prev = [refs.pop(0) for _ in range(3)] if pending else None
    g_ref, win_ref, mix_ref, scale_ref, wout_ref = [refs.pop(0) for _ in range(5)]
    fg_ref = refs.pop(0) if final else None
    y_ref, tail_ref, *bufs = refs
    i = pl.program_id(1)
    hist = slice(PAD, PAD + HALO)
    ext = slice(PAD, PAD + HALO + tp)
    tile = slice(PAD + HALO, PAD + HALO + tp)

    def back(rows, k):
        return slice(rows.start - k, rows.stop - k)

    @pl.when(i == 0)
    def _():
        for buf in bufs:
            buf[0:PAD + HALO, :] = jnp.zeros((PAD + HALO, buf.shape[1]), F32)

    x = x_ref[...]
    if pending:
        po_ref, pgate_ref, pw_ref = prev
        x = x + _dot((po_ref[...] * _silu(pgate_ref[...])).astype(BF16), pw_ref[...])
    hb = _rms_bf16(x, g_ref[...])
    u = _dot(hb, win_ref[:, :D_MODEL])
    gate = _dot(hb, win_ref[:, D_MODEL:])
    bufs[0][tile, :] = u
    pos = i * tp + lax.broadcasted_iota(jnp.int32, (tp, 1), 0)
    pooled = []
    for g, w in enumerate(POOL_WINDOWS):
        src, span = bufs[g], 1 << g
        if g + 1 < len(bufs):
            nxt = bufs[g + 1]
            nxt[ext, :] = src[ext, POOL_GROUP:] + src[back(ext, span), POOL_GROUP:]
        acc = src[tile, 0:POOL_GROUP] + src[back(tile, span), 0:POOL_GROUP]
        inv_cnt = 1.0 / jnp.minimum(pos + 1, w).astype(F32)
        pooled.append(acc * inv_cnt)
    _pool_layer_tail(x, u, gate, pooled, mix_ref, scale_ref, wout_ref, fg_ref, y_ref)
    bufs[0][hist, :] = u[tp - HALO:tp, :]

    @pl.when(i == pl.num_programs(1) - 1)
    def _():
        tail_ref[...] = u[tp - HALO:tp, :]


def _pool_prompt(x, pending, g, w_in, mix, scale, w_out, fg, batch, seq, tp):
    ns = seq // tp
    final = fg is not None
    tile = pl.BlockSpec((tp, D_MODEL), lambda b, i: (b * ns + i, 0))
    ins = [x] + (list(pending) if pending else []) + [g, w_in, mix, scale, w_out] + ([fg] if final else [])
    in_specs = [tile] + ([tile, tile, _full(pending[2].shape)] if pending else [])
    in_specs += [_full(a.shape) for a in ins[len(in_specs):]]
    return pl.pallas_call(
        functools.partial(_pool_prompt_kernel, tp=tp, final=final, pending=bool(pending)),
        grid=(batch, ns),
        in_specs=in_specs,
        out_specs=[tile, pl.BlockSpec((None, HALO, D_MODEL), lambda b, i: (b, 0, 0))],
        out_shape=[jax.ShapeDtypeStruct((batch * seq, D_MODEL), F32),
                   jax.ShapeDtypeStruct((batch, HALO, D_MODEL), F32)],
        scratch_shapes=[pltpu.VMEM((PAD + HALO + tp, D_MODEL - g * POOL_GROUP), F32)
                        for g in range(len(POOL_WINDOWS))],
        compiler_params=_cparams(("parallel", "arbitrary")),
        name="pool_prompt",
    )(*ins)


def _pool_sample_kernel(*refs, final):
    if final:
        x_ref, st_ref, g_ref, win_ref, mix_ref, scale_ref, wout_ref, fg_ref, y_ref, u_ref = refs
    else:
        x_ref, st_ref, g_ref, win_ref, mix_ref, scale_ref, wout_ref, y_ref, u_ref = refs
        fg_ref = None
    x = x_ref[...]
    hb = _rms_bf16(x, g_ref[...])
    u = _dot(hb, win_ref[:, :D_MODEL])
    gate = _dot(hb, win_ref[:, D_MODEL:])
    u_ref[...] = u
    pooled = []
    for g, w in enumerate(POOL_WINDOWS):
        c0 = g * POOL_GROUP
        acc = u[:, c0:c0 + POOL_GROUP]
        for k in range(1, w):
            acc = acc + st_ref[POOL_STATE - k, :, c0:c0 + POOL_GROUP]
        pooled.append(acc / float(w))
    _pool_layer_tail(x, u, gate, pooled, mix_ref, scale_ref, wout_ref, fg_ref, y_ref)


def _pool_sample(x, state_t, g, w_in, mix, scale, w_out, fg):
    m = x.shape[0]
    final = fg is not None
    ins = [x, state_t, g, w_in, mix, scale, w_out] + ([fg] if final else [])
    return pl.pallas_call(
        functools.partial(_pool_sample_kernel, final=final),
        grid=(1,),
        in_specs=[_full(a.shape) for a in ins],
        out_specs=[_full((m, D_MODEL)), _full((m, D_MODEL))],
        out_shape=[jax.ShapeDtypeStruct((m, D_MODEL), F32)] * 2,
        compiler_params=_cparams(("arbitrary",)),
        name="pool_sample",
    )(*ins)


def _t5_bucket_np(dist):
    n = np.maximum(dist, 0)
    max_exact = N_BUCKETS // 2
    nf = np.maximum(n, 1).astype(np.float32)
    large = max_exact + (np.log(nf / max_exact) / math.log(MAX_DISTANCE / max_exact)
                         * (N_BUCKETS - max_exact)).astype(np.int32)
    large = np.minimum(large, N_BUCKETS - 1)
    return np.where(n < max_exact, n, large)


def _swa_prompt_kernel(sink_ref, rb_ref, qt_ref, kp_ref, kc_ref, vp_ref, vc_ref, bucket_ref, gate_ref,
                       wout_ref, x_ref, y_ref, bias_ref, o_buf, *, nb):
    i = pl.program_id(1)
    cols = GROUP * WINDOW

    @pl.when(i == 0)
    def _():
        bucket = bucket_ref[...]
        hits = [bucket == bk for bk in range(N_BUCKETS)]
        for hd in range(N_HEADS):
            h, g = divmod(hd, GROUP)
            b = jnp.full((2 * WINDOW, WINDOW), NEG, F32)
            for bk in range(N_BUCKETS):
                b = jnp.where(hits[bk], rb_ref[bk, hd] * LOG2E, b)
            bias_ref[h, :, g * WINDOW:(g + 1) * WINDOW] = b

    key_row = lax.broadcasted_iota(jnp.int32, (2 * WINDOW, cols), 0)
    no_prev = jnp.logical_and(i == 0, key_row < WINDOW)
    sinks = [jnp.concatenate([jnp.full((1, WINDOW), sink_ref[h * GROUP + g] * LOG2E, F32) for g in range(GROUP)],
                             axis=1) for h in range(N_KV)]

    def logits(jb, h):
        if jb == 0:
            kband = jnp.concatenate([kp_ref[h], kc_ref[h, 0:WINDOW, :]], axis=0)
        else:
            kband = kc_ref[h, (jb - 1) * WINDOW:(jb + 1) * WINDOW, :]
        s = _dot(kband, qt_ref[h, jb]) + bias_ref[h]
        return jnp.where(no_prev, NEG, s) if jb == 0 else s

    def attend(jb, h, s):
        vprev = vp_ref[h] if jb == 0 else vc_ref[h, jb - 1]
        vband = jnp.concatenate([vprev, vc_ref[h, jb]], axis=1)
        m = jnp.maximum(jnp.max(s, axis=0, keepdims=True), sinks[h])
        p = jnp.exp2(s - m)
        denom = jnp.sum(p, axis=0, keepdims=True) + jnp.exp2(sinks[h] - m)
        ot = _dot(vband, p.astype(BF16)) * (1.0 / denom)
        o_heads = jnp.concatenate([ot[:, g * WINDOW:(g + 1) * WINDOW] for g in range(GROUP)], axis=0)
        o_buf[jb * WINDOW:(jb + 1) * WINDOW, h * GROUP * HEAD_DIM:(h + 1) * GROUP * HEAD_DIM] = o_heads.T

    units = [(jb, h) for jb in range(nb) for h in range(N_KV)]
    s = logits(*units[0])
    for u, unit in enumerate(units):
        s_next = logits(*units[u + 1]) if u + 1 < len(units) else None
        attend(*unit, s)
        s = s_next
    ob = (o_buf[...] * _silu(gate_ref[...])).astype(BF16)
    y_ref[...] = x_ref[...] + _dot(ob, wout_ref[...])


def _swa_prompt(qt, kh, vtb, gate, w_out, x, rel_bias, sinks, batch, seq, tm):
    ns = seq // tm
    nb = tm // WINDOW
    dist = np.arange(WINDOW)[None, :] + WINDOW - np.arange(2 * WINDOW)[:, None]
    bucket = np.where((dist >= 0) & (dist < WINDOW), _t5_bucket_np(dist), -1).astype(np.int32)
    tile = pl.BlockSpec((tm, D_MODEL), lambda b, i: (b * ns + i, 0))
    smem = pl.BlockSpec(memory_space=pltpu.SMEM)
    prev_blk = lambda i: jnp.maximum(i * nb - 1, 0)
    return pl.pallas_call(
        functools.partial(_swa_prompt_kernel, nb=nb),
        grid=(batch, ns),
        in_specs=[smem, smem,
                  pl.BlockSpec((None, N_KV, nb, HEAD_DIM, GROUP * WINDOW), lambda b, i: (b, 0, i, 0, 0)),
                  pl.BlockSpec((N_KV, WINDOW, HEAD_DIM), lambda b, i: (0, b * ns * nb + prev_blk(i), 0)),
                  pl.BlockSpec((N_KV, tm, HEAD_DIM), lambda b, i: (0, b * ns + i, 0)),
                  pl.BlockSpec((None, N_KV, None, HEAD_DIM, WINDOW), lambda b, i: (b, 0, prev_blk(i), 0, 0)),
                  pl.BlockSpec((None, N_KV, nb, HEAD_DIM, WINDOW), lambda b, i: (b, 0, i, 0, 0)),
                  _full(bucket.shape), tile, _full(w_out.shape), tile],
        out_specs=tile,
        out_shape=jax.ShapeDtypeStruct((batch * seq, D_MODEL), F32),
        scratch_shapes=[pltpu.VMEM((N_KV, 2 * WINDOW, GROUP * WINDOW), F32),
                        pltpu.VMEM((tm, D_MODEL), F32)],
        compiler_params=_cparams(("parallel", "arbitrary")),
        name="swa_prompt",
    )(sinks, rel_bias, qt, kh, kh, vtb, vtb, jnp.asarray(bucket), gate, w_out, x)


def _head_diag(o_full):
    out = jnp.zeros((N_HEADS, HEAD_DIM), F32)
    row_kv = lax.broadcasted_iota(jnp.int32, (N_HEADS, HEAD_DIM), 0) // GROUP
    for h in range(N_KV):
        out = out + jnp.where(row_kv == h, o_full[:, h * HEAD_DIM:(h + 1) * HEAD_DIM], 0.0)
    return out


def _swa_sample_kernel(qm_ref, kc_ref, vc_ref, kn_ref, vn_ref, knt_ref, vnt_ref, bias_ref, bias0_ref,
                       sink_ref, o_ref, wk_ref, wv_ref, *, bt):
    i = pl.program_id(0)
    lane = lax.broadcasted_iota(jnp.int32, (KV_WIDTH, WINDOW), 1)
    sink = sink_ref[...]
    for e in range(bt):
        b = i * bt + e
        kc = kc_ref[e]
        vc = vc_ref[e]
        qm = qm_ref[e]
        s = _dot(qm, kc.astype(BF16)) + bias_ref[...]
        s_n = jnp.sum(qm.astype(F32) * _bf16_round(kn_ref[e]), axis=1, keepdims=True) + bias0_ref[...]
        m = jnp.maximum(jnp.maximum(jnp.max(s, axis=1, keepdims=True), s_n), sink)
        p = jnp.exp(s - m)
        p_n = jnp.exp(s_n - m)
        denom = jnp.sum(p, axis=1, keepdims=True) + p_n + jnp.exp(sink - m)
        o_full = _dot_nt((p / denom).astype(BF16), vc.astype(BF16))
        o_full = o_full + _bf16_round(p_n / denom) * _bf16_round(vn_ref[e])
        o_ref[e] = _head_diag(o_full)
        kcol = jnp.sum(jnp.where(lane == b, knt_ref[...], 0.0), axis=1, keepdims=True)
        vcol = jnp.sum(jnp.where(lane == b, vnt_ref[...], 0.0), axis=1, keepdims=True)
        wk_ref[e] = jnp.where(lane == WINDOW - 1, kcol, pltpu.roll(kc, WINDOW - 1, 1))
        wv_ref[e] = jnp.where(lane == WINDOW - 1, vcol, pltpu.roll(vc, WINDOW - 1, 1))


def _swa_sample(qm, kc, vc, kn, vn, knt, vnt, bias_keys, bias0, sinks, bt):
    m = qm.shape[0]
    blk3 = lambda s1, s2: pl.BlockSpec((bt, s1, s2), lambda i: (i, 0, 0))
    cache = jax.ShapeDtypeStruct((m, KV_WIDTH, WINDOW), F32)
    return pl.pallas_call(
        functools.partial(_swa_sample_kernel, bt=bt),
        grid=(m // bt,),
        in_specs=[blk3(N_HEADS, KV_WIDTH), blk3(KV_WIDTH, WINDOW), blk3(KV_WIDTH, WINDOW),
                  blk3(1, KV_WIDTH), blk3(1, KV_WIDTH), _full(knt.shape), _full(vnt.shape),
                  _full(bias_keys.shape), _full(bias0.shape), _full(sinks.shape)],
        out_specs=[blk3(N_HEADS, HEAD_DIM), blk3(KV_WIDTH, WINDOW), blk3(KV_WIDTH, WINDOW)],
        out_shape=[jax.ShapeDtypeStruct((m, N_HEADS, HEAD_DIM), F32), cache, cache],
        compiler_params=_cparams(("parallel",)),
        name="swa_sample",
    )(qm, kc, vc, kn, vn, knt, vnt, bias_keys, bias0, sinks)


def _split3(x):
    hi = x.astype(BF16)
    r = x - hi.astype(F32)
    mid = r.astype(BF16)
    lo = (r - mid.astype(F32)).astype(BF16)
    return hi, mid, lo


def _cumsum_lanes_blocks(x, blk):
    n = x.shape[0]
    hi, mid, lo = _split3(x)
    r_i = lax.broadcasted_iota(jnp.int32, (blk, blk), 0)
    c_i = lax.broadcasted_iota(jnp.int32, (blk, blk), 1)
    upper = jnp.where(r_i <= c_i, 1.0, 0.0).astype(BF16)
    r = _dot(jnp.concatenate([hi, mid, lo], axis=0), upper)
    return r[0:n] + r[n:2 * n] + r[2 * n:]


def _fox_decay_kernel(x_ref, kin_ref, kout_ref, carry, *, blk):
    i = pl.program_id(1)

    @pl.when(i == 0)
    def _():
        carry[...] = jnp.zeros_like(carry)

    c = _cumsum_lanes_blocks(x_ref[...], blk) + carry[:, 0:1]
    carry[...] = jnp.broadcast_to(c[:, blk - 1:blk], carry.shape)
    hi, mid, lo = _split3(c * LOG2E)
    pieces = jnp.concatenate([hi, mid, lo], axis=0).astype(F32)
    pad = jnp.zeros((LANES - N_PIECES * N_HEADS, blk), F32)
    pieces_t = jnp.concatenate([pieces, pad], axis=0).T.astype(BF16)
    src = lax.broadcasted_iota(jnp.int32, (LANES, AUG), 0)
    dst = lax.broadcasted_iota(jnp.int32, (LANES, AUG), 1) - HEAD_DIM
    in_aug = jnp.logical_and(dst >= 0, dst < N_PIECES * GROUP)
    for h in range(N_KV):
        want = (dst >> GROUP_SHIFT) * N_HEADS + h * GROUP + (dst & (GROUP - 1))
        place = jnp.where(jnp.logical_and(in_aug, src == want), 1.0, 0.0).astype(BF16)
        kout_ref[h] = kin_ref[h] + _dot(pieces_t, place).astype(BF16)


def _fox_decay(logft, kaug, blk):
    b, h, s = logft.shape
    ns = s // blk
    kspec = pl.BlockSpec((N_KV, blk, AUG), lambda bi, i: (0, bi * ns + i, 0))
    return pl.pallas_call(
        functools.partial(_fox_decay_kernel, blk=blk),
        grid=(b, ns),
        in_specs=[pl.BlockSpec((None, h, blk), lambda bi, i: (bi, 0, i)), kspec],
        out_specs=kspec,
        out_shape=jax.ShapeDtypeStruct(kaug.shape, kaug.dtype),
        scratch_shapes=[pltpu.VMEM((h, LANES), F32)],
        input_output_aliases={1: 0},
        compiler_params=_cparams(("parallel", "arbitrary")),
        name="fox_decay",
    )(logft, kaug)


def _fox_prompt_kernel(qt_ref, k_ref, vt_ref, o_ref, *scratch, tq, tk):
    assert tq == tk
    t = pl.program_id(2)
    m_ref, acc_ref, sa_ref, sb_ref, ca_ref, cb_ref = scratch
    cols = GROUP * tq

    def reset():
        m_ref[...] = jnp.full(m_ref.shape, NEG, F32)
        acc_ref[...] = jnp.zeros(acc_ref.shape, F32)

    def logits(tile, kb, s_ref, c_ref):
        k0 = pl.multiple_of(kb * tk, tk)
        s = _dot(k_ref[pl.ds(k0, tk), :], qt_ref[tile])
        s_ref[...] = s
        c_ref[...] = jnp.max(s, axis=0, keepdims=True)

    def accumulate(kb, s_ref, c_ref, masked):
        s = s_ref[...]
        if masked:
            kpos = lax.broadcasted_iota(jnp.int32, (tk, cols), 0)
            qpos = lax.broadcasted_iota(jnp.int32, (tk, cols), 1) & (tq - 1)
            s = jnp.where(kpos <= qpos, s, NEG)
            cmax = jnp.max(s, axis=0, keepdims=True)
        else:
            cmax = c_ref[...]
        m_old = m_ref[...]
        m_new = jnp.maximum(m_old, cmax)
        alpha = jnp.exp2(m_old - m_new)
        p = jnp.exp2(s - m_new)
        acc_ref[...] = alpha * acc_ref[...] + _dot(vt_ref[kb], p.astype(BF16))
        m_ref[...] = m_new

    def finish(tile):
        o = acc_ref[0:HEAD_DIM, :] / acc_ref[HEAD_DIM:HEAD_DIM + 1, :]
        o_ref[tile * tq:(tile + 1) * tq, :] = jnp.concatenate(
            [o[:, g * tq:(g + 1) * tq] for g in range(GROUP)], axis=0).T

    def pairs(tile, first, second):
        def body(j, carry):
            kb = 2 * j
            logits(tile, kb + 1, *second)
            accumulate(kb, *first, False)
            logits(tile, kb + 2, *first)
            accumulate(kb + 1, *second, False)
            return carry
        lax.fori_loop(0, t, body, 0)

    buf_a, buf_b = (sa_ref, ca_ref), (sb_ref, cb_ref)
    reset()
    logits(0, 0, *buf_a)
    pairs(0, buf_a, buf_b)
    logits(1, 0, *buf_b)
    accumulate(2 * t, *buf_a, True)
    finish(0)
    reset()
    pairs(1, buf_b, buf_a)
    logits(1, 2 * t + 1, *buf_a)
    accumulate(2 * t, *buf_b, False)
    accumulate(2 * t + 1, *buf_a, True)
    finish(1)


def _fox_prompt(qt, kaug, vtb, batch, seq, tq):
    nq = seq // tq
    nk, tk = vtb.shape[2], vtb.shape[4]
    cols = GROUP * tq
    nt = nq // 2
    return pl.pallas_call(
        functools.partial(_fox_prompt_kernel, tq=tq, tk=tk),
        grid=(batch, N_KV, nt),
        in_specs=[pl.BlockSpec((None, None, 2, AUG, cols), lambda b, h, t: (b, h, t, 0, 0)),
                  pl.BlockSpec((None, seq, AUG), lambda b, h, t: (h, b, 0)),
                  pl.BlockSpec((None, None, nk, V_ROWS, tk), lambda b, h, t: (b, h, 0, 0, 0))],
        out_specs=pl.BlockSpec((2 * tq, GROUP * HEAD_DIM), lambda b, h, t: (b * nt + t, h)),
        out_shape=jax.ShapeDtypeStruct((batch * seq, D_MODEL), F32),
        scratch_shapes=[pltpu.VMEM((1, cols), F32),
                        pltpu.VMEM((V_ROWS, cols), F32),
                        pltpu.VMEM((tk, cols), F32), pltpu.VMEM((tk, cols), F32),
                        pltpu.VMEM((1, cols), F32), pltpu.VMEM((1, cols), F32)],
        compiler_params=_cparams(("parallel", "parallel", "arbitrary")),
        name="fox_prompt",
    )(qt, kaug, vtb)


def _fox_sample_kernel(pt_ref, qm_ref, kn_ref, vn_ref, fn_ref, ck_hbm, cv_hbm, cf_hbm, o_ref,
                       kbuf, vbuf, fbuf, sems, m_ref, l_ref, acc_ref, carry_ref, *, cp, nchunk):
    b = pl.program_id(0)
    c = pl.program_id(1)
    nb = pl.num_programs(0)
    step = b * nchunk + c
    slot = step % 2

    def copies(bb, cc, sl, p):
        page = pt_ref[bb, cc * cp + p]
        return (pltpu.make_async_copy(ck_hbm.at[page], kbuf.at[sl, p], sems.at[sl, 0]),
                pltpu.make_async_copy(cv_hbm.at[page], vbuf.at[sl, p], sems.at[sl, 1]),
                pltpu.make_async_copy(cf_hbm.at[page], fbuf.at[sl, p], sems.at[sl, 2]))

    def issue(bb, cc, sl):
        def body(p, carry):
            for cpy in copies(bb, cc, sl, p):
                cpy.start()
            return carry
        lax.fori_loop(0, cp, body, 0)

    @pl.when(step == 0)
    def _():
        issue(b, c, slot)

    @pl.when(step + 1 < nb * nchunk)
    def _():
        nxt = step + 1
        issue(nxt // nchunk, nxt % nchunk, 1 - slot)

    def wait_body(p, carry):
        for cpy in copies(b, c, slot, p):
            cpy.wait()
        return carry
    lax.fori_loop(0, cp, wait_body, 0)

    @pl.when(c == 0)
    def _():
        m_ref[...] = jnp.full(m_ref.shape, NEG, F32)
        l_ref[...] = jnp.zeros(l_ref.shape, F32)
        acc_ref[...] = jnp.zeros(acc_ref.shape, F32)
        carry_ref[...] = jnp.zeros(carry_ref.shape, F32)

    qm = qm_ref[...]
    cin = _cumsum_lanes_blocks(fbuf[slot].reshape(cp * N_HEADS, PAGE), PAGE).reshape(cp, N_HEADS, PAGE)
    carry = carry_ref[...]
    cs = []
    for p in range(cp):
        cs.append(cin[p] + carry)
        carry = carry + cin[p][:, PAGE - 1:PAGE]
    carry_ref[...] = carry
    kcat = jnp.concatenate([kbuf[slot, p] for p in range(cp)], axis=1).astype(BF16)
    t = _dot(qm, kcat) - jnp.concatenate(cs, axis=1)
    m_old = m_ref[...]
    m_new = jnp.maximum(m_old, jnp.max(t, axis=1, keepdims=True))
    alpha = jnp.exp(m_old - m_new)
    p = jnp.exp(t - m_new)
    l_new = alpha * l_ref[...] + jnp.sum(p, axis=1, keepdims=True)
    vcat = jnp.concatenate([vbuf[slot, p_] for p_ in range(cp)], axis=1).astype(BF16)
    acc_new = alpha * acc_ref[...] + _dot_nt(p.astype(BF16), vcat)
    m_ref[...] = m_new
    l_ref[...] = l_new
    acc_ref[...] = acc_new

    @pl.when(c == nchunk - 1)
    def _():
        s_n = jnp.sum(qm.astype(F32) * _bf16_round(kn_ref[...]), axis=1, keepdims=True)
        t_n = s_n - (carry + fn_ref[...])
        m_f = jnp.maximum(m_new, t_n)
        a = jnp.exp(m_new - m_f)
        p_n = jnp.exp(t_n - m_f)
        l_f = a * l_new + p_n
        acc = a * acc_new + _bf16_round(p_n) * _bf16_round(vn_ref[...])
        o_ref[...] = _head_diag(acc / l_f)


def _fox_sample(page_table, qm, kn, vn, fn, ck, cv, cf, cp):
    m, n_pages = page_table.shape
    nchunk = n_pages // cp
    per_b = lambda s1, s2: pl.BlockSpec((None, s1, s2), lambda b, c, pt: (b, 0, 0))
    any_spec = pl.BlockSpec(memory_space=pl.ANY)
    return pl.pallas_call(
        functools.partial(_fox_sample_kernel, cp=cp, nchunk=nchunk),
        grid_spec=pltpu.PrefetchScalarGridSpec(
            num_scalar_prefetch=1,
            grid=(m, nchunk),
            in_specs=[per_b(N_HEADS, KV_WIDTH), per_b(1, KV_WIDTH), per_b(1, KV_WIDTH), per_b(N_HEADS, 1),
                      any_spec, any_spec, any_spec],
            out_specs=per_b(N_HEADS, HEAD_DIM),
            scratch_shapes=[pltpu.VMEM((2, cp, KV_WIDTH, PAGE), F32),
                            pltpu.VMEM((2, cp, KV_WIDTH, PAGE), F32),
                            pltpu.VMEM((2, cp, N_HEADS, PAGE), F32),
                            pltpu.SemaphoreType.DMA((2, 3)),
                            pltpu.VMEM((N_HEADS, 1), F32), pltpu.VMEM((N_HEADS, 1), F32),
                            pltpu.VMEM((N_HEADS, KV_WIDTH), F32), pltpu.VMEM((N_HEADS, 1), F32)],
        ),
        out_shape=jax.ShapeDtypeStruct((m, N_HEADS, HEAD_DIM), F32),
        compiler_params=_cparams(("arbitrary", "arbitrary")),
        name="fox_sample",
    )(page_table, qm, kn, vn, fn, ck, cv, cf)


def _head_rows_q(q_rows):
    m = q_rows.shape[0]
    q4 = q_rows.reshape(m, N_KV, GROUP, HEAD_DIM)
    eye = jnp.eye(N_KV, dtype=q_rows.dtype)
    return (q4[:, :, :, None, :] * eye[None, :, None, :, None]).reshape(m, N_HEADS, KV_WIDTH)


def _tokens_last(x):
    lead = x.shape[:-3]
    n = len(lead)
    xt = jnp.transpose(x, tuple(range(n)) + (n + 1, n + 2, n))
    return xt.reshape(lead + (KV_WIDTH, x.shape[-3]))


def _tokens_first(xt):
    lead = xt.shape[:-2]
    n = len(lead)
    x4 = xt.reshape(lead + (N_KV, HEAD_DIM, xt.shape[-1]))
    return jnp.transpose(x4, tuple(range(n)) + (n + 2, n, n + 1))


def kernel(x_prompt, x_sample, state_pool, cache_win_k, cache_win_v, cache_fox_k, cache_fox_v,
           cache_fox_logf, page_table, norm_g, final_norm_g, rel_bias, pool_w_in, pool_mix,
           pool_scale, pool_w_out, swa_w_in, swa_sinks, swa_w_out, fox_w_in, fox_f_bias, fox_w_out):
    batch, seq, _ = x_prompt.shape
    db = x_sample.shape[0]
    depth = norm_g.shape[0]
    mp = batch * seq
    tm_p, tm_s = 512, db

    xp = x_prompt.reshape(mp, D_MODEL)
    xs = x_sample.reshape(db, D_MODEL)

    rb = rel_bias.astype(F32)
    dist_keys = WINDOW - np.arange(WINDOW)
    bias_keys = jnp.where((dist_keys < WINDOW)[None, :], rb[_t5_bucket_np(dist_keys)].T, NEG)
    bias0 = rb[0].reshape(N_HEADS, 1)
    fg = final_norm_g.reshape(1, D_MODEL)

    pool_p, pool_s = [], []
    wk_p, wv_p, wk_s, wv_s = [], [], [], []
    fk_p, fv_p, fl_p, fk_s, fv_s, fl_s = [], [], [], [], [], []
    pending = None
    for i in range(depth):
        kind, j = i % 3, i // 3
        g = norm_g[i].reshape(1, D_MODEL)
        if kind == 0:
            w_in = pool_w_in[j].astype(BF16)
            mix = pool_mix[j].astype(BF16)
            scale = pool_scale[j].reshape(1, D_MODEL)
            w_out = pool_w_out[j].astype(BF16)
            fg_l = fg if i == depth - 1 else None
            xp, u_tail = _pool_prompt(xp, pending, g, w_in, mix, scale, w_out, fg_l, batch, seq, tm_p)
            pending = None
            pool_p.append(u_tail[:, HALO - POOL_STATE:])
            xs, u_s = _pool_sample(xs, jnp.transpose(state_pool[j], (1, 0, 2)), g, w_in, mix, scale, w_out, fg_l)
            pool_s.append(jnp.concatenate([state_pool[j][:, 1:], u_s[:, None, :]], axis=1))
        elif kind == 1:
            w_in = swa_w_in[j].astype(BF16)
            w_out = swa_w_out[j].astype(BF16)
            sinks = swa_sinks[j].astype(F32)
            qt_p, kh_p, kt_p, vt_p, vtb_p, gate_p = _swa_proj(xp, g, w_in, tm_p, batch, seq, True)
            xp = _swa_prompt(qt_p, kh_p, vtb_p, gate_p, w_out, xp, rb, sinks, batch, seq, tm_p)
            wk_p.append(_tokens_first(kt_p[:, :, seq - WINDOW:]))
            wv_p.append(_tokens_first(vt_p[:, :, seq - WINDOW:]))
            q_s, k_s, v_s, kt_s, vt_s, gate_s = _swa_proj(xs, g, w_in, tm_s, 1, db, False)
            o_s, wk, wv = _swa_sample(_head_rows_q(q_s), _tokens_last(cache_win_k[j]),
                                      _tokens_last(cache_win_v[j]), k_s[:, None, :], v_s[:, None, :],
                                      kt_s[0], vt_s[0], bias_keys, bias0, sinks.reshape(N_HEADS, 1),
                                      SWA_ROWS_PER_STEP)
            xs = _out_proj(o_s.reshape(db, D_MODEL), gate_s, w_out, xs, tm_s)
            wk_s.append(_tokens_first(wk))
            wv_s.append(_tokens_first(wv))
        else:
            w_full = fox_w_in[j]
            nqkv = D_MODEL + 2 * KV_WIDTH
            w_in = jnp.concatenate([w_full[:, :nqkv], w_full[:, nqkv + N_HEADS:]], axis=1).astype(BF16)
            wf = jnp.pad(w_full[:, nqkv:nqkv + N_HEADS], ((0, 0), (0, LANES - N_HEADS))).astype(BF16)
            fb = jnp.pad(fox_f_bias[j].astype(F32), (0, LANES - N_HEADS)).reshape(1, LANES)
            w_out = fox_w_out[j].astype(BF16)
            qt, kt_p, vt_p, kaug, vtb, logft_p, gate_p = _fox_proj(xp, g, w_in, wf, fb, FOX_TILE, batch, seq, True)
            kaug = _fox_decay(logft_p, kaug, FOX_TILE)
            o_p = _fox_prompt(qt, kaug, vtb, batch, seq, FOX_TILE)
            if i + 1 < depth and (i + 1) % 3 == 0:
                pending = (o_p, gate_p, w_out)
            else:
                xp = _out_proj(o_p, gate_p, w_out, xp, tm_p)
            fk_p.append(_tokens_first(kt_p))
            fv_p.append(_tokens_first(vt_p))
            fl_p.append(jnp.transpose(logft_p, (0, 2, 1)))
            q_s, k_s, v_s, kt_s, vt_s, logf_s, logft_s, gate_s = _fox_proj(
                xs, g, w_in, wf, fb, tm_s, 1, db, False)
            o_s = _fox_sample(page_table, _head_rows_q(q_s), k_s[:, None, :], v_s[:, None, :],
                              logf_s[:, :, None], _tokens_last(cache_fox_k[j]),
                              _tokens_last(cache_fox_v[j]), jnp.transpose(cache_fox_logf[j], (0, 2, 1)),
                              FOX_PAGES_PER_STEP)
            xs = _out_proj(o_s.reshape(db, D_MODEL), gate_s, w_out, xs, tm_s)
            fk_s.append(_tokens_first(kt_s[0])[:, None])
            fv_s.append(_tokens_first(vt_s[0])[:, None])
            fl_s.append(logft_s[0].T[:, None, :])

    if (depth - 1) % 3 != 0:
        xp, xs = _final_norm(xp, fg, tm_p), _final_norm(xs, fg, tm_s)
    y_prompt = xp.reshape(batch, seq, D_MODEL)
    y_sample = xs.reshape(db, 1, D_MODEL)
    return (y_prompt, y_sample, jnp.stack(pool_p), jnp.stack(pool_s), jnp.stack(wk_p), jnp.stack(wv_p),
            jnp.stack(wk_s), jnp.stack(wv_s), jnp.stack(fk_p), jnp.stack(fv_p), jnp.stack(fl_p),
            jnp.stack(fk_s), jnp.stack(fv_s), jnp.stack(fl_s))
```

```python
import functools
import math

import numpy as np
import jax
import jax.numpy as jnp
from jax import lax
from jax.experimental import pallas as pl
from jax.experimental.pallas import tpu as pltpu

D_MODEL = 1024
HEAD_DIM = 64
N_HEADS = 16
N_KV = 4
GROUP = 4
KV_WIDTH = N_KV * HEAD_DIM
POOL_WINDOWS = (2, 4, 8, 16)
POOL_GROUP = 256
POOL_STATE = 15
WINDOW = 128
N_BUCKETS = 32
MAX_DISTANCE = 128
SCALE = HEAD_DIM ** -0.5
EPS = 1e-6
NEG = -1e30
PAGE = 128
LANES = 128

BF16 = jnp.bfloat16
F32 = jnp.float32

VMEM_LIMIT = 56 * 1024 * 1024
VMEM_LIMIT_RIDER = 62 * 1024 * 1024


def _cparams(sem):
    return pltpu.CompilerParams(dimension_semantics=sem, vmem_limit_bytes=VMEM_LIMIT)


def _rms_bf16(x, g):
    ms = jnp.mean(x * x, axis=-1, keepdims=True)
    return (x * lax.rsqrt(ms + EPS) * g).astype(BF16)


def _silu(x):
    return x * jax.nn.sigmoid(x)


def _dot(a, b):
    return jnp.dot(a, b, preferred_element_type=F32)


def _dot_nt(a, b):
    return lax.dot_general(a, b, (((1,), (1,)), ((), ())), preferred_element_type=F32)


def _bf16_round(x):
    return x.astype(BF16).astype(F32)


def _full(shape):
    n = len(shape)
    return pl.BlockSpec(shape, lambda *_: (0,) * n)


LOG2E = 1.4426950408889634


def _swa_proj_kernel(x_ref, g_ref, w_ref, *out_refs, prompt):
    if prompt:
        qt_ref, kh_ref, kt_ref, vt_ref, vtb_ref, gate_ref = out_refs
    else:
        q_ref, k_ref, v_ref, kt_ref, vt_ref, gate_ref = out_refs
    hb = _rms_bf16(x_ref[...], g_ref[...])
    q = _dot(hb, w_ref[:, :D_MODEL])
    k = _dot(hb, w_ref[:, D_MODEL:D_MODEL + KV_WIDTH])
    v = _dot(hb, w_ref[:, D_MODEL + KV_WIDTH:D_MODEL + 2 * KV_WIDTH])
    vt = v.T
    kt_ref[...] = k.T
    vt_ref[...] = vt
    if prompt:
        tm = q.shape[0]
        qt = (q * (SCALE * LOG2E)).T.astype(BF16)
        for hd in range(N_HEADS):
            h, g = divmod(hd, GROUP)
            for jb in range(tm // WINDOW):
                qt_ref[h, jb, :, g * WINDOW:(g + 1) * WINDOW] = (
                    qt[hd * HEAD_DIM:(hd + 1) * HEAD_DIM, jb * WINDOW:(jb + 1) * WINDOW])
        for h in range(N_KV):
            kh_ref[h] = k[:, h * HEAD_DIM:(h + 1) * HEAD_DIM].astype(BF16)
            vth = vt[h * HEAD_DIM:(h + 1) * HEAD_DIM, :].astype(BF16)
            for jb in range(tm // WINDOW):
                vtb_ref[h, jb] = vth[:, jb * WINDOW:(jb + 1) * WINDOW]
    else:
        q_ref[...] = (q * SCALE).astype(BF16)
        k_ref[...] = k
        v_ref[...] = v
    gate_ref[...] = _dot(hb, w_ref[:, D_MODEL + 2 * KV_WIDTH:])


def _swa_proj(x, g, w, tm, batch, seq, prompt):
    m = x.shape[0]
    ns = seq // tm
    nb = tm // WINDOW
    row = lambda i: (i, 0)
    sds = jax.ShapeDtypeStruct
    tcol = pl.BlockSpec((None, KV_WIDTH, tm), lambda i: (i // ns, 0, i % ns))
    blocks = lambda last: pl.BlockSpec((None, N_KV, nb, HEAD_DIM, last), lambda i: (i // ns, 0, i % ns, 0, 0))
    kv_t_shape = [sds((batch, KV_WIDTH, seq), F32)] * 2
    if prompt:
        out_specs = [blocks(GROUP * WINDOW), pl.BlockSpec((N_KV, tm, HEAD_DIM), lambda i: (0, i, 0)),
                     tcol, tcol, blocks(WINDOW)]
        out_shape = ([sds((batch, N_KV, seq // WINDOW, HEAD_DIM, GROUP * WINDOW), BF16),
                      sds((N_KV, m, HEAD_DIM), BF16)] + kv_t_shape
                     + [sds((batch, N_KV, seq // WINDOW, HEAD_DIM, WINDOW), BF16)])
    else:
        out_specs = [pl.BlockSpec((tm, D_MODEL), row), pl.BlockSpec((tm, KV_WIDTH), row),
                     pl.BlockSpec((tm, KV_WIDTH), row), tcol, tcol]
        out_shape = [sds((m, D_MODEL), BF16), sds((m, KV_WIDTH), F32), sds((m, KV_WIDTH), F32)] + kv_t_shape
    out_specs.append(pl.BlockSpec((tm, D_MODEL), row))
    out_shape.append(sds((m, D_MODEL), F32))
    return pl.pallas_call(
        functools.partial(_swa_proj_kernel, prompt=prompt),
        grid=(m // tm,),
        in_specs=[pl.BlockSpec((tm, D_MODEL), row), _full((1, D_MODEL)), _full(w.shape)],
        out_specs=out_specs,
        out_shape=out_shape,
        compiler_params=_cparams(("parallel",)),
        name="swa_proj",
    )(x, g, w)


def _log_sigmoid(x):
    return -(jnp.maximum(-x, 0.0) + jnp.log1p(jnp.exp(-jnp.abs(x))))


AUG = 128
N_PIECES = 3
SWA_ROWS_PER_STEP = 16
V_ROWS = HEAD_DIM + 16
FOX_PAGES_PER_STEP = 64
FOX_TILE = 512
FOX_KEYS = 512
FOX_QUERIES = 256
GROUP_SHIFT = GROUP.bit_length() - 1


def _fox_proj_kernel(x_ref, g_ref, w_ref, wf_ref, fb_ref, *out_refs, prompt):
    if prompt:
        qt_ref, kt_ref, vt_ref, kaug_ref, vtb_ref, logft_ref, gate_ref = out_refs
    else:
        q_ref, k_ref, v_ref, kt_ref, vt_ref, logf_ref, logft_ref, gate_ref = out_refs
    hb = _rms_bf16(x_ref[...], g_ref[...])
    q = _dot(hb, w_ref[:, :D_MODEL])
    k = _dot(hb, w_ref[:, D_MODEL:D_MODEL + KV_WIDTH])
    v = _dot(hb, w_ref[:, D_MODEL + KV_WIDTH:D_MODEL + 2 * KV_WIDTH])
    vt = v.T
    kt_ref[...] = k.T
    vt_ref[...] = vt
    logf = _log_sigmoid(_dot(hb, wf_ref[...]) + fb_ref[...])
    logft_ref[...] = logf.T[:N_HEADS, :]
    if prompt:
        tm = q.shape[0]
        qt = (q * (SCALE * LOG2E)).T.astype(BF16)
        r = lax.broadcasted_iota(jnp.int32, (AUG - HEAD_DIM, tm), 0)
        for hd in range(N_HEADS):
            h, g = divmod(hd, GROUP)
            cols = slice(g * tm, (g + 1) * tm)
            qt_ref[h, 0:HEAD_DIM, cols] = qt[hd * HEAD_DIM:(hd + 1) * HEAD_DIM, :]
            pick = jnp.logical_and(r < N_PIECES * GROUP, (r & (GROUP - 1)) == g)
            qt_ref[h, HEAD_DIM:AUG, cols] = jnp.where(pick, -1.0, 0.0).astype(BF16)
        zeros = jnp.zeros((tm, AUG - HEAD_DIM), F32)
        for h in range(N_KV):
            kaug_ref[h] = jnp.concatenate([k[:, h * HEAD_DIM:(h + 1) * HEAD_DIM], zeros], axis=1).astype(BF16)
            ones_row = lax.broadcasted_iota(jnp.int32, (V_ROWS - HEAD_DIM, tm), 0) == 0
            vth = jnp.concatenate([vt[h * HEAD_DIM:(h + 1) * HEAD_DIM, :],
                                   jnp.where(ones_row, 1.0, 0.0)], axis=0).astype(BF16)
            for kb in range(tm // FOX_KEYS):
                vtb_ref[h, kb] = vth[:, kb * FOX_KEYS:(kb + 1) * FOX_KEYS]
    else:
        q_ref[...] = (q * SCALE).astype(BF16)
        k_ref[...] = k
        v_ref[...] = v
        logf_ref[...] = logf[:, :N_HEADS]
    gate_ref[...] = _dot(hb, w_ref[:, D_MODEL + 2 * KV_WIDTH:])


def _fox_proj(x, g, w, wf, fb, tm, batch, seq, prompt):
    m = x.shape[0]
    ns = seq // tm
    row = lambda i: (i, 0)
    tcol = lambda r: pl.BlockSpec((None, r, tm), lambda i: (i // ns, 0, i % ns))
    sds = jax.ShapeDtypeStruct
    kv_t = [tcol(KV_WIDTH), tcol(KV_WIDTH)]
    kv_t_shape = [sds((batch, KV_WIDTH, seq), F32)] * 2
    if prompt:
        out_specs = ([pl.BlockSpec((None, N_KV, None, AUG, GROUP * tm), lambda i: (i // ns, 0, i % ns, 0, 0))] + kv_t
                     + [pl.BlockSpec((N_KV, tm, AUG), lambda i: (0, i, 0)),
                        pl.BlockSpec((None, N_KV, tm // FOX_KEYS, V_ROWS, FOX_KEYS),
                                     lambda i: (i // ns, 0, i % ns, 0, 0)),
                        tcol(N_HEADS)])
        out_shape = ([sds((batch, N_KV, ns, AUG, GROUP * tm), BF16)] + kv_t_shape
                     + [sds((N_KV, m, AUG), BF16),
                        sds((batch, N_KV, seq // FOX_KEYS, V_ROWS, FOX_KEYS), BF16),
                        sds((batch, N_HEADS, seq), F32)])
    else:
        out_specs = ([pl.BlockSpec((tm, D_MODEL), row), pl.BlockSpec((tm, KV_WIDTH), row),
                      pl.BlockSpec((tm, KV_WIDTH), row)] + kv_t
                     + [pl.BlockSpec((tm, N_HEADS), row), tcol(N_HEADS)])
        out_shape = ([sds((m, D_MODEL), BF16), sds((m, KV_WIDTH), F32), sds((m, KV_WIDTH), F32)]
                     + kv_t_shape + [sds((m, N_HEADS), F32), sds((batch, N_HEADS, seq), F32)])
    out_specs.append(pl.BlockSpec((tm, D_MODEL), row))
    out_shape.append(sds((m, D_MODEL), F32))
    return pl.pallas_call(
        functools.partial(_fox_proj_kernel, prompt=prompt),
        grid=(m // tm,),
        in_specs=[pl.BlockSpec((tm, D_MODEL), row), _full((1, D_MODEL)), _full(w.shape),
                  _full(wf.shape), _full(fb.shape)],
        out_specs=out_specs,
        out_shape=out_shape,
        compiler_params=_cparams(("parallel",)),
        name="fox_proj",
    )(x, g, w, wf, fb)


def _out_proj_kernel(*refs, gated):
    if gated:
        o_ref, gate_ref, w_ref, x_ref, y_ref = refs
        ob = (o_ref[...] * _silu(gate_ref[...])).astype(BF16)
    else:
        o_ref, w_ref, x_ref, y_ref = refs
        ob = o_ref[...]
    y_ref[...] = x_ref[...] + _dot(ob, w_ref[...])


def _out_proj(o, gate, w, x, tm):
    m = x.shape[0]
    row = lambda i: (i, 0)
    tile = pl.BlockSpec((tm, D_MODEL), row)
    gated = gate is not None
    ins = [o, gate, w, x] if gated else [o, w, x]
    in_specs = [tile, tile, _full(w.shape), tile] if gated else [tile, _full(w.shape), tile]
    return pl.pallas_call(
        functools.partial(_out_proj_kernel, gated=gated),
        grid=(m // tm,),
        in_specs=in_specs,
        out_specs=tile,
        out_shape=jax.ShapeDtypeStruct((m, D_MODEL), F32),
        compiler_params=_cparams(("parallel",)),
        name="out_proj",
    )(*ins)


def _final_norm_kernel(x_ref, g_ref, y_ref):
    x = x_ref[...]
    ms = jnp.mean(x * x, axis=-1, keepdims=True)
    y_ref[...] = x * lax.rsqrt(ms + EPS) * g_ref[...]


def _final_norm(x, g, tm):
    m = x.shape[0]
    row = lambda i: (i, 0)
    return pl.pallas_call(
        _final_norm_kernel,
        grid=(m // tm,),
        in_specs=[pl.BlockSpec((tm, D_MODEL), row), _full((1, D_MODEL))],
        out_specs=pl.BlockSpec((tm, D_MODEL), row),
        out_shape=jax.ShapeDtypeStruct((m, D_MODEL), F32),
        compiler_params=_cparams(("parallel",)),
        name="final_norm",
    )(x, g)


HALO = 16
PAD = 8
assert all(w == 2 << g for g, w in enumerate(POOL_WINDOWS)) and HALO >= max(POOL_WINDOWS) and PAD >= HALO // 2


def _pool_layer_tail(x, u, gate, pooled_groups, mix_ref, scale_ref, wout_ref, fg_ref, y_ref):
    pieces = []
    for g in range(len(POOL_WINDOWS)):
        c0 = g * POOL_GROUP
        p = (pooled_groups[g] - u[:, c0:c0 + POOL_GROUP]).astype(BF16)
        pieces.append(_dot(p, mix_ref[g]))
    pm = jnp.concatenate(pieces, axis=1)
    o = (pm * scale_ref[...] * _silu(gate)).astype(BF16)
    y = x + _dot(o, wout_ref[...])
    if fg_ref is not None:
        ms = jnp.mean(y * y, axis=-1, keepdims=True)
        y = y * lax.rsqrt(ms + EPS) * fg_ref[...]
    y_ref[...] = y


def _n_pool_inputs(final, pending):
    return 6 + (3 if pending else 0) + (1 if final else 0)


def _pool_prompt_kernel(*refs, tp, final, pending):
    n_in = _n_pool_inputs(final, pending)
    _pool_tile(pl.program_id(1), pl.num_programs(1), refs[:n_in], refs[n_in:n_in + 2], refs[n_in + 2:],
               tp=tp, final=final, pending=pending)


def _pool_tile(i, n_tiles, in_refs, out_refs, bufs, *, tp, final, pending):
    refs = list(in_refs)
    x_ref = refs.pop(0)
    prev = [refs.pop(0) for _ in range(3)] if pending else None
    g_ref, win_ref, mix_ref, scale_ref, wout_ref = [refs.pop(0) for _ in range(5)]
    fg_ref = refs.pop(0) if final else None
    y_ref, tail_ref = out_refs
    hist = slice(PAD, PAD + HALO)
    ext = slice(PAD, PAD + HALO + tp)
    tile = slice(PAD + HALO, PAD + HALO + tp)

    def back(rows, k):
        return slice(rows.start - k, rows.stop - k)

    @pl.when(i == 0)
    def _():
        for buf in bufs:
            buf[0:PAD + HALO, :] = jnp.zeros((PAD + HALO, buf.shape[1]), F32)

    x = x_ref[...]
    if pending:
        po_ref, pgate_ref, pw_ref = prev
        x = x + _dot((po_ref[...] * _silu(pgate_ref[...])).astype(BF16), pw_ref[...])
    hb = _rms_bf16(x, g_ref[...])
    u = _dot(hb, win_ref[:, :D_MODEL])
    gate = _dot(hb, win_ref[:, D_MODEL:])
    bufs[0][tile, :] = u
    pos = i * tp + lax.broadcasted_iota(jnp.int32, (tp, 1), 0)
    pooled = []
    for g, w in enumerate(POOL_WINDOWS):
        src, span = bufs[g], 1 << g
        if g + 1 < len(bufs):
            nxt = bufs[g + 1]
            nxt[ext, :] = src[ext, POOL_GROUP:] + src[back(ext, span), POOL_GROUP:]
        acc = src[tile, 0:POOL_GROUP] + src[back(tile, span), 0:POOL_GROUP]
        inv_cnt = 1.0 / jnp.minimum(pos + 1, w).astype(F32)
        pooled.append(acc * inv_cnt)
    _pool_layer_tail(x, u, gate, pooled, mix_ref, scale_ref, wout_ref, fg_ref, y_ref)
    bufs[0][hist, :] = u[tp - HALO:tp, :]

    @pl.when(i == n_tiles - 1)
    def _():
        tail_ref[...] = u[tp - HALO:tp, :]


def _pool_prompt_operands(x, pending, g, w_in, mix, scale, w_out, fg, tp, row_tile, batch_of, single_buffer):
    final = fg is not None
    tile = pl.BlockSpec((tp, D_MODEL), lambda *idx: (row_tile(*idx), 0))
    whole = (lambda a: pl.BlockSpec(a.shape, lambda *_: (0,) * a.ndim, pipeline_mode=pl.Buffered(1))
             if single_buffer else _full(a.shape))
    ins = [x] + (list(pending) if pending else []) + [g, w_in, mix, scale, w_out] + ([fg] if final else [])
    in_specs = [tile] + ([tile, tile, whole(pending[2])] if pending else [])
    in_specs += [whole(a) for a in ins[len(in_specs):]]
    out_specs = [tile, pl.BlockSpec((None, HALO, D_MODEL), lambda *idx: (batch_of(*idx), 0, 0))]
    scratch = [pltpu.VMEM((PAD + HALO + tp, D_MODEL - g * POOL_GROUP), F32) for g in range(len(POOL_WINDOWS))]
    return ins, in_specs, out_specs, scratch, final


def _pool_prompt(x, pending, g, w_in, mix, scale, w_out, fg, batch, seq, tp):
    ns = seq // tp
    ins, in_specs, out_specs, scratch, final = _pool_prompt_operands(
        x, pending, g, w_in, mix, scale, w_out, fg, tp, lambda b, i: b * ns + i, lambda b, i: b, False)
    return pl.pallas_call(
        functools.partial(_pool_prompt_kernel, tp=tp, final=final, pending=bool(pending)),
        grid=(batch, ns),
        in_specs=in_specs,
        out_specs=out_specs,
        out_shape=[jax.ShapeDtypeStruct((batch * seq, D_MODEL), F32),
                   jax.ShapeDtypeStruct((batch, HALO, D_MODEL), F32)],
        scratch_shapes=scratch,
        compiler_params=_cparams(("parallel", "arbitrary")),
        name="pool_prompt",
    )(*ins)


def _pool_sample_kernel(*refs, final):
    if final:
        x_ref, st_ref, g_ref, win_ref, mix_ref, scale_ref, wout_ref, fg_ref, y_ref, u_ref = refs
    else:
        x_ref, st_ref, g_ref, win_ref, mix_ref, scale_ref, wout_ref, y_ref, u_ref = refs
        fg_ref = None
    x = x_ref[...]
    hb = _rms_bf16(x, g_ref[...])
    u = _dot(hb, win_ref[:, :D_MODEL])
    gate = _dot(hb, win_ref[:, D_MODEL:])
    u_ref[...] = u
    pooled = []
    for g, w in enumerate(POOL_WINDOWS):
        c0 = g * POOL_GROUP
        acc = u[:, c0:c0 + POOL_GROUP]
        for k in range(1, w):
            acc = acc + st_ref[POOL_STATE - k, :, c0:c0 + POOL_GROUP]
        pooled.append(acc / float(w))
    _pool_layer_tail(x, u, gate, pooled, mix_ref, scale_ref, wout_ref, fg_ref, y_ref)


def _pool_sample(x, state_t, g, w_in, mix, scale, w_out, fg):
    m = x.shape[0]
    final = fg is not None
    ins = [x, state_t, g, w_in, mix, scale, w_out] + ([fg] if final else [])
    return pl.pallas_call(
        functools.partial(_pool_sample_kernel, final=final),
        grid=(1,),
        in_specs=[_full(a.shape) for a in ins],
        out_specs=[_full((m, D_MODEL)), _full((m, D_MODEL))],
        out_shape=[jax.ShapeDtypeStruct((m, D_MODEL), F32)] * 2,
        compiler_params=_cparams(("arbitrary",)),
        name="pool_sample",
    )(*ins)


def _t5_bucket_np(dist):
    n = np.maximum(dist, 0)
    max_exact = N_BUCKETS // 2
    nf = np.maximum(n, 1).astype(np.float32)
    large = max_exact + (np.log(nf / max_exact) / math.log(MAX_DISTANCE / max_exact)
                         * (N_BUCKETS - max_exact)).astype(np.int32)
    large = np.minimum(large, N_BUCKETS - 1)
    return np.where(n < max_exact, n, large)


def _swa_prompt_kernel(sink_ref, rb_ref, qt_ref, kp_ref, kc_ref, vp_ref, vc_ref, bucket_ref, gate_ref,
                       wout_ref, x_ref, y_ref, bias_ref, o_buf, *, nb):
    i = pl.program_id(1)
    cols = GROUP * WINDOW

    @pl.when(i == 0)
    def _():
        bucket = bucket_ref[...]
        hits = [bucket == bk for bk in range(N_BUCKETS)]
        for hd in range(N_HEADS):
            h, g = divmod(hd, GROUP)
            b = jnp.full((2 * WINDOW, WINDOW), NEG, F32)
            for bk in range(N_BUCKETS):
                b = jnp.where(hits[bk], rb_ref[bk, hd] * LOG2E, b)
            bias_ref[h, :, g * WINDOW:(g + 1) * WINDOW] = b

    key_row = lax.broadcasted_iota(jnp.int32, (2 * WINDOW, cols), 0)
    no_prev = jnp.logical_and(i == 0, key_row < WINDOW)
    sinks = [jnp.concatenate([jnp.full((1, WINDOW), sink_ref[h * GROUP + g] * LOG2E, F32) for g in range(GROUP)],
                             axis=1) for h in range(N_KV)]

    def logits(jb, h):
        if jb == 0:
            kband = jnp.concatenate([kp_ref[h], kc_ref[h, 0:WINDOW, :]], axis=0)
        else:
            kband = kc_ref[h, (jb - 1) * WINDOW:(jb + 1) * WINDOW, :]
        s = _dot(kband, qt_ref[h, jb]) + bias_ref[h]
        return jnp.where(no_prev, NEG, s) if jb == 0 else s

    def attend(jb, h, s):
        vprev = vp_ref[h] if jb == 0 else vc_ref[h, jb - 1]
        vband = jnp.concatenate([vprev, vc_ref[h, jb]], axis=1)
        m = jnp.maximum(jnp.max(s, axis=0, keepdims=True), sinks[h])
        p = jnp.exp2(s - m)
        denom = jnp.sum(p, axis=0, keepdims=True) + jnp.exp2(sinks[h] - m)
        ot = _dot(vband, p.astype(BF16)) * (1.0 / denom)
        o_heads = jnp.concatenate([ot[:, g * WINDOW:(g + 1) * WINDOW] for g in range(GROUP)], axis=0)
        o_buf[jb * WINDOW:(jb + 1) * WINDOW, h * GROUP * HEAD_DIM:(h + 1) * GROUP * HEAD_DIM] = o_heads.T

    units = [(jb, h) for jb in range(nb) for h in range(N_KV)]
    s = logits(*units[0])
    for u, unit in enumerate(units):
        s_next = logits(*units[u + 1]) if u + 1 < len(units) else None
        attend(*unit, s)
        s = s_next
    ob = (o_buf[...] * _silu(gate_ref[...])).astype(BF16)
    y_ref[...] = x_ref[...] + _dot(ob, wout_ref[...])


def _swa_prompt(qt, kh, vtb, gate, w_out, x, rel_bias, sinks, batch, seq, tm):
    ns = seq // tm
    nb = tm // WINDOW
    dist = np.arange(WINDOW)[None, :] + WINDOW - np.arange(2 * WINDOW)[:, None]
    bucket = np.where((dist >= 0) & (dist < WINDOW), _t5_bucket_np(dist), -1).astype(np.int32)
    tile = pl.BlockSpec((tm, D_MODEL), lambda b, i: (b * ns + i, 0))
    smem = pl.BlockSpec(memory_space=pltpu.SMEM)
    prev_blk = lambda i: jnp.maximum(i * nb - 1, 0)
    return pl.pallas_call(
        functools.partial(_swa_prompt_kernel, nb=nb),
        grid=(batch, ns),
        in_specs=[smem, smem,
                  pl.BlockSpec((None, N_KV, nb, HEAD_DIM, GROUP * WINDOW), lambda b, i: (b, 0, i, 0, 0)),
                  pl.BlockSpec((N_KV, WINDOW, HEAD_DIM), lambda b, i: (0, b * ns * nb + prev_blk(i), 0)),
                  pl.BlockSpec((N_KV, tm, HEAD_DIM), lambda b, i: (0, b * ns + i, 0)),
                  pl.BlockSpec((None, N_KV, None, HEAD_DIM, WINDOW), lambda b, i: (b, 0, prev_blk(i), 0, 0)),
                  pl.BlockSpec((None, N_KV, nb, HEAD_DIM, WINDOW), lambda b, i: (b, 0, i, 0, 0)),
                  _full(bucket.shape), tile, _full(w_out.shape), tile],
        out_specs=tile,
        out_shape=jax.ShapeDtypeStruct((batch * seq, D_MODEL), F32),
        scratch_shapes=[pltpu.VMEM((N_KV, 2 * WINDOW, GROUP * WINDOW), F32),
                        pltpu.VMEM((tm, D_MODEL), F32)],
        compiler_params=_cparams(("parallel", "arbitrary")),
        name="swa_prompt",
    )(sinks, rel_bias, qt, kh, kh, vtb, vtb, jnp.asarray(bucket), gate, w_out, x)


def _head_diag(o_full):
    out = jnp.zeros((N_HEADS, HEAD_DIM), F32)
    row_kv = lax.broadcasted_iota(jnp.int32, (N_HEADS, HEAD_DIM), 0) // GROUP
    for h in range(N_KV):
        out = out + jnp.where(row_kv == h, o_full[:, h * HEAD_DIM:(h + 1) * HEAD_DIM], 0.0)
    return out


def _swa_sample_kernel(qm_ref, kc_ref, vc_ref, kn_ref, vn_ref, knt_ref, vnt_ref, bias_ref, bias0_ref,
                       sink_ref, o_ref, wk_ref, wv_ref, *, bt):
    i = pl.program_id(0)
    lane = lax.broadcasted_iota(jnp.int32, (KV_WIDTH, WINDOW), 1)
    sink = sink_ref[...]
    for e in range(bt):
        b = i * bt + e
        kc = kc_ref[e]
        vc = vc_ref[e]
        qm = qm_ref[e]
        s = _dot(qm, kc.astype(BF16)) + bias_ref[...]
        s_n = jnp.sum(qm.astype(F32) * _bf16_round(kn_ref[e]), axis=1, keepdims=True) + bias0_ref[...]
        m = jnp.maximum(jnp.maximum(jnp.max(s, axis=1, keepdims=True), s_n), sink)
        p = jnp.exp(s - m)
        p_n = jnp.exp(s_n - m)
        denom = jnp.sum(p, axis=1, keepdims=True) + p_n + jnp.exp(sink - m)
        o_full = _dot_nt((p / denom).astype(BF16), vc.astype(BF16))
        o_full = o_full + _bf16_round(p_n / denom) * _bf16_round(vn_ref[e])
        o_ref[e] = _head_diag(o_full)
        kcol = jnp.sum(jnp.where(lane == b, knt_ref[...], 0.0), axis=1, keepdims=True)
        vcol = jnp.sum(jnp.where(lane == b, vnt_ref[...], 0.0), axis=1, keepdims=True)
        wk_ref[e] = jnp.where(lane == WINDOW - 1, kcol, pltpu.roll(kc, WINDOW - 1, 1))
        wv_ref[e] = jnp.where(lane == WINDOW - 1, vcol, pltpu.roll(vc, WINDOW - 1, 1))


def _swa_sample(qm, kc, vc, kn, vn, knt, vnt, bias_keys, bias0, sinks, bt):
    m = qm.shape[0]
    blk3 = lambda s1, s2: pl.BlockSpec((bt, s1, s2), lambda i: (i, 0, 0))
    cache = jax.ShapeDtypeStruct((m, KV_WIDTH, WINDOW), F32)
    return pl.pallas_call(
        functools.partial(_swa_sample_kernel, bt=bt),
        grid=(m // bt,),
        in_specs=[blk3(N_HEADS, KV_WIDTH), blk3(KV_WIDTH, WINDOW), blk3(KV_WIDTH, WINDOW),
                  blk3(1, KV_WIDTH), blk3(1, KV_WIDTH), _full(knt.shape), _full(vnt.shape),
                  _full(bias_keys.shape), _full(bias0.shape), _full(sinks.shape)],
        out_specs=[blk3(N_HEADS, HEAD_DIM), blk3(KV_WIDTH, WINDOW), blk3(KV_WIDTH, WINDOW)],
        out_shape=[jax.ShapeDtypeStruct((m, N_HEADS, HEAD_DIM), F32), cache, cache],
        compiler_params=_cparams(("parallel",)),
        name="swa_sample",
    )(qm, kc, vc, kn, vn, knt, vnt, bias_keys, bias0, sinks)


def _split3(x):
    hi = x.astype(BF16)
    r = x - hi.astype(F32)
    mid = r.astype(BF16)
    lo = (r - mid.astype(F32)).astype(BF16)
    return hi, mid, lo


def _cumsum_lanes_blocks(x, blk):
    n = x.shape[0]
    hi, mid, lo = _split3(x)
    r_i = lax.broadcasted_iota(jnp.int32, (blk, blk), 0)
    c_i = lax.broadcasted_iota(jnp.int32, (blk, blk), 1)
    upper = jnp.where(r_i <= c_i, 1.0, 0.0).astype(BF16)
    r = _dot(jnp.concatenate([hi, mid, lo], axis=0), upper)
    return r[0:n] + r[n:2 * n] + r[2 * n:]


def _fox_decay_kernel(x_ref, kin_ref, kout_ref, carry, *, blk):
    i = pl.program_id(1)

    @pl.when(i == 0)
    def _():
        carry[...] = jnp.zeros_like(carry)

    c = _cumsum_lanes_blocks(x_ref[...], blk) + carry[:, 0:1]
    carry[...] = jnp.broadcast_to(c[:, blk - 1:blk], carry.shape)
    hi, mid, lo = _split3(c * LOG2E)
    pieces = jnp.concatenate([hi, mid, lo], axis=0).astype(F32)
    pad = jnp.zeros((LANES - N_PIECES * N_HEADS, blk), F32)
    pieces_t = jnp.concatenate([pieces, pad], axis=0).T.astype(BF16)
    src = lax.broadcasted_iota(jnp.int32, (LANES, AUG), 0)
    dst = lax.broadcasted_iota(jnp.int32, (LANES, AUG), 1) - HEAD_DIM
    in_aug = jnp.logical_and(dst >= 0, dst < N_PIECES * GROUP)
    for h in range(N_KV):
        want = (dst >> GROUP_SHIFT) * N_HEADS + h * GROUP + (dst & (GROUP - 1))
        place = jnp.where(jnp.logical_and(in_aug, src == want), 1.0, 0.0).astype(BF16)
        kout_ref[h] = kin_ref[h] + _dot(pieces_t, place).astype(BF16)


def _fox_decay(logft, kaug, blk):
    b, h, s = logft.shape
    ns = s // blk
    kspec = pl.BlockSpec((N_KV, blk, AUG), lambda bi, i: (0, bi * ns + i, 0))
    return pl.pallas_call(
        functools.partial(_fox_decay_kernel, blk=blk),
        grid=(b, ns),
        in_specs=[pl.BlockSpec((None, h, blk), lambda bi, i: (bi, 0, i)), kspec],
        out_specs=kspec,
        out_shape=jax.ShapeDtypeStruct(kaug.shape, kaug.dtype),
        scratch_shapes=[pltpu.VMEM((h, LANES), F32)],
        input_output_aliases={1: 0},
        compiler_params=_cparams(("parallel", "arbitrary")),
        name="fox_decay",
    )(logft, kaug)


def _fox_prompt_kernel(qt_ref, k_ref, vt_ref, o_ref, *scratch, tq, tk):
    assert tq == tk
    t = pl.program_id(2)
    m_ref, acc_ref, sa_ref, sb_ref, ca_ref, cb_ref = scratch
    cols = GROUP * tq

    def reset():
        m_ref[...] = jnp.full(m_ref.shape, NEG, F32)
        acc_ref[...] = jnp.zeros(acc_ref.shape, F32)

    def logits(tile, kb, s_ref, c_ref):
        k0 = pl.multiple_of(kb * tk, tk)
        s = _dot(k_ref[pl.ds(k0, tk), :], qt_ref[tile])
        s_ref[...] = s
        c_ref[...] = jnp.max(s, axis=0, keepdims=True)

    def accumulate(kb, s_ref, c_ref, masked):
        s = s_ref[...]
        if masked:
            kpos = lax.broadcasted_iota(jnp.int32, (tk, cols), 0)
            qpos = lax.broadcasted_iota(jnp.int32, (tk, cols), 1) & (tq - 1)
            s = jnp.where(kpos <= qpos, s, NEG)
            cmax = jnp.max(s, axis=0, keepdims=True)
        else:
            cmax = c_ref[...]
        m_old = m_ref[...]
        m_new = jnp.maximum(m_old, cmax)
        alpha = jnp.exp2(m_old - m_new)
        p = jnp.exp2(s - m_new)
        acc_ref[...] = alpha * acc_ref[...] + _dot(vt_ref[kb], p.astype(BF16))
        m_ref[...] = m_new

    def finish(tile):
        o = acc_ref[0:HEAD_DIM, :] / acc_ref[HEAD_DIM:HEAD_DIM + 1, :]
        o_ref[tile * tq:(tile + 1) * tq, :] = jnp.concatenate(
            [o[:, g * tq:(g + 1) * tq] for g in range(GROUP)], axis=0).T

    def pairs(tile, first, second):
        def body(j, carry):
            kb = 2 * j
            logits(tile, kb + 1, *second)
            accumulate(kb, *first, False)
            logits(tile, kb + 2, *first)
            accumulate(kb + 1, *second, False)
            return carry
        lax.fori_loop(0, t, body, 0)

    buf_a, buf_b = (sa_ref, ca_ref), (sb_ref, cb_ref)
    reset()
    logits(0, 0, *buf_a)
    pairs(0, buf_a, buf_b)
    logits(1, 0, *buf_b)
    accumulate(2 * t, *buf_a, True)
    finish(0)
    reset()
    pairs(1, buf_b, buf_a)
    logits(1, 2 * t + 1, *buf_a)
    accumulate(2 * t, *buf_b, False)
    accumulate(2 * t + 1, *buf_a, True)
    finish(1)


def _fox_prompt(qt, kaug, vtb, batch, seq, tq):
    nq = seq // tq
    nk, tk = vtb.shape[2], vtb.shape[4]
    cols = GROUP * tq
    nt = nq // 2
    return pl.pallas_call(
        functools.partial(_fox_prompt_kernel, tq=tq, tk=tk),
        grid=(batch, N_KV, nt),
        in_specs=[pl.BlockSpec((None, None, 2, AUG, cols), lambda b, h, t: (b, h, t, 0, 0)),
                  pl.BlockSpec((None, seq, AUG), lambda b, h, t: (h, b, 0)),
                  pl.BlockSpec((None, None, nk, V_ROWS, tk), lambda b, h, t: (b, h, 0, 0, 0))],
        out_specs=pl.BlockSpec((2 * tq, GROUP * HEAD_DIM), lambda b, h, t: (b * nt + t, h)),
        out_shape=jax.ShapeDtypeStruct((batch * seq, D_MODEL), F32),
        scratch_shapes=[pltpu.VMEM((1, cols), F32),
                        pltpu.VMEM((V_ROWS, cols), F32),
                        pltpu.VMEM((tk, cols), F32), pltpu.VMEM((tk, cols), F32),
                        pltpu.VMEM((1, cols), F32), pltpu.VMEM((1, cols), F32)],
        compiler_params=_cparams(("parallel", "parallel", "arbitrary")),
        name="fox_prompt",
    )(qt, kaug, vtb)


N_DECODE_SCRATCH = 8


def _fox_sample_kernel(pt_ref, qm_ref, kn_ref, vn_ref, fn_ref, ck_hbm, cv_hbm, cf_hbm, *rest, cp, nchunk, rider):
    n_in = _n_pool_inputs(rider["final"], rider["pending"]) if rider else 0
    rider_in, o_ref, rest = rest[:n_in], rest[n_in], rest[n_in + 1:]
    rider_out, rest = (rest[:2], rest[2:]) if rider else ((), rest)
    kbuf, vbuf, fbuf, sems, m_ref, l_ref, acc_ref, carry_ref = rest[:N_DECODE_SCRATCH]
    rider_bufs = rest[N_DECODE_SCRATCH:]
    b = pl.program_id(0)
    c = pl.program_id(1)
    nb = pl.num_programs(0)
    step = b * nchunk + c
    slot = step % 2

    def copies(bb, cc, sl, p):
        page = pt_ref[bb, cc * cp + p]
        return (pltpu.make_async_copy(ck_hbm.at[page], kbuf.at[sl, p], sems.at[sl, 0]),
                pltpu.make_async_copy(cv_hbm.at[page], vbuf.at[sl, p], sems.at[sl, 1]),
                pltpu.make_async_copy(cf_hbm.at[page], fbuf.at[sl, p], sems.at[sl, 2]))

    def issue(bb, cc, sl):
        for p in range(cp):
            for cpy in copies(bb, cc, sl, p):
                cpy.start()

    @pl.when(step == 0)
    def _():
        issue(b, c, slot)

    @pl.when(step + 1 < nb * nchunk)
    def _():
        nxt = step + 1
        issue(nxt // nchunk, nxt % nchunk, 1 - slot)

    if rider:
        per_batch = rider["tiles_per_batch"]
        _pool_tile(step % per_batch, per_batch, rider_in, rider_out, rider_bufs,
                   tp=rider["tp"], final=rider["final"], pending=rider["pending"])

    for p in range(cp):
        for cpy in copies(b, c, slot, p):
            cpy.wait()

    @pl.when(c == 0)
    def _():
        m_ref[...] = jnp.full(m_ref.shape, NEG, F32)
        l_ref[...] = jnp.zeros(l_ref.shape, F32)
        acc_ref[...] = jnp.zeros(acc_ref.shape, F32)
        carry_ref[...] = jnp.zeros(carry_ref.shape, F32)

    qm = qm_ref[...]
    cin = _cumsum_lanes_blocks(fbuf[slot].reshape(cp * N_HEADS, PAGE), PAGE).reshape(cp, N_HEADS, PAGE)
    carry = carry_ref[...]
    cs = []
    for p in range(cp):
        cs.append(cin[p] + carry)
        carry = carry + cin[p][:, PAGE - 1:PAGE]
    carry_ref[...] = carry
    kcat = jnp.concatenate([kbuf[slot, p] for p in range(cp)], axis=1).astype(BF16)
    t = _dot(qm, kcat) - jnp.concatenate(cs, axis=1)
    m_old = m_ref[...]
    m_new = jnp.maximum(m_old, jnp.max(t, axis=1, keepdims=True))
    alpha = jnp.exp(m_old - m_new)
    p = jnp.exp(t - m_new)
    l_new = alpha * l_ref[...] + jnp.sum(p, axis=1, keepdims=True)
    vcat = jnp.concatenate([vbuf[slot, p_] for p_ in range(cp)], axis=1).astype(BF16)
    acc_new = alpha * acc_ref[...] + _dot_nt(p.astype(BF16), vcat)
    m_ref[...] = m_new
    l_ref[...] = l_new
    acc_ref[...] = acc_new

    @pl.when(c == nchunk - 1)
    def _():
        s_n = jnp.sum(qm.astype(F32) * _bf16_round(kn_ref[...]), axis=1, keepdims=True)
        t_n = s_n - (carry + fn_ref[...])
        m_f = jnp.maximum(m_new, t_n)
        a = jnp.exp(m_new - m_f)
        p_n = jnp.exp(t_n - m_f)
        l_f = a * l_new + p_n
        acc = a * acc_new + _bf16_round(p_n) * _bf16_round(vn_ref[...])
        o_ref[...] = _head_diag(acc / l_f)


def _fox_sample(page_table, qm, kn, vn, fn, ck, cv, cf, cp, pool_layer=None):
    m, n_pages = page_table.shape
    nchunk = n_pages // cp
    per_b = lambda s1, s2: pl.BlockSpec((None, s1, s2), lambda b, c, pt: (b, 0, 0))
    any_spec = pl.BlockSpec(memory_space=pl.ANY)
    ins = [qm, kn, vn, fn, ck, cv, cf]
    in_specs = [per_b(N_HEADS, KV_WIDTH), per_b(1, KV_WIDTH), per_b(1, KV_WIDTH), per_b(N_HEADS, 1),
                any_spec, any_spec, any_spec]
    out_specs = [per_b(N_HEADS, HEAD_DIM)]
    out_shape = [jax.ShapeDtypeStruct((m, N_HEADS, HEAD_DIM), F32)]
    scratch = [pltpu.VMEM((2, cp, KV_WIDTH, PAGE), F32),
               pltpu.VMEM((2, cp, KV_WIDTH, PAGE), F32),
               pltpu.VMEM((2, cp, N_HEADS, PAGE), F32),
               pltpu.SemaphoreType.DMA((2, 3)),
               pltpu.VMEM((N_HEADS, 1), F32), pltpu.VMEM((N_HEADS, 1), F32),
               pltpu.VMEM((N_HEADS, KV_WIDTH), F32), pltpu.VMEM((N_HEADS, 1), F32)]
    assert len(scratch) == N_DECODE_SCRATCH
    rider = None
    if pool_layer is not None:
        *layer, batch, seq = pool_layer
        assert nchunk == 1 and (batch * seq) % m == 0
        tp = batch * seq // m
        per_batch = seq // tp
        assert seq % tp == 0 and tp % HALO == 0
        p_ins, p_in_specs, p_out_specs, p_scratch, final = _pool_prompt_operands(
            *layer, tp, lambda b, c, pt: b, lambda b, c, pt: b // per_batch, True)
        rider = dict(tp=tp, final=final, pending=layer[1] is not None, tiles_per_batch=per_batch)
        ins += p_ins
        in_specs += p_in_specs
        out_specs += p_out_specs
        out_shape += [jax.ShapeDtypeStruct((batch * seq, D_MODEL), F32),
                      jax.ShapeDtypeStruct((batch, HALO, D_MODEL), F32)]
        scratch += p_scratch
    outs = pl.pallas_call(
        functools.partial(_fox_sample_kernel, cp=cp, nchunk=nchunk, rider=rider),
        grid_spec=pltpu.PrefetchScalarGridSpec(
            num_scalar_prefetch=1,
            grid=(m, nchunk),
            in_specs=in_specs,
            out_specs=out_specs,
            scratch_shapes=scratch,
        ),
        out_shape=out_shape,
        compiler_params=pltpu.CompilerParams(dimension_semantics=("arbitrary", "arbitrary"),
                                             vmem_limit_bytes=VMEM_LIMIT_RIDER if rider else VMEM_LIMIT),
        name="fox_sample",
    )(page_table, *ins)
    return outs if rider else outs[0]


def _head_rows_q(q_rows):
    m = q_rows.shape[0]
    q4 = q_rows.reshape(m, N_KV, GROUP, HEAD_DIM)
    eye = jnp.eye(N_KV, dtype=q_rows.dtype)
    return (q4[:, :, :, None, :] * eye[None, :, None, :, None]).reshape(m, N_HEADS, KV_WIDTH)


def _tokens_last(x):
    lead = x.shape[:-3]
    n = len(lead)
    xt = jnp.transpose(x, tuple(range(n)) + (n + 1, n + 2, n))
    return xt.reshape(lead + (KV_WIDTH, x.shape[-3]))


def _tokens_first(xt):
    lead = xt.shape[:-2]
    n = len(lead)
    x4 = xt.reshape(lead + (N_KV, HEAD_DIM, xt.shape[-1]))
    return jnp.transpose(x4, tuple(range(n)) + (n + 2, n, n + 1))


def kernel(x_prompt, x_sample, state_pool, cache_win_k, cache_win_v, cache_fox_k, cache_fox_v,
           cache_fox_logf, page_table, norm_g, final_norm_g, rel_bias, pool_w_in, pool_mix,
           pool_scale, pool_w_out, swa_w_in, swa_sinks, swa_w_out, fox_w_in, fox_f_bias, fox_w_out):
    batch, seq, _ = x_prompt.shape
    db = x_sample.shape[0]
    depth = norm_g.shape[0]
    mp = batch * seq
    tm_p, tm_s = 512, db

    xp = x_prompt.reshape(mp, D_MODEL)
    xs = x_sample.reshape(db, D_MODEL)

    rb = rel_bias.astype(F32)
    dist_keys = WINDOW - np.arange(WINDOW)
    bias_keys = jnp.where((dist_keys < WINDOW)[None, :], rb[_t5_bucket_np(dist_keys)].T, NEG)
    bias0 = rb[0].reshape(N_HEADS, 1)
    fg = final_norm_g.reshape(1, D_MODEL)

    pool_p, pool_s = [], []
    wk_p, wv_p, wk_s, wv_s = [], [], [], []
    fk_p, fv_p, fl_p, fk_s, fv_s, fl_s = [], [], [], [], [], []
    def pool_layer_args(layer):
        jj = layer // 3
        fg_l = fg if layer == depth - 1 else None
        return (norm_g[layer].reshape(1, D_MODEL), pool_w_in[jj].astype(BF16), pool_mix[jj].astype(BF16),
                pool_scale[jj].reshape(1, D_MODEL), pool_w_out[jj].astype(BF16), fg_l)

    prompt_done = None
    for i in range(depth):
        kind, j = i % 3, i // 3
        g = norm_g[i].reshape(1, D_MODEL)
        if kind == 0:
            g, w_in, mix, scale, w_out, fg_l = pool_layer_args(i)
            if prompt_done is None:
                xp, u_tail = _pool_prompt(xp, None, g, w_in, mix, scale, w_out, fg_l, batch, seq, tm_p)
            else:
                xp, u_tail = prompt_done
                prompt_done = None
            pool_p.append(u_tail[:, HALO - POOL_STATE:])
            xs, u_s = _pool_sample(xs, jnp.transpose(state_pool[j], (1, 0, 2)), g, w_in, mix, scale, w_out, fg_l)
            pool_s.append(jnp.concatenate([state_pool[j][:, 1:], u_s[:, None, :]], axis=1))
        elif kind == 1:
            w_in = swa_w_in[j].astype(BF16)
            w_out = swa_w_out[j].astype(BF16)
            sinks = swa_sinks[j].astype(F32)
            qt_p, kh_p, kt_p, vt_p, vtb_p, gate_p = _swa_proj(xp, g, w_in, tm_p, batch, seq, True)
            xp = _swa_prompt(qt_p, kh_p, vtb_p, gate_p, w_out, xp, rb, sinks, batch, seq, tm_p)
            wk_p.append(_tokens_first(kt_p[:, :, seq - WINDOW:]))
            wv_p.append(_tokens_first(vt_p[:, :, seq - WINDOW:]))
            q_s, k_s, v_s, kt_s, vt_s, gate_s = _swa_proj(xs, g, w_in, tm_s, 1, db, False)
            o_s, wk, wv = _swa_sample(_head_rows_q(q_s), _tokens_last(cache_win_k[j]),
                                      _tokens_last(cache_win_v[j]), k_s[:, None, :], v_s[:, None, :],
                                      kt_s[0], vt_s[0], bias_keys, bias0, sinks.reshape(N_HEADS, 1),
                                      SWA_ROWS_PER_STEP)
            xs = _out_proj(o_s.reshape(db, D_MODEL), gate_s, w_out, xs, tm_s)
            wk_s.append(_tokens_first(wk))
            wv_s.append(_tokens_first(wv))
        else:
            w_full = fox_w_in[j]
            nqkv = D_MODEL + 2 * KV_WIDTH
            w_in = jnp.concatenate([w_full[:, :nqkv], w_full[:, nqkv + N_HEADS:]], axis=1).astype(BF16)
            wf = jnp.pad(w_full[:, nqkv:nqkv + N_HEADS], ((0, 0), (0, LANES - N_HEADS))).astype(BF16)
            fb = jnp.pad(fox_f_bias[j].astype(F32), (0, LANES - N_HEADS)).reshape(1, LANES)
            w_out = fox_w_out[j].astype(BF16)
            qt, kt_p, vt_p, kaug, vtb, logft_p, gate_p = _fox_proj(xp, g, w_in, wf, fb, FOX_TILE, batch, seq, True)
            kaug = _fox_decay(logft_p, kaug, FOX_TILE)
            o_p = _fox_prompt(qt, kaug, vtb, batch, seq, FOX_TILE)
            fk_p.append(_tokens_first(kt_p))
            fv_p.append(_tokens_first(vt_p))
            fl_p.append(jnp.transpose(logft_p, (0, 2, 1)))
            q_s, k_s, v_s, kt_s, vt_s, logf_s, logft_s, gate_s = _fox_proj(
                xs, g, w_in, wf, fb, tm_s, 1, db, False)
            decode_args = (page_table, _head_rows_q(q_s), k_s[:, None, :], v_s[:, None, :], logf_s[:, :, None],
                           _tokens_last(cache_fox_k[j]), _tokens_last(cache_fox_v[j]),
                           jnp.transpose(cache_fox_logf[j], (0, 2, 1)), FOX_PAGES_PER_STEP)
            if i + 1 < depth and (i + 1) % 3 == 0:
                o_s, xp, u_tail = _fox_sample(
                    *decode_args, pool_layer=(xp, (o_p, gate_p, w_out), *pool_layer_args(i + 1), batch, seq))
                prompt_done = (xp, u_tail)
            else:
                xp = _out_proj(o_p, gate_p, w_out, xp, tm_p)
                o_s = _fox_sample(*decode_args)
            xs = _out_proj(o_s.reshape(db, D_MODEL), gate_s, w_out, xs, tm_s)
            fk_s.append(_tokens_first(kt_s[0])[:, None])
            fv_s.append(_tokens_first(vt_s[0])[:, None])
            fl_s.append(logft_s[0].T[:, None, :])

    if (depth - 1) % 3 != 0:
        xp, xs = _final_norm(xp, fg, tm_p), _final_norm(xs, fg, tm_s)
    y_prompt = xp.reshape(batch, seq, D_MODEL)
    y_sample = xs.reshape(db, 1, D_MODEL)
    return (y_prompt, y_sample, jnp.stack(pool_p), jnp.stack(pool_s), jnp.stack(wk_p), jnp.stack(wv_p),
            jnp.stack(wk_s), jnp.stack(wv_s), jnp.stack(fk_p), jnp.stack(fv_p), jnp.stack(fl_p),
            jnp.stack(fk_s), jnp.stack(fv_s), jnp.stack(fl_s))
```

```python
import functools
import math

import numpy as np
import jax
import jax.numpy as jnp
from jax import lax
from jax.experimental import pallas as pl
from jax.experimental.pallas import tpu as pltpu

D_MODEL = 1024
HEAD_DIM = 64
N_HEADS = 16
N_KV = 4
GROUP = 4
KV_WIDTH = N_KV * HEAD_DIM
POOL_WINDOWS = (2, 4, 8, 16)
POOL_GROUP = 256
POOL_STATE = 15
WINDOW = 128
N_BUCKETS = 32
MAX_DISTANCE = 128
SCALE = HEAD_DIM ** -0.5
EPS = 1e-6
NEG = -1e30
PAGE = 128
LANES = 128

BF16 = jnp.bfloat16
F32 = jnp.float32

VMEM_LIMIT = 56 * 1024 * 1024
VMEM_LIMIT_RIDER = 62 * 1024 * 1024


def _cparams(sem):
    return pltpu.CompilerParams(dimension_semantics=sem, vmem_limit_bytes=VMEM_LIMIT)


def _rms_bf16(x, g):
    ms = jnp.mean(x * x, axis=-1, keepdims=True)
    return (x * lax.rsqrt(ms + EPS) * g).astype(BF16)


def _silu(x):
    return x * jax.nn.sigmoid(x)


def _dot(a, b):
    return jnp.dot(a, b, preferred_element_type=F32)


def _dot_nt(a, b):
    return lax.dot_general(a, b, (((1,), (1,)), ((), ())), preferred_element_type=F32)


def _bf16_round(x):
    return x.astype(BF16).astype(F32)


def _full(shape):
    n = len(shape)
    return pl.BlockSpec(shape, lambda *_: (0,) * n)


LOG2E = 1.4426950408889634


def _swa_proj_kernel(x_ref, g_ref, w_ref, *out_refs, prompt):
    if prompt:
        qt_ref, kh_ref, kt_ref, vt_ref, vtb_ref, gate_ref = out_refs
    else:
        q_ref, k_ref, v_ref, kt_ref, vt_ref, gate_ref = out_refs
    hb = _rms_bf16(x_ref[...], g_ref[...])
    q = _dot(hb, w_ref[:, :D_MODEL])
    k = _dot(hb, w_ref[:, D_MODEL:D_MODEL + KV_WIDTH])
    v = _dot(hb, w_ref[:, D_MODEL + KV_WIDTH:D_MODEL + 2 * KV_WIDTH])
    vt = v.T
    kt_ref[...] = k.T
    vt_ref[...] = vt
    if prompt:
        tm = q.shape[0]
        qt = (q * (SCALE * LOG2E)).T.astype(BF16)
        for hd in range(N_HEADS):
            h, g = divmod(hd, GROUP)
            for jb in range(tm // WINDOW):
                qt_ref[h, jb, :, g * WINDOW:(g + 1) * WINDOW] = (
                    qt[hd * HEAD_DIM:(hd + 1) * HEAD_DIM, jb * WINDOW:(jb + 1) * WINDOW])
        for h in range(N_KV):
            kh_ref[h] = k[:, h * HEAD_DIM:(h + 1) * HEAD_DIM].astype(BF16)
            vth = vt[h * HEAD_DIM:(h + 1) * HEAD_DIM, :].astype(BF16)
            for jb in range(tm // WINDOW):
                vtb_ref[h, jb] = vth[:, jb * WINDOW:(jb + 1) * WINDOW]
    else:
        q_ref[...] = (q * SCALE).astype(BF16)
        k_ref[...] = k
        v_ref[...] = v
    gate_ref[...] = _dot(hb, w_ref[:, D_MODEL + 2 * KV_WIDTH:])


def _swa_proj(x, g, w, tm, batch, seq, prompt):
    m = x.shape[0]
    ns = seq // tm
    nb = tm // WINDOW
    row = lambda i: (i, 0)
    sds = jax.ShapeDtypeStruct
    tcol = pl.BlockSpec((None, KV_WIDTH, tm), lambda i: (i // ns, 0, i % ns))
    blocks = lambda last: pl.BlockSpec((None, N_KV, nb, HEAD_DIM, last), lambda i: (i // ns, 0, i % ns, 0, 0))
    kv_t_shape = [sds((batch, KV_WIDTH, seq), F32)] * 2
    if prompt:
        out_specs = [blocks(GROUP * WINDOW), pl.BlockSpec((N_KV, tm, HEAD_DIM), lambda i: (0, i, 0)),
                     tcol, tcol, blocks(WINDOW)]
        out_shape = ([sds((batch, N_KV, seq // WINDOW, HEAD_DIM, GROUP * WINDOW), BF16),
                      sds((N_KV, m, HEAD_DIM), BF16)] + kv_t_shape
                     + [sds((batch, N_KV, seq // WINDOW, HEAD_DIM, WINDOW), BF16)])
    else:
        out_specs = [pl.BlockSpec((tm, D_MODEL), row), pl.BlockSpec((tm, KV_WIDTH), row),
                     pl.BlockSpec((tm, KV_WIDTH), row), tcol, tcol]
        out_shape = [sds((m, D_MODEL), BF16), sds((m, KV_WIDTH), F32), sds((m, KV_WIDTH), F32)] + kv_t_shape
    out_specs.append(pl.BlockSpec((tm, D_MODEL), row))
    out_shape.append(sds((m, D_MODEL), F32))
    return pl.pallas_call(
        functools.partial(_swa_proj_kernel, prompt=prompt),
        grid=(m // tm,),
        in_specs=[pl.BlockSpec((tm, D_MODEL), row), _full((1, D_MODEL)), _full(w.shape)],
        out_specs=out_specs,
        out_shape=out_shape,
        compiler_params=_cparams(("parallel",)),
        name="swa_proj",
    )(x, g, w)


def _log_sigmoid(x):
    return -(jnp.maximum(-x, 0.0) + jnp.log1p(jnp.exp(-jnp.abs(x))))


AUG = 128
N_PIECES = 3
SWA_ROWS_PER_STEP = 16
V_ROWS = HEAD_DIM + 16
FOX_PAGES_PER_STEP = 64
FOX_TILE = 512
FOX_KEYS = 512
FOX_QUERIES = 256
GROUP_SHIFT = GROUP.bit_length() - 1


def _fox_proj_kernel(x_ref, g_ref, w_ref, wf_ref, fb_ref, *out_refs, prompt):
    if prompt:
        qt_ref, kt_ref, vt_ref, kaug_ref, vtb_ref, logft_ref, gate_ref = out_refs
    else:
        q_ref, k_ref, v_ref, kt_ref, vt_ref, logf_ref, logft_ref, gate_ref = out_refs
    hb = _rms_bf16(x_ref[...], g_ref[...])
    q = _dot(hb, w_ref[:, :D_MODEL])
    k = _dot(hb, w_ref[:, D_MODEL:D_MODEL + KV_WIDTH])
    v = _dot(hb, w_ref[:, D_MODEL + KV_WIDTH:D_MODEL + 2 * KV_WIDTH])
    vt = v.T
    kt_ref[...] = k.T
    vt_ref[...] = vt
    logf = _log_sigmoid(_dot(hb, wf_ref[...]) + fb_ref[...])
    logft_ref[...] = logf.T[:N_HEADS, :]
    if prompt:
        tm = q.shape[0]
        qt = (q * (SCALE * LOG2E)).T.astype(BF16)
        r = lax.broadcasted_iota(jnp.int32, (AUG - HEAD_DIM, tm), 0)
        for hd in range(N_HEADS):
            h, g = divmod(hd, GROUP)
            cols = slice(g * tm, (g + 1) * tm)
            qt_ref[h, 0:HEAD_DIM, cols] = qt[hd * HEAD_DIM:(hd + 1) * HEAD_DIM, :]
            pick = jnp.logical_and(r < N_PIECES * GROUP, (r & (GROUP - 1)) == g)
            qt_ref[h, HEAD_DIM:AUG, cols] = jnp.where(pick, -1.0, 0.0).astype(BF16)
        zeros = jnp.zeros((tm, AUG - HEAD_DIM), F32)
        for h in range(N_KV):
            kaug_ref[h] = jnp.concatenate([k[:, h * HEAD_DIM:(h + 1) * HEAD_DIM], zeros], axis=1).astype(BF16)
            ones_row = lax.broadcasted_iota(jnp.int32, (V_ROWS - HEAD_DIM, tm), 0) == 0
            vth = jnp.concatenate([vt[h * HEAD_DIM:(h + 1) * HEAD_DIM, :],
                                   jnp.where(ones_row, 1.0, 0.0)], axis=0).astype(BF16)
            for kb in range(tm // FOX_KEYS):
                vtb_ref[h, kb] = vth[:, kb * FOX_KEYS:(kb + 1) * FOX_KEYS]
    else:
        q_ref[...] = (q * SCALE).astype(BF16)
        k_ref[...] = k
        v_ref[...] = v
        logf_ref[...] = logf[:, :N_HEADS]
    gate_ref[...] = _dot(hb, w_ref[:, D_MODEL + 2 * KV_WIDTH:])


def _fox_proj(x, g, w, wf, fb, tm, batch, seq, prompt):
    m = x.shape[0]
    ns = seq // tm
    row = lambda i: (i, 0)
    tcol = lambda r: pl.BlockSpec((None, r, tm), lambda i: (i // ns, 0, i % ns))
    sds = jax.ShapeDtypeStruct
    kv_t = [tcol(KV_WIDTH), tcol(KV_WIDTH)]
    kv_t_shape = [sds((batch, KV_WIDTH, seq), F32)] * 2
    if prompt:
        out_specs = ([pl.BlockSpec((None, N_KV, None, AUG, GROUP * tm), lambda i: (i // ns, 0, i % ns, 0, 0))] + kv_t
                     + [pl.BlockSpec((N_KV, tm, AUG), lambda i: (0, i, 0)),
                        pl.BlockSpec((None, N_KV, tm // FOX_KEYS, V_ROWS, FOX_KEYS),
                                     lambda i: (i // ns, 0, i % ns, 0, 0)),
                        tcol(N_HEADS)])
        out_shape = ([sds((batch, N_KV, ns, AUG, GROUP * tm), BF16)] + kv_t_shape
                     + [sds((N_KV, m, AUG), BF16),
                        sds((batch, N_KV, seq // FOX_KEYS, V_ROWS, FOX_KEYS), BF16),
                        sds((batch, N_HEADS, seq), F32)])
    else:
        out_specs = ([pl.BlockSpec((tm, D_MODEL), row), pl.BlockSpec((tm, KV_WIDTH), row),
                      pl.BlockSpec((tm, KV_WIDTH), row)] + kv_t
                     + [pl.BlockSpec((tm, N_HEADS), row), tcol(N_HEADS)])
        out_shape = ([sds((m, D_MODEL), BF16), sds((m, KV_WIDTH), F32), sds((m, KV_WIDTH), F32)]
                     + kv_t_shape + [sds((m, N_HEADS), F32), sds((batch, N_HEADS, seq), F32)])
    out_specs.append(pl.BlockSpec((tm, D_MODEL), row))
    out_shape.append(sds((m, D_MODEL), F32))
    return pl.pallas_call(
        functools.partial(_fox_proj_kernel, prompt=prompt),
        grid=(m // tm,),
        in_specs=[pl.BlockSpec((tm, D_MODEL), row), _full((1, D_MODEL)), _full(w.shape),
                  _full(wf.shape), _full(fb.shape)],
        out_specs=out_specs,
        out_shape=out_shape,
        compiler_params=_cparams(("parallel",)),
        name="fox_proj",
    )(x, g, w, wf, fb)


def _out_proj_kernel(*refs, gated):
    if gated:
        o_ref, gate_ref, w_ref, x_ref, y_ref = refs
        ob = (o_ref[...] * _silu(gate_ref[...])).astype(BF16)
    else:
        o_ref, w_ref, x_ref, y_ref = refs
        ob = o_ref[...]
    y_ref[...] = x_ref[...] + _dot(ob, w_ref[...])


def _out_proj(o, gate, w, x, tm):
    m = x.shape[0]
    row = lambda i: (i, 0)
    tile = pl.BlockSpec((tm, D_MODEL), row)
    gated = gate is not None
    ins = [o, gate, w, x] if gated else [o, w, x]
    in_specs = [tile, tile, _full(w.shape), tile] if gated else [tile, _full(w.shape), tile]
    return pl.pallas_call(
        functools.partial(_out_proj_kernel, gated=gated),
        grid=(m // tm,),
        in_specs=in_specs,
        out_specs=tile,
        out_shape=jax.ShapeDtypeStruct((m, D_MODEL), F32),
        compiler_params=_cparams(("parallel",)),
        name="out_proj",
    )(*ins)


def _final_norm_kernel(x_ref, g_ref, y_ref):
    x = x_ref[...]
    ms = jnp.mean(x * x, axis=-1, keepdims=True)
    y_ref[...] = x * lax.rsqrt(ms + EPS) * g_ref[...]


def _final_norm(x, g, tm):
    m = x.shape[0]
    row = lambda i: (i, 0)
    return pl.pallas_call(
        _final_norm_kernel,
        grid=(m // tm,),
        in_specs=[pl.BlockSpec((tm, D_MODEL), row), _full((1, D_MODEL))],
        out_specs=pl.BlockSpec((tm, D_MODEL), row),
        out_shape=jax.ShapeDtypeStruct((m, D_MODEL), F32),
        compiler_params=_cparams(("parallel",)),
        name="final_norm",
    )(x, g)


HALO = 16
PAD = 8
assert all(w == 2 << g for g, w in enumerate(POOL_WINDOWS)) and HALO >= max(POOL_WINDOWS) and PAD >= HALO // 2


def _pool_layer_tail(x, u, gate, pooled_groups, mix_ref, scale_ref, wout_ref, fg_ref, y_ref):
    pieces = []
    for g in range(len(POOL_WINDOWS)):
        c0 = g * POOL_GROUP
        p = (pooled_groups[g] - u[:, c0:c0 + POOL_GROUP]).astype(BF16)
        pieces.append(_dot(p, mix_ref[g]))
    pm = jnp.concatenate(pieces, axis=1)
    o = (pm * scale_ref[...] * _silu(gate)).astype(BF16)
    y = x + _dot(o, wout_ref[...])
    if fg_ref is not None:
        ms = jnp.mean(y * y, axis=-1, keepdims=True)
        y = y * lax.rsqrt(ms + EPS) * fg_ref[...]
    y_ref[...] = y


def _n_pool_inputs(final, pending):
    return 6 + (3 if pending else 0) + (1 if final else 0)


def _pool_prompt_kernel(*refs, tp, final, pending):
    n_in = _n_pool_inputs(final, pending)
    _pool_tile(pl.program_id(1), pl.num_programs(1), refs[:n_in], refs[n_in:n_in + 2], refs[n_in + 2:],
               tp=tp, final=final, pending=pending)


def _pool_tile(i, n_tiles, in_refs, out_refs, bufs, *, tp, final, pending):
    refs = list(in_refs)
    x_ref = refs.pop(0)
    prev = [refs.pop(0) for _ in range(3)] if pending else None
    g_ref, win_ref, mix_ref, scale_ref, wout_ref = [refs.pop(0) for _ in range(5)]
    fg_ref = refs.pop(0) if final else None
    y_ref, tail_ref = out_refs
    hist = slice(PAD, PAD + HALO)
    ext = slice(PAD, PAD + HALO + tp)
    tile = slice(PAD + HALO, PAD + HALO + tp)

    def back(rows, k):
        return slice(rows.start - k, rows.stop - k)

    @pl.when(i == 0)
    def _():
        for buf in bufs:
            buf[0:PAD + HALO, :] = jnp.zeros((PAD + HALO, buf.shape[1]), F32)

    x = x_ref[...]
    if pending:
        po_ref, pgate_ref, pw_ref = prev
        x = x + _dot((po_ref[...] * _silu(pgate_ref[...])).astype(BF16), pw_ref[...])
    hb = _rms_bf16(x, g_ref[...])
    u = _dot(hb, win_ref[:, :D_MODEL])
    gate = _dot(hb, win_ref[:, D_MODEL:])
    bufs[0][tile, :] = u
    pos = i * tp + lax.broadcasted_iota(jnp.int32, (tp, 1), 0)
    pooled = []
    for g, w in enumerate(POOL_WINDOWS):
        src, span = bufs[g], 1 << g
        if g + 1 < len(bufs):
            nxt = bufs[g + 1]
            nxt[ext, :] = src[ext, POOL_GROUP:] + src[back(ext, span), POOL_GROUP:]
        acc = src[tile, 0:POOL_GROUP] + src[back(tile, span), 0:POOL_GROUP]
        inv_cnt = 1.0 / jnp.minimum(pos + 1, w).astype(F32)
        pooled.append(acc * inv_cnt)
    _pool_layer_tail(x, u, gate, pooled, mix_ref, scale_ref, wout_ref, fg_ref, y_ref)
    bufs[0][hist, :] = u[tp - HALO:tp, :]

    @pl.when(i == n_tiles - 1)
    def _():
        tail_ref[...] = u[tp - HALO:tp, :]


def _pool_prompt_operands(x, pending, g, w_in, mix, scale, w_out, fg, tp, row_tile, batch_of, single_buffer):
    final = fg is not None
    tile = pl.BlockSpec((tp, D_MODEL), lambda *idx: (row_tile(*idx), 0))
    whole = (lambda a: pl.BlockSpec(a.shape, lambda *_: (0,) * a.ndim, pipeline_mode=pl.Buffered(1))
             if single_buffer else _full(a.shape))
    ins = [x] + (list(pending) if pending else []) + [g, w_in, mix, scale, w_out] + ([fg] if final else [])
    in_specs = [tile] + ([tile, tile, whole(pending[2])] if pending else [])
    in_specs += [whole(a) for a in ins[len(in_specs):]]
    out_specs = [tile, pl.BlockSpec((None, HALO, D_MODEL), lambda *idx: (batch_of(*idx), 0, 0))]
    scratch = [pltpu.VMEM((PAD + HALO + tp, D_MODEL - g * POOL_GROUP), F32) for g in range(len(POOL_WINDOWS))]
    return ins, in_specs, out_specs, scratch, final


def _pool_prompt(x, pending, g, w_in, mix, scale, w_out, fg, batch, seq, tp):
    ns = seq // tp
    ins, in_specs, out_specs, scratch, final = _pool_prompt_operands(
        x, pending, g, w_in, mix, scale, w_out, fg, tp, lambda b, i: b * ns + i, lambda b, i: b, False)
    return pl.pallas_call(
        functools.partial(_pool_prompt_kernel, tp=tp, final=final, pending=bool(pending)),
        grid=(batch, ns),
        in_specs=in_specs,
        out_specs=out_specs,
        out_shape=[jax.ShapeDtypeStruct((batch * seq, D_MODEL), F32),
                   jax.ShapeDtypeStruct((batch, HALO, D_MODEL), F32)],
        scratch_shapes=scratch,
        compiler_params=_cparams(("parallel", "arbitrary")),
        name="pool_prompt",
    )(*ins)


def _pool_sample_kernel(*refs, final):
    if final:
        x_ref, st_ref, g_ref, win_ref, mix_ref, scale_ref, wout_ref, fg_ref, y_ref, u_ref = refs
    else:
        x_ref, st_ref, g_ref, win_ref, mix_ref, scale_ref, wout_ref, y_ref, u_ref = refs
        fg_ref = None
    x = x_ref[...]
    hb = _rms_bf16(x, g_ref[...])
    u = _dot(hb, win_ref[:, :D_MODEL])
    gate = _dot(hb, win_ref[:, D_MODEL:])
    u_ref[...] = u
    pooled = []
    for g, w in enumerate(POOL_WINDOWS):
        c0 = g * POOL_GROUP
        acc = u[:, c0:c0 + POOL_GROUP]
        for k in range(1, w):
            acc = acc + st_ref[POOL_STATE - k, :, c0:c0 + POOL_GROUP]
        pooled.append(acc / float(w))
    _pool_layer_tail(x, u, gate, pooled, mix_ref, scale_ref, wout_ref, fg_ref, y_ref)


def _pool_sample(x, state_t, g, w_in, mix, scale, w_out, fg):
    m = x.shape[0]
    final = fg is not None
    ins = [x, state_t, g, w_in, mix, scale, w_out] + ([fg] if final else [])
    return pl.pallas_call(
        functools.partial(_pool_sample_kernel, final=final),
        grid=(1,),
        in_specs=[_full(a.shape) for a in ins],
        out_specs=[_full((m, D_MODEL)), _full((m, D_MODEL))],
        out_shape=[jax.ShapeDtypeStruct((m, D_MODEL), F32)] * 2,
        compiler_params=_cparams(("arbitrary",)),
        name="pool_sample",
    )(*ins)


def _t5_bucket_np(dist):
    n = np.maximum(dist, 0)
    max_exact = N_BUCKETS // 2
    nf = np.maximum(n, 1).astype(np.float32)
    large = max_exact + (np.log(nf / max_exact) / math.log(MAX_DISTANCE / max_exact)
                         * (N_BUCKETS - max_exact)).astype(np.int32)
    large = np.minimum(large, N_BUCKETS - 1)
    return np.where(n < max_exact, n, large)


def _swa_prompt_kernel(sink_ref, rb_ref, qt_ref, kp_ref, kc_ref, vp_ref, vc_ref, bucket_ref, gate_ref,
                       wout_ref, x_ref, y_ref, bias_ref, o_buf, *, nb):
    i = pl.program_id(1)
    cols = GROUP * WINDOW

    @pl.when(i == 0)
    def _():
        bucket = bucket_ref[...]
        hits = [bucket == bk for bk in range(N_BUCKETS)]
        for hd in range(N_HEADS):
            h, g = divmod(hd, GROUP)
            b = jnp.full((2 * WINDOW, WINDOW), NEG, F32)
            for bk in range(N_BUCKETS):
                b = jnp.where(hits[bk], rb_ref[bk, hd] * LOG2E, b)
            bias_ref[h, :, g * WINDOW:(g + 1) * WINDOW] = b

    key_row = lax.broadcasted_iota(jnp.int32, (2 * WINDOW, cols), 0)
    no_prev = jnp.logical_and(i == 0, key_row < WINDOW)
    sinks = [jnp.concatenate([jnp.full((1, WINDOW), sink_ref[h * GROUP + g] * LOG2E, F32) for g in range(GROUP)],
                             axis=1) for h in range(N_KV)]

    def logits(jb, h):
        if jb == 0:
            kband = jnp.concatenate([kp_ref[h], kc_ref[h, 0:WINDOW, :]], axis=0)
        else:
            kband = kc_ref[h, (jb - 1) * WINDOW:(jb + 1) * WINDOW, :]
        s = _dot(kband, qt_ref[h, jb]) + bias_ref[h]
        return jnp.where(no_prev, NEG, s) if jb == 0 else s

    def attend(jb, h, s):
        vprev = vp_ref[h] if jb == 0 else vc_ref[h, jb - 1]
        vband = jnp.concatenate([vprev, vc_ref[h, jb]], axis=1)
        m = jnp.maximum(jnp.max(s, axis=0, keepdims=True), sinks[h])
        p = jnp.exp2(s - m)
        denom = jnp.sum(p, axis=0, keepdims=True) + jnp.exp2(sinks[h] - m)
        ot = _dot(vband, p.astype(BF16)) * (1.0 / denom)
        o_heads = jnp.concatenate([ot[:, g * WINDOW:(g + 1) * WINDOW] for g in range(GROUP)], axis=0)
        o_buf[jb * WINDOW:(jb + 1) * WINDOW, h * GROUP * HEAD_DIM:(h + 1) * GROUP * HEAD_DIM] = o_heads.T

    units = [(jb, h) for jb in range(nb) for h in range(N_KV)]
    s = logits(*units[0])
    for u, unit in enumerate(units):
        s_next = logits(*units[u + 1]) if u + 1 < len(units) else None
        attend(*unit, s)
        s = s_next
    ob = (o_buf[...] * _silu(gate_ref[...])).astype(BF16)
    y_ref[...] = x_ref[...] + _dot(ob, wout_ref[...])


def _swa_prompt(qt, kh, vtb, gate, w_out, x, rel_bias, sinks, batch, seq, tm):
    ns = seq // tm
    nb = tm // WINDOW
    dist = np.arange(WINDOW)[None, :] + WINDOW - np.arange(2 * WINDOW)[:, None]
    bucket = np.where((dist >= 0) & (dist < WINDOW), _t5_bucket_np(dist), -1).astype(np.int32)
    tile = pl.BlockSpec((tm, D_MODEL), lambda b, i: (b * ns + i, 0))
    smem = pl.BlockSpec(memory_space=pltpu.SMEM)
    prev_blk = lambda i: jnp.maximum(i * nb - 1, 0)
    return pl.pallas_call(
        functools.partial(_swa_prompt_kernel, nb=nb),
        grid=(batch, ns),
        in_specs=[smem, smem,
                  pl.BlockSpec((None, N_KV, nb, HEAD_DIM, GROUP * WINDOW), lambda b, i: (b, 0, i, 0, 0)),
                  pl.BlockSpec((N_KV, WINDOW, HEAD_DIM), lambda b, i: (0, b * ns * nb + prev_blk(i), 0)),
                  pl.BlockSpec((N_KV, tm, HEAD_DIM), lambda b, i: (0, b * ns + i, 0)),
                  pl.BlockSpec((None, N_KV, None, HEAD_DIM, WINDOW), lambda b, i: (b, 0, prev_blk(i), 0, 0)),
                  pl.BlockSpec((None, N_KV, nb, HEAD_DIM, WINDOW), lambda b, i: (b, 0, i, 0, 0)),
                  _full(bucket.shape), tile, _full(w_out.shape), tile],
        out_specs=tile,
        out_shape=jax.ShapeDtypeStruct((batch * seq, D_MODEL), F32),
        scratch_shapes=[pltpu.VMEM((N_KV, 2 * WINDOW, GROUP * WINDOW), F32),
                        pltpu.VMEM((tm, D_MODEL), F32)],
        compiler_params=_cparams(("parallel", "arbitrary")),
        name="swa_prompt",
    )(sinks, rel_bias, qt, kh, kh, vtb, vtb, jnp.asarray(bucket), gate, w_out, x)


def _head_diag(o_full):
    out = jnp.zeros((N_HEADS, HEAD_DIM), F32)
    row_kv = lax.broadcasted_iota(jnp.int32, (N_HEADS, HEAD_DIM), 0) // GROUP
    for h in range(N_KV):
        out = out + jnp.where(row_kv == h, o_full[:, h * HEAD_DIM:(h + 1) * HEAD_DIM], 0.0)
    return out


def _swa_sample_kernel(qm_ref, kc_ref, vc_ref, kn_ref, vn_ref, knt_ref, vnt_ref, bias_ref, bias0_ref,
                       sink_ref, o_ref, wk_ref, wv_ref, *, bt):
    i = pl.program_id(0)
    lane = lax.broadcasted_iota(jnp.int32, (KV_WIDTH, WINDOW), 1)
    sink = sink_ref[...]
    for e in range(bt):
        b = i * bt + e
        kc = kc_ref[e]
        vc = vc_ref[e]
        qm = qm_ref[e]
        s = _dot(qm, kc.astype(BF16)) + bias_ref[...]
        s_n = jnp.sum(qm.astype(F32) * _bf16_round(kn_ref[e]), axis=1, keepdims=True) + bias0_ref[...]
        m = jnp.maximum(jnp.maximum(jnp.max(s, axis=1, keepdims=True), s_n), sink)
        p = jnp.exp(s - m)
        p_n = jnp.exp(s_n - m)
        denom = jnp.sum(p, axis=1, keepdims=True) + p_n + jnp.exp(sink - m)
        o_full = _dot_nt((p / denom).astype(BF16), vc.astype(BF16))
        o_full = o_full + _bf16_round(p_n / denom) * _bf16_round(vn_ref[e])
        o_ref[e] = _head_diag(o_full)
        kcol = jnp.sum(jnp.where(lane == b, knt_ref[...], 0.0), axis=1, keepdims=True)
        vcol = jnp.sum(jnp.where(lane == b, vnt_ref[...], 0.0), axis=1, keepdims=True)
        wk_ref[e] = jnp.where(lane == WINDOW - 1, kcol, pltpu.roll(kc, WINDOW - 1, 1))
        wv_ref[e] = jnp.where(lane == WINDOW - 1, vcol, pltpu.roll(vc, WINDOW - 1, 1))


def _swa_sample(qm, kc, vc, kn, vn, knt, vnt, bias_keys, bias0, sinks, bt):
    m = qm.shape[0]
    blk3 = lambda s1, s2: pl.BlockSpec((bt, s1, s2), lambda i: (i, 0, 0))
    cache = jax.ShapeDtypeStruct((m, KV_WIDTH, WINDOW), F32)
    return pl.pallas_call(
        functools.partial(_swa_sample_kernel, bt=bt),
        grid=(m // bt,),
        in_specs=[blk3(N_HEADS, KV_WIDTH), blk3(KV_WIDTH, WINDOW), blk3(KV_WIDTH, WINDOW),
                  blk3(1, KV_WIDTH), blk3(1, KV_WIDTH), _full(knt.shape), _full(vnt.shape),
                  _full(bias_keys.shape), _full(bias0.shape), _full(sinks.shape)],
        out_specs=[blk3(N_HEADS, HEAD_DIM), blk3(KV_WIDTH, WINDOW), blk3(KV_WIDTH, WINDOW)],
        out_shape=[jax.ShapeDtypeStruct((m, N_HEADS, HEAD_DIM), F32), cache, cache],
        compiler_params=_cparams(("parallel",)),
        name="swa_sample",
    )(qm, kc, vc, kn, vn, knt, vnt, bias_keys, bias0, sinks)


def _split3(x):
    hi = x.astype(BF16)
    r = x - hi.astype(F32)
    mid = r.astype(BF16)
    lo = (r - mid.astype(F32)).astype(BF16)
    return hi, mid, lo


def _cumsum_lanes_blocks(x, blk):
    n = x.shape[0]
    hi, mid, lo = _split3(x)
    r_i = lax.broadcasted_iota(jnp.int32, (blk, blk), 0)
    c_i = lax.broadcasted_iota(jnp.int32, (blk, blk), 1)
    upper = jnp.where(r_i <= c_i, 1.0, 0.0).astype(BF16)
    r = _dot(jnp.concatenate([hi, mid, lo], axis=0), upper)
    return r[0:n] + r[n:2 * n] + r[2 * n:]


def _fox_decay_kernel(x_ref, kin_ref, kout_ref, carry, *, blk):
    i = pl.program_id(1)

    @pl.when(i == 0)
    def _():
        carry[...] = jnp.zeros_like(carry)

    c = _cumsum_lanes_blocks(x_ref[...], blk) + carry[:, 0:1]
    carry[...] = jnp.broadcast_to(c[:, blk - 1:blk], carry.shape)
    hi, mid, lo = _split3(c * LOG2E)
    pieces = jnp.concatenate([hi, mid, lo], axis=0).astype(F32)
    pad = jnp.zeros((LANES - N_PIECES * N_HEADS, blk), F32)
    pieces_t = jnp.concatenate([pieces, pad], axis=0).T.astype(BF16)
    src = lax.broadcasted_iota(jnp.int32, (LANES, AUG), 0)
    dst = lax.broadcasted_iota(jnp.int32, (LANES, AUG), 1) - HEAD_DIM
    in_aug = jnp.logical_and(dst >= 0, dst < N_PIECES * GROUP)
    for h in range(N_KV):
        want = (dst >> GROUP_SHIFT) * N_HEADS + h * GROUP + (dst & (GROUP - 1))
        place = jnp.where(jnp.logical_and(in_aug, src == want), 1.0, 0.0).astype(BF16)
        kout_ref[h] = kin_ref[h] + _dot(pieces_t, place).astype(BF16)


def _fox_decay(logft, kaug, blk):
    b, h, s = logft.shape
    ns = s // blk
    kspec = pl.BlockSpec((N_KV, blk, AUG), lambda bi, i: (0, bi * ns + i, 0))
    return pl.pallas_call(
        functools.partial(_fox_decay_kernel, blk=blk),
        grid=(b, ns),
        in_specs=[pl.BlockSpec((None, h, blk), lambda bi, i: (bi, 0, i)), kspec],
        out_specs=kspec,
        out_shape=jax.ShapeDtypeStruct(kaug.shape, kaug.dtype),
        scratch_shapes=[pltpu.VMEM((h, LANES), F32)],
        input_output_aliases={1: 0},
        compiler_params=_cparams(("parallel", "arbitrary")),
        name="fox_decay",
    )(logft, kaug)


def _decode_element(qm, kn, vn, fn, kbuf, vbuf, fbuf):
    cp = kbuf.shape[0]
    cin = _cumsum_lanes_blocks(fbuf[...].reshape(cp * N_HEADS, PAGE), PAGE).reshape(cp, N_HEADS, PAGE)
    carry = jnp.zeros((N_HEADS, 1), F32)
    cs = []
    for p in range(cp):
        cs.append(cin[p] + carry)
        carry = carry + cin[p][:, PAGE - 1:PAGE]
    kcat = jnp.concatenate([kbuf[p] for p in range(cp)], axis=1).astype(BF16)
    t = _dot(qm, kcat) - jnp.concatenate(cs, axis=1)
    t_n = jnp.sum(qm.astype(F32) * _bf16_round(kn), axis=1, keepdims=True) - (carry + fn)
    m = jnp.maximum(jnp.max(t, axis=1, keepdims=True), t_n)
    p = jnp.exp(t - m)
    p_n = jnp.exp(t_n - m)
    denom = jnp.sum(p, axis=1, keepdims=True) + p_n
    vcat = jnp.concatenate([vbuf[p_] for p_ in range(cp)], axis=1).astype(BF16)
    pb = p.astype(BF16)
    half = KV_WIDTH // 2
    acc = jnp.concatenate([_dot_nt(pb, vcat[:half]), _dot_nt(pb, vcat[half:])], axis=1)
    acc = acc + _bf16_round(p_n) * _bf16_round(vn)
    return _head_diag(acc / denom)


def _fox_prompt_kernel(*refs, tq, tk, decode):
    assert tq == tk
    if decode:
        (pt_ref, qt_ref, k_ref, vt_ref, qm_ref, kn_ref, vn_ref, fn_ref, ck_hbm, cv_hbm, cf_hbm,
         o_ref, os_ref, *scratch) = refs
        kbuf, vbuf, fbuf, sems = scratch[6:]
        scratch = scratch[:6]
    else:
        qt_ref, k_ref, vt_ref, o_ref, *scratch = refs
    t = pl.program_id(2)
    m_ref, acc_ref, sa_ref, sb_ref, ca_ref, cb_ref = scratch
    cols = GROUP * tq

    def sample_row(tile):
        if not decode:
            return
        step = (pl.program_id(0) * pl.num_programs(1) + pl.program_id(1)) * pl.num_programs(2) + t
        row = 2 * step + tile
        n_rows = 2 * pl.num_programs(0) * pl.num_programs(1) * pl.num_programs(2)
        n_pages = kbuf.shape[0]

        def copies(r, p):
            page = pt_ref[r, p]
            return (pltpu.make_async_copy(ck_hbm.at[page], kbuf.at[p], sems.at[0]),
                    pltpu.make_async_copy(cv_hbm.at[page], vbuf.at[p], sems.at[1]),
                    pltpu.make_async_copy(cf_hbm.at[page], fbuf.at[p], sems.at[2]))

        def start(r):
            for p in range(n_pages):
                for cpy in copies(r, p):
                    cpy.start()

        @pl.when(row == 0)
        def _():
            start(row)

        for p in range(n_pages):
            for cpy in copies(row, p):
                cpy.wait()
        os_ref[tile] = _decode_element(qm_ref[tile], kn_ref[tile], vn_ref[tile], fn_ref[tile], kbuf, vbuf, fbuf)

        @pl.when(row + 1 < n_rows)
        def _():
            start(row + 1)

    def reset():
        m_ref[...] = jnp.full(m_ref.shape, NEG, F32)
        acc_ref[...] = jnp.zeros(acc_ref.shape, F32)

    def logits(tile, kb, s_ref, c_ref):
        k0 = pl.multiple_of(kb * tk, tk)
        s = _dot(k_ref[pl.ds(k0, tk), :], qt_ref[tile])
        s_ref[...] = s
        c_ref[...] = jnp.max(s, axis=0, keepdims=True)

    def accumulate(kb, s_ref, c_ref, masked):
        s = s_ref[...]
        if masked:
            kpos = lax.broadcasted_iota(jnp.int32, (tk, cols), 0)
            qpos = lax.broadcasted_iota(jnp.int32, (tk, cols), 1) & (tq - 1)
            s = jnp.where(kpos <= qpos, s, NEG)
            cmax = jnp.max(s, axis=0, keepdims=True)
        else:
            cmax = c_ref[...]
        m_old = m_ref[...]
        m_new = jnp.maximum(m_old, cmax)
        alpha = jnp.exp2(m_old - m_new)
        p = jnp.exp2(s - m_new)
        acc_ref[...] = alpha * acc_ref[...] + _dot(vt_ref[kb], p.astype(BF16))
        m_ref[...] = m_new

    def finish(tile):
        o = acc_ref[0:HEAD_DIM, :] / acc_ref[HEAD_DIM:HEAD_DIM + 1, :]
        o_ref[tile * tq:(tile + 1) * tq, :] = jnp.concatenate(
            [o[:, g * tq:(g + 1) * tq] for g in range(GROUP)], axis=0).T

    def pairs(tile, first, second):
        def body(j, carry):
            kb = 2 * j
            logits(tile, kb + 1, *second)
            accumulate(kb, *first, False)
            logits(tile, kb + 2, *first)
            accumulate(kb + 1, *second, False)
            return carry
        lax.fori_loop(0, t, body, 0)

    buf_a, buf_b = (sa_ref, ca_ref), (sb_ref, cb_ref)
    sample_row(0)
    reset()
    logits(0, 0, *buf_a)
    pairs(0, buf_a, buf_b)
    logits(1, 0, *buf_b)
    accumulate(2 * t, *buf_a, True)
    finish(0)
    sample_row(1)
    reset()
    pairs(1, buf_b, buf_a)
    logits(1, 2 * t + 1, *buf_a)
    accumulate(2 * t, *buf_b, False)
    accumulate(2 * t + 1, *buf_a, True)
    finish(1)


def _fox_prompt_fits_decode(batch, seq, tq, n_rows):
    return n_rows == batch * N_KV * (seq // tq)


def _fox_prompt(qt, kaug, vtb, batch, seq, tq, decode=None):
    nq = seq // tq
    nk, tk = vtb.shape[2], vtb.shape[4]
    cols = GROUP * tq
    nt = nq // 2
    ins = [qt, kaug, vtb]
    in_specs = [pl.BlockSpec((None, None, 2, AUG, cols), lambda b, h, t, *_: (b, h, t, 0, 0)),
                pl.BlockSpec((None, seq, AUG), lambda b, h, t, *_: (h, b, 0)),
                pl.BlockSpec((None, None, nk, V_ROWS, tk), lambda b, h, t, *_: (b, h, 0, 0, 0))]
    out_specs = [pl.BlockSpec((2 * tq, GROUP * HEAD_DIM), lambda b, h, t, *_: (b * nt + t, h))]
    out_shape = [jax.ShapeDtypeStruct((batch * seq, D_MODEL), F32)]
    scratch = [pltpu.VMEM((1, cols), F32),
               pltpu.VMEM((V_ROWS, cols), F32),
               pltpu.VMEM((tk, cols), F32), pltpu.VMEM((tk, cols), F32),
               pltpu.VMEM((1, cols), F32), pltpu.VMEM((1, cols), F32)]
    prefetch = []
    if decode is not None:
        page_table, qm, kn, vn, fn, ck, cv, cf = decode
        n_rows, n_pages = page_table.shape
        assert _fox_prompt_fits_decode(batch, seq, tq, n_rows)
        pair = lambda s1, s2: pl.BlockSpec((2, s1, s2), lambda b, h, t, pt: ((b * N_KV + h) * nt + t, 0, 0))
        any_spec = pl.BlockSpec(memory_space=pl.ANY)
        prefetch = [page_table]
        ins += [qm, kn, vn, fn, ck, cv, cf]
        in_specs += [pair(N_HEADS, KV_WIDTH), pair(1, KV_WIDTH), pair(1, KV_WIDTH), pair(N_HEADS, 1),
                     any_spec, any_spec, any_spec]
        out_specs.append(pair(N_HEADS, HEAD_DIM))
        out_shape.append(jax.ShapeDtypeStruct((n_rows, N_HEADS, HEAD_DIM), F32))
        scratch += [pltpu.VMEM((n_pages, KV_WIDTH, PAGE), F32), pltpu.VMEM((n_pages, KV_WIDTH, PAGE), F32),
                    pltpu.VMEM((n_pages, N_HEADS, PAGE), F32), pltpu.SemaphoreType.DMA((3,))]
    semantics = ("arbitrary",) * 3 if decode is not None else ("parallel", "parallel", "arbitrary")
    outs = pl.pallas_call(
        functools.partial(_fox_prompt_kernel, tq=tq, tk=tk, decode=decode is not None),
        grid_spec=pltpu.PrefetchScalarGridSpec(
            num_scalar_prefetch=len(prefetch),
            grid=(batch, N_KV, nt),
            in_specs=in_specs,
            out_specs=out_specs,
            scratch_shapes=scratch,
        ),
        out_shape=out_shape,
        compiler_params=_cparams(semantics),
        name="fox_prompt",
    )(*prefetch, *ins)
    return outs if decode is not None else outs[0]


N_DECODE_SCRATCH = 8


def _fox_sample_kernel(pt_ref, qm_ref, kn_ref, vn_ref, fn_ref, ck_hbm, cv_hbm, cf_hbm, *rest, cp, nchunk, rider):
    n_in = _n_pool_inputs(rider["final"], rider["pending"]) if rider else 0
    rider_in, o_ref, rest = rest[:n_in], rest[n_in], rest[n_in + 1:]
    rider_out, rest = (rest[:2], rest[2:]) if rider else ((), rest)
    kbuf, vbuf, fbuf, sems, m_ref, l_ref, acc_ref, carry_ref = rest[:N_DECODE_SCRATCH]
    rider_bufs = rest[N_DECODE_SCRATCH:]
    b = pl.program_id(0)
    c = pl.program_id(1)
    nb = pl.num_programs(0)
    step = b * nchunk + c
    slot = step % 2

    def copies(bb, cc, sl, p):
        page = pt_ref[bb, cc * cp + p]
        return (pltpu.make_async_copy(ck_hbm.at[page], kbuf.at[sl, p], sems.at[sl, 0]),
                pltpu.make_async_copy(cv_hbm.at[page], vbuf.at[sl, p], sems.at[sl, 1]),
                pltpu.make_async_copy(cf_hbm.at[page], fbuf.at[sl, p], sems.at[sl, 2]))

    def issue(bb, cc, sl):
        for p in range(cp):
            for cpy in copies(bb, cc, sl, p):
                cpy.start()

    @pl.when(step == 0)
    def _():
        issue(b, c, slot)

    @pl.when(step + 1 < nb * nchunk)
    def _():
        nxt = step + 1
        issue(nxt // nchunk, nxt % nchunk, 1 - slot)

    if rider:
        per_batch = rider["tiles_per_batch"]
        _pool_tile(step % per_batch, per_batch, rider_in, rider_out, rider_bufs,
                   tp=rider["tp"], final=rider["final"], pending=rider["pending"])

    for p in range(cp):
        for cpy in copies(b, c, slot, p):
            cpy.wait()

    @pl.when(c == 0)
    def _():
        m_ref[...] = jnp.full(m_ref.shape, NEG, F32)
        l_ref[...] = jnp.zeros(l_ref.shape, F32)
        acc_ref[...] = jnp.zeros(acc_ref.shape, F32)
        carry_ref[...] = jnp.zeros(carry_ref.shape, F32)

    qm = qm_ref[...]
    cin = _cumsum_lanes_blocks(fbuf[slot].reshape(cp * N_HEADS, PAGE), PAGE).reshape(cp, N_HEADS, PAGE)
    carry = carry_ref[...]
    cs = []
    for p in range(cp):
        cs.append(cin[p] + carry)
        carry = carry + cin[p][:, PAGE - 1:PAGE]
    carry_ref[...] = carry
    kcat = jnp.concatenate([kbuf[slot, p] for p in range(cp)], axis=1).astype(BF16)
    t = _dot(qm, kcat) - jnp.concatenate(cs, axis=1)
    m_old = m_ref[...]
    m_new = jnp.maximum(m_old, jnp.max(t, axis=1, keepdims=True))
    alpha = jnp.exp(m_old - m_new)
    p = jnp.exp(t - m_new)
    l_new = alpha * l_ref[...] + jnp.sum(p, axis=1, keepdims=True)
    vcat = jnp.concatenate([vbuf[slot, p_] for p_ in range(cp)], axis=1).astype(BF16)
    acc_new = alpha * acc_ref[...] + _dot_nt(p.astype(BF16), vcat)
    m_ref[...] = m_new
    l_ref[...] = l_new
    acc_ref[...] = acc_new

    @pl.when(c == nchunk - 1)
    def _():
        s_n = jnp.sum(qm.astype(F32) * _bf16_round(kn_ref[...]), axis=1, keepdims=True)
        t_n = s_n - (carry + fn_ref[...])
        m_f = jnp.maximum(m_new, t_n)
        a = jnp.exp(m_new - m_f)
        p_n = jnp.exp(t_n - m_f)
        l_f = a * l_new + p_n
        acc = a * acc_new + _bf16_round(p_n) * _bf16_round(vn_ref[...])
        o_ref[...] = _head_diag(acc / l_f)


def _fox_sample(page_table, qm, kn, vn, fn, ck, cv, cf, cp, pool_layer=None):
    m, n_pages = page_table.shape
    nchunk = n_pages // cp
    per_b = lambda s1, s2: pl.BlockSpec((None, s1, s2), lambda b, c, pt: (b, 0, 0))
    any_spec = pl.BlockSpec(memory_space=pl.ANY)
    ins = [qm, kn, vn, fn, ck, cv, cf]
    in_specs = [per_b(N_HEADS, KV_WIDTH), per_b(1, KV_WIDTH), per_b(1, KV_WIDTH), per_b(N_HEADS, 1),
                any_spec, any_spec, any_spec]
    out_specs = [per_b(N_HEADS, HEAD_DIM)]
    out_shape = [jax.ShapeDtypeStruct((m, N_HEADS, HEAD_DIM), F32)]
    scratch = [pltpu.VMEM((2, cp, KV_WIDTH, PAGE), F32),
               pltpu.VMEM((2, cp, KV_WIDTH, PAGE), F32),
               pltpu.VMEM((2, cp, N_HEADS, PAGE), F32),
               pltpu.SemaphoreType.DMA((2, 3)),
               pltpu.VMEM((N_HEADS, 1), F32), pltpu.VMEM((N_HEADS, 1), F32),
               pltpu.VMEM((N_HEADS, KV_WIDTH), F32), pltpu.VMEM((N_HEADS, 1), F32)]
    assert len(scratch) == N_DECODE_SCRATCH
    rider = None
    if pool_layer is not None:
        *layer, batch, seq = pool_layer
        assert nchunk == 1 and (batch * seq) % m == 0
        tp = batch * seq // m
        per_batch = seq // tp
        assert seq % tp == 0 and tp % HALO == 0
        p_ins, p_in_specs, p_out_specs, p_scratch, final = _pool_prompt_operands(
            *layer, tp, lambda b, c, pt: b, lambda b, c, pt: b // per_batch, True)
        rider = dict(tp=tp, final=final, pending=layer[1] is not None, tiles_per_batch=per_batch)
        ins += p_ins
        in_specs += p_in_specs
        out_specs += p_out_specs
        out_shape += [jax.ShapeDtypeStruct((batch * seq, D_MODEL), F32),
                      jax.ShapeDtypeStruct((batch, HALO, D_MODEL), F32)]
        scratch += p_scratch
    outs = pl.pallas_call(
        functools.partial(_fox_sample_kernel, cp=cp, nchunk=nchunk, rider=rider),
        grid_spec=pltpu.PrefetchScalarGridSpec(
            num_scalar_prefetch=1,
            grid=(m, nchunk),
            in_specs=in_specs,
            out_specs=out_specs,
            scratch_shapes=scratch,
        ),
        out_shape=out_shape,
        compiler_params=pltpu.CompilerParams(dimension_semantics=("arbitrary", "arbitrary"),
                                             vmem_limit_bytes=VMEM_LIMIT_RIDER if rider else VMEM_LIMIT),
        name="fox_sample",
    )(page_table, *ins)
    return outs if rider else outs[0]


def _head_rows_q(q_rows):
    m = q_rows.shape[0]
    q4 = q_rows.reshape(m, N_KV, GROUP, HEAD_DIM)
    eye = jnp.eye(N_KV, dtype=q_rows.dtype)
    return (q4[:, :, :, None, :] * eye[None, :, None, :, None]).reshape(m, N_HEADS, KV_WIDTH)


def _tokens_last(x):
    lead = x.shape[:-3]
    n = len(lead)
    xt = jnp.transpose(x, tuple(range(n)) + (n + 1, n + 2, n))
    return xt.reshape(lead + (KV_WIDTH, x.shape[-3]))


def _tokens_first(xt):
    lead = xt.shape[:-2]
    n = len(lead)
    x4 = xt.reshape(lead + (N_KV, HEAD_DIM, xt.shape[-1]))
    return jnp.transpose(x4, tuple(range(n)) + (n + 2, n, n + 1))


def kernel(x_prompt, x_sample, state_pool, cache_win_k, cache_win_v, cache_fox_k, cache_fox_v,
           cache_fox_logf, page_table, norm_g, final_norm_g, rel_bias, pool_w_in, pool_mix,
           pool_scale, pool_w_out, swa_w_in, swa_sinks, swa_w_out, fox_w_in, fox_f_bias, fox_w_out):
    batch, seq, _ = x_prompt.shape
    db = x_sample.shape[0]
    depth = norm_g.shape[0]
    mp = batch * seq
    tm_p, tm_s = 512, db

    xp = x_prompt.reshape(mp, D_MODEL)
    xs = x_sample.reshape(db, D_MODEL)

    rb = rel_bias.astype(F32)
    dist_keys = WINDOW - np.arange(WINDOW)
    bias_keys = jnp.where((dist_keys < WINDOW)[None, :], rb[_t5_bucket_np(dist_keys)].T, NEG)
    bias0 = rb[0].reshape(N_HEADS, 1)
    fg = final_norm_g.reshape(1, D_MODEL)

    pool_p, pool_s = [], []
    wk_p, wv_p, wk_s, wv_s = [], [], [], []
    fk_p, fv_p, fl_p, fk_s, fv_s, fl_s = [], [], [], [], [], []
    def pool_layer_args(layer):
        jj = layer // 3
        fg_l = fg if layer == depth - 1 else None
        return (norm_g[layer].reshape(1, D_MODEL), pool_w_in[jj].astype(BF16), pool_mix[jj].astype(BF16),
                pool_scale[jj].reshape(1, D_MODEL), pool_w_out[jj].astype(BF16), fg_l)

    prompt_done = None
    pending = None
    for i in range(depth):
        kind, j = i % 3, i // 3
        g = norm_g[i].reshape(1, D_MODEL)
        if kind == 0:
            g, w_in, mix, scale, w_out, fg_l = pool_layer_args(i)
            if prompt_done is None:
                xp, u_tail = _pool_prompt(xp, pending, g, w_in, mix, scale, w_out, fg_l, batch, seq, tm_p)
                pending = None
            else:
                xp, u_tail = prompt_done
                prompt_done = None
            pool_p.append(u_tail[:, HALO - POOL_STATE:])
            xs, u_s = _pool_sample(xs, jnp.transpose(state_pool[j], (1, 0, 2)), g, w_in, mix, scale, w_out, fg_l)
            pool_s.append(jnp.concatenate([state_pool[j][:, 1:], u_s[:, None, :]], axis=1))
        elif kind == 1:
            w_in = swa_w_in[j].astype(BF16)
            w_out = swa_w_out[j].astype(BF16)
            sinks = swa_sinks[j].astype(F32)
            qt_p, kh_p, kt_p, vt_p, vtb_p, gate_p = _swa_proj(xp, g, w_in, tm_p, batch, seq, True)
            xp = _swa_prompt(qt_p, kh_p, vtb_p, gate_p, w_out, xp, rb, sinks, batch, seq, tm_p)
            wk_p.append(_tokens_first(kt_p[:, :, seq - WINDOW:]))
            wv_p.append(_tokens_first(vt_p[:, :, seq - WINDOW:]))
            q_s, k_s, v_s, kt_s, vt_s, gate_s = _swa_proj(xs, g, w_in, tm_s, 1, db, False)
            o_s, wk, wv = _swa_sample(_head_rows_q(q_s), _tokens_last(cache_win_k[j]),
                                      _tokens_last(cache_win_v[j]), k_s[:, None, :], v_s[:, None, :],
                                      kt_s[0], vt_s[0], bias_keys, bias0, sinks.reshape(N_HEADS, 1),
                                      SWA_ROWS_PER_STEP)
            xs = _out_proj(o_s.reshape(db, D_MODEL), gate_s, w_out, xs, tm_s)
            wk_s.append(_tokens_first(wk))
            wv_s.append(_tokens_first(wv))
        else:
            w_full = fox_w_in[j]
            nqkv = D_MODEL + 2 * KV_WIDTH
            w_in = jnp.concatenate([w_full[:, :nqkv], w_full[:, nqkv + N_HEADS:]], axis=1).astype(BF16)
            wf = jnp.pad(w_full[:, nqkv:nqkv + N_HEADS], ((0, 0), (0, LANES - N_HEADS))).astype(BF16)
            fb = jnp.pad(fox_f_bias[j].astype(F32), (0, LANES - N_HEADS)).reshape(1, LANES)
            w_out = fox_w_out[j].astype(BF16)
            qt, kt_p, vt_p, kaug, vtb, logft_p, gate_p = _fox_proj(xp, g, w_in, wf, fb, FOX_TILE, batch, seq, True)
            kaug = _fox_decay(logft_p, kaug, FOX_TILE)
            fk_p.append(_tokens_first(kt_p))
            fv_p.append(_tokens_first(vt_p))
            fl_p.append(jnp.transpose(logft_p, (0, 2, 1)))
            q_s, k_s, v_s, kt_s, vt_s, logf_s, logft_s, gate_s = _fox_proj(
                xs, g, w_in, wf, fb, tm_s, 1, db, False)
            decode_args = (page_table, _head_rows_q(q_s), k_s[:, None, :], v_s[:, None, :], logf_s[:, :, None],
                           _tokens_last(cache_fox_k[j]), _tokens_last(cache_fox_v[j]),
                           jnp.transpose(cache_fox_logf[j], (0, 2, 1)))
            next_is_pool = i + 1 < depth and (i + 1) % 3 == 0
            if _fox_prompt_fits_decode(batch, seq, FOX_TILE, db):
                o_p, o_s = _fox_prompt(qt, kaug, vtb, batch, seq, FOX_TILE, decode=decode_args)
                if next_is_pool:
                    pending = (o_p, gate_p, w_out)
                else:
                    xp = _out_proj(o_p, gate_p, w_out, xp, tm_p)
            else:
                o_p = _fox_prompt(qt, kaug, vtb, batch, seq, FOX_TILE)
                if next_is_pool:
                    o_s, xp, u_tail = _fox_sample(
                        *decode_args, FOX_PAGES_PER_STEP,
                        pool_layer=(xp, (o_p, gate_p, w_out), *pool_layer_args(i + 1), batch, seq))
                    prompt_done = (xp, u_tail)
                else:
                    xp = _out_proj(o_p, gate_p, w_out, xp, tm_p)
                    o_s = _fox_sample(*decode_args, FOX_PAGES_PER_STEP)
            xs = _out_proj(o_s.reshape(db, D_MODEL), gate_s, w_out, xs, tm_s)
            fk_s.append(_tokens_first(kt_s[0])[:, None])
            fv_s.append(_tokens_first(vt_s[0])[:, None])
            fl_s.append(logft_s[0].T[:, None, :])

    if (depth - 1) % 3 != 0:
        xp, xs = _final_norm(xp, fg, tm_p), _final_norm(xs, fg, tm_s)
    y_prompt = xp.reshape(batch, seq, D_MODEL)
    y_sample = xs.reshape(db, 1, D_MODEL)
    return (y_prompt, y_sample, jnp.stack(pool_p), jnp.stack(pool_s), jnp.stack(wk_p), jnp.stack(wv_p),
            jnp.stack(wk_s), jnp.stack(wv_s), jnp.stack(fk_p), jnp.stack(fv_p), jnp.stack(fl_p),
            jnp.stack(fk_s), jnp.stack(fv_s), jnp.stack(fl_s))
```

```python
import functools
import math

import numpy as np
import jax
import jax.numpy as jnp
from jax import lax
from jax.experimental import pallas as pl
from jax.experimental.pallas import tpu as pltpu

D_MODEL = 1024
HEAD_DIM = 64
N_HEADS = 16
N_KV = 4
GROUP = 4
KV_WIDTH = N_KV * HEAD_DIM
POOL_WINDOWS = (2, 4, 8, 16)
POOL_GROUP = 256
POOL_STATE = 15
WINDOW = 128
N_BUCKETS = 32
MAX_DISTANCE = 128
SCALE = HEAD_DIM ** -0.5
EPS = 1e-6
NEG = -1e30
PAGE = 128
LANES = 128

BF16 = jnp.bfloat16
F32 = jnp.float32

VMEM_LIMIT = 56 * 1024 * 1024
VMEM_LIMIT_RIDER = 62 * 1024 * 1024


def _cparams(sem):
    return pltpu.CompilerParams(dimension_semantics=sem, vmem_limit_bytes=VMEM_LIMIT)


def _rms_bf16(x, g):
    ms = jnp.mean(x * x, axis=-1, keepdims=True)
    return (x * lax.rsqrt(ms + EPS) * g).astype(BF16)


def _silu(x):
    return x * jax.nn.sigmoid(x)


def _dot(a, b):
    return jnp.dot(a, b, preferred_element_type=F32)


def _dot_nt(a, b):
    return lax.dot_general(a, b, (((1,), (1,)), ((), ())), preferred_element_type=F32)


def _bf16_round(x):
    return x.astype(BF16).astype(F32)


def _full(shape):
    n = len(shape)
    return pl.BlockSpec(shape, lambda *_: (0,) * n)


LOG2E = 1.4426950408889634


def _swa_proj_kernel(x_ref, g_ref, w_ref, *out_refs, prompt):
    if prompt:
        qt_ref, kh_ref, kt_ref, vt_ref, vtb_ref, gate_ref = out_refs
    else:
        q_ref, k_ref, v_ref, kt_ref, vt_ref, gate_ref = out_refs
    hb = _rms_bf16(x_ref[...], g_ref[...])
    q = _dot(hb, w_ref[:, :D_MODEL])
    k = _dot(hb, w_ref[:, D_MODEL:D_MODEL + KV_WIDTH])
    v = _dot(hb, w_ref[:, D_MODEL + KV_WIDTH:D_MODEL + 2 * KV_WIDTH])
    vt = v.T
    kt_ref[...] = k.T
    vt_ref[...] = vt
    if prompt:
        tm = q.shape[0]
        qt = (q * (SCALE * LOG2E)).T.astype(BF16)
        for hd in range(N_HEADS):
            h, g = divmod(hd, GROUP)
            for jb in range(tm // WINDOW):
                qt_ref[h, jb, :, g * WINDOW:(g + 1) * WINDOW] = (
                    qt[hd * HEAD_DIM:(hd + 1) * HEAD_DIM, jb * WINDOW:(jb + 1) * WINDOW])
        for h in range(N_KV):
            kh_ref[h] = k[:, h * HEAD_DIM:(h + 1) * HEAD_DIM].astype(BF16)
            ones_row = lax.broadcasted_iota(jnp.int32, (V_ROWS - HEAD_DIM, tm), 0) == 0
            vth = jnp.concatenate([vt[h * HEAD_DIM:(h + 1) * HEAD_DIM, :],
                                   jnp.where(ones_row, 1.0, 0.0)], axis=0).astype(BF16)
            for jb in range(tm // WINDOW):
                vtb_ref[h, jb] = vth[:, jb * WINDOW:(jb + 1) * WINDOW]
    else:
        q_ref[...] = (q * SCALE).astype(BF16)
        k_ref[...] = k
        v_ref[...] = v
    gate_ref[...] = _dot(hb, w_ref[:, D_MODEL + 2 * KV_WIDTH:])


def _swa_proj(x, g, w, tm, batch, seq, prompt):
    m = x.shape[0]
    ns = seq // tm
    nb = tm // WINDOW
    row = lambda i: (i, 0)
    sds = jax.ShapeDtypeStruct
    tcol = pl.BlockSpec((None, KV_WIDTH, tm), lambda i: (i // ns, 0, i % ns))
    blocks = lambda rows, last: pl.BlockSpec((None, N_KV, nb, rows, last), lambda i: (i // ns, 0, i % ns, 0, 0))
    kv_t_shape = [sds((batch, KV_WIDTH, seq), F32)] * 2
    if prompt:
        out_specs = [blocks(HEAD_DIM, GROUP * WINDOW), pl.BlockSpec((N_KV, tm, HEAD_DIM), lambda i: (0, i, 0)),
                     tcol, tcol, blocks(V_ROWS, WINDOW)]
        out_shape = ([sds((batch, N_KV, seq // WINDOW, HEAD_DIM, GROUP * WINDOW), BF16),
                      sds((N_KV, m, HEAD_DIM), BF16)] + kv_t_shape
                     + [sds((batch, N_KV, seq // WINDOW, V_ROWS, WINDOW), BF16)])
    else:
        out_specs = [pl.BlockSpec((tm, D_MODEL), row), pl.BlockSpec((tm, KV_WIDTH), row),
                     pl.BlockSpec((tm, KV_WIDTH), row), tcol, tcol]
        out_shape = [sds((m, D_MODEL), BF16), sds((m, KV_WIDTH), F32), sds((m, KV_WIDTH), F32)] + kv_t_shape
    out_specs.append(pl.BlockSpec((tm, D_MODEL), row))
    out_shape.append(sds((m, D_MODEL), F32))
    return pl.pallas_call(
        functools.partial(_swa_proj_kernel, prompt=prompt),
        grid=(m // tm,),
        in_specs=[pl.BlockSpec((tm, D_MODEL), row), _full((1, D_MODEL)), _full(w.shape)],
        out_specs=out_specs,
        out_shape=out_shape,
        compiler_params=_cparams(("parallel",)),
        name="swa_proj",
    )(x, g, w)


def _log_sigmoid(x):
    return -(jnp.maximum(-x, 0.0) + jnp.log1p(jnp.exp(-jnp.abs(x))))


AUG = 128
N_PIECES = 3
SWA_ROWS_PER_STEP = 8
V_ROWS = HEAD_DIM + 16
FOX_PAGES_PER_STEP = 64
FOX_TILE = 512
FOX_KEYS = 512
FOX_QUERIES = 256
GROUP_SHIFT = GROUP.bit_length() - 1


def _fox_proj_kernel(x_ref, g_ref, w_ref, wf_ref, fb_ref, *out_refs, prompt):
    if prompt:
        qt_ref, kt_ref, vt_ref, kaug_ref, vtb_ref, logft_ref, gate_ref = out_refs
    else:
        q_ref, k_ref, v_ref, kt_ref, vt_ref, logf_ref, logft_ref, gate_ref = out_refs
    hb = _rms_bf16(x_ref[...], g_ref[...])
    q = _dot(hb, w_ref[:, :D_MODEL])
    k = _dot(hb, w_ref[:, D_MODEL:D_MODEL + KV_WIDTH])
    v = _dot(hb, w_ref[:, D_MODEL + KV_WIDTH:D_MODEL + 2 * KV_WIDTH])
    vt = v.T
    kt_ref[...] = k.T
    vt_ref[...] = vt
    logf = _log_sigmoid(_dot(hb, wf_ref[...]) + fb_ref[...])
    logft_ref[...] = logf.T[:N_HEADS, :]
    if prompt:
        tm = q.shape[0]
        qt = (q * (SCALE * LOG2E)).T.astype(BF16)
        r = lax.broadcasted_iota(jnp.int32, (AUG - HEAD_DIM, tm), 0)
        for hd in range(N_HEADS):
            h, g = divmod(hd, GROUP)
            cols = slice(g * tm, (g + 1) * tm)
            qt_ref[h, 0:HEAD_DIM, cols] = qt[hd * HEAD_DIM:(hd + 1) * HEAD_DIM, :]
            pick = jnp.logical_and(r < N_PIECES * GROUP, (r & (GROUP - 1)) == g)
            qt_ref[h, HEAD_DIM:AUG, cols] = jnp.where(pick, -1.0, 0.0).astype(BF16)
        zeros = jnp.zeros((tm, AUG - HEAD_DIM), F32)
        for h in range(N_KV):
            kaug_ref[h] = jnp.concatenate([k[:, h * HEAD_DIM:(h + 1) * HEAD_DIM], zeros], axis=1).astype(BF16)
            ones_row = lax.broadcasted_iota(jnp.int32, (V_ROWS - HEAD_DIM, tm), 0) == 0
            vth = jnp.concatenate([vt[h * HEAD_DIM:(h + 1) * HEAD_DIM, :],
                                   jnp.where(ones_row, 1.0, 0.0)], axis=0).astype(BF16)
            for kb in range(tm // FOX_KEYS):
                vtb_ref[h, kb] = vth[:, kb * FOX_KEYS:(kb + 1) * FOX_KEYS]
    else:
        q_ref[...] = (q * SCALE).astype(BF16)
        k_ref[...] = k
        v_ref[...] = v
        logf_ref[...] = logf[:, :N_HEADS]
    gate_ref[...] = _dot(hb, w_ref[:, D_MODEL + 2 * KV_WIDTH:])


def _fox_proj(x, g, w, wf, fb, tm, batch, seq, prompt):
    m = x.shape[0]
    ns = seq // tm
    row = lambda i: (i, 0)
    tcol = lambda r: pl.BlockSpec((None, r, tm), lambda i: (i // ns, 0, i % ns))
    sds = jax.ShapeDtypeStruct
    kv_t = [tcol(KV_WIDTH), tcol(KV_WIDTH)]
    kv_t_shape = [sds((batch, KV_WIDTH, seq), F32)] * 2
    if prompt:
        out_specs = ([pl.BlockSpec((None, N_KV, None, AUG, GROUP * tm), lambda i: (i // ns, 0, i % ns, 0, 0))] + kv_t
                     + [pl.BlockSpec((N_KV, tm, AUG), lambda i: (0, i, 0)),
                        pl.BlockSpec((None, N_KV, tm // FOX_KEYS, V_ROWS, FOX_KEYS),
                                     lambda i: (i // ns, 0, i % ns, 0, 0)),
                        tcol(N_HEADS)])
        out_shape = ([sds((batch, N_KV, ns, AUG, GROUP * tm), BF16)] + kv_t_shape
                     + [sds((N_KV, m, AUG), BF16),
                        sds((batch, N_KV, seq // FOX_KEYS, V_ROWS, FOX_KEYS), BF16),
                        sds((batch, N_HEADS, seq), F32)])
    else:
        out_specs = ([pl.BlockSpec((tm, D_MODEL), row), pl.BlockSpec((tm, KV_WIDTH), row),
                      pl.BlockSpec((tm, KV_WIDTH), row)] + kv_t
                     + [pl.BlockSpec((tm, N_HEADS), row), tcol(N_HEADS)])
        out_shape = ([sds((m, D_MODEL), BF16), sds((m, KV_WIDTH), F32), sds((m, KV_WIDTH), F32)]
                     + kv_t_shape + [sds((m, N_HEADS), F32), sds((batch, N_HEADS, seq), F32)])
    out_specs.append(pl.BlockSpec((tm, D_MODEL), row))
    out_shape.append(sds((m, D_MODEL), F32))
    return pl.pallas_call(
        functools.partial(_fox_proj_kernel, prompt=prompt),
        grid=(m // tm,),
        in_specs=[pl.BlockSpec((tm, D_MODEL), row), _full((1, D_MODEL)), _full(w.shape),
                  _full(wf.shape), _full(fb.shape)],
        out_specs=out_specs,
        out_shape=out_shape,
        compiler_params=_cparams(("parallel",)),
        name="fox_proj",
    )(x, g, w, wf, fb)


def _out_proj_kernel(*refs, gated):
    if gated:
        o_ref, gate_ref, w_ref, x_ref, y_ref = refs
        ob = (o_ref[...] * _silu(gate_ref[...])).astype(BF16)
    else:
        o_ref, w_ref, x_ref, y_ref = refs
        ob = o_ref[...]
    y_ref[...] = x_ref[...] + _dot(ob, w_ref[...])


def _out_proj(o, gate, w, x, tm):
    m = x.shape[0]
    row = lambda i: (i, 0)
    tile = pl.BlockSpec((tm, D_MODEL), row)
    gated = gate is not None
    ins = [o, gate, w, x] if gated else [o, w, x]
    in_specs = [tile, tile, _full(w.shape), tile] if gated else [tile, _full(w.shape), tile]
    return pl.pallas_call(
        functools.partial(_out_proj_kernel, gated=gated),
        grid=(m // tm,),
        in_specs=in_specs,
        out_specs=tile,
        out_shape=jax.ShapeDtypeStruct((m, D_MODEL), F32),
        compiler_params=_cparams(("parallel",)),
        name="out_proj",
    )(*ins)


def _final_norm_kernel(x_ref, g_ref, y_ref):
    x = x_ref[...]
    ms = jnp.mean(x * x, axis=-1, keepdims=True)
    y_ref[...] = x * lax.rsqrt(ms + EPS) * g_ref[...]


def _final_norm(x, g, tm):
    m = x.shape[0]
    row = lambda i: (i, 0)
    return pl.pallas_call(
        _final_norm_kernel,
        grid=(m // tm,),
        in_specs=[pl.BlockSpec((tm, D_MODEL), row), _full((1, D_MODEL))],
        out_specs=pl.BlockSpec((tm, D_MODEL), row),
        out_shape=jax.ShapeDtypeStruct((m, D_MODEL), F32),
        compiler_params=_cparams(("parallel",)),
        name="final_norm",
    )(x, g)


HALO = 16
PAD = 8
assert all(w == 2 << g for g, w in enumerate(POOL_WINDOWS)) and HALO >= max(POOL_WINDOWS) and PAD >= HALO // 2


def _pool_layer_tail(x, u, gate, pooled_groups, mix_ref, scale_ref, wout_ref, fg_ref, y_ref):
    pieces = []
    for g in range(len(POOL_WINDOWS)):
        c0 = g * POOL_GROUP
        p = (pooled_groups[g] - u[:, c0:c0 + POOL_GROUP]).astype(BF16)
        pieces.append(_dot(p, mix_ref[g]))
    pm = jnp.concatenate(pieces, axis=1)
    o = (pm * scale_ref[...] * _silu(gate)).astype(BF16)
    y = x + _dot(o, wout_ref[...])
    if fg_ref is not None:
        ms = jnp.mean(y * y, axis=-1, keepdims=True)
        y = y * lax.rsqrt(ms + EPS) * fg_ref[...]
    y_ref[...] = y


def _n_pool_inputs(final, pending):
    return 6 + (3 if pending else 0) + (1 if final else 0)


def _pool_prompt_kernel(*refs, tp, final, pending):
    n_in = _n_pool_inputs(final, pending)
    _pool_tile(pl.program_id(1), pl.num_programs(1), refs[:n_in], refs[n_in:n_in + 2], refs[n_in + 2:],
               tp=tp, final=final, pending=pending)


def _pool_tile(i, n_tiles, in_refs, out_refs, bufs, *, tp, final, pending):
    refs = list(in_refs)
    x_ref = refs.pop(0)
    prev = [refs.pop(0) for _ in range(3)] if pending else None
    g_ref, win_ref, mix_ref, scale_ref, wout_ref = [refs.pop(0) for _ in range(5)]
    fg_ref = refs.pop(0) if final else None
    y_ref, tail_ref = out_refs
    hist = slice(PAD, PAD + HALO)
    ext = slice(PAD, PAD + HALO + tp)
    tile = slice(PAD + HALO, PAD + HALO + tp)

    def back(rows, k):
        return slice(rows.start - k, rows.stop - k)

    @pl.when(i == 0)
    def _():
        for buf in bufs:
            buf[0:PAD + HALO, :] = jnp.zeros((PAD + HALO, buf.shape[1]), F32)

    x = x_ref[...]
    if pending:
        po_ref, pgate_ref, pw_ref = prev
        x = x + _dot((po_ref[...] * _silu(pgate_ref[...])).astype(BF16), pw_ref[...])
    hb = _rms_bf16(x, g_ref[...])
    u = _dot(hb, win_ref[:, :D_MODEL])
    gate = _dot(hb, win_ref[:, D_MODEL:])
    bufs[0][tile, :] = u
    pos = i * tp + lax.broadcasted_iota(jnp.int32, (tp, 1), 0)
    pooled = []
    for g, w in enumerate(POOL_WINDOWS):
        src, span = bufs[g], 1 << g
        if g + 1 < len(bufs):
            nxt = bufs[g + 1]
            nxt[ext, :] = src[ext, POOL_GROUP:] + src[back(ext, span), POOL_GROUP:]
        acc = src[tile, 0:POOL_GROUP] + src[back(tile, span), 0:POOL_GROUP]
        inv_cnt = 1.0 / jnp.minimum(pos + 1, w).astype(F32)
        pooled.append(acc * inv_cnt)
    _pool_layer_tail(x, u, gate, pooled, mix_ref, scale_ref, wout_ref, fg_ref, y_ref)
    bufs[0][hist, :] = u[tp - HALO:tp, :]

    @pl.when(i == n_tiles - 1)
    def _():
        tail_ref[...] = u[tp - HALO:tp, :]


def _pool_prompt_operands(x, pending, g, w_in, mix, scale, w_out, fg, tp, row_tile, batch_of, single_buffer):
    final = fg is not None
    tile = pl.BlockSpec((tp, D_MODEL), lambda *idx: (row_tile(*idx), 0))
    whole = (lambda a: pl.BlockSpec(a.shape, lambda *_: (0,) * a.ndim, pipeline_mode=pl.Buffered(1))
             if single_buffer else _full(a.shape))
    ins = [x] + (list(pending) if pending else []) + [g, w_in, mix, scale, w_out] + ([fg] if final else [])
    in_specs = [tile] + ([tile, tile, whole(pending[2])] if pending else [])
    in_specs += [whole(a) for a in ins[len(in_specs):]]
    out_specs = [tile, pl.BlockSpec((None, HALO, D_MODEL), lambda *idx: (batch_of(*idx), 0, 0))]
    scratch = [pltpu.VMEM((PAD + HALO + tp, D_MODEL - g * POOL_GROUP), F32) for g in range(len(POOL_WINDOWS))]
    return ins, in_specs, out_specs, scratch, final


def _pool_prompt(x, pending, g, w_in, mix, scale, w_out, fg, batch, seq, tp):
    ns = seq // tp
    ins, in_specs, out_specs, scratch, final = _pool_prompt_operands(
        x, pending, g, w_in, mix, scale, w_out, fg, tp, lambda b, i: b * ns + i, lambda b, i: b, False)
    return pl.pallas_call(
        functools.partial(_pool_prompt_kernel, tp=tp, final=final, pending=bool(pending)),
        grid=(batch, ns),
        in_specs=in_specs,
        out_specs=out_specs,
        out_shape=[jax.ShapeDtypeStruct((batch * seq, D_MODEL), F32),
                   jax.ShapeDtypeStruct((batch, HALO, D_MODEL), F32)],
        scratch_shapes=scratch,
        compiler_params=_cparams(("parallel", "arbitrary")),
        name="pool_prompt",
    )(*ins)


def _pool_sample_kernel(*refs, final):
    if final:
        x_ref, st_ref, g_ref, win_ref, mix_ref, scale_ref, wout_ref, fg_ref, y_ref, u_ref = refs
    else:
        x_ref, st_ref, g_ref, win_ref, mix_ref, scale_ref, wout_ref, y_ref, u_ref = refs
        fg_ref = None
    x = x_ref[...]
    hb = _rms_bf16(x, g_ref[...])
    u = _dot(hb, win_ref[:, :D_MODEL])
    gate = _dot(hb, win_ref[:, D_MODEL:])
    u_ref[...] = u
    pooled = []
    for g, w in enumerate(POOL_WINDOWS):
        c0 = g * POOL_GROUP
        acc = u[:, c0:c0 + POOL_GROUP]
        for k in range(1, w):
            acc = acc + st_ref[POOL_STATE - k, :, c0:c0 + POOL_GROUP]
        pooled.append(acc / float(w))
    _pool_layer_tail(x, u, gate, pooled, mix_ref, scale_ref, wout_ref, fg_ref, y_ref)


def _pool_sample(x, state_t, g, w_in, mix, scale, w_out, fg):
    m = x.shape[0]
    final = fg is not None
    ins = [x, state_t, g, w_in, mix, scale, w_out] + ([fg] if final else [])
    return pl.pallas_call(
        functools.partial(_pool_sample_kernel, final=final),
        grid=(1,),
        in_specs=[_full(a.shape) for a in ins],
        out_specs=[_full((m, D_MODEL)), _full((m, D_MODEL))],
        out_shape=[jax.ShapeDtypeStruct((m, D_MODEL), F32)] * 2,
        compiler_params=_cparams(("arbitrary",)),
        name="pool_sample",
    )(*ins)


def _t5_bucket_np(dist):
    n = np.maximum(dist, 0)
    max_exact = N_BUCKETS // 2
    nf = np.maximum(n, 1).astype(np.float32)
    large = max_exact + (np.log(nf / max_exact) / math.log(MAX_DISTANCE / max_exact)
                         * (N_BUCKETS - max_exact)).astype(np.int32)
    large = np.minimum(large, N_BUCKETS - 1)
    return np.where(n < max_exact, n, large)


def _swa_prompt_kernel(sink_ref, rb_ref, qt_ref, kp_ref, kc_ref, vp_ref, vc_ref, bucket_ref, gate_ref,
                       wout_ref, x_ref, y_ref, bias_ref, o_buf, *, nb):
    i = pl.program_id(1)
    cols = GROUP * WINDOW

    @pl.when(i == 0)
    def _():
        bucket = bucket_ref[...]
        hits = [bucket == bk for bk in range(N_BUCKETS)]
        for hd in range(N_HEADS):
            h, g = divmod(hd, GROUP)
            b = jnp.full((2 * WINDOW, WINDOW), NEG, F32)
            for bk in range(N_BUCKETS):
                b = jnp.where(hits[bk], rb_ref[bk, hd] * LOG2E, b)
            bias_ref[h, :, g * WINDOW:(g + 1) * WINDOW] = b

    key_row = lax.broadcasted_iota(jnp.int32, (2 * WINDOW, cols), 0)
    no_prev = jnp.logical_and(i == 0, key_row < WINDOW)
    sinks = [jnp.concatenate([jnp.full((1, WINDOW), sink_ref[h * GROUP + g] * LOG2E, F32) for g in range(GROUP)],
                             axis=1) for h in range(N_KV)]

    def logits(jb, h):
        if jb == 0:
            kband = jnp.concatenate([kp_ref[h], kc_ref[h, 0:WINDOW, :]], axis=0)
        else:
            kband = kc_ref[h, (jb - 1) * WINDOW:(jb + 1) * WINDOW, :]
        s = _dot(kband, qt_ref[h, jb]) + bias_ref[h]
        return jnp.where(no_prev, NEG, s) if jb == 0 else s

    def attend(jb, h, s):
        vprev = vp_ref[h] if jb == 0 else vc_ref[h, jb - 1]
        vband = jnp.concatenate([vprev, vc_ref[h, jb]], axis=1)
        m = jnp.maximum(jnp.max(s, axis=0, keepdims=True), sinks[h])
        pv = _dot(vband, jnp.exp2(s - m).astype(BF16))
        denom = pv[HEAD_DIM:HEAD_DIM + 1, :] + jnp.exp2(sinks[h] - m)
        ot = pv[0:HEAD_DIM, :] * (1.0 / denom)
        o_heads = jnp.concatenate([ot[:, g * WINDOW:(g + 1) * WINDOW] for g in range(GROUP)], axis=0)
        o_buf[jb * WINDOW:(jb + 1) * WINDOW, h * GROUP * HEAD_DIM:(h + 1) * GROUP * HEAD_DIM] = o_heads.T

    units = [(jb, h) for jb in range(nb) for h in range(N_KV)]
    s = logits(*units[0])
    for u, unit in enumerate(units):
        s_next = logits(*units[u + 1]) if u + 1 < len(units) else None
        attend(*unit, s)
        s = s_next
    ob = (o_buf[...] * _silu(gate_ref[...])).astype(BF16)
    y_ref[...] = x_ref[...] + _dot(ob, wout_ref[...])


def _swa_prompt(qt, kh, vtb, gate, w_out, x, rel_bias, sinks, batch, seq, tm):
    ns = seq // tm
    nb = tm // WINDOW
    dist = np.arange(WINDOW)[None, :] + WINDOW - np.arange(2 * WINDOW)[:, None]
    bucket = np.where((dist >= 0) & (dist < WINDOW), _t5_bucket_np(dist), -1).astype(np.int32)
    tile = pl.BlockSpec((tm, D_MODEL), lambda b, i: (b * ns + i, 0))
    smem = pl.BlockSpec(memory_space=pltpu.SMEM)
    prev_blk = lambda i: jnp.maximum(i * nb - 1, 0)
    return pl.pallas_call(
        functools.partial(_swa_prompt_kernel, nb=nb),
        grid=(batch, ns),
        in_specs=[smem, smem,
                  pl.BlockSpec((None, N_KV, nb, HEAD_DIM, GROUP * WINDOW), lambda b, i: (b, 0, i, 0, 0)),
                  pl.BlockSpec((N_KV, WINDOW, HEAD_DIM), lambda b, i: (0, b * ns * nb + prev_blk(i), 0)),
                  pl.BlockSpec((N_KV, tm, HEAD_DIM), lambda b, i: (0, b * ns + i, 0)),
                  pl.BlockSpec((None, N_KV, None, V_ROWS, WINDOW), lambda b, i: (b, 0, prev_blk(i), 0, 0)),
                  pl.BlockSpec((None, N_KV, nb, V_ROWS, WINDOW), lambda b, i: (b, 0, i, 0, 0)),
                  _full(bucket.shape), tile, _full(w_out.shape), tile],
        out_specs=tile,
        out_shape=jax.ShapeDtypeStruct((batch * seq, D_MODEL), F32),
        scratch_shapes=[pltpu.VMEM((N_KV, 2 * WINDOW, GROUP * WINDOW), F32),
                        pltpu.VMEM((tm, D_MODEL), F32)],
        compiler_params=_cparams(("parallel", "arbitrary")),
        name="swa_prompt",
    )(sinks, rel_bias, qt, kh, kh, vtb, vtb, jnp.asarray(bucket), gate, w_out, x)


def _head_diag(o_full):
    out = jnp.zeros((N_HEADS, HEAD_DIM), F32)
    row_kv = lax.broadcasted_iota(jnp.int32, (N_HEADS, HEAD_DIM), 0) // GROUP
    for h in range(N_KV):
        out = out + jnp.where(row_kv == h, o_full[:, h * HEAD_DIM:(h + 1) * HEAD_DIM], 0.0)
    return out


def _swa_sample_kernel(qm_ref, kc_ref, vc_ref, kn_ref, vn_ref, knt_ref, vnt_ref, bias_ref, bias0_ref,
                       sink_ref, o_ref, wk_ref, wv_ref, *, bt):
    i = pl.program_id(0)
    lane = lax.broadcasted_iota(jnp.int32, (KV_WIDTH, WINDOW), 1)
    sink = sink_ref[...]
    for e in range(bt):
        b = i * bt + e
        kc = kc_ref[e]
        vc = vc_ref[e]
        qm = qm_ref[e]
        s = _dot(qm, kc.astype(BF16)) + bias_ref[...]
        s_n = jnp.sum(qm.astype(F32) * _bf16_round(kn_ref[e]), axis=1, keepdims=True) + bias0_ref[...]
        m = jnp.maximum(jnp.maximum(jnp.max(s, axis=1, keepdims=True), s_n), sink)
        p = jnp.exp(s - m)
        p_n = jnp.exp(s_n - m)
        denom = jnp.sum(p, axis=1, keepdims=True) + p_n + jnp.exp(sink - m)
        o_full = _dot_nt((p / denom).astype(BF16), vc.astype(BF16))
        o_full = o_full + _bf16_round(p_n / denom) * _bf16_round(vn_ref[e])
        o_ref[e] = _head_diag(o_full)
        kcol = jnp.sum(jnp.where(lane == b, knt_ref[...], 0.0), axis=1, keepdims=True)
        vcol = jnp.sum(jnp.where(lane == b, vnt_ref[...], 0.0), axis=1, keepdims=True)
        wk_ref[e] = jnp.where(lane == WINDOW - 1, kcol, pltpu.roll(kc, WINDOW - 1, 1))
        wv_ref[e] = jnp.where(lane == WINDOW - 1, vcol, pltpu.roll(vc, WINDOW - 1, 1))


def _swa_sample(qm, kc, vc, kn, vn, knt, vnt, bias_keys, bias0, sinks, bt):
    m = qm.shape[0]
    blk3 = lambda s1, s2: pl.BlockSpec((bt, s1, s2), lambda i: (i, 0, 0))
    cache = jax.ShapeDtypeStruct((m, KV_WIDTH, WINDOW), F32)
    return pl.pallas_call(
        functools.partial(_swa_sample_kernel, bt=bt),
        grid=(m // bt,),
        in_specs=[blk3(N_HEADS, KV_WIDTH), blk3(KV_WIDTH, WINDOW), blk3(KV_WIDTH, WINDOW),
                  blk3(1, KV_WIDTH), blk3(1, KV_WIDTH), _full(knt.shape), _full(vnt.shape),
                  _full(bias_keys.shape), _full(bias0.shape), _full(sinks.shape)],
        out_specs=[blk3(N_HEADS, HEAD_DIM), blk3(KV_WIDTH, WINDOW), blk3(KV_WIDTH, WINDOW)],
        out_shape=[jax.ShapeDtypeStruct((m, N_HEADS, HEAD_DIM), F32), cache, cache],
        compiler_params=_cparams(("parallel",)),
        name="swa_sample",
    )(qm, kc, vc, kn, vn, knt, vnt, bias_keys, bias0, sinks)


def _split3(x):
    hi = x.astype(BF16)
    r = x - hi.astype(F32)
    mid = r.astype(BF16)
    lo = (r - mid.astype(F32)).astype(BF16)
    return hi, mid, lo


def _cumsum_lanes_blocks(x, blk):
    n = x.shape[0]
    hi, mid, lo = _split3(x)
    r_i = lax.broadcasted_iota(jnp.int32, (blk, blk), 0)
    c_i = lax.broadcasted_iota(jnp.int32, (blk, blk), 1)
    upper = jnp.where(r_i <= c_i, 1.0, 0.0).astype(BF16)
    r = _dot(jnp.concatenate([hi, mid, lo], axis=0), upper)
    return r[0:n] + r[n:2 * n] + r[2 * n:]


def _fox_decay_kernel(x_ref, kin_ref, kout_ref, carry, *, blk):
    i = pl.program_id(1)

    @pl.when(i == 0)
    def _():
        carry[...] = jnp.zeros_like(carry)

    c = _cumsum_lanes_blocks(x_ref[...], blk) + carry[:, 0:1]
    carry[...] = jnp.broadcast_to(c[:, blk - 1:blk], carry.shape)
    hi, mid, lo = _split3(c * LOG2E)
    pieces = jnp.concatenate([hi, mid, lo], axis=0).astype(F32)
    pad = jnp.zeros((LANES - N_PIECES * N_HEADS, blk), F32)
    pieces_t = jnp.concatenate([pieces, pad], axis=0).T.astype(BF16)
    src = lax.broadcasted_iota(jnp.int32, (LANES, AUG), 0)
    dst = lax.broadcasted_iota(jnp.int32, (LANES, AUG), 1) - HEAD_DIM
    in_aug = jnp.logical_and(dst >= 0, dst < N_PIECES * GROUP)
    for h in range(N_KV):
        want = (dst >> GROUP_SHIFT) * N_HEADS + h * GROUP + (dst & (GROUP - 1))
        place = jnp.where(jnp.logical_and(in_aug, src == want), 1.0, 0.0).astype(BF16)
        kout_ref[h] = kin_ref[h] + _dot(pieces_t, place).astype(BF16)


def _fox_decay(logft, kaug, blk):
    b, h, s = logft.shape
    ns = s // blk
    kspec = pl.BlockSpec((N_KV, blk, AUG), lambda bi, i: (0, bi * ns + i, 0))
    return pl.pallas_call(
        functools.partial(_fox_decay_kernel, blk=blk),
        grid=(b, ns),
        in_specs=[pl.BlockSpec((None, h, blk), lambda bi, i: (bi, 0, i)), kspec],
        out_specs=kspec,
        out_shape=jax.ShapeDtypeStruct(kaug.shape, kaug.dtype),
        scratch_shapes=[pltpu.VMEM((h, LANES), F32)],
        input_output_aliases={1: 0},
        compiler_params=_cparams(("parallel", "arbitrary")),
        name="fox_decay",
    )(logft, kaug)


def _decode_element(qm, kn, vn, fn, kbuf, vbuf, fbuf):
    cp = kbuf.shape[0]
    cin = _cumsum_lanes_blocks(fbuf[...].reshape(cp * N_HEADS, PAGE), PAGE).reshape(cp, N_HEADS, PAGE)
    carry = jnp.zeros((N_HEADS, 1), F32)
    cs = []
    for p in range(cp):
        cs.append(cin[p] + carry)
        carry = carry + cin[p][:, PAGE - 1:PAGE]
    kcat = jnp.concatenate([kbuf[p] for p in range(cp)], axis=1).astype(BF16)
    t = _dot(qm, kcat) - jnp.concatenate(cs, axis=1)
    t_n = jnp.sum(qm.astype(F32) * _bf16_round(kn), axis=1, keepdims=True) - (carry + fn)
    m = jnp.maximum(jnp.max(t, axis=1, keepdims=True), t_n)
    p = jnp.exp(t - m)
    p_n = jnp.exp(t_n - m)
    denom = jnp.sum(p, axis=1, keepdims=True) + p_n
    vcat = jnp.concatenate([vbuf[p_] for p_ in range(cp)], axis=1).astype(BF16)
    pb = p.astype(BF16)
    half = KV_WIDTH // 2
    acc = jnp.concatenate([_dot_nt(pb, vcat[:half]), _dot_nt(pb, vcat[half:])], axis=1)
    acc = acc + _bf16_round(p_n) * _bf16_round(vn)
    return _head_diag(acc / denom)


def _fox_prompt_kernel(*refs, tq, tk, decode):
    assert tq == tk
    if decode:
        (pt_ref, qt_ref, k_ref, vt_ref, qm_ref, kn_ref, vn_ref, fn_ref, ck_hbm, cv_hbm, cf_hbm,
         o_ref, os_ref, *scratch) = refs
        kbuf, vbuf, fbuf, sems = scratch[6:]
        scratch = scratch[:6]
    else:
        qt_ref, k_ref, vt_ref, o_ref, *scratch = refs
    t = pl.program_id(2)
    m_ref, acc_ref, sa_ref, sb_ref, ca_ref, cb_ref = scratch
    cols = GROUP * tq

    if decode:
        step = (pl.program_id(0) * pl.num_programs(1) + pl.program_id(1)) * pl.num_programs(2) + t
        n_rows = 2 * pl.num_programs(0) * pl.num_programs(1) * pl.num_programs(2)
        n_pages = kbuf.shape[0]

    def page_copies(r, p):
        page = pt_ref[r, p]
        return (pltpu.make_async_copy(ck_hbm.at[page], kbuf.at[p], sems.at[0]),
                pltpu.make_async_copy(cv_hbm.at[page], vbuf.at[p], sems.at[1]),
                pltpu.make_async_copy(cf_hbm.at[page], fbuf.at[p], sems.at[2]))

    def request_pages(r):
        for p in range(n_pages):
            for cpy in page_copies(r, p):
                cpy.start()

    def sample_wait(tile):
        if not decode:
            return
        row = 2 * step + tile

        @pl.when(row == 0)
        def _():
            request_pages(row)

        for p in range(n_pages):
            for cpy in page_copies(row, p):
                cpy.wait()

    def sample_attend(tile):
        if not decode:
            return
        row = 2 * step + tile
        os_ref[tile] = _decode_element(qm_ref[tile], kn_ref[tile], vn_ref[tile], fn_ref[tile], kbuf, vbuf, fbuf)

        @pl.when(row + 1 < n_rows)
        def _():
            request_pages(row + 1)

    def reset():
        m_ref[...] = jnp.full(m_ref.shape, NEG, F32)
        acc_ref[...] = jnp.zeros(acc_ref.shape, F32)

    def logits(tile, kb, s_ref, c_ref):
        k0 = pl.multiple_of(kb * tk, tk)
        s = _dot(k_ref[pl.ds(k0, tk), :], qt_ref[tile])
        s_ref[...] = s
        c_ref[...] = jnp.max(s, axis=0, keepdims=True)

    def accumulate(kb, s_ref, c_ref, masked):
        s = s_ref[...]
        if masked:
            kpos = lax.broadcasted_iota(jnp.int32, (tk, cols), 0)
            qpos = lax.broadcasted_iota(jnp.int32, (tk, cols), 1) & (tq - 1)
            s = jnp.where(kpos <= qpos, s, NEG)
            cmax = jnp.max(s, axis=0, keepdims=True)
        else:
            cmax = c_ref[...]
        m_old = m_ref[...]
        m_new = jnp.maximum(m_old, cmax)
        alpha = jnp.exp2(m_old - m_new)
        p = jnp.exp2(s - m_new)
        acc_ref[...] = alpha * acc_ref[...] + _dot(vt_ref[kb], p.astype(BF16))
        m_ref[...] = m_new

    def finish(tile):
        o = acc_ref[0:HEAD_DIM, :] / acc_ref[HEAD_DIM:HEAD_DIM + 1, :]
        o_ref[tile * tq:(tile + 1) * tq, :] = jnp.concatenate(
            [o[:, g * tq:(g + 1) * tq] for g in range(GROUP)], axis=0).T

    def pairs(tile, first, second):
        def body(j, carry):
            kb = 2 * j
            logits(tile, kb + 1, *second)
            accumulate(kb, *first, False)
            logits(tile, kb + 2, *first)
            accumulate(kb + 1, *second, False)
            return carry
        lax.fori_loop(0, t, body, 0)

    buf_a, buf_b = (sa_ref, ca_ref), (sb_ref, cb_ref)
    sample_wait(0)
    reset()
    logits(0, 0, *buf_a)
    sample_attend(0)
    pairs(0, buf_a, buf_b)
    sample_wait(1)
    logits(1, 0, *buf_b)
    accumulate(2 * t, *buf_a, True)
    finish(0)
    sample_attend(1)
    reset()
    pairs(1, buf_b, buf_a)
    logits(1, 2 * t + 1, *buf_a)
    accumulate(2 * t, *buf_b, False)
    accumulate(2 * t + 1, *buf_a, True)
    finish(1)


def _fox_prompt_fits_decode(batch, seq, tq, n_rows):
    return n_rows == batch * N_KV * (seq // tq)


def _fox_prompt(qt, kaug, vtb, batch, seq, tq, decode=None):
    nq = seq // tq
    nk, tk = vtb.shape[2], vtb.shape[4]
    cols = GROUP * tq
    nt = nq // 2
    ins = [qt, kaug, vtb]
    in_specs = [pl.BlockSpec((None, None, 2, AUG, cols), lambda b, h, t, *_: (b, h, t, 0, 0)),
                pl.BlockSpec((None, seq, AUG), lambda b, h, t, *_: (h, b, 0)),
                pl.BlockSpec((None, None, nk, V_ROWS, tk), lambda b, h, t, *_: (b, h, 0, 0, 0))]
    out_specs = [pl.BlockSpec((2 * tq, GROUP * HEAD_DIM), lambda b, h, t, *_: (b * nt + t, h))]
    out_shape = [jax.ShapeDtypeStruct((batch * seq, D_MODEL), F32)]
    scratch = [pltpu.VMEM((1, cols), F32),
               pltpu.VMEM((V_ROWS, cols), F32),
               pltpu.VMEM((tk, cols), F32), pltpu.VMEM((tk, cols), F32),
               pltpu.VMEM((1, cols), F32), pltpu.VMEM((1, cols), F32)]
    prefetch = []
    if decode is not None:
        page_table, qm, kn, vn, fn, ck, cv, cf = decode
        n_rows, n_pages = page_table.shape
        assert _fox_prompt_fits_decode(batch, seq, tq, n_rows)
        pair = lambda s1, s2: pl.BlockSpec((2, s1, s2), lambda b, h, t, pt: ((b * N_KV + h) * nt + t, 0, 0))
        any_spec = pl.BlockSpec(memory_space=pl.ANY)
        prefetch = [page_table]
        ins += [qm, kn, vn, fn, ck, cv, cf]
        in_specs += [pair(N_HEADS, KV_WIDTH), pair(1, KV_WIDTH), pair(1, KV_WIDTH), pair(N_HEADS, 1),
                     any_spec, any_spec, any_spec]
        out_specs.append(pair(N_HEADS, HEAD_DIM))
        out_shape.append(jax.ShapeDtypeStruct((n_rows, N_HEADS, HEAD_DIM), F32))
        scratch += [pltpu.VMEM((n_pages, KV_WIDTH, PAGE), F32), pltpu.VMEM((n_pages, KV_WIDTH, PAGE), F32),
                    pltpu.VMEM((n_pages, N_HEADS, PAGE), F32), pltpu.SemaphoreType.DMA((3,))]
    semantics = ("arbitrary",) * 3 if decode is not None else ("parallel", "parallel", "arbitrary")
    outs = pl.pallas_call(
        functools.partial(_fox_prompt_kernel, tq=tq, tk=tk, decode=decode is not None),
        grid_spec=pltpu.PrefetchScalarGridSpec(
            num_scalar_prefetch=len(prefetch),
            grid=(batch, N_KV, nt),
            in_specs=in_specs,
            out_specs=out_specs,
            scratch_shapes=scratch,
        ),
        out_shape=out_shape,
        compiler_params=_cparams(semantics),
        name="fox_prompt",
    )(*prefetch, *ins)
    return outs if decode is not None else outs[0]


N_DECODE_SCRATCH = 8


def _fox_sample_kernel(pt_ref, qm_ref, kn_ref, vn_ref, fn_ref, ck_hbm, cv_hbm, cf_hbm, *rest, cp, nchunk, rider):
    n_in = _n_pool_inputs(rider["final"], rider["pending"]) if rider else 0
    rider_in, o_ref, rest = rest[:n_in], rest[n_in], rest[n_in + 1:]
    rider_out, rest = (rest[:2], rest[2:]) if rider else ((), rest)
    kbuf, vbuf, fbuf, sems, m_ref, l_ref, acc_ref, carry_ref = rest[:N_DECODE_SCRATCH]
    rider_bufs = rest[N_DECODE_SCRATCH:]
    b = pl.program_id(0)
    c = pl.program_id(1)
    nb = pl.num_programs(0)
    step = b * nchunk + c
    slot = step % 2

    def copies(bb, cc, sl, p):
        page = pt_ref[bb, cc * cp + p]
        return (pltpu.make_async_copy(ck_hbm.at[page], kbuf.at[sl, p], sems.at[sl, 0]),
                pltpu.make_async_copy(cv_hbm.at[page], vbuf.at[sl, p], sems.at[sl, 1]),
                pltpu.make_async_copy(cf_hbm.at[page], fbuf.at[sl, p], sems.at[sl, 2]))

    def issue(bb, cc, sl):
        for p in range(cp):
            for cpy in copies(bb, cc, sl, p):
                cpy.start()

    @pl.when(step == 0)
    def _():
        issue(b, c, slot)

    @pl.when(step + 1 < nb * nchunk)
    def _():
        nxt = step + 1
        issue(nxt // nchunk, nxt % nchunk, 1 - slot)

    if rider:
        per_batch = rider["tiles_per_batch"]
        _pool_tile(step % per_batch, per_batch, rider_in, rider_out, rider_bufs,
                   tp=rider["tp"], final=rider["final"], pending=rider["pending"])

    for p in range(cp):
        for cpy in copies(b, c, slot, p):
            cpy.wait()

    @pl.when(c == 0)
    def _():
        m_ref[...] = jnp.full(m_ref.shape, NEG, F32)
        l_ref[...] = jnp.zeros(l_ref.shape, F32)
        acc_ref[...] = jnp.zeros(acc_ref.shape, F32)
        carry_ref[...] = jnp.zeros(carry_ref.shape, F32)

    qm = qm_ref[...]
    cin = _cumsum_lanes_blocks(fbuf[slot].reshape(cp * N_HEADS, PAGE), PAGE).reshape(cp, N_HEADS, PAGE)
    carry = carry_ref[...]
    cs = []
    for p in range(cp):
        cs.append(cin[p] + carry)
        carry = carry + cin[p][:, PAGE - 1:PAGE]
    carry_ref[...] = carry
    kcat = jnp.concatenate([kbuf[slot, p] for p in range(cp)], axis=1).astype(BF16)
    t = _dot(qm, kcat) - jnp.concatenate(cs, axis=1)
    m_old = m_ref[...]
    m_new = jnp.maximum(m_old, jnp.max(t, axis=1, keepdims=True))
    alpha = jnp.exp(m_old - m_new)
    p = jnp.exp(t - m_new)
    l_new = alpha * l_ref[...] + jnp.sum(p, axis=1, keepdims=True)
    vcat = jnp.concatenate([vbuf[slot, p_] for p_ in range(cp)], axis=1).astype(BF16)
    acc_new = alpha * acc_ref[...] + _dot_nt(p.astype(BF16), vcat)
    m_ref[...] = m_new
    l_ref[...] = l_new
    acc_ref[...] = acc_new

    @pl.when(c == nchunk - 1)
    def _():
        s_n = jnp.sum(qm.astype(F32) * _bf16_round(kn_ref[...]), axis=1, keepdims=True)
        t_n = s_n - (carry + fn_ref[...])
        m_f = jnp.maximum(m_new, t_n)
        a = jnp.exp(m_new - m_f)
        p_n = jnp.exp(t_n - m_f)
        l_f = a * l_new + p_n
        acc = a * acc_new + _bf16_round(p_n) * _bf16_round(vn_ref[...])
        o_ref[...] = _head_diag(acc / l_f)


def _fox_sample(page_table, qm, kn, vn, fn, ck, cv, cf, cp, pool_layer=None):
    m, n_pages = page_table.shape
    nchunk = n_pages // cp
    per_b = lambda s1, s2: pl.BlockSpec((None, s1, s2), lambda b, c, pt: (b, 0, 0))
    any_spec = pl.BlockSpec(memory_space=pl.ANY)
    ins = [qm, kn, vn, fn, ck, cv, cf]
    in_specs = [per_b(N_HEADS, KV_WIDTH), per_b(1, KV_WIDTH), per_b(1, KV_WIDTH), per_b(N_HEADS, 1),
                any_spec, any_spec, any_spec]
    out_specs = [per_b(N_HEADS, HEAD_DIM)]
    out_shape = [jax.ShapeDtypeStruct((m, N_HEADS, HEAD_DIM), F32)]
    scratch = [pltpu.VMEM((2, cp, KV_WIDTH, PAGE), F32),
               pltpu.VMEM((2, cp, KV_WIDTH, PAGE), F32),
               pltpu.VMEM((2, cp, N_HEADS, PAGE), F32),
               pltpu.SemaphoreType.DMA((2, 3)),
               pltpu.VMEM((N_HEADS, 1), F32), pltpu.VMEM((N_HEADS, 1), F32),
               pltpu.VMEM((N_HEADS, KV_WIDTH), F32), pltpu.VMEM((N_HEADS, 1), F32)]
    assert len(scratch) == N_DECODE_SCRATCH
    rider = None
    if pool_layer is not None:
        *layer, batch, seq = pool_layer
        assert nchunk == 1 and (batch * seq) % m == 0
        tp = batch * seq // m
        per_batch = seq // tp
        assert seq % tp == 0 and tp % HALO == 0
        p_ins, p_in_specs, p_out_specs, p_scratch, final = _pool_prompt_operands(
            *layer, tp, lambda b, c, pt: b, lambda b, c, pt: b // per_batch, True)
        rider = dict(tp=tp, final=final, pending=layer[1] is not None, tiles_per_batch=per_batch)
        ins += p_ins
        in_specs += p_in_specs
        out_specs += p_out_specs
        out_shape += [jax.ShapeDtypeStruct((batch * seq, D_MODEL), F32),
                      jax.ShapeDtypeStruct((batch, HALO, D_MODEL), F32)]
        scratch += p_scratch
    outs = pl.pallas_call(
        functools.partial(_fox_sample_kernel, cp=cp, nchunk=nchunk, rider=rider),
        grid_spec=pltpu.PrefetchScalarGridSpec(
            num_scalar_prefetch=1,
            grid=(m, nchunk),
            in_specs=in_specs,
            out_specs=out_specs,
            scratch_shapes=scratch,
        ),
        out_shape=out_shape,
        compiler_params=pltpu.CompilerParams(dimension_semantics=("arbitrary", "arbitrary"),
                                             vmem_limit_bytes=VMEM_LIMIT_RIDER if rider else VMEM_LIMIT),
        name="fox_sample",
    )(page_table, *ins)
    return outs if rider else outs[0]


def _head_rows_q(q_rows):
    m = q_rows.shape[0]
    q4 = q_rows.reshape(m, N_KV, GROUP, HEAD_DIM)
    eye = jnp.eye(N_KV, dtype=q_rows.dtype)
    return (q4[:, :, :, None, :] * eye[None, :, None, :, None]).reshape(m, N_HEADS, KV_WIDTH)


def _tokens_last(x):
    lead = x.shape[:-3]
    n = len(lead)
    xt = jnp.transpose(x, tuple(range(n)) + (n + 1, n + 2, n))
    return xt.reshape(lead + (KV_WIDTH, x.shape[-3]))


def _tokens_first(xt):
    lead = xt.shape[:-2]
    n = len(lead)
    x4 = xt.reshape(lead + (N_KV, HEAD_DIM, xt.shape[-1]))
    return jnp.transpose(x4, tuple(range(n)) + (n + 2, n, n + 1))


def kernel(x_prompt, x_sample, state_pool, cache_win_k, cache_win_v, cache_fox_k, cache_fox_v,
           cache_fox_logf, page_table, norm_g, final_norm_g, rel_bias, pool_w_in, pool_mix,
           pool_scale, pool_w_out, swa_w_in, swa_sinks, swa_w_out, fox_w_in, fox_f_bias, fox_w_out):
    batch, seq, _ = x_prompt.shape
    db = x_sample.shape[0]
    depth = norm_g.shape[0]
    mp = batch * seq
    tm_p, tm_s = 512, db

    xp = x_prompt.reshape(mp, D_MODEL)
    xs = x_sample.reshape(db, D_MODEL)

    rb = rel_bias.astype(F32)
    dist_keys = WINDOW - np.arange(WINDOW)
    bias_keys = jnp.where((dist_keys < WINDOW)[None, :], rb[_t5_bucket_np(dist_keys)].T, NEG)
    bias0 = rb[0].reshape(N_HEADS, 1)
    fg = final_norm_g.reshape(1, D_MODEL)

    pool_p, pool_s = [], []
    wk_p, wv_p, wk_s, wv_s = [], [], [], []
    fk_p, fv_p, fl_p, fk_s, fv_s, fl_s = [], [], [], [], [], []
    def pool_layer_args(layer):
        jj = layer // 3
        fg_l = fg if layer == depth - 1 else None
        return (norm_g[layer].reshape(1, D_MODEL), pool_w_in[jj].astype(BF16), pool_mix[jj].astype(BF16),
                pool_scale[jj].reshape(1, D_MODEL), pool_w_out[jj].astype(BF16), fg_l)

    prompt_done = None
    pending = None
    for i in range(depth):
        kind, j = i % 3, i // 3
        g = norm_g[i].reshape(1, D_MODEL)
        if kind == 0:
            g, w_in, mix, scale, w_out, fg_l = pool_layer_args(i)
            if prompt_done is None:
                xp, u_tail = _pool_prompt(xp, pending, g, w_in, mix, scale, w_out, fg_l, batch, seq, tm_p)
                pending = None
            else:
                xp, u_tail = prompt_done
                prompt_done = None
            pool_p.append(u_tail[:, HALO - POOL_STATE:])
            xs, u_s = _pool_sample(xs, jnp.transpose(state_pool[j], (1, 0, 2)), g, w_in, mix, scale, w_out, fg_l)
            pool_s.append(jnp.concatenate([state_pool[j][:, 1:], u_s[:, None, :]], axis=1))
        elif kind == 1:
            w_in = swa_w_in[j].astype(BF16)
            w_out = swa_w_out[j].astype(BF16)
            sinks = swa_sinks[j].astype(F32)
            qt_p, kh_p, kt_p, vt_p, vtb_p, gate_p = _swa_proj(xp, g, w_in, tm_p, batch, seq, True)
            xp = _swa_prompt(qt_p, kh_p, vtb_p, gate_p, w_out, xp, rb, sinks, batch, seq, tm_p)
            wk_p.append(_tokens_first(kt_p[:, :, seq - WINDOW:]))
            wv_p.append(_tokens_first(vt_p[:, :, seq - WINDOW:]))
            q_s, k_s, v_s, kt_s, vt_s, gate_s = _swa_proj(xs, g, w_in, tm_s, 1, db, False)
            o_s, wk, wv = _swa_sample(_head_rows_q(q_s), _tokens_last(cache_win_k[j]),
                                      _tokens_last(cache_win_v[j]), k_s[:, None, :], v_s[:, None, :],
                                      kt_s[0], vt_s[0], bias_keys, bias0, sinks.reshape(N_HEADS, 1),
                                      SWA_ROWS_PER_STEP)
            xs = _out_proj(o_s.reshape(db, D_MODEL), gate_s, w_out, xs, tm_s)
            wk_s.append(_tokens_first(wk))
            wv_s.append(_tokens_first(wv))
        else:
            w_full = fox_w_in[j]
            nqkv = D_MODEL + 2 * KV_WIDTH
            w_in = jnp.concatenate([w_full[:, :nqkv], w_full[:, nqkv + N_HEADS:]], axis=1).astype(BF16)
            wf = jnp.pad(w_full[:, nqkv:nqkv + N_HEADS], ((0, 0), (0, LANES - N_HEADS))).astype(BF16)
            fb = jnp.pad(fox_f_bias[j].astype(F32), (0, LANES - N_HEADS)).reshape(1, LANES)
            w_out = fox_w_out[j].astype(BF16)
            qt, kt_p, vt_p, kaug, vtb, logft_p, gate_p = _fox_proj(xp, g, w_in, wf, fb, FOX_TILE, batch, seq, True)
            kaug = _fox_decay(logft_p, kaug, FOX_TILE)
            fk_p.append(_tokens_first(kt_p))
            fv_p.append(_tokens_first(vt_p))
            fl_p.append(jnp.transpose(logft_p, (0, 2, 1)))
            q_s, k_s, v_s, kt_s, vt_s, logf_s, logft_s, gate_s = _fox_proj(
                xs, g, w_in, wf, fb, tm_s, 1, db, False)
            decode_args = (page_table, _head_rows_q(q_s), k_s[:, None, :], v_s[:, None, :], logf_s[:, :, None],
                           _tokens_last(cache_fox_k[j]), _tokens_last(cache_fox_v[j]),
                           jnp.transpose(cache_fox_logf[j], (0, 2, 1)))
            next_is_pool = i + 1 < depth and (i + 1) % 3 == 0
            if _fox_prompt_fits_decode(batch, seq, FOX_TILE, db):
                o_p, o_s = _fox_prompt(qt, kaug, vtb, batch, seq, FOX_TILE, decode=decode_args)
                if next_is_pool:
                    pending = (o_p, gate_p, w_out)
                else:
                    xp = _out_proj(o_p, gate_p, w_out, xp, tm_p)
            else:
                o_p = _fox_prompt(qt, kaug, vtb, batch, seq, FOX_TILE)
                if next_is_pool:
                    o_s, xp, u_tail = _fox_sample(
                        *decode_args, FOX_PAGES_PER_STEP,
                        pool_layer=(xp, (o_p, gate_p, w_out), *pool_layer_args(i + 1), batch, seq))
                    prompt_done = (xp, u_tail)
                else:
                    xp = _out_proj(o_p, gate_p, w_out, xp, tm_p)
                    o_s = _fox_sample(*decode_args, FOX_PAGES_PER_STEP)
            xs = _out_proj(o_s.reshape(db, D_MODEL), gate_s, w_out, xs, tm_s)
            fk_s.append(_tokens_first(kt_s[0])[:, None])
            fv_s.append(_tokens_first(vt_s[0])[:, None])
            fl_s.append(logft_s[0].T[:, None, :])

    if (depth - 1) % 3 != 0:
        xp, xs = _final_norm(xp, fg, tm_p), _final_norm(xs, fg, tm_s)
    y_prompt = xp.reshape(batch, seq, D_MODEL)
    y_sample = xs.reshape(db, 1, D_MODEL)
    return (y_prompt, y_sample, jnp.stack(pool_p), jnp.stack(pool_s), jnp.stack(wk_p), jnp.stack(wv_p),
            jnp.stack(wk_s), jnp.stack(wv_s), jnp.stack(fk_p), jnp.stack(fv_p), jnp.stack(fl_p),
            jnp.stack(fk_s), jnp.stack(fv_s), jnp.stack(fl_s))
```

```python
import functools
import math

import numpy as np
import jax
import jax.numpy as jnp
from jax import lax
from jax.experimental import pallas as pl
from jax.experimental.pallas import tpu as pltpu

D_MODEL = 1024
HEAD_DIM = 64
N_HEADS = 16
N_KV = 4
GROUP = 4
KV_WIDTH = N_KV * HEAD_DIM
POOL_WINDOWS = (2, 4, 8, 16)
POOL_GROUP = 256
POOL_STATE = 15
WINDOW = 128
N_BUCKETS = 32
MAX_DISTANCE = 128
SCALE = HEAD_DIM ** -0.5
EPS = 1e-6
NEG = -1e30
PAGE = 128
LANES = 128

BF16 = jnp.bfloat16
F32 = jnp.float32

VMEM_LIMIT = 56 * 1024 * 1024
VMEM_LIMIT_RIDER = 62 * 1024 * 1024


def _cparams(sem):
    return pltpu.CompilerParams(dimension_semantics=sem, vmem_limit_bytes=VMEM_LIMIT)


def _rms_bf16(x, g):
    ms = jnp.mean(x * x, axis=-1, keepdims=True)
    return (x * lax.rsqrt(ms + EPS) * g).astype(BF16)


def _silu(x):
    return x * jax.nn.sigmoid(x)


def _dot(a, b):
    return jnp.dot(a, b, preferred_element_type=F32)


def _dot_nt(a, b):
    return lax.dot_general(a, b, (((1,), (1,)), ((), ())), preferred_element_type=F32)


def _bf16_round(x):
    return x.astype(BF16).astype(F32)


def _full(shape):
    n = len(shape)
    return pl.BlockSpec(shape, lambda *_: (0,) * n)


LOG2E = 1.4426950408889634


def _swa_proj_kernel(x_ref, g_ref, w_ref, *out_refs, prompt):
    if prompt:
        qt_ref, kh_ref, kt_ref, vt_ref, vtb_ref, gate_ref = out_refs
    else:
        q_ref, k_ref, v_ref, kt_ref, vt_ref, gate_ref = out_refs
    hb = _rms_bf16(x_ref[...], g_ref[...])
    q = _dot(hb, w_ref[:, :D_MODEL])
    k = _dot(hb, w_ref[:, D_MODEL:D_MODEL + KV_WIDTH])
    v = _dot(hb, w_ref[:, D_MODEL + KV_WIDTH:D_MODEL + 2 * KV_WIDTH])
    vt = v.T
    kt_ref[...] = k.T
    vt_ref[...] = vt
    if prompt:
        tm = q.shape[0]
        qt = (q * (SCALE * LOG2E)).T.astype(BF16)
        for hd in range(N_HEADS):
            h, g = divmod(hd, GROUP)
            for jb in range(tm // WINDOW):
                qt_ref[h, jb, :, g * WINDOW:(g + 1) * WINDOW] = (
                    qt[hd * HEAD_DIM:(hd + 1) * HEAD_DIM, jb * WINDOW:(jb + 1) * WINDOW])
        for h in range(N_KV):
            kh_ref[h] = k[:, h * HEAD_DIM:(h + 1) * HEAD_DIM].astype(BF16)
            ones_row = lax.broadcasted_iota(jnp.int32, (V_ROWS - HEAD_DIM, tm), 0) == 0
            vth = jnp.concatenate([vt[h * HEAD_DIM:(h + 1) * HEAD_DIM, :],
                                   jnp.where(ones_row, 1.0, 0.0)], axis=0).astype(BF16)
            for jb in range(tm // WINDOW):
                vtb_ref[h, jb] = vth[:, jb * WINDOW:(jb + 1) * WINDOW]
    else:
        q_ref[...] = (q * SCALE).astype(BF16)
        k_ref[...] = k
        v_ref[...] = v
    gate_ref[...] = _dot(hb, w_ref[:, D_MODEL + 2 * KV_WIDTH:])


def _swa_proj(x, g, w, tm, batch, seq, prompt):
    m = x.shape[0]
    ns = seq // tm
    nb = tm // WINDOW
    row = lambda i: (i, 0)
    sds = jax.ShapeDtypeStruct
    tcol = pl.BlockSpec((None, KV_WIDTH, tm), lambda i: (i // ns, 0, i % ns))
    blocks = lambda rows, last: pl.BlockSpec((None, N_KV, nb, rows, last), lambda i: (i // ns, 0, i % ns, 0, 0))
    kv_t_shape = [sds((batch, KV_WIDTH, seq), F32)] * 2
    if prompt:
        out_specs = [blocks(HEAD_DIM, GROUP * WINDOW), pl.BlockSpec((N_KV, tm, HEAD_DIM), lambda i: (0, i, 0)),
                     tcol, tcol, blocks(V_ROWS, WINDOW)]
        out_shape = ([sds((batch, N_KV, seq // WINDOW, HEAD_DIM, GROUP * WINDOW), BF16),
                      sds((N_KV, m, HEAD_DIM), BF16)] + kv_t_shape
                     + [sds((batch, N_KV, seq // WINDOW, V_ROWS, WINDOW), BF16)])
    else:
        out_specs = [pl.BlockSpec((tm, D_MODEL), row), pl.BlockSpec((tm, KV_WIDTH), row),
                     pl.BlockSpec((tm, KV_WIDTH), row), tcol, tcol]
        out_shape = [sds((m, D_MODEL), BF16), sds((m, KV_WIDTH), F32), sds((m, KV_WIDTH), F32)] + kv_t_shape
    out_specs.append(pl.BlockSpec((tm, D_MODEL), row))
    out_shape.append(sds((m, D_MODEL), F32))
    return pl.pallas_call(
        functools.partial(_swa_proj_kernel, prompt=prompt),
        grid=(m // tm,),
        in_specs=[pl.BlockSpec((tm, D_MODEL), row), _full((1, D_MODEL)), _full(w.shape)],
        out_specs=out_specs,
        out_shape=out_shape,
        compiler_params=_cparams(("parallel",)),
        name="swa_proj",
    )(x, g, w)


def _log_sigmoid(x):
    return -(jnp.maximum(-x, 0.0) + jnp.log1p(jnp.exp(-jnp.abs(x))))


AUG = 128
N_PIECES = 3
SWA_ROWS_PER_STEP = 8
V_ROWS = HEAD_DIM + 16
FOX_PAGES_PER_STEP = 64
FOX_TILE = 512
FOX_KEYS = 512
FOX_QUERIES = 256
GROUP_SHIFT = GROUP.bit_length() - 1


def _fox_proj_kernel(x_ref, g_ref, w_ref, wf_ref, fb_ref, *out_refs, prompt):
    if prompt:
        qt_ref, kt_ref, vt_ref, kaug_ref, vtb_ref, logft_ref, gate_ref = out_refs
    else:
        q_ref, k_ref, v_ref, kt_ref, vt_ref, logf_ref, logft_ref, gate_ref = out_refs
    hb = _rms_bf16(x_ref[...], g_ref[...])
    q = _dot(hb, w_ref[:, :D_MODEL])
    k = _dot(hb, w_ref[:, D_MODEL:D_MODEL + KV_WIDTH])
    v = _dot(hb, w_ref[:, D_MODEL + KV_WIDTH:D_MODEL + 2 * KV_WIDTH])
    vt = v.T
    kt_ref[...] = k.T
    vt_ref[...] = vt
    logf = _log_sigmoid(_dot(hb, wf_ref[...]) + fb_ref[...])
    logft_ref[...] = logf.T[:N_HEADS, :]
    if prompt:
        tm = q.shape[0]
        qt = (q * (SCALE * LOG2E)).T.astype(BF16)
        r = lax.broadcasted_iota(jnp.int32, (AUG - HEAD_DIM, tm), 0)
        for hd in range(N_HEADS):
            h, g = divmod(hd, GROUP)
            cols = slice(g * tm, (g + 1) * tm)
            qt_ref[h, 0:HEAD_DIM, cols] = qt[hd * HEAD_DIM:(hd + 1) * HEAD_DIM, :]
            pick = jnp.logical_and(r < N_PIECES * GROUP, (r & (GROUP - 1)) == g)
            qt_ref[h, HEAD_DIM:AUG, cols] = jnp.where(pick, -1.0, 0.0).astype(BF16)
        zeros = jnp.zeros((tm, AUG - HEAD_DIM), F32)
        for h in range(N_KV):
            kaug_ref[h] = jnp.concatenate([k[:, h * HEAD_DIM:(h + 1) * HEAD_DIM], zeros], axis=1).astype(BF16)
            ones_row = lax.broadcasted_iota(jnp.int32, (V_ROWS - HEAD_DIM, tm), 0) == 0
            vth = jnp.concatenate([vt[h * HEAD_DIM:(h + 1) * HEAD_DIM, :],
                                   jnp.where(ones_row, 1.0, 0.0)], axis=0).astype(BF16)
            for kb in range(tm // FOX_KEYS):
                vtb_ref[h, kb] = vth[:, kb * FOX_KEYS:(kb + 1) * FOX_KEYS]
    else:
        q_ref[...] = (q * SCALE).astype(BF16)
        k_ref[...] = k
        v_ref[...] = v
        logf_ref[...] = logf[:, :N_HEADS]
    gate_ref[...] = _dot(hb, w_ref[:, D_MODEL + 2 * KV_WIDTH:])


def _fox_proj(x, g, w, wf, fb, tm, batch, seq, prompt):
    m = x.shape[0]
    ns = seq // tm
    row = lambda i: (i, 0)
    tcol = lambda r: pl.BlockSpec((None, r, tm), lambda i: (i // ns, 0, i % ns))
    sds = jax.ShapeDtypeStruct
    kv_t = [tcol(KV_WIDTH), tcol(KV_WIDTH)]
    kv_t_shape = [sds((batch, KV_WIDTH, seq), F32)] * 2
    if prompt:
        out_specs = ([pl.BlockSpec((None, N_KV, None, AUG, GROUP * tm), lambda i: (i // ns, 0, i % ns, 0, 0))] + kv_t
                     + [pl.BlockSpec((N_KV, tm, AUG), lambda i: (0, i, 0)),
                        pl.BlockSpec((None, N_KV, tm // FOX_KEYS, V_ROWS, FOX_KEYS),
                                     lambda i: (i // ns, 0, i % ns, 0, 0)),
                        tcol(N_HEADS)])
        out_shape = ([sds((batch, N_KV, ns, AUG, GROUP * tm), BF16)] + kv_t_shape
                     + [sds((N_KV, m, AUG), BF16),
                        sds((batch, N_KV, seq // FOX_KEYS, V_ROWS, FOX_KEYS), BF16),
                        sds((batch, N_HEADS, seq), F32)])
    else:
        out_specs = ([pl.BlockSpec((tm, D_MODEL), row), pl.BlockSpec((tm, KV_WIDTH), row),
                      pl.BlockSpec((tm, KV_WIDTH), row)] + kv_t
                     + [pl.BlockSpec((tm, N_HEADS), row), tcol(N_HEADS)])
        out_shape = ([sds((m, D_MODEL), BF16), sds((m, KV_WIDTH), F32), sds((m, KV_WIDTH), F32)]
                     + kv_t_shape + [sds((m, N_HEADS), F32), sds((batch, N_HEADS, seq), F32)])
    out_specs.append(pl.BlockSpec((tm, D_MODEL), row))
    out_shape.append(sds((m, D_MODEL), F32))
    return pl.pallas_call(
        functools.partial(_fox_proj_kernel, prompt=prompt),
        grid=(m // tm,),
        in_specs=[pl.BlockSpec((tm, D_MODEL), row), _full((1, D_MODEL)), _full(w.shape),
                  _full(wf.shape), _full(fb.shape)],
        out_specs=out_specs,
        out_shape=out_shape,
        compiler_params=_cparams(("parallel",)),
        name="fox_proj",
    )(x, g, w, wf, fb)


def _out_proj_kernel(*refs, gated):
    if gated:
        o_ref, gate_ref, w_ref, x_ref, y_ref = refs
        ob = (o_ref[...] * _silu(gate_ref[...])).astype(BF16)
    else:
        o_ref, w_ref, x_ref, y_ref = refs
        ob = o_ref[...]
    y_ref[...] = x_ref[...] + _dot(ob, w_ref[...])


def _out_proj(o, gate, w, x, tm):
    m = x.shape[0]
    row = lambda i: (i, 0)
    tile = pl.BlockSpec((tm, D_MODEL), row)
    gated = gate is not None
    ins = [o, gate, w, x] if gated else [o, w, x]
    in_specs = [tile, tile, _full(w.shape), tile] if gated else [tile, _full(w.shape), tile]
    return pl.pallas_call(
        functools.partial(_out_proj_kernel, gated=gated),
        grid=(m // tm,),
        in_specs=in_specs,
        out_specs=tile,
        out_shape=jax.ShapeDtypeStruct((m, D_MODEL), F32),
        compiler_params=_cparams(("parallel",)),
        name="out_proj",
    )(*ins)


def _final_norm_kernel(x_ref, g_ref, y_ref):
    x = x_ref[...]
    ms = jnp.mean(x * x, axis=-1, keepdims=True)
    y_ref[...] = x * lax.rsqrt(ms + EPS) * g_ref[...]


def _final_norm(x, g, tm):
    m = x.shape[0]
    row = lambda i: (i, 0)
    return pl.pallas_call(
        _final_norm_kernel,
        grid=(m // tm,),
        in_specs=[pl.BlockSpec((tm, D_MODEL), row), _full((1, D_MODEL))],
        out_specs=pl.BlockSpec((tm, D_MODEL), row),
        out_shape=jax.ShapeDtypeStruct((m, D_MODEL), F32),
        compiler_params=_cparams(("parallel",)),
        name="final_norm",
    )(x, g)


HALO = 16
PAD = 8
assert all(w == 2 << g for g, w in enumerate(POOL_WINDOWS)) and HALO >= max(POOL_WINDOWS) and PAD >= HALO // 2


def _pool_layer_tail(x, u, gate, pooled_groups, mix_ref, scale_ref, wout_ref, fg_ref, y_ref):
    pieces = []
    for g in range(len(POOL_WINDOWS)):
        c0 = g * POOL_GROUP
        p = (pooled_groups[g] - u[:, c0:c0 + POOL_GROUP]).astype(BF16)
        pieces.append(_dot(p, mix_ref[g]))
    pm = jnp.concatenate(pieces, axis=1)
    o = (pm * scale_ref[...] * _silu(gate)).astype(BF16)
    y = x + _dot(o, wout_ref[...])
    if fg_ref is not None:
        ms = jnp.mean(y * y, axis=-1, keepdims=True)
        y = y * lax.rsqrt(ms + EPS) * fg_ref[...]
    y_ref[...] = y


def _n_pool_inputs(final, pending):
    return 6 + (3 if pending else 0) + (1 if final else 0)


def _pool_prompt_kernel(*refs, tp, final, pending):
    n_in = _n_pool_inputs(final, pending)
    _pool_tile(pl.program_id(1), pl.num_programs(1), refs[:n_in], refs[n_in:n_in + 2], refs[n_in + 2:],
               tp=tp, final=final, pending=pending)


def _pool_tile(i, n_tiles, in_refs, out_refs, bufs, *, tp, final, pending):
    refs = list(in_refs)
    x_ref = refs.pop(0)
    prev = [refs.pop(0) for _ in range(3)] if pending else None
    g_ref, win_ref, mix_ref, scale_ref, wout_ref = [refs.pop(0) for _ in range(5)]
    fg_ref = refs.pop(0) if final else None
    y_ref, tail_ref = out_refs
    hist = slice(PAD, PAD + HALO)
    ext = slice(PAD, PAD + HALO + tp)
    tile = slice(PAD + HALO, PAD + HALO + tp)

    def back(rows, k):
        return slice(rows.start - k, rows.stop - k)

    @pl.when(i == 0)
    def _():
        for buf in bufs:
            buf[0:PAD + HALO, :] = jnp.zeros((PAD + HALO, buf.shape[1]), F32)

    x = x_ref[...]
    if pending:
        po_ref, pgate_ref, pw_ref = prev
        x = x + _dot((po_ref[...] * _silu(pgate_ref[...])).astype(BF16), pw_ref[...])
    hb = _rms_bf16(x, g_ref[...])
    u = _dot(hb, win_ref[:, :D_MODEL])
    gate = _dot(hb, win_ref[:, D_MODEL:])
    bufs[0][tile, :] = u
    pos = i * tp + lax.broadcasted_iota(jnp.int32, (tp, 1), 0)
    pooled = []
    for g, w in enumerate(POOL_WINDOWS):
        src, span = bufs[g], 1 << g
        if g + 1 < len(bufs):
            nxt = bufs[g + 1]
            nxt[ext, :] = src[ext, POOL_GROUP:] + src[back(ext, span), POOL_GROUP:]
        acc = src[tile, 0:POOL_GROUP] + src[back(tile, span), 0:POOL_GROUP]
        inv_cnt = 1.0 / jnp.minimum(pos + 1, w).astype(F32)
        pooled.append(acc * inv_cnt)
    _pool_layer_tail(x, u, gate, pooled, mix_ref, scale_ref, wout_ref, fg_ref, y_ref)
    bufs[0][hist, :] = u[tp - HALO:tp, :]

    @pl.when(i == n_tiles - 1)
    def _():
        tail_ref[...] = u[tp - HALO:tp, :]


def _pool_prompt_operands(x, pending, g, w_in, mix, scale, w_out, fg, tp, row_tile, batch_of, single_buffer):
    final = fg is not None
    tile = pl.BlockSpec((tp, D_MODEL), lambda *idx: (row_tile(*idx), 0))
    whole = (lambda a: pl.BlockSpec(a.shape, lambda *_: (0,) * a.ndim, pipeline_mode=pl.Buffered(1))
             if single_buffer else _full(a.shape))
    ins = [x] + (list(pending) if pending else []) + [g, w_in, mix, scale, w_out] + ([fg] if final else [])
    in_specs = [tile] + ([tile, tile, whole(pending[2])] if pending else [])
    in_specs += [whole(a) for a in ins[len(in_specs):]]
    out_specs = [tile, pl.BlockSpec((None, HALO, D_MODEL), lambda *idx: (batch_of(*idx), 0, 0))]
    scratch = [pltpu.VMEM((PAD + HALO + tp, D_MODEL - g * POOL_GROUP), F32) for g in range(len(POOL_WINDOWS))]
    return ins, in_specs, out_specs, scratch, final


def _pool_prompt(x, pending, g, w_in, mix, scale, w_out, fg, batch, seq, tp):
    ns = seq // tp
    ins, in_specs, out_specs, scratch, final = _pool_prompt_operands(
        x, pending, g, w_in, mix, scale, w_out, fg, tp, lambda b, i: b * ns + i, lambda b, i: b, False)
    return pl.pallas_call(
        functools.partial(_pool_prompt_kernel, tp=tp, final=final, pending=bool(pending)),
        grid=(batch, ns),
        in_specs=in_specs,
        out_specs=out_specs,
        out_shape=[jax.ShapeDtypeStruct((batch * seq, D_MODEL), F32),
                   jax.ShapeDtypeStruct((batch, HALO, D_MODEL), F32)],
        scratch_shapes=scratch,
        compiler_params=_cparams(("parallel", "arbitrary")),
        name="pool_prompt",
    )(*ins)


def _pool_sample_kernel(*refs, final):
    if final:
        x_ref, st_ref, g_ref, win_ref, mix_ref, scale_ref, wout_ref, fg_ref, y_ref, u_ref = refs
    else:
        x_ref, st_ref, g_ref, win_ref, mix_ref, scale_ref, wout_ref, y_ref, u_ref = refs
        fg_ref = None
    x = x_ref[...]
    hb = _rms_bf16(x, g_ref[...])
    u = _dot(hb, win_ref[:, :D_MODEL])
    gate = _dot(hb, win_ref[:, D_MODEL:])
    u_ref[...] = u
    pooled = []
    for g, w in enumerate(POOL_WINDOWS):
        c0 = g * POOL_GROUP
        acc = u[:, c0:c0 + POOL_GROUP]
        for k in range(1, w):
            acc = acc + st_ref[POOL_STATE - k, :, c0:c0 + POOL_GROUP]
        pooled.append(acc / float(w))
    _pool_layer_tail(x, u, gate, pooled, mix_ref, scale_ref, wout_ref, fg_ref, y_ref)


def _pool_sample(x, state_t, g, w_in, mix, scale, w_out, fg):
    m = x.shape[0]
    final = fg is not None
    ins = [x, state_t, g, w_in, mix, scale, w_out] + ([fg] if final else [])
    return pl.pallas_call(
        functools.partial(_pool_sample_kernel, final=final),
        grid=(1,),
        in_specs=[_full(a.shape) for a in ins],
        out_specs=[_full((m, D_MODEL)), _full((m, D_MODEL))],
        out_shape=[jax.ShapeDtypeStruct((m, D_MODEL), F32)] * 2,
        compiler_params=_cparams(("arbitrary",)),
        name="pool_sample",
    )(*ins)


def _t5_bucket_np(dist):
    n = np.maximum(dist, 0)
    max_exact = N_BUCKETS // 2
    nf = np.maximum(n, 1).astype(np.float32)
    large = max_exact + (np.log(nf / max_exact) / math.log(MAX_DISTANCE / max_exact)
                         * (N_BUCKETS - max_exact)).astype(np.int32)
    large = np.minimum(large, N_BUCKETS - 1)
    return np.where(n < max_exact, n, large)


def _swa_prompt_kernel(sink_ref, rb_ref, qt_ref, kp_ref, kc_ref, vp_ref, vc_ref, bucket_ref, gate_ref,
                       wout_ref, x_ref, y_ref, bias_ref, o_buf, *, nb):
    i = pl.program_id(1)
    cols = GROUP * WINDOW

    @pl.when(i == 0)
    def _():
        bucket = bucket_ref[...]
        hits = [bucket == bk for bk in range(N_BUCKETS)]
        for hd in range(N_HEADS):
            h, g = divmod(hd, GROUP)
            b = jnp.full((2 * WINDOW, WINDOW), NEG, F32)
            for bk in range(N_BUCKETS):
                b = jnp.where(hits[bk], rb_ref[bk, hd] * LOG2E, b)
            bias_ref[h, :, g * WINDOW:(g + 1) * WINDOW] = b

    key_row = lax.broadcasted_iota(jnp.int32, (2 * WINDOW, cols), 0)
    no_prev = jnp.logical_and(i == 0, key_row < WINDOW)
    sinks = [jnp.concatenate([jnp.full((1, WINDOW), sink_ref[h * GROUP + g] * LOG2E, F32) for g in range(GROUP)],
                             axis=1) for h in range(N_KV)]

    def logits(jb, h):
        if jb == 0:
            kband = jnp.concatenate([kp_ref[h], kc_ref[h, 0:WINDOW, :]], axis=0)
        else:
            kband = kc_ref[h, (jb - 1) * WINDOW:(jb + 1) * WINDOW, :]
        s = _dot(kband, qt_ref[h, jb]) + bias_ref[h]
        return jnp.where(no_prev, NEG, s) if jb == 0 else s

    def attend(jb, h, s):
        vprev = vp_ref[h] if jb == 0 else vc_ref[h, jb - 1]
        vband = jnp.concatenate([vprev, vc_ref[h, jb]], axis=1)
        m = jnp.maximum(jnp.max(s, axis=0, keepdims=True), sinks[h])
        pv = _dot(vband, jnp.exp2(s - m).astype(BF16))
        denom = pv[HEAD_DIM:HEAD_DIM + 1, :] + jnp.exp2(sinks[h] - m)
        ot = pv[0:HEAD_DIM, :] * (1.0 / denom)
        o_heads = jnp.concatenate([ot[:, g * WINDOW:(g + 1) * WINDOW] for g in range(GROUP)], axis=0)
        o_buf[jb * WINDOW:(jb + 1) * WINDOW, h * GROUP * HEAD_DIM:(h + 1) * GROUP * HEAD_DIM] = o_heads.T

    units = [(jb, h) for jb in range(nb) for h in range(N_KV)]
    s = logits(*units[0])
    for u, unit in enumerate(units):
        s_next = logits(*units[u + 1]) if u + 1 < len(units) else None
        attend(*unit, s)
        s = s_next
    ob = (o_buf[...] * _silu(gate_ref[...])).astype(BF16)
    y_ref[...] = x_ref[...] + _dot(ob, wout_ref[...])


def _swa_prompt(qt, kh, vtb, gate, w_out, x, rel_bias, sinks, batch, seq, tm):
    ns = seq // tm
    nb = tm // WINDOW
    dist = np.arange(WINDOW)[None, :] + WINDOW - np.arange(2 * WINDOW)[:, None]
    bucket = np.where((dist >= 0) & (dist < WINDOW), _t5_bucket_np(dist), -1).astype(np.int32)
    tile = pl.BlockSpec((tm, D_MODEL), lambda b, i: (b * ns + i, 0))
    smem = pl.BlockSpec(memory_space=pltpu.SMEM)
    prev_blk = lambda i: jnp.maximum(i * nb - 1, 0)
    return pl.pallas_call(
        functools.partial(_swa_prompt_kernel, nb=nb),
        grid=(batch, ns),
        in_specs=[smem, smem,
                  pl.BlockSpec((None, N_KV, nb, HEAD_DIM, GROUP * WINDOW), lambda b, i: (b, 0, i, 0, 0)),
                  pl.BlockSpec((N_KV, WINDOW, HEAD_DIM), lambda b, i: (0, b * ns * nb + prev_blk(i), 0)),
                  pl.BlockSpec((N_KV, tm, HEAD_DIM), lambda b, i: (0, b * ns + i, 0)),
                  pl.BlockSpec((None, N_KV, None, V_ROWS, WINDOW), lambda b, i: (b, 0, prev_blk(i), 0, 0)),
                  pl.BlockSpec((None, N_KV, nb, V_ROWS, WINDOW), lambda b, i: (b, 0, i, 0, 0)),
                  _full(bucket.shape), tile, _full(w_out.shape), tile],
        out_specs=tile,
        out_shape=jax.ShapeDtypeStruct((batch * seq, D_MODEL), F32),
        scratch_shapes=[pltpu.VMEM((N_KV, 2 * WINDOW, GROUP * WINDOW), F32),
                        pltpu.VMEM((tm, D_MODEL), F32)],
        compiler_params=_cparams(("parallel", "arbitrary")),
        name="swa_prompt",
    )(sinks, rel_bias, qt, kh, kh, vtb, vtb, jnp.asarray(bucket), gate, w_out, x)


def _head_diag(o_full):
    out = jnp.zeros((N_HEADS, HEAD_DIM), F32)
    row_kv = lax.broadcasted_iota(jnp.int32, (N_HEADS, HEAD_DIM), 0) // GROUP
    for h in range(N_KV):
        out = out + jnp.where(row_kv == h, o_full[:, h * HEAD_DIM:(h + 1) * HEAD_DIM], 0.0)
    return out


def _swa_sample_kernel(qm_ref, kc_ref, vc_ref, kn_ref, vn_ref, knt_ref, vnt_ref, bias_ref, bias0_ref,
                       sink_ref, o_ref, wk_ref, wv_ref, *, bt):
    i = pl.program_id(0)
    lane = lax.broadcasted_iota(jnp.int32, (KV_WIDTH, WINDOW), 1)
    sink = sink_ref[...]
    for e in range(bt):
        b = i * bt + e
        kc = kc_ref[e]
        vc = vc_ref[e]
        qm = qm_ref[e]
        s = _dot(qm, kc.astype(BF16)) + bias_ref[...]
        s_n = jnp.sum(qm.astype(F32) * _bf16_round(kn_ref[e]), axis=1, keepdims=True) + bias0_ref[...]
        m = jnp.maximum(jnp.maximum(jnp.max(s, axis=1, keepdims=True), s_n), sink)
        p = jnp.exp(s - m)
        p_n = jnp.exp(s_n - m)
        denom = jnp.sum(p, axis=1, keepdims=True) + p_n + jnp.exp(sink - m)
        o_full = _dot_nt((p / denom).astype(BF16), vc.astype(BF16))
        o_full = o_full + _bf16_round(p_n / denom) * _bf16_round(vn_ref[e])
        o_ref[e] = _head_diag(o_full)
        kcol = jnp.sum(jnp.where(lane == b, knt_ref[...], 0.0), axis=1, keepdims=True)
        vcol = jnp.sum(jnp.where(lane == b, vnt_ref[...], 0.0), axis=1, keepdims=True)
        wk_ref[e] = jnp.where(lane == WINDOW - 1, kcol, pltpu.roll(kc, WINDOW - 1, 1))
        wv_ref[e] = jnp.where(lane == WINDOW - 1, vcol, pltpu.roll(vc, WINDOW - 1, 1))


def _swa_sample(qm, kc, vc, kn, vn, knt, vnt, bias_keys, bias0, sinks, bt):
    m = qm.shape[0]
    blk3 = lambda s1, s2: pl.BlockSpec((bt, s1, s2), lambda i: (i, 0, 0))
    cache = jax.ShapeDtypeStruct((m, KV_WIDTH, WINDOW), F32)
    return pl.pallas_call(
        functools.partial(_swa_sample_kernel, bt=bt),
        grid=(m // bt,),
        in_specs=[blk3(N_HEADS, KV_WIDTH), blk3(KV_WIDTH, WINDOW), blk3(KV_WIDTH, WINDOW),
                  blk3(1, KV_WIDTH), blk3(1, KV_WIDTH), _full(knt.shape), _full(vnt.shape),
                  _full(bias_keys.shape), _full(bias0.shape), _full(sinks.shape)],
        out_specs=[blk3(N_HEADS, HEAD_DIM), blk3(KV_WIDTH, WINDOW), blk3(KV_WIDTH, WINDOW)],
        out_shape=[jax.ShapeDtypeStruct((m, N_HEADS, HEAD_DIM), F32), cache, cache],
        compiler_params=_cparams(("parallel",)),
        name="swa_sample",
    )(qm, kc, vc, kn, vn, knt, vnt, bias_keys, bias0, sinks)


def _split3(x):
    hi = x.astype(BF16)
    r = x - hi.astype(F32)
    mid = r.astype(BF16)
    lo = (r - mid.astype(F32)).astype(BF16)
    return hi, mid, lo


def _cumsum_lanes_blocks(x, blk):
    n = x.shape[0]
    hi, mid, lo = _split3(x)
    r_i = lax.broadcasted_iota(jnp.int32, (blk, blk), 0)
    c_i = lax.broadcasted_iota(jnp.int32, (blk, blk), 1)
    upper = jnp.where(r_i <= c_i, 1.0, 0.0).astype(BF16)
    r = _dot(jnp.concatenate([hi, mid, lo], axis=0), upper)
    return r[0:n] + r[n:2 * n] + r[2 * n:]


def _fox_decay_kernel(x_ref, kin_ref, kout_ref, carry, *, blk):
    i = pl.program_id(1)

    @pl.when(i == 0)
    def _():
        carry[...] = jnp.zeros_like(carry)

    c = _cumsum_lanes_blocks(x_ref[...], blk) + carry[:, 0:1]
    carry[...] = jnp.broadcast_to(c[:, blk - 1:blk], carry.shape)
    hi, mid, lo = _split3(c * LOG2E)
    pieces = jnp.concatenate([hi, mid, lo], axis=0).astype(F32)
    pad = jnp.zeros((LANES - N_PIECES * N_HEADS, blk), F32)
    pieces_t = jnp.concatenate([pieces, pad], axis=0).T.astype(BF16)
    src = lax.broadcasted_iota(jnp.int32, (LANES, AUG), 0)
    dst = lax.broadcasted_iota(jnp.int32, (LANES, AUG), 1) - HEAD_DIM
    in_aug = jnp.logical_and(dst >= 0, dst < N_PIECES * GROUP)
    for h in range(N_KV):
        want = (dst >> GROUP_SHIFT) * N_HEADS + h * GROUP + (dst & (GROUP - 1))
        place = jnp.where(jnp.logical_and(in_aug, src == want), 1.0, 0.0).astype(BF16)
        kout_ref[h] = kin_ref[h] + _dot(pieces_t, place).astype(BF16)


def _fox_decay(logft, kaug, blk):
    b, h, s = logft.shape
    ns = s // blk
    kspec = pl.BlockSpec((N_KV, blk, AUG), lambda bi, i: (0, bi * ns + i, 0))
    return pl.pallas_call(
        functools.partial(_fox_decay_kernel, blk=blk),
        grid=(b, ns),
        in_specs=[pl.BlockSpec((None, h, blk), lambda bi, i: (bi, 0, i)), kspec],
        out_specs=kspec,
        out_shape=jax.ShapeDtypeStruct(kaug.shape, kaug.dtype),
        scratch_shapes=[pltpu.VMEM((h, LANES), F32)],
        input_output_aliases={1: 0},
        compiler_params=_cparams(("parallel", "arbitrary")),
        name="fox_decay",
    )(logft, kaug)


def _decode_element(qm, kn, vn, fn, kbuf, vbuf, fbuf):
    cp = kbuf.shape[0]
    cin = _cumsum_lanes_blocks(fbuf[...].reshape(cp * N_HEADS, PAGE), PAGE).reshape(cp, N_HEADS, PAGE)
    carry = jnp.zeros((N_HEADS, 1), F32)
    cs = []
    for p in range(cp):
        cs.append(cin[p] + carry)
        carry = carry + cin[p][:, PAGE - 1:PAGE]
    kcat = jnp.concatenate([kbuf[p] for p in range(cp)], axis=1).astype(BF16)
    t = _dot(qm, kcat) - jnp.concatenate(cs, axis=1)
    t_n = jnp.sum(qm.astype(F32) * _bf16_round(kn), axis=1, keepdims=True) - (carry + fn)
    m = jnp.maximum(jnp.max(t, axis=1, keepdims=True), t_n)
    p = jnp.exp(t - m)
    p_n = jnp.exp(t_n - m)
    denom = jnp.sum(p, axis=1, keepdims=True) + p_n
    vcat = jnp.concatenate([vbuf[p_] for p_ in range(cp)], axis=1).astype(BF16)
    pb = p.astype(BF16)
    half = KV_WIDTH // 2
    acc = jnp.concatenate([_dot_nt(pb, vcat[:half]), _dot_nt(pb, vcat[half:])], axis=1)
    acc = acc + _bf16_round(p_n) * _bf16_round(vn)
    return _head_diag(acc / denom)


def _fox_prompt_kernel(*refs, tq, tk, decode):
    assert tq == tk
    if decode:
        (pt_ref, qt_ref, k_ref, vt_ref, qm_ref, kn_ref, vn_ref, fn_ref, ck_hbm, cv_hbm, cf_hbm,
         o_ref, os_ref, *scratch) = refs
        kbuf, vbuf, fbuf, sems = scratch[6:]
        scratch = scratch[:6]
    else:
        qt_ref, k_ref, vt_ref, o_ref, *scratch = refs
    t = pl.program_id(2)
    m_ref, acc_ref, sa_ref, sb_ref, ca_ref, cb_ref = scratch
    cols = GROUP * tq

    if decode:
        step = (pl.program_id(0) * pl.num_programs(1) + pl.program_id(1)) * pl.num_programs(2) + t
        n_rows = 2 * pl.num_programs(0) * pl.num_programs(1) * pl.num_programs(2)
        n_pages = kbuf.shape[1]

    def page_copies(r, slot, p):
        page = pt_ref[r, p]
        return (pltpu.make_async_copy(ck_hbm.at[page], kbuf.at[slot, p], sems.at[slot, 0]),
                pltpu.make_async_copy(cv_hbm.at[page], vbuf.at[slot, p], sems.at[slot, 1]),
                pltpu.make_async_copy(cf_hbm.at[page], fbuf.at[slot, p], sems.at[slot, 2]))

    def request_pages(r, slot):
        for p in range(n_pages):
            for cpy in page_copies(r, slot, p):
                cpy.start()

    def sample_wait(tile):
        if not decode:
            return
        row = 2 * step + tile

        if tile == 0:
            @pl.when(step == 0)
            def _():
                request_pages(row, 0)
                request_pages(row + 1, 1)

        for p in range(n_pages):
            for cpy in page_copies(row, tile, p):
                cpy.wait()

    def sample_attend(tile):
        if not decode:
            return
        row = 2 * step + tile
        os_ref[tile] = _decode_element(qm_ref[tile], kn_ref[tile], vn_ref[tile], fn_ref[tile],
                                       kbuf.at[tile], vbuf.at[tile], fbuf.at[tile])

        @pl.when(row + 2 < n_rows)
        def _():
            request_pages(row + 2, tile)

    def reset():
        m_ref[...] = jnp.full(m_ref.shape, NEG, F32)
        acc_ref[...] = jnp.zeros(acc_ref.shape, F32)

    def logits(tile, kb, s_ref, c_ref):
        k0 = pl.multiple_of(kb * tk, tk)
        s = _dot(k_ref[pl.ds(k0, tk), :], qt_ref[tile])
        s_ref[...] = s
        c_ref[...] = jnp.max(s, axis=0, keepdims=True)

    def accumulate(kb, s_ref, c_ref, masked):
        s = s_ref[...]
        if masked:
            kpos = lax.broadcasted_iota(jnp.int32, (tk, cols), 0)
            qpos = lax.broadcasted_iota(jnp.int32, (tk, cols), 1) & (tq - 1)
            s = jnp.where(kpos <= qpos, s, NEG)
            cmax = jnp.max(s, axis=0, keepdims=True)
        else:
            cmax = c_ref[...]
        m_old = m_ref[...]
        m_new = jnp.maximum(m_old, cmax)
        alpha = jnp.exp2(m_old - m_new)
        p = jnp.exp2(s - m_new)
        acc_ref[...] = alpha * acc_ref[...] + _dot(vt_ref[kb], p.astype(BF16))
        m_ref[...] = m_new

    def finish(tile):
        o = acc_ref[0:HEAD_DIM, :] / acc_ref[HEAD_DIM:HEAD_DIM + 1, :]
        o_ref[tile * tq:(tile + 1) * tq, :] = jnp.concatenate(
            [o[:, g * tq:(g + 1) * tq] for g in range(GROUP)], axis=0).T

    def pairs(tile, first, second):
        def body(j, carry):
            kb = 2 * j
            logits(tile, kb + 1, *second)
            accumulate(kb, *first, False)
            logits(tile, kb + 2, *first)
            accumulate(kb + 1, *second, False)
            return carry
        lax.fori_loop(0, t, body, 0)

    buf_a, buf_b = (sa_ref, ca_ref), (sb_ref, cb_ref)
    sample_wait(0)
    reset()
    logits(0, 0, *buf_a)
    sample_attend(0)
    pairs(0, buf_a, buf_b)
    sample_wait(1)
    logits(1, 0, *buf_b)
    accumulate(2 * t, *buf_a, True)
    finish(0)
    sample_attend(1)
    reset()
    pairs(1, buf_b, buf_a)
    logits(1, 2 * t + 1, *buf_a)
    accumulate(2 * t, *buf_b, False)
    accumulate(2 * t + 1, *buf_a, True)
    finish(1)


def _fox_prompt_fits_decode(batch, seq, tq, n_rows):
    return n_rows == batch * N_KV * (seq // tq)


def _fox_prompt(qt, kaug, vtb, batch, seq, tq, decode=None):
    nq = seq // tq
    nk, tk = vtb.shape[2], vtb.shape[4]
    cols = GROUP * tq
    nt = nq // 2
    ins = [qt, kaug, vtb]
    in_specs = [pl.BlockSpec((None, None, 2, AUG, cols), lambda b, h, t, *_: (b, h, t, 0, 0)),
                pl.BlockSpec((None, seq, AUG), lambda b, h, t, *_: (h, b, 0)),
                pl.BlockSpec((None, None, nk, V_ROWS, tk), lambda b, h, t, *_: (b, h, 0, 0, 0))]
    out_specs = [pl.BlockSpec((2 * tq, GROUP * HEAD_DIM), lambda b, h, t, *_: (b * nt + t, h))]
    out_shape = [jax.ShapeDtypeStruct((batch * seq, D_MODEL), F32)]
    scratch = [pltpu.VMEM((1, cols), F32),
               pltpu.VMEM((V_ROWS, cols), F32),
               pltpu.VMEM((tk, cols), F32), pltpu.VMEM((tk, cols), F32),
               pltpu.VMEM((1, cols), F32), pltpu.VMEM((1, cols), F32)]
    prefetch = []
    if decode is not None:
        page_table, qm, kn, vn, fn, ck, cv, cf = decode
        n_rows, n_pages = page_table.shape
        assert _fox_prompt_fits_decode(batch, seq, tq, n_rows)
        pair = lambda s1, s2: pl.BlockSpec((2, s1, s2), lambda b, h, t, pt: ((b * N_KV + h) * nt + t, 0, 0))
        any_spec = pl.BlockSpec(memory_space=pl.ANY)
        prefetch = [page_table]
        ins += [qm, kn, vn, fn, ck, cv, cf]
        in_specs += [pair(N_HEADS, KV_WIDTH), pair(1, KV_WIDTH), pair(1, KV_WIDTH), pair(N_HEADS, 1),
                     any_spec, any_spec, any_spec]
        out_specs.append(pair(N_HEADS, HEAD_DIM))
        out_shape.append(jax.ShapeDtypeStruct((n_rows, N_HEADS, HEAD_DIM), F32))
        scratch += [pltpu.VMEM((2, n_pages, KV_WIDTH, PAGE), F32), pltpu.VMEM((2, n_pages, KV_WIDTH, PAGE), F32),
                    pltpu.VMEM((2, n_pages, N_HEADS, PAGE), F32), pltpu.SemaphoreType.DMA((2, 3))]
    semantics = ("arbitrary",) * 3 if decode is not None else ("parallel", "parallel", "arbitrary")
    outs = pl.pallas_call(
        functools.partial(_fox_prompt_kernel, tq=tq, tk=tk, decode=decode is not None),
        grid_spec=pltpu.PrefetchScalarGridSpec(
            num_scalar_prefetch=len(prefetch),
            grid=(batch, N_KV, nt),
            in_specs=in_specs,
            out_specs=out_specs,
            scratch_shapes=scratch,
        ),
        out_shape=out_shape,
        compiler_params=pltpu.CompilerParams(
            dimension_semantics=semantics,
            vmem_limit_bytes=VMEM_LIMIT_RIDER if decode is not None else VMEM_LIMIT),
        name="fox_prompt",
    )(*prefetch, *ins)
    return outs if decode is not None else outs[0]


N_DECODE_SCRATCH = 8


def _fox_sample_kernel(pt_ref, qm_ref, kn_ref, vn_ref, fn_ref, ck_hbm, cv_hbm, cf_hbm, *rest, cp, nchunk, rider):
    n_in = _n_pool_inputs(rider["final"], rider["pending"]) if rider else 0
    rider_in, o_ref, rest = rest[:n_in], rest[n_in], rest[n_in + 1:]
    rider_out, rest = (rest[:2], rest[2:]) if rider else ((), rest)
    kbuf, vbuf, fbuf, sems, m_ref, l_ref, acc_ref, carry_ref = rest[:N_DECODE_SCRATCH]
    rider_bufs = rest[N_DECODE_SCRATCH:]
    b = pl.program_id(0)
    c = pl.program_id(1)
    nb = pl.num_programs(0)
    step = b * nchunk + c
    slot = step % 2

    def copies(bb, cc, sl, p):
        page = pt_ref[bb, cc * cp + p]
        return (pltpu.make_async_copy(ck_hbm.at[page], kbuf.at[sl, p], sems.at[sl, 0]),
                pltpu.make_async_copy(cv_hbm.at[page], vbuf.at[sl, p], sems.at[sl, 1]),
                pltpu.make_async_copy(cf_hbm.at[page], fbuf.at[sl, p], sems.at[sl, 2]))

    def issue(bb, cc, sl):
        for p in range(cp):
            for cpy in copies(bb, cc, sl, p):
                cpy.start()

    @pl.when(step == 0)
    def _():
        issue(b, c, slot)

    @pl.when(step + 1 < nb * nchunk)
    def _():
        nxt = step + 1
        issue(nxt // nchunk, nxt % nchunk, 1 - slot)

    if rider:
        per_batch = rider["tiles_per_batch"]
        _pool_tile(step % per_batch, per_batch, rider_in, rider_out, rider_bufs,
                   tp=rider["tp"], final=rider["final"], pending=rider["pending"])

    for p in range(cp):
        for cpy in copies(b, c, slot, p):
            cpy.wait()

    @pl.when(c == 0)
    def _():
        m_ref[...] = jnp.full(m_ref.shape, NEG, F32)
        l_ref[...] = jnp.zeros(l_ref.shape, F32)
        acc_ref[...] = jnp.zeros(acc_ref.shape, F32)
        carry_ref[...] = jnp.zeros(carry_ref.shape, F32)

    qm = qm_ref[...]
    cin = _cumsum_lanes_blocks(fbuf[slot].reshape(cp * N_HEADS, PAGE), PAGE).reshape(cp, N_HEADS, PAGE)
    carry = carry_ref[...]
    cs = []
    for p in range(cp):
        cs.append(cin[p] + carry)
        carry = carry + cin[p][:, PAGE - 1:PAGE]
    carry_ref[...] = carry
    kcat = jnp.concatenate([kbuf[slot, p] for p in range(cp)], axis=1).astype(BF16)
    t = _dot(qm, kcat) - jnp.concatenate(cs, axis=1)
    m_old = m_ref[...]
    m_new = jnp.maximum(m_old, jnp.max(t, axis=1, keepdims=True))
    alpha = jnp.exp(m_old - m_new)
    p = jnp.exp(t - m_new)
    l_new = alpha * l_ref[...] + jnp.sum(p, axis=1, keepdims=True)
    vcat = jnp.concatenate([vbuf[slot, p_] for p_ in range(cp)], axis=1).astype(BF16)
    acc_new = alpha * acc_ref[...] + _dot_nt(p.astype(BF16), vcat)
    m_ref[...] = m_new
    l_ref[...] = l_new
    acc_ref[...] = acc_new

    @pl.when(c == nchunk - 1)
    def _():
        s_n = jnp.sum(qm.astype(F32) * _bf16_round(kn_ref[...]), axis=1, keepdims=True)
        t_n = s_n - (carry + fn_ref[...])
        m_f = jnp.maximum(m_new, t_n)
        a = jnp.exp(m_new - m_f)
        p_n = jnp.exp(t_n - m_f)
        l_f = a * l_new + p_n
        acc = a * acc_new + _bf16_round(p_n) * _bf16_round(vn_ref[...])
        o_ref[...] = _head_diag(acc / l_f)


def _fox_sample(page_table, qm, kn, vn, fn, ck, cv, cf, cp, pool_layer=None):
    m, n_pages = page_table.shape
    nchunk = n_pages // cp
    per_b = lambda s1, s2: pl.BlockSpec((None, s1, s2), lambda b, c, pt: (b, 0, 0))
    any_spec = pl.BlockSpec(memory_space=pl.ANY)
    ins = [qm, kn, vn, fn, ck, cv, cf]
    in_specs = [per_b(N_HEADS, KV_WIDTH), per_b(1, KV_WIDTH), per_b(1, KV_WIDTH), per_b(N_HEADS, 1),
                any_spec, any_spec, any_spec]
    out_specs = [per_b(N_HEADS, HEAD_DIM)]
    out_shape = [jax.ShapeDtypeStruct((m, N_HEADS, HEAD_DIM), F32)]
    scratch = [pltpu.VMEM((2, cp, KV_WIDTH, PAGE), F32),
               pltpu.VMEM((2, cp, KV_WIDTH, PAGE), F32),
               pltpu.VMEM((2, cp, N_HEADS, PAGE), F32),
               pltpu.SemaphoreType.DMA((2, 3)),
               pltpu.VMEM((N_HEADS, 1), F32), pltpu.VMEM((N_HEADS, 1), F32),
               pltpu.VMEM((N_HEADS, KV_WIDTH), F32), pltpu.VMEM((N_HEADS, 1), F32)]
    assert len(scratch) == N_DECODE_SCRATCH
    rider = None
    if pool_layer is not None:
        *layer, batch, seq = pool_layer
        assert nchunk == 1 and (batch * seq) % m == 0
        tp = batch * seq // m
        per_batch = seq // tp
        assert seq % tp == 0 and tp % HALO == 0
        p_ins, p_in_specs, p_out_specs, p_scratch, final = _pool_prompt_operands(
            *layer, tp, lambda b, c, pt: b, lambda b, c, pt: b // per_batch, True)
        rider = dict(tp=tp, final=final, pending=layer[1] is not None, tiles_per_batch=per_batch)
        ins += p_ins
        in_specs += p_in_specs
        out_specs += p_out_specs
        out_shape += [jax.ShapeDtypeStruct((batch * seq, D_MODEL), F32),
                      jax.ShapeDtypeStruct((batch, HALO, D_MODEL), F32)]
        scratch += p_scratch
    outs = pl.pallas_call(
        functools.partial(_fox_sample_kernel, cp=cp, nchunk=nchunk, rider=rider),
        grid_spec=pltpu.PrefetchScalarGridSpec(
            num_scalar_prefetch=1,
            grid=(m, nchunk),
            in_specs=in_specs,
            out_specs=out_specs,
            scratch_shapes=scratch,
        ),
        out_shape=out_shape,
        compiler_params=pltpu.CompilerParams(dimension_semantics=("arbitrary", "arbitrary"),
                                             vmem_limit_bytes=VMEM_LIMIT_RIDER if rider else VMEM_LIMIT),
        name="fox_sample",
    )(page_table, *ins)
    return outs if rider else outs[0]


def _head_rows_q(q_rows):
    m = q_rows.shape[0]
    q4 = q_rows.reshape(m, N_KV, GROUP, HEAD_DIM)
    eye = jnp.eye(N_KV, dtype=q_rows.dtype)
    return (q4[:, :, :, None, :] * eye[None, :, None, :, None]).reshape(m, N_HEADS, KV_WIDTH)


def _tokens_last(x):
    lead = x.shape[:-3]
    n = len(lead)
    xt = jnp.transpose(x, tuple(range(n)) + (n + 1, n + 2, n))
    return xt.reshape(lead + (KV_WIDTH, x.shape[-3]))


def _tokens_first(xt):
    lead = xt.shape[:-2]
    n = len(lead)
    x4 = xt.reshape(lead + (N_KV, HEAD_DIM, xt.shape[-1]))
    return jnp.transpose(x4, tuple(range(n)) + (n + 2, n, n + 1))


def kernel(x_prompt, x_sample, state_pool, cache_win_k, cache_win_v, cache_fox_k, cache_fox_v,
           cache_fox_logf, page_table, norm_g, final_norm_g, rel_bias, pool_w_in, pool_mix,
           pool_scale, pool_w_out, swa_w_in, swa_sinks, swa_w_out, fox_w_in, fox_f_bias, fox_w_out):
    batch, seq, _ = x_prompt.shape
    db = x_sample.shape[0]
    depth = norm_g.shape[0]
    mp = batch * seq
    tm_p, tm_s = 512, db

    xp = x_prompt.reshape(mp, D_MODEL)
    xs = x_sample.reshape(db, D_MODEL)

    rb = rel_bias.astype(F32)
    dist_keys = WINDOW - np.arange(WINDOW)
    bias_keys = jnp.where((dist_keys < WINDOW)[None, :], rb[_t5_bucket_np(dist_keys)].T, NEG)
    bias0 = rb[0].reshape(N_HEADS, 1)
    fg = final_norm_g.reshape(1, D_MODEL)

    pool_p, pool_s = [], []
    wk_p, wv_p, wk_s, wv_s = [], [], [], []
    fk_p, fv_p, fl_p, fk_s, fv_s, fl_s = [], [], [], [], [], []
    def pool_layer_args(layer):
        jj = layer // 3
        fg_l = fg if layer == depth - 1 else None
        return (norm_g[layer].reshape(1, D_MODEL), pool_w_in[jj].astype(BF16), pool_mix[jj].astype(BF16),
                pool_scale[jj].reshape(1, D_MODEL), pool_w_out[jj].astype(BF16), fg_l)

    prompt_done = None
    pending = None
    for i in range(depth):
        kind, j = i % 3, i // 3
        g = norm_g[i].reshape(1, D_MODEL)
        if kind == 0:
            g, w_in, mix, scale, w_out, fg_l = pool_layer_args(i)
            if prompt_done is None:
                xp, u_tail = _pool_prompt(xp, pending, g, w_in, mix, scale, w_out, fg_l, batch, seq, tm_p)
                pending = None
            else:
                xp, u_tail = prompt_done
                prompt_done = None
            pool_p.append(u_tail[:, HALO - POOL_STATE:])
            xs, u_s = _pool_sample(xs, jnp.transpose(state_pool[j], (1, 0, 2)), g, w_in, mix, scale, w_out, fg_l)
            pool_s.append(jnp.concatenate([state_pool[j][:, 1:], u_s[:, None, :]], axis=1))
        elif kind == 1:
            w_in = swa_w_in[j].astype(BF16)
            w_out = swa_w_out[j].astype(BF16)
            sinks = swa_sinks[j].astype(F32)
            qt_p, kh_p, kt_p, vt_p, vtb_p, gate_p = _swa_proj(xp, g, w_in, tm_p, batch, seq, True)
            xp = _swa_prompt(qt_p, kh_p, vtb_p, gate_p, w_out, xp, rb, sinks, batch, seq, tm_p)
            wk_p.append(_tokens_first(kt_p[:, :, seq - WINDOW:]))
            wv_p.append(_tokens_first(vt_p[:, :, seq - WINDOW:]))
            q_s, k_s, v_s, kt_s, vt_s, gate_s = _swa_proj(xs, g, w_in, tm_s, 1, db, False)
            o_s, wk, wv = _swa_sample(_head_rows_q(q_s), _tokens_last(cache_win_k[j]),
                                      _tokens_last(cache_win_v[j]), k_s[:, None, :], v_s[:, None, :],
                                      kt_s[0], vt_s[0], bias_keys, bias0, sinks.reshape(N_HEADS, 1),
                                      SWA_ROWS_PER_STEP)
            xs = _out_proj(o_s.reshape(db, D_MODEL), gate_s, w_out, xs, tm_s)
            wk_s.append(_tokens_first(wk))
            wv_s.append(_tokens_first(wv))
        else:
            w_full = fox_w_in[j]
            nqkv = D_MODEL + 2 * KV_WIDTH
            w_in = jnp.concatenate([w_full[:, :nqkv], w_full[:, nqkv + N_HEADS:]], axis=1).astype(BF16)
            wf = jnp.pad(w_full[:, nqkv:nqkv + N_HEADS], ((0, 0), (0, LANES - N_HEADS))).astype(BF16)
            fb = jnp.pad(fox_f_bias[j].astype(F32), (0, LANES - N_HEADS)).reshape(1, LANES)
            w_out = fox_w_out[j].astype(BF16)
            qt, kt_p, vt_p, kaug, vtb, logft_p, gate_p = _fox_proj(xp, g, w_in, wf, fb, FOX_TILE, batch, seq, True)
            kaug = _fox_decay(logft_p, kaug, FOX_TILE)
            fk_p.append(_tokens_first(kt_p))
            fv_p.append(_tokens_first(vt_p))
            fl_p.append(jnp.transpose(logft_p, (0, 2, 1)))
            q_s, k_s, v_s, kt_s, vt_s, logf_s, logft_s, gate_s = _fox_proj(
                xs, g, w_in, wf, fb, tm_s, 1, db, False)
            decode_args = (page_table, _head_rows_q(q_s), k_s[:, None, :], v_s[:, None, :], logf_s[:, :, None],
                           _tokens_last(cache_fox_k[j]), _tokens_last(cache_fox_v[j]),
                           jnp.transpose(cache_fox_logf[j], (0, 2, 1)))
            next_is_pool = i + 1 < depth and (i + 1) % 3 == 0
            if _fox_prompt_fits_decode(batch, seq, FOX_TILE, db):
                o_p, o_s = _fox_prompt(qt, kaug, vtb, batch, seq, FOX_TILE, decode=decode_args)
                if next_is_pool:
                    pending = (o_p, gate_p, w_out)
                else:
                    xp = _out_proj(o_p, gate_p, w_out, xp, tm_p)
            else:
                o_p = _fox_prompt(qt, kaug, vtb, batch, seq, FOX_TILE)
                if next_is_pool:
                    o_s, xp, u_tail = _fox_sample(
                        *decode_args, FOX_PAGES_PER_STEP,
                        pool_layer=(xp, (o_p, gate_p, w_out), *pool_layer_args(i + 1), batch, seq))
                    prompt_done = (xp, u_tail)
                else:
                    xp = _out_proj(o_p, gate_p, w_out, xp, tm_p)
                    o_s = _fox_sample(*decode_args, FOX_PAGES_PER_STEP)
            xs = _out_proj(o_s.reshape(db, D_MODEL), gate_s, w_out, xs, tm_s)
            fk_s.append(_tokens_first(kt_s[0])[:, None])
            fv_s.append(_tokens_first(vt_s[0])[:, None])
            fl_s.append(logft_s[0].T[:, None, :])

    if (depth - 1) % 3 != 0:
        xp, xs = _final_norm(xp, fg, tm_p), _final_norm(xs, fg, tm_s)
    y_prompt = xp.reshape(batch, seq, D_MODEL)
    y_sample = xs.reshape(db, 1, D_MODEL)
    return (y_prompt, y_sample, jnp.stack(pool_p), jnp.stack(pool_s), jnp.stack(wk_p), jnp.stack(wv_p),
            jnp.stack(wk_s), jnp.stack(wv_s), jnp.stack(fk_p), jnp.stack(fv_p), jnp.stack(fl_p),
            jnp.stack(fk_s), jnp.stack(fv_s), jnp.stack(fl_s))
```

```python
import functools
import math

import numpy as np
import jax
import jax.numpy as jnp
from jax import lax
from jax.experimental import pallas as pl
from jax.experimental.pallas import tpu as pltpu

D_MODEL = 1024
HEAD_DIM = 64
N_HEADS = 16
N_KV = 4
GROUP = 4
KV_WIDTH = N_KV * HEAD_DIM
POOL_WINDOWS = (2, 4, 8, 16)
POOL_GROUP = 256
POOL_STATE = 15
WINDOW = 128
N_BUCKETS = 32
MAX_DISTANCE = 128
SCALE = HEAD_DIM ** -0.5
EPS = 1e-6
NEG = -1e30
PAGE = 128
LANES = 128

BF16 = jnp.bfloat16
F32 = jnp.float32

VMEM_LIMIT = 56 * 1024 * 1024
VMEM_LIMIT_RIDER = 62 * 1024 * 1024


def _cparams(sem):
    return pltpu.CompilerParams(dimension_semantics=sem, vmem_limit_bytes=VMEM_LIMIT)


def _rms_bf16(x, g):
    ms = jnp.mean(x * x, axis=-1, keepdims=True)
    return (x * lax.rsqrt(ms + EPS) * g).astype(BF16)


def _silu(x):
    return x * jax.nn.sigmoid(x)


def _dot(a, b):
    return jnp.dot(a, b, preferred_element_type=F32)


def _dot_nt(a, b):
    return lax.dot_general(a, b, (((1,), (1,)), ((), ())), preferred_element_type=F32)


def _bf16_round(x):
    return x.astype(BF16).astype(F32)


def _full(shape):
    n = len(shape)
    return pl.BlockSpec(shape, lambda *_: (0,) * n)


LOG2E = 1.4426950408889634


def _swa_proj_kernel(x_ref, g_ref, w_ref, *out_refs, prompt):
    if prompt:
        qt_ref, kh_ref, kt_ref, vt_ref, vtb_ref, gate_ref = out_refs
    else:
        q_ref, k_ref, v_ref, kt_ref, vt_ref, gate_ref = out_refs
    hb = _rms_bf16(x_ref[...], g_ref[...])
    q = _dot(hb, w_ref[:, :D_MODEL])
    k = _dot(hb, w_ref[:, D_MODEL:D_MODEL + KV_WIDTH])
    v = _dot(hb, w_ref[:, D_MODEL + KV_WIDTH:D_MODEL + 2 * KV_WIDTH])
    vt = v.T
    kt_ref[...] = k.T
    vt_ref[...] = vt
    if prompt:
        tm = q.shape[0]
        qt = (q * (SCALE * LOG2E)).T.astype(BF16)
        for hd in range(N_HEADS):
            h, g = divmod(hd, GROUP)
            for jb in range(tm // WINDOW):
                qt_ref[h, jb, :, g * WINDOW:(g + 1) * WINDOW] = (
                    qt[hd * HEAD_DIM:(hd + 1) * HEAD_DIM, jb * WINDOW:(jb + 1) * WINDOW])
        for h in range(N_KV):
            kh_ref[h] = k[:, h * HEAD_DIM:(h + 1) * HEAD_DIM].astype(BF16)
            ones_row = lax.broadcasted_iota(jnp.int32, (V_ROWS - HEAD_DIM, tm), 0) == 0
            vth = jnp.concatenate([vt[h * HEAD_DIM:(h + 1) * HEAD_DIM, :],
                                   jnp.where(ones_row, 1.0, 0.0)], axis=0).astype(BF16)
            for jb in range(tm // WINDOW):
                vtb_ref[h, jb] = vth[:, jb * WINDOW:(jb + 1) * WINDOW]
    else:
        q_ref[...] = (q * SCALE).astype(BF16)
        k_ref[...] = k
        v_ref[...] = v
    gate_ref[...] = _dot(hb, w_ref[:, D_MODEL + 2 * KV_WIDTH:])


def _swa_proj(x, g, w, tm, batch, seq, prompt):
    m = x.shape[0]
    ns = seq // tm
    nb = tm // WINDOW
    row = lambda i: (i, 0)
    sds = jax.ShapeDtypeStruct
    tcol = pl.BlockSpec((None, KV_WIDTH, tm), lambda i: (i // ns, 0, i % ns))
    blocks = lambda rows, last: pl.BlockSpec((None, N_KV, nb, rows, last), lambda i: (i // ns, 0, i % ns, 0, 0))
    kv_t_shape = [sds((batch, KV_WIDTH, seq), F32)] * 2
    if prompt:
        out_specs = [blocks(HEAD_DIM, GROUP * WINDOW), pl.BlockSpec((N_KV, tm, HEAD_DIM), lambda i: (0, i, 0)),
                     tcol, tcol, blocks(V_ROWS, WINDOW)]
        out_shape = ([sds((batch, N_KV, seq // WINDOW, HEAD_DIM, GROUP * WINDOW), BF16),
                      sds((N_KV, m, HEAD_DIM), BF16)] + kv_t_shape
                     + [sds((batch, N_KV, seq // WINDOW, V_ROWS, WINDOW), BF16)])
    else:
        out_specs = [pl.BlockSpec((tm, D_MODEL), row), pl.BlockSpec((tm, KV_WIDTH), row),
                     pl.BlockSpec((tm, KV_WIDTH), row), tcol, tcol]
        out_shape = [sds((m, D_MODEL), BF16), sds((m, KV_WIDTH), F32), sds((m, KV_WIDTH), F32)] + kv_t_shape
    out_specs.append(pl.BlockSpec((tm, D_MODEL), row))
    out_shape.append(sds((m, D_MODEL), F32))
    return pl.pallas_call(
        functools.partial(_swa_proj_kernel, prompt=prompt),
        grid=(m // tm,),
        in_specs=[pl.BlockSpec((tm, D_MODEL), row), _full((1, D_MODEL)), _full(w.shape)],
        out_specs=out_specs,
        out_shape=out_shape,
        compiler_params=_cparams(("parallel",)),
        name="swa_proj",
    )(x, g, w)


def _log_sigmoid(x):
    return -(jnp.maximum(-x, 0.0) + jnp.log1p(jnp.exp(-jnp.abs(x))))


AUG = 128
N_PIECES = 3
SWA_ROWS_PER_STEP = 8
V_ROWS = HEAD_DIM + 16
FOX_PAGES_PER_STEP = 64
FOX_TILE = 512
FOX_KEYS = 512
FOX_QUERIES = 256
GROUP_SHIFT = GROUP.bit_length() - 1


def _fox_proj_kernel(x_ref, g_ref, w_ref, wf_ref, fb_ref, *out_refs, prompt):
    if prompt:
        qt_ref, kt_ref, vt_ref, kaug_ref, vtb_ref, logft_ref, gate_ref = out_refs
    else:
        q_ref, k_ref, v_ref, kt_ref, vt_ref, logf_ref, logft_ref, gate_ref = out_refs
    hb = _rms_bf16(x_ref[...], g_ref[...])
    q = _dot(hb, w_ref[:, :D_MODEL])
    k = _dot(hb, w_ref[:, D_MODEL:D_MODEL + KV_WIDTH])
    v = _dot(hb, w_ref[:, D_MODEL + KV_WIDTH:D_MODEL + 2 * KV_WIDTH])
    vt = v.T
    kt_ref[...] = k.T
    vt_ref[...] = vt
    logf = _log_sigmoid(_dot(hb, wf_ref[...]) + fb_ref[...])
    logft_ref[...] = logf.T[:N_HEADS, :]
    if prompt:
        tm = q.shape[0]
        qt = (q * (SCALE * LOG2E)).T.astype(BF16)
        r = lax.broadcasted_iota(jnp.int32, (AUG - HEAD_DIM, tm), 0)
        for hd in range(N_HEADS):
            h, g = divmod(hd, GROUP)
            cols = slice(g * tm, (g + 1) * tm)
            qt_ref[h, 0:HEAD_DIM, cols] = qt[hd * HEAD_DIM:(hd + 1) * HEAD_DIM, :]
            pick = jnp.logical_and(r < N_PIECES * GROUP, (r & (GROUP - 1)) == g)
            qt_ref[h, HEAD_DIM:AUG, cols] = jnp.where(pick, -1.0, 0.0).astype(BF16)
        zeros = jnp.zeros((tm, AUG - HEAD_DIM), F32)
        for h in range(N_KV):
            kaug_ref[h] = jnp.concatenate([k[:, h * HEAD_DIM:(h + 1) * HEAD_DIM], zeros], axis=1).astype(BF16)
            ones_row = lax.broadcasted_iota(jnp.int32, (V_ROWS - HEAD_DIM, tm), 0) == 0
            vth = jnp.concatenate([vt[h * HEAD_DIM:(h + 1) * HEAD_DIM, :],
                                   jnp.where(ones_row, 1.0, 0.0)], axis=0).astype(BF16)
            for kb in range(tm // FOX_KEYS):
                vtb_ref[h, kb] = vth[:, kb * FOX_KEYS:(kb + 1) * FOX_KEYS]
    else:
        q_ref[...] = (q * SCALE).astype(BF16)
        k_ref[...] = k
        v_ref[...] = v
        logf_ref[...] = logf[:, :N_HEADS]
    gate_ref[...] = _dot(hb, w_ref[:, D_MODEL + 2 * KV_WIDTH:])


def _fox_proj(x, g, w, wf, fb, tm, batch, seq, prompt):
    m = x.shape[0]
    ns = seq // tm
    row = lambda i: (i, 0)
    tcol = lambda r: pl.BlockSpec((None, r, tm), lambda i: (i // ns, 0, i % ns))
    sds = jax.ShapeDtypeStruct
    kv_t = [tcol(KV_WIDTH), tcol(KV_WIDTH)]
    kv_t_shape = [sds((batch, KV_WIDTH, seq), F32)] * 2
    if prompt:
        out_specs = ([pl.BlockSpec((None, N_KV, None, AUG, GROUP * tm), lambda i: (i // ns, 0, i % ns, 0, 0))] + kv_t
                     + [pl.BlockSpec((N_KV, tm, AUG), lambda i: (0, i, 0)),
                        pl.BlockSpec((None, N_KV, tm // FOX_KEYS, V_ROWS, FOX_KEYS),
                                     lambda i: (i // ns, 0, i % ns, 0, 0)),
                        tcol(N_HEADS)])
        out_shape = ([sds((batch, N_KV, ns, AUG, GROUP * tm), BF16)] + kv_t_shape
                     + [sds((N_KV, m, AUG), BF16),
                        sds((batch, N_KV, seq // FOX_KEYS, V_ROWS, FOX_KEYS), BF16),
                        sds((batch, N_HEADS, seq), F32)])
    else:
        out_specs = ([pl.BlockSpec((tm, D_MODEL), row), pl.BlockSpec((tm, KV_WIDTH), row),
                      pl.BlockSpec((tm, KV_WIDTH), row)] + kv_t
                     + [pl.BlockSpec((tm, N_HEADS), row), tcol(N_HEADS)])
        out_shape = ([sds((m, D_MODEL), BF16), sds((m, KV_WIDTH), F32), sds((m, KV_WIDTH), F32)]
                     + kv_t_shape + [sds((m, N_HEADS), F32), sds((batch, N_HEADS, seq), F32)])
    out_specs.append(pl.BlockSpec((tm, D_MODEL), row))
    out_shape.append(sds((m, D_MODEL), F32))
    return pl.pallas_call(
        functools.partial(_fox_proj_kernel, prompt=prompt),
        grid=(m // tm,),
        in_specs=[pl.BlockSpec((tm, D_MODEL), row), _full((1, D_MODEL)), _full(w.shape),
                  _full(wf.shape), _full(fb.shape)],
        out_specs=out_specs,
        out_shape=out_shape,
        compiler_params=_cparams(("parallel",)),
        name="fox_proj",
    )(x, g, w, wf, fb)


def _out_proj_kernel(*refs, gated):
    if gated:
        o_ref, gate_ref, w_ref, x_ref, y_ref = refs
        ob = (o_ref[...] * _silu(gate_ref[...])).astype(BF16)
    else:
        o_ref, w_ref, x_ref, y_ref = refs
        ob = o_ref[...]
    y_ref[...] = x_ref[...] + _dot(ob, w_ref[...])


def _out_proj(o, gate, w, x, tm):
    m = x.shape[0]
    row = lambda i: (i, 0)
    tile = pl.BlockSpec((tm, D_MODEL), row)
    gated = gate is not None
    ins = [o, gate, w, x] if gated else [o, w, x]
    in_specs = [tile, tile, _full(w.shape), tile] if gated else [tile, _full(w.shape), tile]
    return pl.pallas_call(
        functools.partial(_out_proj_kernel, gated=gated),
        grid=(m // tm,),
        in_specs=in_specs,
        out_specs=tile,
        out_shape=jax.ShapeDtypeStruct((m, D_MODEL), F32),
        compiler_params=_cparams(("parallel",)),
        name="out_proj",
    )(*ins)


def _final_norm_kernel(x_ref, g_ref, y_ref):
    x = x_ref[...]
    ms = jnp.mean(x * x, axis=-1, keepdims=True)
    y_ref[...] = x * lax.rsqrt(ms + EPS) * g_ref[...]


def _final_norm(x, g, tm):
    m = x.shape[0]
    row = lambda i: (i, 0)
    return pl.pallas_call(
        _final_norm_kernel,
        grid=(m // tm,),
        in_specs=[pl.BlockSpec((tm, D_MODEL), row), _full((1, D_MODEL))],
        out_specs=pl.BlockSpec((tm, D_MODEL), row),
        out_shape=jax.ShapeDtypeStruct((m, D_MODEL), F32),
        compiler_params=_cparams(("parallel",)),
        name="final_norm",
    )(x, g)


HALO = 16
PAD = 8
assert all(w == 2 << g for g, w in enumerate(POOL_WINDOWS)) and HALO >= max(POOL_WINDOWS) and PAD >= HALO // 2


def _pool_layer_tail(x, u, gate, pooled_groups, mix_ref, scale_ref, wout_ref, fg_ref, y_ref):
    pieces = []
    for g in range(len(POOL_WINDOWS)):
        c0 = g * POOL_GROUP
        p = (pooled_groups[g] - u[:, c0:c0 + POOL_GROUP]).astype(BF16)
        pieces.append(_dot(p, mix_ref[g]))
    pm = jnp.concatenate(pieces, axis=1)
    o = (pm * scale_ref[...] * _silu(gate)).astype(BF16)
    y = x + _dot(o, wout_ref[...])
    if fg_ref is not None:
        ms = jnp.mean(y * y, axis=-1, keepdims=True)
        y = y * lax.rsqrt(ms + EPS) * fg_ref[...]
    y_ref[...] = y


def _n_pool_inputs(final, pending):
    return 6 + (3 if pending else 0) + (1 if final else 0)


def _pool_prompt_kernel(*refs, tp, final, pending):
    n_in = _n_pool_inputs(final, pending)
    _pool_tile(pl.program_id(1), pl.num_programs(1), refs[:n_in], refs[n_in:n_in + 2], refs[n_in + 2:],
               tp=tp, final=final, pending=pending)


def _pool_tile(i, n_tiles, in_refs, out_refs, bufs, *, tp, final, pending):
    refs = list(in_refs)
    x_ref = refs.pop(0)
    prev = [refs.pop(0) for _ in range(3)] if pending else None
    g_ref, win_ref, mix_ref, scale_ref, wout_ref = [refs.pop(0) for _ in range(5)]
    fg_ref = refs.pop(0) if final else None
    y_ref, tail_ref = out_refs
    hist = slice(PAD, PAD + HALO)
    ext = slice(PAD, PAD + HALO + tp)
    tile = slice(PAD + HALO, PAD + HALO + tp)

    def back(rows, k):
        return slice(rows.start - k, rows.stop - k)

    @pl.when(i == 0)
    def _():
        for buf in bufs:
            buf[0:PAD + HALO, :] = jnp.zeros((PAD + HALO, buf.shape[1]), F32)

    x = x_ref[...]
    if pending:
        po_ref, pgate_ref, pw_ref = prev
        x = x + _dot((po_ref[...] * _silu(pgate_ref[...])).astype(BF16), pw_ref[...])
    hb = _rms_bf16(x, g_ref[...])
    u = _dot(hb, win_ref[:, :D_MODEL])
    gate = _dot(hb, win_ref[:, D_MODEL:])
    bufs[0][tile, :] = u
    pos = i * tp + lax.broadcasted_iota(jnp.int32, (tp, 1), 0)
    pooled = []
    for g, w in enumerate(POOL_WINDOWS):
        src, span = bufs[g], 1 << g
        if g + 1 < len(bufs):
            nxt = bufs[g + 1]
            nxt[ext, :] = src[ext, POOL_GROUP:] + src[back(ext, span), POOL_GROUP:]
        acc = src[tile, 0:POOL_GROUP] + src[back(tile, span), 0:POOL_GROUP]
        inv_cnt = 1.0 / jnp.minimum(pos + 1, w).astype(F32)
        pooled.append(acc * inv_cnt)
    _pool_layer_tail(x, u, gate, pooled, mix_ref, scale_ref, wout_ref, fg_ref, y_ref)
    bufs[0][hist, :] = u[tp - HALO:tp, :]

    @pl.when(i == n_tiles - 1)
    def _():
        tail_ref[...] = u[tp - HALO:tp, :]


def _pool_prompt_operands(x, pending, g, w_in, mix, scale, w_out, fg, tp, row_tile, batch_of, single_buffer):
    final = fg is not None
    tile = pl.BlockSpec((tp, D_MODEL), lambda *idx: (row_tile(*idx), 0))
    whole = (lambda a: pl.BlockSpec(a.shape, lambda *_: (0,) * a.ndim, pipeline_mode=pl.Buffered(1))
             if single_buffer else _full(a.shape))
    ins = [x] + (list(pending) if pending else []) + [g, w_in, mix, scale, w_out] + ([fg] if final else [])
    in_specs = [tile] + ([tile, tile, whole(pending[2])] if pending else [])
    in_specs += [whole(a) for a in ins[len(in_specs):]]
    out_specs = [tile, pl.BlockSpec((None, HALO, D_MODEL), lambda *idx: (batch_of(*idx), 0, 0))]
    scratch = [pltpu.VMEM((PAD + HALO + tp, D_MODEL - g * POOL_GROUP), F32) for g in range(len(POOL_WINDOWS))]
    return ins, in_specs, out_specs, scratch, final


def _pool_prompt(x, pending, g, w_in, mix, scale, w_out, fg, batch, seq, tp):
    ns = seq // tp
    ins, in_specs, out_specs, scratch, final = _pool_prompt_operands(
        x, pending, g, w_in, mix, scale, w_out, fg, tp, lambda b, i: b * ns + i, lambda b, i: b, False)
    return pl.pallas_call(
        functools.partial(_pool_prompt_kernel, tp=tp, final=final, pending=bool(pending)),
        grid=(batch, ns),
        in_specs=in_specs,
        out_specs=out_specs,
        out_shape=[jax.ShapeDtypeStruct((batch * seq, D_MODEL), F32),
                   jax.ShapeDtypeStruct((batch, HALO, D_MODEL), F32)],
        scratch_shapes=scratch,
        compiler_params=_cparams(("parallel", "arbitrary")),
        name="pool_prompt",
    )(*ins)


def _pool_sample_kernel(*refs, final):
    if final:
        x_ref, st_ref, g_ref, win_ref, mix_ref, scale_ref, wout_ref, fg_ref, y_ref, st_out_ref = refs
    else:
        x_ref, st_ref, g_ref, win_ref, mix_ref, scale_ref, wout_ref, y_ref, st_out_ref = refs
        fg_ref = None
    x = x_ref[...]
    hb = _rms_bf16(x, g_ref[...])
    u = _dot(hb, win_ref[:, :D_MODEL])
    gate = _dot(hb, win_ref[:, D_MODEL:])
    for r in range(POOL_STATE - 1):
        st_out_ref[r] = st_ref[r + 1]
    st_out_ref[POOL_STATE - 1] = u
    pooled = []
    for g, w in enumerate(POOL_WINDOWS):
        c0 = g * POOL_GROUP
        acc = u[:, c0:c0 + POOL_GROUP]
        for k in range(1, w):
            acc = acc + st_ref[POOL_STATE - k, :, c0:c0 + POOL_GROUP]
        pooled.append(acc / float(w))
    _pool_layer_tail(x, u, gate, pooled, mix_ref, scale_ref, wout_ref, fg_ref, y_ref)


def _pool_sample(x, state_t, g, w_in, mix, scale, w_out, fg):
    m = x.shape[0]
    final = fg is not None
    ins = [x, state_t, g, w_in, mix, scale, w_out] + ([fg] if final else [])
    return pl.pallas_call(
        functools.partial(_pool_sample_kernel, final=final),
        grid=(1,),
        in_specs=[_full(a.shape) for a in ins],
        out_specs=[_full((m, D_MODEL)), _full(state_t.shape)],
        out_shape=[jax.ShapeDtypeStruct((m, D_MODEL), F32), jax.ShapeDtypeStruct(state_t.shape, F32)],
        compiler_params=_cparams(("arbitrary",)),
        name="pool_sample",
    )(*ins)


def _t5_bucket_np(dist):
    n = np.maximum(dist, 0)
    max_exact = N_BUCKETS // 2
    nf = np.maximum(n, 1).astype(np.float32)
    large = max_exact + (np.log(nf / max_exact) / math.log(MAX_DISTANCE / max_exact)
                         * (N_BUCKETS - max_exact)).astype(np.int32)
    large = np.minimum(large, N_BUCKETS - 1)
    return np.where(n < max_exact, n, large)


def _swa_prompt_kernel(sink_ref, rb_ref, qt_ref, kp_ref, kc_ref, vp_ref, vc_ref, bucket_ref, gate_ref,
                       wout_ref, x_ref, y_ref, bias_ref, o_buf, *, nb):
    i = pl.program_id(1)
    cols = GROUP * WINDOW

    @pl.when(i == 0)
    def _():
        bucket = bucket_ref[...]
        hits = [bucket == bk for bk in range(N_BUCKETS)]
        for hd in range(N_HEADS):
            h, g = divmod(hd, GROUP)
            b = jnp.full((2 * WINDOW, WINDOW), NEG, F32)
            for bk in range(N_BUCKETS):
                b = jnp.where(hits[bk], rb_ref[bk, hd] * LOG2E, b)
            bias_ref[h, :, g * WINDOW:(g + 1) * WINDOW] = b

    key_row = lax.broadcasted_iota(jnp.int32, (2 * WINDOW, cols), 0)
    no_prev = jnp.logical_and(i == 0, key_row < WINDOW)
    sinks = [jnp.concatenate([jnp.full((1, WINDOW), sink_ref[h * GROUP + g] * LOG2E, F32) for g in range(GROUP)],
                             axis=1) for h in range(N_KV)]

    def logits(jb, h):
        if jb == 0:
            kband = jnp.concatenate([kp_ref[h], kc_ref[h, 0:WINDOW, :]], axis=0)
        else:
            kband = kc_ref[h, (jb - 1) * WINDOW:(jb + 1) * WINDOW, :]
        s = _dot(kband, qt_ref[h, jb]) + bias_ref[h]
        return jnp.where(no_prev, NEG, s) if jb == 0 else s

    def attend(jb, h, s):
        vprev = vp_ref[h] if jb == 0 else vc_ref[h, jb - 1]
        vband = jnp.concatenate([vprev, vc_ref[h, jb]], axis=1)
        m = jnp.maximum(jnp.max(s, axis=0, keepdims=True), sinks[h])
        pv = _dot(vband, jnp.exp2(s - m).astype(BF16))
        denom = pv[HEAD_DIM:HEAD_DIM + 1, :] + jnp.exp2(sinks[h] - m)
        ot = pv[0:HEAD_DIM, :] * (1.0 / denom)
        o_heads = jnp.concatenate([ot[:, g * WINDOW:(g + 1) * WINDOW] for g in range(GROUP)], axis=0)
        o_buf[jb * WINDOW:(jb + 1) * WINDOW, h * GROUP * HEAD_DIM:(h + 1) * GROUP * HEAD_DIM] = o_heads.T

    units = [(jb, h) for jb in range(nb) for h in range(N_KV)]
    s = logits(*units[0])
    for u, unit in enumerate(units):
        s_next = logits(*units[u + 1]) if u + 1 < len(units) else None
        attend(*unit, s)
        s = s_next
    ob = (o_buf[...] * _silu(gate_ref[...])).astype(BF16)
    y_ref[...] = x_ref[...] + _dot(ob, wout_ref[...])


def _swa_prompt(qt, kh, vtb, gate, w_out, x, rel_bias, sinks, batch, seq, tm):
    ns = seq // tm
    nb = tm // WINDOW
    dist = np.arange(WINDOW)[None, :] + WINDOW - np.arange(2 * WINDOW)[:, None]
    bucket = np.where((dist >= 0) & (dist < WINDOW), _t5_bucket_np(dist), -1).astype(np.int32)
    tile = pl.BlockSpec((tm, D_MODEL), lambda b, i: (b * ns + i, 0))
    smem = pl.BlockSpec(memory_space=pltpu.SMEM)
    prev_blk = lambda i: jnp.maximum(i * nb - 1, 0)
    return pl.pallas_call(
        functools.partial(_swa_prompt_kernel, nb=nb),
        grid=(batch, ns),
        in_specs=[smem, smem,
                  pl.BlockSpec((None, N_KV, nb, HEAD_DIM, GROUP * WINDOW), lambda b, i: (b, 0, i, 0, 0)),
                  pl.BlockSpec((N_KV, WINDOW, HEAD_DIM), lambda b, i: (0, b * ns * nb + prev_blk(i), 0)),
                  pl.BlockSpec((N_KV, tm, HEAD_DIM), lambda b, i: (0, b * ns + i, 0)),
                  pl.BlockSpec((None, N_KV, None, V_ROWS, WINDOW), lambda b, i: (b, 0, prev_blk(i), 0, 0)),
                  pl.BlockSpec((None, N_KV, nb, V_ROWS, WINDOW), lambda b, i: (b, 0, i, 0, 0)),
                  _full(bucket.shape), tile, _full(w_out.shape), tile],
        out_specs=tile,
        out_shape=jax.ShapeDtypeStruct((batch * seq, D_MODEL), F32),
        scratch_shapes=[pltpu.VMEM((N_KV, 2 * WINDOW, GROUP * WINDOW), F32),
                        pltpu.VMEM((tm, D_MODEL), F32)],
        compiler_params=_cparams(("parallel", "arbitrary")),
        name="swa_prompt",
    )(sinks, rel_bias, qt, kh, kh, vtb, vtb, jnp.asarray(bucket), gate, w_out, x)


def _head_diag(o_full):
    out = jnp.zeros((N_HEADS, HEAD_DIM), F32)
    row_kv = lax.broadcasted_iota(jnp.int32, (N_HEADS, HEAD_DIM), 0) // GROUP
    for h in range(N_KV):
        out = out + jnp.where(row_kv == h, o_full[:, h * HEAD_DIM:(h + 1) * HEAD_DIM], 0.0)
    return out


def _swa_sample_kernel(qm_ref, kc_ref, vc_ref, kn_ref, vn_ref, knt_ref, vnt_ref, bias_ref, bias0_ref,
                       sink_ref, o_ref, wk_ref, wv_ref, *, bt):
    i = pl.program_id(0)
    lane = lax.broadcasted_iota(jnp.int32, (KV_WIDTH, WINDOW), 1)
    sink = sink_ref[...]
    for e in range(bt):
        b = i * bt + e
        kc = kc_ref[e]
        vc = vc_ref[e]
        qm = qm_ref[e]
        s = _dot(qm, kc.astype(BF16)) + bias_ref[...]
        s_n = jnp.sum(qm.astype(F32) * _bf16_round(kn_ref[e]), axis=1, keepdims=True) + bias0_ref[...]
        m = jnp.maximum(jnp.maximum(jnp.max(s, axis=1, keepdims=True), s_n), sink)
        p = jnp.exp(s - m)
        p_n = jnp.exp(s_n - m)
        denom = jnp.sum(p, axis=1, keepdims=True) + p_n + jnp.exp(sink - m)
        o_full = _dot_nt((p / denom).astype(BF16), vc.astype(BF16))
        o_full = o_full + _bf16_round(p_n / denom) * _bf16_round(vn_ref[e])
        o_ref[e] = _head_diag(o_full)
        kcol = jnp.sum(jnp.where(lane == b, knt_ref[...], 0.0), axis=1, keepdims=True)
        vcol = jnp.sum(jnp.where(lane == b, vnt_ref[...], 0.0), axis=1, keepdims=True)
        wk_ref[e] = jnp.where(lane == WINDOW - 1, kcol, pltpu.roll(kc, WINDOW - 1, 1))
        wv_ref[e] = jnp.where(lane == WINDOW - 1, vcol, pltpu.roll(vc, WINDOW - 1, 1))


def _swa_sample(qm, kc, vc, kn, vn, knt, vnt, bias_keys, bias0, sinks, bt):
    m = qm.shape[0]
    blk3 = lambda s1, s2: pl.BlockSpec((bt, s1, s2), lambda i: (i, 0, 0))
    cache = jax.ShapeDtypeStruct((m, KV_WIDTH, WINDOW), F32)
    return pl.pallas_call(
        functools.partial(_swa_sample_kernel, bt=bt),
        grid=(m // bt,),
        in_specs=[blk3(N_HEADS, KV_WIDTH), blk3(KV_WIDTH, WINDOW), blk3(KV_WIDTH, WINDOW),
                  blk3(1, KV_WIDTH), blk3(1, KV_WIDTH), _full(knt.shape), _full(vnt.shape),
                  _full(bias_keys.shape), _full(bias0.shape), _full(sinks.shape)],
        out_specs=[blk3(N_HEADS, HEAD_DIM), blk3(KV_WIDTH, WINDOW), blk3(KV_WIDTH, WINDOW)],
        out_shape=[jax.ShapeDtypeStruct((m, N_HEADS, HEAD_DIM), F32), cache, cache],
        compiler_params=_cparams(("parallel",)),
        name="swa_sample",
    )(qm, kc, vc, kn, vn, knt, vnt, bias_keys, bias0, sinks)


def _split3(x):
    hi = x.astype(BF16)
    r = x - hi.astype(F32)
    mid = r.astype(BF16)
    lo = (r - mid.astype(F32)).astype(BF16)
    return hi, mid, lo


def _cumsum_lanes_blocks(x, blk):
    n = x.shape[0]
    hi, mid, lo = _split3(x)
    r_i = lax.broadcasted_iota(jnp.int32, (blk, blk), 0)
    c_i = lax.broadcasted_iota(jnp.int32, (blk, blk), 1)
    upper = jnp.where(r_i <= c_i, 1.0, 0.0).astype(BF16)
    r = _dot(jnp.concatenate([hi, mid, lo], axis=0), upper)
    return r[0:n] + r[n:2 * n] + r[2 * n:]


def _fox_decay_kernel(x_ref, kin_ref, kout_ref, carry, *, blk):
    i = pl.program_id(1)

    @pl.when(i == 0)
    def _():
        carry[...] = jnp.zeros_like(carry)

    c = _cumsum_lanes_blocks(x_ref[...], blk) + carry[:, 0:1]
    carry[...] = jnp.broadcast_to(c[:, blk - 1:blk], carry.shape)
    hi, mid, lo = _split3(c * LOG2E)
    pieces = jnp.concatenate([hi, mid, lo], axis=0).astype(F32)
    pad = jnp.zeros((LANES - N_PIECES * N_HEADS, blk), F32)
    pieces_t = jnp.concatenate([pieces, pad], axis=0).T.astype(BF16)
    src = lax.broadcasted_iota(jnp.int32, (LANES, AUG), 0)
    dst = lax.broadcasted_iota(jnp.int32, (LANES, AUG), 1) - HEAD_DIM
    in_aug = jnp.logical_and(dst >= 0, dst < N_PIECES * GROUP)
    for h in range(N_KV):
        want = (dst >> GROUP_SHIFT) * N_HEADS + h * GROUP + (dst & (GROUP - 1))
        place = jnp.where(jnp.logical_and(in_aug, src == want), 1.0, 0.0).astype(BF16)
        kout_ref[h] = kin_ref[h] + _dot(pieces_t, place).astype(BF16)


def _fox_decay(logft, kaug, blk):
    b, h, s = logft.shape
    ns = s // blk
    kspec = pl.BlockSpec((N_KV, blk, AUG), lambda bi, i: (0, bi * ns + i, 0))
    return pl.pallas_call(
        functools.partial(_fox_decay_kernel, blk=blk),
        grid=(b, ns),
        in_specs=[pl.BlockSpec((None, h, blk), lambda bi, i: (bi, 0, i)), kspec],
        out_specs=kspec,
        out_shape=jax.ShapeDtypeStruct(kaug.shape, kaug.dtype),
        scratch_shapes=[pltpu.VMEM((h, LANES), F32)],
        input_output_aliases={1: 0},
        compiler_params=_cparams(("parallel", "arbitrary")),
        name="fox_decay",
    )(logft, kaug)


def _decode_element(qm, kn, vn, fn, kbuf, vbuf, fbuf):
    cp = kbuf.shape[0]
    cin = _cumsum_lanes_blocks(fbuf[...].reshape(cp * N_HEADS, PAGE), PAGE).reshape(cp, N_HEADS, PAGE)
    carry = jnp.zeros((N_HEADS, 1), F32)
    cs = []
    for p in range(cp):
        cs.append(cin[p] + carry)
        carry = carry + cin[p][:, PAGE - 1:PAGE]
    kcat = jnp.concatenate([kbuf[p] for p in range(cp)], axis=1).astype(BF16)
    t = _dot(qm, kcat) - jnp.concatenate(cs, axis=1)
    t_n = jnp.sum(qm.astype(F32) * _bf16_round(kn), axis=1, keepdims=True) - (carry + fn)
    m = jnp.maximum(jnp.max(t, axis=1, keepdims=True), t_n)
    p = jnp.exp(t - m)
    p_n = jnp.exp(t_n - m)
    denom = jnp.sum(p, axis=1, keepdims=True) + p_n
    vcat = jnp.concatenate([vbuf[p_] for p_ in range(cp)], axis=1).astype(BF16)
    pb = p.astype(BF16)
    half = KV_WIDTH // 2
    acc = jnp.concatenate([_dot_nt(pb, vcat[:half]), _dot_nt(pb, vcat[half:])], axis=1)
    acc = acc + _bf16_round(p_n) * _bf16_round(vn)
    return _head_diag(acc / denom)


def _fox_prompt_kernel(*refs, tq, tk, decode):
    assert tq == tk
    if decode:
        (pt_ref, qt_ref, k_ref, vt_ref, qm_ref, kn_ref, vn_ref, fn_ref, ck_hbm, cv_hbm, cf_hbm,
         o_ref, os_ref, *scratch) = refs
        kbuf, vbuf, fbuf, sems = scratch[6:]
        scratch = scratch[:6]
    else:
        qt_ref, k_ref, vt_ref, o_ref, *scratch = refs
    t = pl.program_id(2)
    m_ref, acc_ref, sa_ref, sb_ref, ca_ref, cb_ref = scratch
    cols = GROUP * tq

    if decode:
        step = (pl.program_id(0) * pl.num_programs(1) + pl.program_id(1)) * pl.num_programs(2) + t
        n_rows = 2 * pl.num_programs(0) * pl.num_programs(1) * pl.num_programs(2)
        n_pages = kbuf.shape[1]

    def page_copies(r, slot, p):
        page = pt_ref[r, p]
        return (pltpu.make_async_copy(ck_hbm.at[page], kbuf.at[slot, p], sems.at[slot, 0]),
                pltpu.make_async_copy(cv_hbm.at[page], vbuf.at[slot, p], sems.at[slot, 1]),
                pltpu.make_async_copy(cf_hbm.at[page], fbuf.at[slot, p], sems.at[slot, 2]))

    def request_pages(r, slot):
        for p in range(n_pages):
            for cpy in page_copies(r, slot, p):
                cpy.start()

    def sample_wait(tile):
        if not decode:
            return
        row = 2 * step + tile

        if tile == 0:
            @pl.when(step == 0)
            def _():
                request_pages(row, 0)
                request_pages(row + 1, 1)

        for p in range(n_pages):
            for cpy in page_copies(row, tile, p):
                cpy.wait()

    def sample_attend(tile):
        if not decode:
            return
        row = 2 * step + tile
        os_ref[tile] = _decode_element(qm_ref[tile], kn_ref[tile], vn_ref[tile], fn_ref[tile],
                                       kbuf.at[tile], vbuf.at[tile], fbuf.at[tile])

        @pl.when(row + 2 < n_rows)
        def _():
            request_pages(row + 2, tile)

    def reset():
        m_ref[...] = jnp.full(m_ref.shape, NEG, F32)
        acc_ref[...] = jnp.zeros(acc_ref.shape, F32)

    def logits(tile, kb, s_ref, c_ref):
        k0 = pl.multiple_of(kb * tk, tk)
        s = _dot(k_ref[pl.ds(k0, tk), :], qt_ref[tile])
        s_ref[...] = s
        c_ref[...] = jnp.max(s, axis=0, keepdims=True)

    def accumulate(kb, s_ref, c_ref, masked):
        s = s_ref[...]
        if masked:
            kpos = lax.broadcasted_iota(jnp.int32, (tk, cols), 0)
            qpos = lax.broadcasted_iota(jnp.int32, (tk, cols), 1) & (tq - 1)
            s = jnp.where(kpos <= qpos, s, NEG)
            cmax = jnp.max(s, axis=0, keepdims=True)
        else:
            cmax = c_ref[...]
        m_old = m_ref[...]
        m_new = jnp.maximum(m_old, cmax)
        alpha = jnp.exp2(m_old - m_new)
        p = jnp.exp2(s - m_new)
        acc_ref[...] = alpha * acc_ref[...] + _dot(vt_ref[kb], p.astype(BF16))
        m_ref[...] = m_new

    def finish(tile):
        o = acc_ref[0:HEAD_DIM, :] / acc_ref[HEAD_DIM:HEAD_DIM + 1, :]
        o_ref[tile * tq:(tile + 1) * tq, :] = jnp.concatenate(
            [o[:, g * tq:(g + 1) * tq] for g in range(GROUP)], axis=0).T

    def pairs(tile, first, second):
        def body(j, carry):
            kb = 2 * j
            logits(tile, kb + 1, *second)
            accumulate(kb, *first, False)
            logits(tile, kb + 2, *first)
            accumulate(kb + 1, *second, False)
            return carry
        lax.fori_loop(0, t, body, 0)

    buf_a, buf_b = (sa_ref, ca_ref), (sb_ref, cb_ref)
    sample_wait(0)
    reset()
    logits(0, 0, *buf_a)
    sample_attend(0)
    pairs(0, buf_a, buf_b)
    sample_wait(1)
    logits(1, 0, *buf_b)
    accumulate(2 * t, *buf_a, True)
    finish(0)
    sample_attend(1)
    reset()
    pairs(1, buf_b, buf_a)
    logits(1, 2 * t + 1, *buf_a)
    accumulate(2 * t, *buf_b, False)
    accumulate(2 * t + 1, *buf_a, True)
    finish(1)


def _fox_prompt_fits_decode(batch, seq, tq, n_rows):
    return n_rows == batch * N_KV * (seq // tq)


def _fox_prompt(qt, kaug, vtb, batch, seq, tq, decode=None):
    nq = seq // tq
    nk, tk = vtb.shape[2], vtb.shape[4]
    cols = GROUP * tq
    nt = nq // 2
    ins = [qt, kaug, vtb]
    in_specs = [pl.BlockSpec((None, None, 2, AUG, cols), lambda b, h, t, *_: (b, h, t, 0, 0)),
                pl.BlockSpec((None, seq, AUG), lambda b, h, t, *_: (h, b, 0)),
                pl.BlockSpec((None, None, nk, V_ROWS, tk), lambda b, h, t, *_: (b, h, 0, 0, 0))]
    out_specs = [pl.BlockSpec((2 * tq, GROUP * HEAD_DIM), lambda b, h, t, *_: (b * nt + t, h))]
    out_shape = [jax.ShapeDtypeStruct((batch * seq, D_MODEL), F32)]
    scratch = [pltpu.VMEM((1, cols), F32),
               pltpu.VMEM((V_ROWS, cols), F32),
               pltpu.VMEM((tk, cols), F32), pltpu.VMEM((tk, cols), F32),
               pltpu.VMEM((1, cols), F32), pltpu.VMEM((1, cols), F32)]
    prefetch = []
    if decode is not None:
        page_table, qm, kn, vn, fn, ck, cv, cf = decode
        n_rows, n_pages = page_table.shape
        assert _fox_prompt_fits_decode(batch, seq, tq, n_rows)
        pair = lambda s1, s2: pl.BlockSpec((2, s1, s2), lambda b, h, t, pt: ((b * N_KV + h) * nt + t, 0, 0))
        any_spec = pl.BlockSpec(memory_space=pl.ANY)
        prefetch = [page_table]
        ins += [qm, kn, vn, fn, ck, cv, cf]
        in_specs += [pair(N_HEADS, KV_WIDTH), pair(1, KV_WIDTH), pair(1, KV_WIDTH), pair(N_HEADS, 1),
                     any_spec, any_spec, any_spec]
        out_specs.append(pair(N_HEADS, HEAD_DIM))
        out_shape.append(jax.ShapeDtypeStruct((n_rows, N_HEADS, HEAD_DIM), F32))
        scratch += [pltpu.VMEM((2, n_pages, KV_WIDTH, PAGE), F32), pltpu.VMEM((2, n_pages, KV_WIDTH, PAGE), F32),
                    pltpu.VMEM((2, n_pages, N_HEADS, PAGE), F32), pltpu.SemaphoreType.DMA((2, 3))]
    semantics = ("arbitrary",) * 3 if decode is not None else ("parallel", "parallel", "arbitrary")
    outs = pl.pallas_call(
        functools.partial(_fox_prompt_kernel, tq=tq, tk=tk, decode=decode is not None),
        grid_spec=pltpu.PrefetchScalarGridSpec(
            num_scalar_prefetch=len(prefetch),
            grid=(batch, N_KV, nt),
            in_specs=in_specs,
            out_specs=out_specs,
            scratch_shapes=scratch,
        ),
        out_shape=out_shape,
        compiler_params=pltpu.CompilerParams(
            dimension_semantics=semantics,
            vmem_limit_bytes=VMEM_LIMIT_RIDER if decode is not None else VMEM_LIMIT),
        name="fox_prompt",
    )(*prefetch, *ins)
    return outs if decode is not None else outs[0]


N_DECODE_SCRATCH = 8


def _fox_sample_kernel(pt_ref, qm_ref, kn_ref, vn_ref, fn_ref, ck_hbm, cv_hbm, cf_hbm, *rest, cp, nchunk, rider):
    n_in = _n_pool_inputs(rider["final"], rider["pending"]) if rider else 0
    rider_in, o_ref, rest = rest[:n_in], rest[n_in], rest[n_in + 1:]
    rider_out, rest = (rest[:2], rest[2:]) if rider else ((), rest)
    kbuf, vbuf, fbuf, sems, m_ref, l_ref, acc_ref, carry_ref = rest[:N_DECODE_SCRATCH]
    rider_bufs = rest[N_DECODE_SCRATCH:]
    b = pl.program_id(0)
    c = pl.program_id(1)
    nb = pl.num_programs(0)
    step = b * nchunk + c
    slot = step % 2

    def copies(bb, cc, sl, p):
        page = pt_ref[bb, cc * cp + p]
        return (pltpu.make_async_copy(ck_hbm.at[page], kbuf.at[sl, p], sems.at[sl, 0]),
                pltpu.make_async_copy(cv_hbm.at[page], vbuf.at[sl, p], sems.at[sl, 1]),
                pltpu.make_async_copy(cf_hbm.at[page], fbuf.at[sl, p], sems.at[sl, 2]))

    def issue(bb, cc, sl):
        for p in range(cp):
            for cpy in copies(bb, cc, sl, p):
                cpy.start()

    @pl.when(step == 0)
    def _():
        issue(b, c, slot)

    @pl.when(step + 1 < nb * nchunk)
    def _():
        nxt = step + 1
        issue(nxt // nchunk, nxt % nchunk, 1 - slot)

    if rider:
        per_batch = rider["tiles_per_batch"]
        _pool_tile(step % per_batch, per_batch, rider_in, rider_out, rider_bufs,
                   tp=rider["tp"], final=rider["final"], pending=rider["pending"])

    for p in range(cp):
        for cpy in copies(b, c, slot, p):
            cpy.wait()

    @pl.when(c == 0)
    def _():
        m_ref[...] = jnp.full(m_ref.shape, NEG, F32)
        l_ref[...] = jnp.zeros(l_ref.shape, F32)
        acc_ref[...] = jnp.zeros(acc_ref.shape, F32)
        carry_ref[...] = jnp.zeros(carry_ref.shape, F32)

    qm = qm_ref[...]
    cin = _cumsum_lanes_blocks(fbuf[slot].reshape(cp * N_HEADS, PAGE), PAGE).reshape(cp, N_HEADS, PAGE)
    carry = carry_ref[...]
    cs = []
    for p in range(cp):
        cs.append(cin[p] + carry)
        carry = carry + cin[p][:, PAGE - 1:PAGE]
    carry_ref[...] = carry
    kcat = jnp.concatenate([kbuf[slot, p] for p in range(cp)], axis=1).astype(BF16)
    t = _dot(qm, kcat) - jnp.concatenate(cs, axis=1)
    m_old = m_ref[...]
    m_new = jnp.maximum(m_old, jnp.max(t, axis=1, keepdims=True))
    alpha = jnp.exp(m_old - m_new)
    p = jnp.exp(t - m_new)
    l_new = alpha * l_ref[...] + jnp.sum(p, axis=1, keepdims=True)
    vcat = jnp.concatenate([vbuf[slot, p_] for p_ in range(cp)], axis=1).astype(BF16)
    acc_new = alpha * acc_ref[...] + _dot_nt(p.astype(BF16), vcat)
    m_ref[...] = m_new
    l_ref[...] = l_new
    acc_ref[...] = acc_new

    @pl.when(c == nchunk - 1)
    def _():
        s_n = jnp.sum(qm.astype(F32) * _bf16_round(kn_ref[...]), axis=1, keepdims=True)
        t_n = s_n - (carry + fn_ref[...])
        m_f = jnp.maximum(m_new, t_n)
        a = jnp.exp(m_new - m_f)
        p_n = jnp.exp(t_n - m_f)
        l_f = a * l_new + p_n
        acc = a * acc_new + _bf16_round(p_n) * _bf16_round(vn_ref[...])
        o_ref[...] = _head_diag(acc / l_f)


def _fox_sample(page_table, qm, kn, vn, fn, ck, cv, cf, cp, pool_layer=None):
    m, n_pages = page_table.shape
    nchunk = n_pages // cp
    per_b = lambda s1, s2: pl.BlockSpec((None, s1, s2), lambda b, c, pt: (b, 0, 0))
    any_spec = pl.BlockSpec(memory_space=pl.ANY)
    ins = [qm, kn, vn, fn, ck, cv, cf]
    in_specs = [per_b(N_HEADS, KV_WIDTH), per_b(1, KV_WIDTH), per_b(1, KV_WIDTH), per_b(N_HEADS, 1),
                any_spec, any_spec, any_spec]
    out_specs = [per_b(N_HEADS, HEAD_DIM)]
    out_shape = [jax.ShapeDtypeStruct((m, N_HEADS, HEAD_DIM), F32)]
    scratch = [pltpu.VMEM((2, cp, KV_WIDTH, PAGE), F32),
               pltpu.VMEM((2, cp, KV_WIDTH, PAGE), F32),
               pltpu.VMEM((2, cp, N_HEADS, PAGE), F32),
               pltpu.SemaphoreType.DMA((2, 3)),
               pltpu.VMEM((N_HEADS, 1), F32), pltpu.VMEM((N_HEADS, 1), F32),
               pltpu.VMEM((N_HEADS, KV_WIDTH), F32), pltpu.VMEM((N_HEADS, 1), F32)]
    assert len(scratch) == N_DECODE_SCRATCH
    rider = None
    if pool_layer is not None:
        *layer, batch, seq = pool_layer
        assert nchunk == 1 and (batch * seq) % m == 0
        tp = batch * seq // m
        per_batch = seq // tp
        assert seq % tp == 0 and tp % HALO == 0
        p_ins, p_in_specs, p_out_specs, p_scratch, final = _pool_prompt_operands(
            *layer, tp, lambda b, c, pt: b, lambda b, c, pt: b // per_batch, True)
        rider = dict(tp=tp, final=final, pending=layer[1] is not None, tiles_per_batch=per_batch)
        ins += p_ins
        in_specs += p_in_specs
        out_specs += p_out_specs
        out_shape += [jax.ShapeDtypeStruct((batch * seq, D_MODEL), F32),
                      jax.ShapeDtypeStruct((batch, HALO, D_MODEL), F32)]
        scratch += p_scratch
    outs = pl.pallas_call(
        functools.partial(_fox_sample_kernel, cp=cp, nchunk=nchunk, rider=rider),
        grid_spec=pltpu.PrefetchScalarGridSpec(
            num_scalar_prefetch=1,
            grid=(m, nchunk),
            in_specs=in_specs,
            out_specs=out_specs,
            scratch_shapes=scratch,
        ),
        out_shape=out_shape,
        compiler_params=pltpu.CompilerParams(dimension_semantics=("arbitrary", "arbitrary"),
                                             vmem_limit_bytes=VMEM_LIMIT_RIDER if rider else VMEM_LIMIT),
        name="fox_sample",
    )(page_table, *ins)
    return outs if rider else outs[0]


def _head_rows_q(q_rows):
    m = q_rows.shape[0]
    q4 = q_rows.reshape(m, N_KV, GROUP, HEAD_DIM)
    eye = jnp.eye(N_KV, dtype=q_rows.dtype)
    return (q4[:, :, :, None, :] * eye[None, :, None, :, None]).reshape(m, N_HEADS, KV_WIDTH)


def _tokens_last(x):
    lead = x.shape[:-3]
    n = len(lead)
    xt = jnp.transpose(x, tuple(range(n)) + (n + 1, n + 2, n))
    return xt.reshape(lead + (KV_WIDTH, x.shape[-3]))


def _tokens_first(xt):
    lead = xt.shape[:-2]
    n = len(lead)
    x4 = xt.reshape(lead + (N_KV, HEAD_DIM, xt.shape[-1]))
    return jnp.transpose(x4, tuple(range(n)) + (n + 2, n, n + 1))


def kernel(x_prompt, x_sample, state_pool, cache_win_k, cache_win_v, cache_fox_k, cache_fox_v,
           cache_fox_logf, page_table, norm_g, final_norm_g, rel_bias, pool_w_in, pool_mix,
           pool_scale, pool_w_out, swa_w_in, swa_sinks, swa_w_out, fox_w_in, fox_f_bias, fox_w_out):
    batch, seq, _ = x_prompt.shape
    db = x_sample.shape[0]
    depth = norm_g.shape[0]
    mp = batch * seq
    tm_p, tm_s = 512, db

    xp = x_prompt.reshape(mp, D_MODEL)
    xs = x_sample.reshape(db, D_MODEL)

    rb = rel_bias.astype(F32)
    dist_keys = WINDOW - np.arange(WINDOW)
    bias_keys = jnp.where((dist_keys < WINDOW)[None, :], rb[_t5_bucket_np(dist_keys)].T, NEG)
    bias0 = rb[0].reshape(N_HEADS, 1)
    fg = final_norm_g.reshape(1, D_MODEL)

    pool_p, pool_s = [], []
    wk_p, wv_p, wk_s, wv_s = [], [], [], []
    fk_p, fv_p, fl_p, fk_s, fv_s, fl_s = [], [], [], [], [], []
    def pool_layer_args(layer):
        jj = layer // 3
        fg_l = fg if layer == depth - 1 else None
        return (norm_g[layer].reshape(1, D_MODEL), pool_w_in[jj].astype(BF16), pool_mix[jj].astype(BF16),
                pool_scale[jj].reshape(1, D_MODEL), pool_w_out[jj].astype(BF16), fg_l)

    prompt_done = None
    pending = None
    for i in range(depth):
        kind, j = i % 3, i // 3
        g = norm_g[i].reshape(1, D_MODEL)
        if kind == 0:
            g, w_in, mix, scale, w_out, fg_l = pool_layer_args(i)
            if prompt_done is None:
                xp, u_tail = _pool_prompt(xp, pending, g, w_in, mix, scale, w_out, fg_l, batch, seq, tm_p)
                pending = None
            else:
                xp, u_tail = prompt_done
                prompt_done = None
            pool_p.append(u_tail[:, HALO - POOL_STATE:])
            xs, st_new = _pool_sample(xs, jnp.transpose(state_pool[j], (1, 0, 2)), g, w_in, mix, scale, w_out, fg_l)
            pool_s.append(jnp.transpose(st_new, (1, 0, 2)))
        elif kind == 1:
            w_in = swa_w_in[j].astype(BF16)
            w_out = swa_w_out[j].astype(BF16)
            sinks = swa_sinks[j].astype(F32)
            qt_p, kh_p, kt_p, vt_p, vtb_p, gate_p = _swa_proj(xp, g, w_in, tm_p, batch, seq, True)
            xp = _swa_prompt(qt_p, kh_p, vtb_p, gate_p, w_out, xp, rb, sinks, batch, seq, tm_p)
            wk_p.append(_tokens_first(kt_p[:, :, seq - WINDOW:]))
            wv_p.append(_tokens_first(vt_p[:, :, seq - WINDOW:]))
            q_s, k_s, v_s, kt_s, vt_s, gate_s = _swa_proj(xs, g, w_in, tm_s, 1, db, False)
            o_s, wk, wv = _swa_sample(_head_rows_q(q_s), _tokens_last(cache_win_k[j]),
                                      _tokens_last(cache_win_v[j]), k_s[:, None, :], v_s[:, None, :],
                                      kt_s[0], vt_s[0], bias_keys, bias0, sinks.reshape(N_HEADS, 1),
                                      SWA_ROWS_PER_STEP)
            xs = _out_proj(o_s.reshape(db, D_MODEL), gate_s, w_out, xs, tm_s)
            wk_s.append(_tokens_first(wk))
            wv_s.append(_tokens_first(wv))
        else:
            w_full = fox_w_in[j]
            nqkv = D_MODEL + 2 * KV_WIDTH
            w_in = jnp.concatenate([w_full[:, :nqkv], w_full[:, nqkv + N_HEADS:]], axis=1).astype(BF16)
            wf = jnp.pad(w_full[:, nqkv:nqkv + N_HEADS], ((0, 0), (0, LANES - N_HEADS))).astype(BF16)
            fb = jnp.pad(fox_f_bias[j].astype(F32), (0, LANES - N_HEADS)).reshape(1, LANES)
            w_out = fox_w_out[j].astype(BF16)
            qt, kt_p, vt_p, kaug, vtb, logft_p, gate_p = _fox_proj(xp, g, w_in, wf, fb, FOX_TILE, batch, seq, True)
            kaug = _fox_decay(logft_p, kaug, FOX_TILE)
            fk_p.append(_tokens_first(kt_p))
            fv_p.append(_tokens_first(vt_p))
            fl_p.append(jnp.transpose(logft_p, (0, 2, 1)))
            q_s, k_s, v_s, kt_s, vt_s, logf_s, logft_s, gate_s = _fox_proj(
                xs, g, w_in, wf, fb, tm_s, 1, db, False)
            decode_args = (page_table, _head_rows_q(q_s), k_s[:, None, :], v_s[:, None, :], logf_s[:, :, None],
                           _tokens_last(cache_fox_k[j]), _tokens_last(cache_fox_v[j]),
                           jnp.transpose(cache_fox_logf[j], (0, 2, 1)))
            next_is_pool = i + 1 < depth and (i + 1) % 3 == 0
            if _fox_prompt_fits_decode(batch, seq, FOX_TILE, db):
                o_p, o_s = _fox_prompt(qt, kaug, vtb, batch, seq, FOX_TILE, decode=decode_args)
                if next_is_pool:
                    pending = (o_p, gate_p, w_out)
                else:
                    xp = _out_proj(o_p, gate_p, w_out, xp, tm_p)
            else:
                o_p = _fox_prompt(qt, kaug, vtb, batch, seq, FOX_TILE)
                if next_is_pool:
                    o_s, xp, u_tail = _fox_sample(
                        *decode_args, FOX_PAGES_PER_STEP,
                        pool_layer=(xp, (o_p, gate_p, w_out), *pool_layer_args(i + 1), batch, seq))
                    prompt_done = (xp, u_tail)
                else:
                    xp = _out_proj(o_p, gate_p, w_out, xp, tm_p)
                    o_s = _fox_sample(*decode_args, FOX_PAGES_PER_STEP)
            xs = _out_proj(o_s.reshape(db, D_MODEL), gate_s, w_out, xs, tm_s)
            fk_s.append(_tokens_first(kt_s[0])[:, None])
            fv_s.append(_tokens_first(vt_s[0])[:, None])
            fl_s.append(logft_s[0].T[:, None, :])

    if (depth - 1) % 3 != 0:
        xp, xs = _final_norm(xp, fg, tm_p), _final_norm(xs, fg, tm_s)
    y_prompt = xp.reshape(batch, seq, D_MODEL)
    y_sample = xs.reshape(db, 1, D_MODEL)
    return (y_prompt, y_sample, jnp.stack(pool_p), jnp.stack(pool_s), jnp.stack(wk_p), jnp.stack(wv_p),
            jnp.stack(wk_s), jnp.stack(wv_s), jnp.stack(fk_p), jnp.stack(fv_p), jnp.stack(fl_p),
            jnp.stack(fk_s), jnp.stack(fv_s), jnp.stack(fl_s))
```

```python
import functools
import math

import numpy as np
import jax
import jax.numpy as jnp
from jax import lax
from jax.experimental import pallas as pl
from jax.experimental.pallas import tpu as pltpu

D_MODEL = 1024
HEAD_DIM = 64
N_HEADS = 16
N_KV = 4
GROUP = 4
KV_WIDTH = N_KV * HEAD_DIM
POOL_WINDOWS = (2, 4, 8, 16)
POOL_GROUP = 256
POOL_STATE = 15
WINDOW = 128
N_BUCKETS = 32
MAX_DISTANCE = 128
SCALE = HEAD_DIM ** -0.5
EPS = 1e-6
NEG = -1e30
PAGE = 128
LANES = 128

BF16 = jnp.bfloat16
F32 = jnp.float32

VMEM_LIMIT = 56 * 1024 * 1024
VMEM_LIMIT_RIDER = 62 * 1024 * 1024


def _cparams(sem):
    return pltpu.CompilerParams(dimension_semantics=sem, vmem_limit_bytes=VMEM_LIMIT)


def _rms_bf16(x, g):
    ms = jnp.mean(x * x, axis=-1, keepdims=True)
    return (x * lax.rsqrt(ms + EPS) * g).astype(BF16)


def _silu(x):
    return x * jax.nn.sigmoid(x)


def _dot(a, b):
    return jnp.dot(a, b, preferred_element_type=F32)


def _dot_nt(a, b):
    return lax.dot_general(a, b, (((1,), (1,)), ((), ())), preferred_element_type=F32)


def _bf16_round(x):
    return x.astype(BF16).astype(F32)


def _full(shape):
    n = len(shape)
    return pl.BlockSpec(shape, lambda *_: (0,) * n)


LOG2E = 1.4426950408889634


def _swa_proj_kernel(x_ref, g_ref, w_ref, *out_refs, prompt):
    if prompt:
        qt_ref, kh_ref, kt_ref, vt_ref, vtb_ref, gate_ref = out_refs
    else:
        q_ref, k_ref, v_ref, kt_ref, vt_ref, gate_ref = out_refs
    hb = _rms_bf16(x_ref[...], g_ref[...])
    q = _dot(hb, w_ref[:, :D_MODEL])
    k = _dot(hb, w_ref[:, D_MODEL:D_MODEL + KV_WIDTH])
    v = _dot(hb, w_ref[:, D_MODEL + KV_WIDTH:D_MODEL + 2 * KV_WIDTH])
    vt = v.T
    kt_ref[...] = k.T
    vt_ref[...] = vt
    if prompt:
        tm = q.shape[0]
        qt = (q * (SCALE * LOG2E)).T.astype(BF16)
        for hd in range(N_HEADS):
            h, g = divmod(hd, GROUP)
            for jb in range(tm // WINDOW):
                qt_ref[h, jb, :, g * WINDOW:(g + 1) * WINDOW] = (
                    qt[hd * HEAD_DIM:(hd + 1) * HEAD_DIM, jb * WINDOW:(jb + 1) * WINDOW])
        for h in range(N_KV):
            kh_ref[h] = k[:, h * HEAD_DIM:(h + 1) * HEAD_DIM].astype(BF16)
            ones_row = lax.broadcasted_iota(jnp.int32, (V_ROWS - HEAD_DIM, tm), 0) == 0
            vth = jnp.concatenate([vt[h * HEAD_DIM:(h + 1) * HEAD_DIM, :],
                                   jnp.where(ones_row, 1.0, 0.0)], axis=0).astype(BF16)
            for jb in range(tm // WINDOW):
                vtb_ref[h, jb] = vth[:, jb * WINDOW:(jb + 1) * WINDOW]
    else:
        q_ref[...] = (q * SCALE).astype(BF16)
        k_ref[...] = k
        v_ref[...] = v
    gate_ref[...] = _dot(hb, w_ref[:, D_MODEL + 2 * KV_WIDTH:])


def _swa_proj(x, g, w, tm, batch, seq, prompt):
    m = x.shape[0]
    ns = seq // tm
    nb = tm // WINDOW
    row = lambda i: (i, 0)
    sds = jax.ShapeDtypeStruct
    tcol = pl.BlockSpec((None, KV_WIDTH, tm), lambda i: (i // ns, 0, i % ns))
    blocks = lambda rows, last: pl.BlockSpec((None, N_KV, nb, rows, last), lambda i: (i // ns, 0, i % ns, 0, 0))
    kv_t_shape = [sds((batch, KV_WIDTH, seq), F32)] * 2
    if prompt:
        out_specs = [blocks(HEAD_DIM, GROUP * WINDOW), pl.BlockSpec((N_KV, tm, HEAD_DIM), lambda i: (0, i, 0)),
                     tcol, tcol, blocks(V_ROWS, WINDOW)]
        out_shape = ([sds((batch, N_KV, seq // WINDOW, HEAD_DIM, GROUP * WINDOW), BF16),
                      sds((N_KV, m, HEAD_DIM), BF16)] + kv_t_shape
                     + [sds((batch, N_KV, seq // WINDOW, V_ROWS, WINDOW), BF16)])
    else:
        out_specs = [pl.BlockSpec((tm, D_MODEL), row), pl.BlockSpec((tm, KV_WIDTH), row),
                     pl.BlockSpec((tm, KV_WIDTH), row), tcol, tcol]
        out_shape = [sds((m, D_MODEL), BF16), sds((m, KV_WIDTH), F32), sds((m, KV_WIDTH), F32)] + kv_t_shape
    out_specs.append(pl.BlockSpec((tm, D_MODEL), row))
    out_shape.append(sds((m, D_MODEL), F32))
    return pl.pallas_call(
        functools.partial(_swa_proj_kernel, prompt=prompt),
        grid=(m // tm,),
        in_specs=[pl.BlockSpec((tm, D_MODEL), row), _full((1, D_MODEL)), _full(w.shape)],
        out_specs=out_specs,
        out_shape=out_shape,
        compiler_params=_cparams(("parallel",)),
        name="swa_proj",
    )(x, g, w)


def _log_sigmoid(x):
    return -(jnp.maximum(-x, 0.0) + jnp.log1p(jnp.exp(-jnp.abs(x))))


AUG = 128
N_PIECES = 3
SWA_UNITS_AHEAD = 2
SWA_ROWS_PER_STEP = 8
V_ROWS = HEAD_DIM + 16
FOX_PAGES_PER_STEP = 64
FOX_TILE = 512
FOX_KEYS = 512
FOX_QUERIES = 256
GROUP_SHIFT = GROUP.bit_length() - 1


def _fox_proj_kernel(x_ref, g_ref, w_ref, wf_ref, fb_ref, *out_refs, prompt):
    if prompt:
        qt_ref, kt_ref, vt_ref, kaug_ref, vtb_ref, logft_ref, gate_ref = out_refs
    else:
        q_ref, k_ref, v_ref, kt_ref, vt_ref, logf_ref, logft_ref, gate_ref = out_refs
    hb = _rms_bf16(x_ref[...], g_ref[...])
    q = _dot(hb, w_ref[:, :D_MODEL])
    k = _dot(hb, w_ref[:, D_MODEL:D_MODEL + KV_WIDTH])
    v = _dot(hb, w_ref[:, D_MODEL + KV_WIDTH:D_MODEL + 2 * KV_WIDTH])
    vt = v.T
    kt_ref[...] = k.T
    vt_ref[...] = vt
    logf = _log_sigmoid(_dot(hb, wf_ref[...]) + fb_ref[...])
    logft_ref[...] = logf.T[:N_HEADS, :]
    if prompt:
        tm = q.shape[0]
        qt = (q * (SCALE * LOG2E)).T.astype(BF16)
        r = lax.broadcasted_iota(jnp.int32, (AUG - HEAD_DIM, tm), 0)
        for hd in range(N_HEADS):
            h, g = divmod(hd, GROUP)
            cols = slice(g * tm, (g + 1) * tm)
            qt_ref[h, 0:HEAD_DIM, cols] = qt[hd * HEAD_DIM:(hd + 1) * HEAD_DIM, :]
            pick = jnp.logical_and(r < N_PIECES * GROUP, (r & (GROUP - 1)) == g)
            qt_ref[h, HEAD_DIM:AUG, cols] = jnp.where(pick, -1.0, 0.0).astype(BF16)
        zeros = jnp.zeros((tm, AUG - HEAD_DIM), F32)
        for h in range(N_KV):
            kaug_ref[h] = jnp.concatenate([k[:, h * HEAD_DIM:(h + 1) * HEAD_DIM], zeros], axis=1).astype(BF16)
            ones_row = lax.broadcasted_iota(jnp.int32, (V_ROWS - HEAD_DIM, tm), 0) == 0
            vth = jnp.concatenate([vt[h * HEAD_DIM:(h + 1) * HEAD_DIM, :],
                                   jnp.where(ones_row, 1.0, 0.0)], axis=0).astype(BF16)
            for kb in range(tm // FOX_KEYS):
                vtb_ref[h, kb] = vth[:, kb * FOX_KEYS:(kb + 1) * FOX_KEYS]
    else:
        q_ref[...] = (q * SCALE).astype(BF16)
        k_ref[...] = k
        v_ref[...] = v
        logf_ref[...] = logf[:, :N_HEADS]
    gate_ref[...] = _dot(hb, w_ref[:, D_MODEL + 2 * KV_WIDTH:])


def _fox_proj(x, g, w, wf, fb, tm, batch, seq, prompt):
    m = x.shape[0]
    ns = seq // tm
    row = lambda i: (i, 0)
    tcol = lambda r: pl.BlockSpec((None, r, tm), lambda i: (i // ns, 0, i % ns))
    sds = jax.ShapeDtypeStruct
    kv_t = [tcol(KV_WIDTH), tcol(KV_WIDTH)]
    kv_t_shape = [sds((batch, KV_WIDTH, seq), F32)] * 2
    if prompt:
        out_specs = ([pl.BlockSpec((None, N_KV, None, AUG, GROUP * tm), lambda i: (i // ns, 0, i % ns, 0, 0))] + kv_t
                     + [pl.BlockSpec((N_KV, tm, AUG), lambda i: (0, i, 0)),
                        pl.BlockSpec((None, N_KV, tm // FOX_KEYS, V_ROWS, FOX_KEYS),
                                     lambda i: (i // ns, 0, i % ns, 0, 0)),
                        tcol(N_HEADS)])
        out_shape = ([sds((batch, N_KV, ns, AUG, GROUP * tm), BF16)] + kv_t_shape
                     + [sds((N_KV, m, AUG), BF16),
                        sds((batch, N_KV, seq // FOX_KEYS, V_ROWS, FOX_KEYS), BF16),
                        sds((batch, N_HEADS, seq), F32)])
    else:
        out_specs = ([pl.BlockSpec((tm, D_MODEL), row), pl.BlockSpec((tm, KV_WIDTH), row),
                      pl.BlockSpec((tm, KV_WIDTH), row)] + kv_t
                     + [pl.BlockSpec((tm, N_HEADS), row), tcol(N_HEADS)])
        out_shape = ([sds((m, D_MODEL), BF16), sds((m, KV_WIDTH), F32), sds((m, KV_WIDTH), F32)]
                     + kv_t_shape + [sds((m, N_HEADS), F32), sds((batch, N_HEADS, seq), F32)])
    out_specs.append(pl.BlockSpec((tm, D_MODEL), row))
    out_shape.append(sds((m, D_MODEL), F32))
    return pl.pallas_call(
        functools.partial(_fox_proj_kernel, prompt=prompt),
        grid=(m // tm,),
        in_specs=[pl.BlockSpec((tm, D_MODEL), row), _full((1, D_MODEL)), _full(w.shape),
                  _full(wf.shape), _full(fb.shape)],
        out_specs=out_specs,
        out_shape=out_shape,
        compiler_params=_cparams(("parallel",)),
        name="fox_proj",
    )(x, g, w, wf, fb)


def _out_proj_kernel(*refs, gated):
    if gated:
        o_ref, gate_ref, w_ref, x_ref, y_ref = refs
        ob = (o_ref[...] * _silu(gate_ref[...])).astype(BF16)
    else:
        o_ref, w_ref, x_ref, y_ref = refs
        ob = o_ref[...]
    y_ref[...] = x_ref[...] + _dot(ob, w_ref[...])


def _out_proj(o, gate, w, x, tm):
    m = x.shape[0]
    row = lambda i: (i, 0)
    tile = pl.BlockSpec((tm, D_MODEL), row)
    gated = gate is not None
    ins = [o, gate, w, x] if gated else [o, w, x]
    in_specs = [tile, tile, _full(w.shape), tile] if gated else [tile, _full(w.shape), tile]
    return pl.pallas_call(
        functools.partial(_out_proj_kernel, gated=gated),
        grid=(m // tm,),
        in_specs=in_specs,
        out_specs=tile,
        out_shape=jax.ShapeDtypeStruct((m, D_MODEL), F32),
        compiler_params=_cparams(("parallel",)),
        name="out_proj",
    )(*ins)


def _final_norm_kernel(x_ref, g_ref, y_ref):
    x = x_ref[...]
    ms = jnp.mean(x * x, axis=-1, keepdims=True)
    y_ref[...] = x * lax.rsqrt(ms + EPS) * g_ref[...]


def _final_norm(x, g, tm):
    m = x.shape[0]
    row = lambda i: (i, 0)
    return pl.pallas_call(
        _final_norm_kernel,
        grid=(m // tm,),
        in_specs=[pl.BlockSpec((tm, D_MODEL), row), _full((1, D_MODEL))],
        out_specs=pl.BlockSpec((tm, D_MODEL), row),
        out_shape=jax.ShapeDtypeStruct((m, D_MODEL), F32),
        compiler_params=_cparams(("parallel",)),
        name="final_norm",
    )(x, g)


HALO = 16
PAD = 8
assert all(w == 2 << g for g, w in enumerate(POOL_WINDOWS)) and HALO >= max(POOL_WINDOWS) and PAD >= HALO // 2


def _pool_layer_tail(x, u, gate, pooled_groups, mix_ref, scale_ref, wout_ref, fg_ref, y_ref):
    pieces = []
    for g in range(len(POOL_WINDOWS)):
        c0 = g * POOL_GROUP
        p = (pooled_groups[g] - u[:, c0:c0 + POOL_GROUP]).astype(BF16)
        pieces.append(_dot(p, mix_ref[g]))
    pm = jnp.concatenate(pieces, axis=1)
    o = (pm * scale_ref[...] * _silu(gate)).astype(BF16)
    y = x + _dot(o, wout_ref[...])
    if fg_ref is not None:
        ms = jnp.mean(y * y, axis=-1, keepdims=True)
        y = y * lax.rsqrt(ms + EPS) * fg_ref[...]
    y_ref[...] = y


def _n_pool_inputs(final, pending):
    return 6 + (3 if pending else 0) + (1 if final else 0)


def _pool_prompt_kernel(*refs, tp, final, pending):
    n_in = _n_pool_inputs(final, pending)
    _pool_tile(pl.program_id(1), pl.num_programs(1), refs[:n_in], refs[n_in:n_in + 2], refs[n_in + 2:],
               tp=tp, final=final, pending=pending)


def _pool_tile(i, n_tiles, in_refs, out_refs, bufs, *, tp, final, pending):
    refs = list(in_refs)
    x_ref = refs.pop(0)
    prev = [refs.pop(0) for _ in range(3)] if pending else None
    g_ref, win_ref, mix_ref, scale_ref, wout_ref = [refs.pop(0) for _ in range(5)]
    fg_ref = refs.pop(0) if final else None
    y_ref, tail_ref = out_refs
    hist = slice(PAD, PAD + HALO)
    ext = slice(PAD, PAD + HALO + tp)
    tile = slice(PAD + HALO, PAD + HALO + tp)

    def back(rows, k):
        return slice(rows.start - k, rows.stop - k)

    @pl.when(i == 0)
    def _():
        for buf in bufs:
            buf[0:PAD + HALO, :] = jnp.zeros((PAD + HALO, buf.shape[1]), F32)

    x = x_ref[...]
    if pending:
        po_ref, pgate_ref, pw_ref = prev
        x = x + _dot((po_ref[...] * _silu(pgate_ref[...])).astype(BF16), pw_ref[...])
    hb = _rms_bf16(x, g_ref[...])
    u = _dot(hb, win_ref[:, :D_MODEL])
    gate = _dot(hb, win_ref[:, D_MODEL:])
    bufs[0][tile, :] = u
    pos = i * tp + lax.broadcasted_iota(jnp.int32, (tp, 1), 0)
    pooled = []
    for g, w in enumerate(POOL_WINDOWS):
        src, span = bufs[g], 1 << g
        if g + 1 < len(bufs):
            nxt = bufs[g + 1]
            nxt[ext, :] = src[ext, POOL_GROUP:] + src[back(ext, span), POOL_GROUP:]
        acc = src[tile, 0:POOL_GROUP] + src[back(tile, span), 0:POOL_GROUP]
        inv_cnt = 1.0 / jnp.minimum(pos + 1, w).astype(F32)
        pooled.append(acc * inv_cnt)
    _pool_layer_tail(x, u, gate, pooled, mix_ref, scale_ref, wout_ref, fg_ref, y_ref)
    bufs[0][hist, :] = u[tp - HALO:tp, :]

    @pl.when(i == n_tiles - 1)
    def _():
        tail_ref[...] = u[tp - HALO:tp, :]


def _pool_prompt_operands(x, pending, g, w_in, mix, scale, w_out, fg, tp, row_tile, batch_of, single_buffer):
    final = fg is not None
    tile = pl.BlockSpec((tp, D_MODEL), lambda *idx: (row_tile(*idx), 0))
    whole = (lambda a: pl.BlockSpec(a.shape, lambda *_: (0,) * a.ndim, pipeline_mode=pl.Buffered(1))
             if single_buffer else _full(a.shape))
    ins = [x] + (list(pending) if pending else []) + [g, w_in, mix, scale, w_out] + ([fg] if final else [])
    in_specs = [tile] + ([tile, tile, whole(pending[2])] if pending else [])
    in_specs += [whole(a) for a in ins[len(in_specs):]]
    out_specs = [tile, pl.BlockSpec((None, HALO, D_MODEL), lambda *idx: (batch_of(*idx), 0, 0))]
    scratch = [pltpu.VMEM((PAD + HALO + tp, D_MODEL - g * POOL_GROUP), F32) for g in range(len(POOL_WINDOWS))]
    return ins, in_specs, out_specs, scratch, final


def _pool_prompt(x, pending, g, w_in, mix, scale, w_out, fg, batch, seq, tp):
    ns = seq // tp
    ins, in_specs, out_specs, scratch, final = _pool_prompt_operands(
        x, pending, g, w_in, mix, scale, w_out, fg, tp, lambda b, i: b * ns + i, lambda b, i: b, False)
    return pl.pallas_call(
        functools.partial(_pool_prompt_kernel, tp=tp, final=final, pending=bool(pending)),
        grid=(batch, ns),
        in_specs=in_specs,
        out_specs=out_specs,
        out_shape=[jax.ShapeDtypeStruct((batch * seq, D_MODEL), F32),
                   jax.ShapeDtypeStruct((batch, HALO, D_MODEL), F32)],
        scratch_shapes=scratch,
        compiler_params=_cparams(("parallel", "arbitrary")),
        name="pool_prompt",
    )(*ins)


def _pool_sample_kernel(*refs, final):
    if final:
        x_ref, st_ref, g_ref, win_ref, mix_ref, scale_ref, wout_ref, fg_ref, y_ref, st_out_ref = refs
    else:
        x_ref, st_ref, g_ref, win_ref, mix_ref, scale_ref, wout_ref, y_ref, st_out_ref = refs
        fg_ref = None
    x = x_ref[...]
    hb = _rms_bf16(x, g_ref[...])
    u = _dot(hb, win_ref[:, :D_MODEL])
    gate = _dot(hb, win_ref[:, D_MODEL:])
    for r in range(POOL_STATE - 1):
        st_out_ref[r] = st_ref[r + 1]
    st_out_ref[POOL_STATE - 1] = u
    pooled = []
    for g, w in enumerate(POOL_WINDOWS):
        c0 = g * POOL_GROUP
        acc = u[:, c0:c0 + POOL_GROUP]
        for k in range(1, w):
            acc = acc + st_ref[POOL_STATE - k, :, c0:c0 + POOL_GROUP]
        pooled.append(acc / float(w))
    _pool_layer_tail(x, u, gate, pooled, mix_ref, scale_ref, wout_ref, fg_ref, y_ref)


def _pool_sample(x, state_t, g, w_in, mix, scale, w_out, fg):
    m = x.shape[0]
    final = fg is not None
    ins = [x, state_t, g, w_in, mix, scale, w_out] + ([fg] if final else [])
    return pl.pallas_call(
        functools.partial(_pool_sample_kernel, final=final),
        grid=(1,),
        in_specs=[_full(a.shape) for a in ins],
        out_specs=[_full((m, D_MODEL)), _full(state_t.shape)],
        out_shape=[jax.ShapeDtypeStruct((m, D_MODEL), F32), jax.ShapeDtypeStruct(state_t.shape, F32)],
        compiler_params=_cparams(("arbitrary",)),
        name="pool_sample",
    )(*ins)


def _t5_bucket_np(dist):
    n = np.maximum(dist, 0)
    max_exact = N_BUCKETS // 2
    nf = np.maximum(n, 1).astype(np.float32)
    large = max_exact + (np.log(nf / max_exact) / math.log(MAX_DISTANCE / max_exact)
                         * (N_BUCKETS - max_exact)).astype(np.int32)
    large = np.minimum(large, N_BUCKETS - 1)
    return np.where(n < max_exact, n, large)


def _swa_prompt_kernel(sink_ref, rb_ref, qt_ref, kp_ref, kc_ref, vp_ref, vc_ref, bucket_ref, gate_ref,
                       wout_ref, x_ref, y_ref, bias_ref, o_buf, *, nb):
    i = pl.program_id(1)
    cols = GROUP * WINDOW

    @pl.when(jnp.logical_and(pl.program_id(0) == 0, i == 0))
    def _():
        bucket = bucket_ref[...]
        hits = [bucket == bk for bk in range(N_BUCKETS)]
        for hd in range(N_HEADS):
            h, g = divmod(hd, GROUP)
            b = jnp.full((2 * WINDOW, WINDOW), NEG, F32)
            for bk in range(N_BUCKETS):
                b = jnp.where(hits[bk], rb_ref[bk, hd] * LOG2E, b)
            bias_ref[h, :, g * WINDOW:(g + 1) * WINDOW] = b

    key_row = lax.broadcasted_iota(jnp.int32, (2 * WINDOW, cols), 0)
    no_prev = jnp.logical_and(i == 0, key_row < WINDOW)
    sinks = [jnp.concatenate([jnp.full((1, WINDOW), sink_ref[h * GROUP + g] * LOG2E, F32) for g in range(GROUP)],
                             axis=1) for h in range(N_KV)]

    def logits(jb, h):
        if jb == 0:
            kband = jnp.concatenate([kp_ref[h], kc_ref[h, 0:WINDOW, :]], axis=0)
        else:
            kband = kc_ref[h, (jb - 1) * WINDOW:(jb + 1) * WINDOW, :]
        s = _dot(kband, qt_ref[h, jb]) + bias_ref[h]
        return jnp.where(no_prev, NEG, s) if jb == 0 else s

    def attend(jb, h, s):
        vprev = vp_ref[h] if jb == 0 else vc_ref[h, jb - 1]
        vband = jnp.concatenate([vprev, vc_ref[h, jb]], axis=1)
        m = jnp.maximum(jnp.max(s, axis=0, keepdims=True), sinks[h])
        pv = _dot(vband, jnp.exp2(s - m).astype(BF16))
        denom = pv[HEAD_DIM:HEAD_DIM + 1, :] + jnp.exp2(sinks[h] - m)
        ot = pv[0:HEAD_DIM, :] * (1.0 / denom)
        o_heads = jnp.concatenate([ot[:, g * WINDOW:(g + 1) * WINDOW] for g in range(GROUP)], axis=0)
        o_buf[jb * WINDOW:(jb + 1) * WINDOW, h * GROUP * HEAD_DIM:(h + 1) * GROUP * HEAD_DIM] = o_heads.T

    units = [(jb, h) for jb in range(nb) for h in range(N_KV)]
    pending_logits = [logits(*unit) for unit in units[:SWA_UNITS_AHEAD]]
    for u, unit in enumerate(units):
        if u + SWA_UNITS_AHEAD < len(units):
            pending_logits.append(logits(*units[u + SWA_UNITS_AHEAD]))
        attend(*unit, pending_logits.pop(0))
    ob = (o_buf[...] * _silu(gate_ref[...])).astype(BF16)
    y_ref[...] = x_ref[...] + _dot(ob, wout_ref[...])


def _swa_prompt(qt, kh, vtb, gate, w_out, x, rel_bias, sinks, batch, seq, tm):
    ns = seq // tm
    nb = tm // WINDOW
    dist = np.arange(WINDOW)[None, :] + WINDOW - np.arange(2 * WINDOW)[:, None]
    bucket = np.where((dist >= 0) & (dist < WINDOW), _t5_bucket_np(dist), -1).astype(np.int32)
    tile = pl.BlockSpec((tm, D_MODEL), lambda b, i: (b * ns + i, 0))
    smem = pl.BlockSpec(memory_space=pltpu.SMEM)
    prev_blk = lambda i: jnp.maximum(i * nb - 1, 0)
    return pl.pallas_call(
        functools.partial(_swa_prompt_kernel, nb=nb),
        grid=(batch, ns),
        in_specs=[smem, smem,
                  pl.BlockSpec((None, N_KV, nb, HEAD_DIM, GROUP * WINDOW), lambda b, i: (b, 0, i, 0, 0)),
                  pl.BlockSpec((N_KV, WINDOW, HEAD_DIM), lambda b, i: (0, b * ns * nb + prev_blk(i), 0)),
                  pl.BlockSpec((N_KV, tm, HEAD_DIM), lambda b, i: (0, b * ns + i, 0)),
                  pl.BlockSpec((None, N_KV, None, V_ROWS, WINDOW), lambda b, i: (b, 0, prev_blk(i), 0, 0)),
                  pl.BlockSpec((None, N_KV, nb, V_ROWS, WINDOW), lambda b, i: (b, 0, i, 0, 0)),
                  _full(bucket.shape), tile, _full(w_out.shape), tile],
        out_specs=tile,
        out_shape=jax.ShapeDtypeStruct((batch * seq, D_MODEL), F32),
        scratch_shapes=[pltpu.VMEM((N_KV, 2 * WINDOW, GROUP * WINDOW), F32),
                        pltpu.VMEM((tm, D_MODEL), F32)],
        compiler_params=_cparams(("arbitrary", "arbitrary")),
        name="swa_prompt",
    )(sinks, rel_bias, qt, kh, kh, vtb, vtb, jnp.asarray(bucket), gate, w_out, x)


def _head_diag(o_full):
    out = jnp.zeros((N_HEADS, HEAD_DIM), F32)
    row_kv = lax.broadcasted_iota(jnp.int32, (N_HEADS, HEAD_DIM), 0) // GROUP
    for h in range(N_KV):
        out = out + jnp.where(row_kv == h, o_full[:, h * HEAD_DIM:(h + 1) * HEAD_DIM], 0.0)
    return out


def _swa_sample_kernel(qm_ref, kc_ref, vc_ref, kn_ref, vn_ref, knt_ref, vnt_ref, bias_ref, bias0_ref,
                       sink_ref, o_ref, wk_ref, wv_ref, *, bt):
    i = pl.program_id(0)
    lane = lax.broadcasted_iota(jnp.int32, (KV_WIDTH, WINDOW), 1)
    sink = sink_ref[...]
    for e in range(bt):
        b = i * bt + e
        kc = kc_ref[e]
        vc = vc_ref[e]
        qm = qm_ref[e]
        s = _dot(qm, kc.astype(BF16)) + bias_ref[...]
        s_n = jnp.sum(qm.astype(F32) * _bf16_round(kn_ref[e]), axis=1, keepdims=True) + bias0_ref[...]
        m = jnp.maximum(jnp.maximum(jnp.max(s, axis=1, keepdims=True), s_n), sink)
        p = jnp.exp(s - m)
        p_n = jnp.exp(s_n - m)
        denom = jnp.sum(p, axis=1, keepdims=True) + p_n + jnp.exp(sink - m)
        o_full = _dot_nt((p / denom).astype(BF16), vc.astype(BF16))
        o_full = o_full + _bf16_round(p_n / denom) * _bf16_round(vn_ref[e])
        o_ref[e] = _head_diag(o_full)
        kcol = jnp.sum(jnp.where(lane == b, knt_ref[...], 0.0), axis=1, keepdims=True)
        vcol = jnp.sum(jnp.where(lane == b, vnt_ref[...], 0.0), axis=1, keepdims=True)
        wk_ref[e] = jnp.where(lane == WINDOW - 1, kcol, pltpu.roll(kc, WINDOW - 1, 1))
        wv_ref[e] = jnp.where(lane == WINDOW - 1, vcol, pltpu.roll(vc, WINDOW - 1, 1))


def _swa_sample(qm, kc, vc, kn, vn, knt, vnt, bias_keys, bias0, sinks, bt):
    m = qm.shape[0]
    blk3 = lambda s1, s2: pl.BlockSpec((bt, s1, s2), lambda i: (i, 0, 0))
    cache = jax.ShapeDtypeStruct((m, KV_WIDTH, WINDOW), F32)
    return pl.pallas_call(
        functools.partial(_swa_sample_kernel, bt=bt),
        grid=(m // bt,),
        in_specs=[blk3(N_HEADS, KV_WIDTH), blk3(KV_WIDTH, WINDOW), blk3(KV_WIDTH, WINDOW),
                  blk3(1, KV_WIDTH), blk3(1, KV_WIDTH), _full(knt.shape), _full(vnt.shape),
                  _full(bias_keys.shape), _full(bias0.shape), _full(sinks.shape)],
        out_specs=[blk3(N_HEADS, HEAD_DIM), blk3(KV_WIDTH, WINDOW), blk3(KV_WIDTH, WINDOW)],
        out_shape=[jax.ShapeDtypeStruct((m, N_HEADS, HEAD_DIM), F32), cache, cache],
        compiler_params=_cparams(("parallel",)),
        name="swa_sample",
    )(qm, kc, vc, kn, vn, knt, vnt, bias_keys, bias0, sinks)


def _split3(x):
    hi = x.astype(BF16)
    r = x - hi.astype(F32)
    mid = r.astype(BF16)
    lo = (r - mid.astype(F32)).astype(BF16)
    return hi, mid, lo


def _cumsum_lanes_blocks(x, blk):
    n = x.shape[0]
    hi, mid, lo = _split3(x)
    r_i = lax.broadcasted_iota(jnp.int32, (blk, blk), 0)
    c_i = lax.broadcasted_iota(jnp.int32, (blk, blk), 1)
    upper = jnp.where(r_i <= c_i, 1.0, 0.0).astype(BF16)
    r = _dot(jnp.concatenate([hi, mid, lo], axis=0), upper)
    return r[0:n] + r[n:2 * n] + r[2 * n:]


def _fox_decay_kernel(x_ref, kin_ref, kout_ref, carry, *, blk):
    i = pl.program_id(1)

    @pl.when(i == 0)
    def _():
        carry[...] = jnp.zeros_like(carry)

    c = _cumsum_lanes_blocks(x_ref[...], blk) + carry[:, 0:1]
    carry[...] = jnp.broadcast_to(c[:, blk - 1:blk], carry.shape)
    hi, mid, lo = _split3(c * LOG2E)
    pieces = jnp.concatenate([hi, mid, lo], axis=0).astype(F32)
    pad = jnp.zeros((LANES - N_PIECES * N_HEADS, blk), F32)
    pieces_t = jnp.concatenate([pieces, pad], axis=0).T.astype(BF16)
    src = lax.broadcasted_iota(jnp.int32, (LANES, AUG), 0)
    dst = lax.broadcasted_iota(jnp.int32, (LANES, AUG), 1) - HEAD_DIM
    in_aug = jnp.logical_and(dst >= 0, dst < N_PIECES * GROUP)
    for h in range(N_KV):
        want = (dst >> GROUP_SHIFT) * N_HEADS + h * GROUP + (dst & (GROUP - 1))
        place = jnp.where(jnp.logical_and(in_aug, src == want), 1.0, 0.0).astype(BF16)
        kout_ref[h] = kin_ref[h] + _dot(pieces_t, place).astype(BF16)


def _fox_decay(logft, kaug, blk):
    b, h, s = logft.shape
    ns = s // blk
    kspec = pl.BlockSpec((N_KV, blk, AUG), lambda bi, i: (0, bi * ns + i, 0))
    return pl.pallas_call(
        functools.partial(_fox_decay_kernel, blk=blk),
        grid=(b, ns),
        in_specs=[pl.BlockSpec((None, h, blk), lambda bi, i: (bi, 0, i)), kspec],
        out_specs=kspec,
        out_shape=jax.ShapeDtypeStruct(kaug.shape, kaug.dtype),
        scratch_shapes=[pltpu.VMEM((h, LANES), F32)],
        input_output_aliases={1: 0},
        compiler_params=_cparams(("parallel", "arbitrary")),
        name="fox_decay",
    )(logft, kaug)


def _decode_element(qm, kn, vn, fn, kbuf, vbuf, fbuf):
    cp = kbuf.shape[0]
    cin = _cumsum_lanes_blocks(fbuf[...].reshape(cp * N_HEADS, PAGE), PAGE).reshape(cp, N_HEADS, PAGE)
    carry = jnp.zeros((N_HEADS, 1), F32)
    cs = []
    for p in range(cp):
        cs.append(cin[p] + carry)
        carry = carry + cin[p][:, PAGE - 1:PAGE]
    kcat = jnp.concatenate([kbuf[p] for p in range(cp)], axis=1).astype(BF16)
    t = _dot(qm, kcat) - jnp.concatenate(cs, axis=1)
    t_n = jnp.sum(qm.astype(F32) * _bf16_round(kn), axis=1, keepdims=True) - (carry + fn)
    m = jnp.maximum(jnp.max(t, axis=1, keepdims=True), t_n)
    p = jnp.exp(t - m)
    p_n = jnp.exp(t_n - m)
    denom = jnp.sum(p, axis=1, keepdims=True) + p_n
    vcat = jnp.concatenate([vbuf[p_] for p_ in range(cp)], axis=1).astype(BF16)
    pb = p.astype(BF16)
    half = KV_WIDTH // 2
    acc = jnp.concatenate([_dot_nt(pb, vcat[:half]), _dot_nt(pb, vcat[half:])], axis=1)
    acc = acc + _bf16_round(p_n) * _bf16_round(vn)
    return _head_diag(acc / denom)


def _fox_prompt_kernel(*refs, tq, tk, decode):
    assert tq == tk
    if decode:
        (pt_ref, qt_ref, k_ref, vt_ref, qm_ref, kn_ref, vn_ref, fn_ref, ck_hbm, cv_hbm, cf_hbm,
         o_ref, os_ref, *scratch) = refs
        kbuf, vbuf, fbuf, sems = scratch[6:]
        scratch = scratch[:6]
    else:
        qt_ref, k_ref, vt_ref, o_ref, *scratch = refs
    t = pl.program_id(2)
    m_ref, acc_ref, sa_ref, sb_ref, ca_ref, cb_ref = scratch
    cols = GROUP * tq

    if decode:
        step = (pl.program_id(0) * pl.num_programs(1) + pl.program_id(1)) * pl.num_programs(2) + t
        n_rows = 2 * pl.num_programs(0) * pl.num_programs(1) * pl.num_programs(2)
        n_pages = kbuf.shape[1]

    def page_copies(r, slot, p):
        page = pt_ref[r, p]
        return (pltpu.make_async_copy(ck_hbm.at[page], kbuf.at[slot, p], sems.at[slot, 0]),
                pltpu.make_async_copy(cv_hbm.at[page], vbuf.at[slot, p], sems.at[slot, 1]),
                pltpu.make_async_copy(cf_hbm.at[page], fbuf.at[slot, p], sems.at[slot, 2]))

    def request_pages(r, slot):
        for p in range(n_pages):
            for cpy in page_copies(r, slot, p):
                cpy.start()

    def sample_wait(tile):
        if not decode:
            return
        row = 2 * step + tile

        if tile == 0:
            @pl.when(step == 0)
            def _():
                request_pages(row, 0)
                request_pages(row + 1, 1)

        for p in range(n_pages):
            for cpy in page_copies(row, tile, p):
                cpy.wait()

    def sample_attend(tile):
        if not decode:
            return
        row = 2 * step + tile
        os_ref[tile] = _decode_element(qm_ref[tile], kn_ref[tile], vn_ref[tile], fn_ref[tile],
                                       kbuf.at[tile], vbuf.at[tile], fbuf.at[tile])

        @pl.when(row + 2 < n_rows)
        def _():
            request_pages(row + 2, tile)

    def reset():
        m_ref[...] = jnp.full(m_ref.shape, NEG, F32)
        acc_ref[...] = jnp.zeros(acc_ref.shape, F32)

    def logits(tile, kb, s_ref, c_ref):
        k0 = pl.multiple_of(kb * tk, tk)
        s = _dot(k_ref[pl.ds(k0, tk), :], qt_ref[tile])
        s_ref[...] = s
        c_ref[...] = jnp.max(s, axis=0, keepdims=True)

    def accumulate(kb, s_ref, c_ref, masked):
        s = s_ref[...]
        if masked:
            kpos = lax.broadcasted_iota(jnp.int32, (tk, cols), 0)
            qpos = lax.broadcasted_iota(jnp.int32, (tk, cols), 1) & (tq - 1)
            s = jnp.where(kpos <= qpos, s, NEG)
            cmax = jnp.max(s, axis=0, keepdims=True)
        else:
            cmax = c_ref[...]
        m_old = m_ref[...]
        m_new = jnp.maximum(m_old, cmax)
        alpha = jnp.exp2(m_old - m_new)
        p = jnp.exp2(s - m_new)
        acc_ref[...] = alpha * acc_ref[...] + _dot(vt_ref[kb], p.astype(BF16))
        m_ref[...] = m_new

    def finish(tile):
        o = acc_ref[0:HEAD_DIM, :] / acc_ref[HEAD_DIM:HEAD_DIM + 1, :]
        o_ref[tile * tq:(tile + 1) * tq, :] = jnp.concatenate(
            [o[:, g * tq:(g + 1) * tq] for g in range(GROUP)], axis=0).T

    def pairs(tile, first, second):
        def body(j, carry):
            kb = 2 * j
            logits(tile, kb + 1, *second)
            accumulate(kb, *first, False)
            logits(tile, kb + 2, *first)
            accumulate(kb + 1, *second, False)
            return carry
        lax.fori_loop(0, t, body, 0)

    buf_a, buf_b = (sa_ref, ca_ref), (sb_ref, cb_ref)
    sample_wait(0)
    reset()
    logits(0, 0, *buf_a)
    sample_attend(0)
    pairs(0, buf_a, buf_b)
    sample_wait(1)
    logits(1, 0, *buf_b)
    accumulate(2 * t, *buf_a, True)
    finish(0)
    sample_attend(1)
    reset()
    pairs(1, buf_b, buf_a)
    logits(1, 2 * t + 1, *buf_a)
    accumulate(2 * t, *buf_b, False)
    accumulate(2 * t + 1, *buf_a, True)
    finish(1)


def _fox_prompt_fits_decode(batch, seq, tq, n_rows):
    return n_rows == batch * N_KV * (seq // tq)


def _fox_prompt(qt, kaug, vtb, batch, seq, tq, decode=None):
    nq = seq // tq
    nk, tk = vtb.shape[2], vtb.shape[4]
    cols = GROUP * tq
    nt = nq // 2
    ins = [qt, kaug, vtb]
    in_specs = [pl.BlockSpec((None, None, 2, AUG, cols), lambda b, h, t, *_: (b, h, t, 0, 0)),
                pl.BlockSpec((None, seq, AUG), lambda b, h, t, *_: (h, b, 0)),
                pl.BlockSpec((None, None, nk, V_ROWS, tk), lambda b, h, t, *_: (b, h, 0, 0, 0))]
    out_specs = [pl.BlockSpec((2 * tq, GROUP * HEAD_DIM), lambda b, h, t, *_: (b * nt + t, h))]
    out_shape = [jax.ShapeDtypeStruct((batch * seq, D_MODEL), F32)]
    scratch = [pltpu.VMEM((1, cols), F32),
               pltpu.VMEM((V_ROWS, cols), F32),
               pltpu.VMEM((tk, cols), F32), pltpu.VMEM((tk, cols), F32),
               pltpu.VMEM((1, cols), F32), pltpu.VMEM((1, cols), F32)]
    prefetch = []
    if decode is not None:
        page_table, qm, kn, vn, fn, ck, cv, cf = decode
        n_rows, n_pages = page_table.shape
        assert _fox_prompt_fits_decode(batch, seq, tq, n_rows)
        pair = lambda s1, s2: pl.BlockSpec((2, s1, s2), lambda b, h, t, pt: ((b * N_KV + h) * nt + t, 0, 0))
        any_spec = pl.BlockSpec(memory_space=pl.ANY)
        prefetch = [page_table]
        ins += [qm, kn, vn, fn, ck, cv, cf]
        in_specs += [pair(N_HEADS, KV_WIDTH), pair(1, KV_WIDTH), pair(1, KV_WIDTH), pair(N_HEADS, 1),
                     any_spec, any_spec, any_spec]
        out_specs.append(pair(N_HEADS, HEAD_DIM))
        out_shape.append(jax.ShapeDtypeStruct((n_rows, N_HEADS, HEAD_DIM), F32))
        scratch += [pltpu.VMEM((2, n_pages, KV_WIDTH, PAGE), F32), pltpu.VMEM((2, n_pages, KV_WIDTH, PAGE), F32),
                    pltpu.VMEM((2, n_pages, N_HEADS, PAGE), F32), pltpu.SemaphoreType.DMA((2, 3))]
    semantics = ("arbitrary",) * 3 if decode is not None else ("parallel", "parallel", "arbitrary")
    outs = pl.pallas_call(
        functools.partial(_fox_prompt_kernel, tq=tq, tk=tk, decode=decode is not None),
        grid_spec=pltpu.PrefetchScalarGridSpec(
            num_scalar_prefetch=len(prefetch),
            grid=(batch, N_KV, nt),
            in_specs=in_specs,
            out_specs=out_specs,
            scratch_shapes=scratch,
        ),
        out_shape=out_shape,
        compiler_params=pltpu.CompilerParams(
            dimension_semantics=semantics,
            vmem_limit_bytes=VMEM_LIMIT_RIDER if decode is not None else VMEM_LIMIT),
        name="fox_prompt",
    )(*prefetch, *ins)
    return outs if decode is not None else outs[0]


N_DECODE_SCRATCH = 8


def _fox_sample_kernel(pt_ref, qm_ref, kn_ref, vn_ref, fn_ref, ck_hbm, cv_hbm, cf_hbm, *rest, cp, nchunk, rider):
    n_in = _n_pool_inputs(rider["final"], rider["pending"]) if rider else 0
    rider_in, o_ref, rest = rest[:n_in], rest[n_in], rest[n_in + 1:]
    rider_out, rest = (rest[:2], rest[2:]) if rider else ((), rest)
    kbuf, vbuf, fbuf, sems, m_ref, l_ref, acc_ref, carry_ref = rest[:N_DECODE_SCRATCH]
    rider_bufs = rest[N_DECODE_SCRATCH:]
    b = pl.program_id(0)
    c = pl.program_id(1)
    nb = pl.num_programs(0)
    step = b * nchunk + c
    slot = step % 2

    def copies(bb, cc, sl, p):
        page = pt_ref[bb, cc * cp + p]
        return (pltpu.make_async_copy(ck_hbm.at[page], kbuf.at[sl, p], sems.at[sl, 0]),
                pltpu.make_async_copy(cv_hbm.at[page], vbuf.at[sl, p], sems.at[sl, 1]),
                pltpu.make_async_copy(cf_hbm.at[page], fbuf.at[sl, p], sems.at[sl, 2]))

    def issue(bb, cc, sl):
        for p in range(cp):
            for cpy in copies(bb, cc, sl, p):
                cpy.start()

    @pl.when(step == 0)
    def _():
        issue(b, c, slot)

    @pl.when(step + 1 < nb * nchunk)
    def _():
        nxt = step + 1
        issue(nxt // nchunk, nxt % nchunk, 1 - slot)

    if rider:
        per_batch = rider["tiles_per_batch"]
        _pool_tile(step % per_batch, per_batch, rider_in, rider_out, rider_bufs,
                   tp=rider["tp"], final=rider["final"], pending=rider["pending"])

    for p in range(cp):
        for cpy in copies(b, c, slot, p):
            cpy.wait()

    @pl.when(c == 0)
    def _():
        m_ref[...] = jnp.full(m_ref.shape, NEG, F32)
        l_ref[...] = jnp.zeros(l_ref.shape, F32)
        acc_ref[...] = jnp.zeros(acc_ref.shape, F32)
        carry_ref[...] = jnp.zeros(carry_ref.shape, F32)

    qm = qm_ref[...]
    cin = _cumsum_lanes_blocks(fbuf[slot].reshape(cp * N_HEADS, PAGE), PAGE).reshape(cp, N_HEADS, PAGE)
    carry = carry_ref[...]
    cs = []
    for p in range(cp):
        cs.append(cin[p] + carry)
        carry = carry + cin[p][:, PAGE - 1:PAGE]
    carry_ref[...] = carry
    kcat = jnp.concatenate([kbuf[slot, p] for p in range(cp)], axis=1).astype(BF16)
    t = _dot(qm, kcat) - jnp.concatenate(cs, axis=1)
    m_old = m_ref[...]
    m_new = jnp.maximum(m_old, jnp.max(t, axis=1, keepdims=True))
    alpha = jnp.exp(m_old - m_new)
    p = jnp.exp(t - m_new)
    l_new = alpha * l_ref[...] + jnp.sum(p, axis=1, keepdims=True)
    vcat = jnp.concatenate([vbuf[slot, p_] for p_ in range(cp)], axis=1).astype(BF16)
    acc_new = alpha * acc_ref[...] + _dot_nt(p.astype(BF16), vcat)
    m_ref[...] = m_new
    l_ref[...] = l_new
    acc_ref[...] = acc_new

    @pl.when(c == nchunk - 1)
    def _():
        s_n = jnp.sum(qm.astype(F32) * _bf16_round(kn_ref[...]), axis=1, keepdims=True)
        t_n = s_n - (carry + fn_ref[...])
        m_f = jnp.maximum(m_new, t_n)
        a = jnp.exp(m_new - m_f)
        p_n = jnp.exp(t_n - m_f)
        l_f = a * l_new + p_n
        acc = a * acc_new + _bf16_round(p_n) * _bf16_round(vn_ref[...])
        o_ref[...] = _head_diag(acc / l_f)


def _fox_sample(page_table, qm, kn, vn, fn, ck, cv, cf, cp, pool_layer=None):
    m, n_pages = page_table.shape
    nchunk = n_pages // cp
    per_b = lambda s1, s2: pl.BlockSpec((None, s1, s2), lambda b, c, pt: (b, 0, 0))
    any_spec = pl.BlockSpec(memory_space=pl.ANY)
    ins = [qm, kn, vn, fn, ck, cv, cf]
    in_specs = [per_b(N_HEADS, KV_WIDTH), per_b(1, KV_WIDTH), per_b(1, KV_WIDTH), per_b(N_HEADS, 1),
                any_spec, any_spec, any_spec]
    out_specs = [per_b(N_HEADS, HEAD_DIM)]
    out_shape = [jax.ShapeDtypeStruct((m, N_HEADS, HEAD_DIM), F32)]
    scratch = [pltpu.VMEM((2, cp, KV_WIDTH, PAGE), F32),
               pltpu.VMEM((2, cp, KV_WIDTH, PAGE), F32),
               pltpu.VMEM((2, cp, N_HEADS, PAGE), F32),
               pltpu.SemaphoreType.DMA((2, 3)),
               pltpu.VMEM((N_HEADS, 1), F32), pltpu.VMEM((N_HEADS, 1), F32),
               pltpu.VMEM((N_HEADS, KV_WIDTH), F32), pltpu.VMEM((N_HEADS, 1), F32)]
    assert len(scratch) == N_DECODE_SCRATCH
    rider = None
    if pool_layer is not None:
        *layer, batch, seq = pool_layer
        assert nchunk == 1 and (batch * seq) % m == 0
        tp = batch * seq // m
        per_batch = seq // tp
        assert seq % tp == 0 and tp % HALO == 0
        p_ins, p_in_specs, p_out_specs, p_scratch, final = _pool_prompt_operands(
            *layer, tp, lambda b, c, pt: b, lambda b, c, pt: b // per_batch, True)
        rider = dict(tp=tp, final=final, pending=layer[1] is not None, tiles_per_batch=per_batch)
        ins += p_ins
        in_specs += p_in_specs
        out_specs += p_out_specs
        out_shape += [jax.ShapeDtypeStruct((batch * seq, D_MODEL), F32),
                      jax.ShapeDtypeStruct((batch, HALO, D_MODEL), F32)]
        scratch += p_scratch
    outs = pl.pallas_call(
        functools.partial(_fox_sample_kernel, cp=cp, nchunk=nchunk, rider=rider),
        grid_spec=pltpu.PrefetchScalarGridSpec(
            num_scalar_prefetch=1,
            grid=(m, nchunk),
            in_specs=in_specs,
            out_specs=out_specs,
            scratch_shapes=scratch,
        ),
        out_shape=out_shape,
        compiler_params=pltpu.CompilerParams(dimension_semantics=("arbitrary", "arbitrary"),
                                             vmem_limit_bytes=VMEM_LIMIT_RIDER if rider else VMEM_LIMIT),
        name="fox_sample",
    )(page_table, *ins)
    return outs if rider else outs[0]


def _head_rows_q(q_rows):
    m = q_rows.shape[0]
    q4 = q_rows.reshape(m, N_KV, GROUP, HEAD_DIM)
    eye = jnp.eye(N_KV, dtype=q_rows.dtype)
    return (q4[:, :, :, None, :] * eye[None, :, None, :, None]).reshape(m, N_HEADS, KV_WIDTH)


def _tokens_last(x):
    lead = x.shape[:-3]
    n = len(lead)
    xt = jnp.transpose(x, tuple(range(n)) + (n + 1, n + 2, n))
    return xt.reshape(lead + (KV_WIDTH, x.shape[-3]))


def _tokens_first(xt):
    lead = xt.shape[:-2]
    n = len(lead)
    x4 = xt.reshape(lead + (N_KV, HEAD_DIM, xt.shape[-1]))
    return jnp.transpose(x4, tuple(range(n)) + (n + 2, n, n + 1))


def kernel(x_prompt, x_sample, state_pool, cache_win_k, cache_win_v, cache_fox_k, cache_fox_v,
           cache_fox_logf, page_table, norm_g, final_norm_g, rel_bias, pool_w_in, pool_mix,
           pool_scale, pool_w_out, swa_w_in, swa_sinks, swa_w_out, fox_w_in, fox_f_bias, fox_w_out):
    batch, seq, _ = x_prompt.shape
    db = x_sample.shape[0]
    depth = norm_g.shape[0]
    mp = batch * seq
    tm_p, tm_s = 512, db

    xp = x_prompt.reshape(mp, D_MODEL)
    xs = x_sample.reshape(db, D_MODEL)

    rb = rel_bias.astype(F32)
    dist_keys = WINDOW - np.arange(WINDOW)
    bias_keys = jnp.where((dist_keys < WINDOW)[None, :], rb[_t5_bucket_np(dist_keys)].T, NEG)
    bias0 = rb[0].reshape(N_HEADS, 1)
    fg = final_norm_g.reshape(1, D_MODEL)

    pool_p, pool_s = [], []
    wk_p, wv_p, wk_s, wv_s = [], [], [], []
    fk_p, fv_p, fl_p, fk_s, fv_s, fl_s = [], [], [], [], [], []
    def pool_layer_args(layer):
        jj = layer // 3
        fg_l = fg if layer == depth - 1 else None
        return (norm_g[layer].reshape(1, D_MODEL), pool_w_in[jj].astype(BF16), pool_mix[jj].astype(BF16),
                pool_scale[jj].reshape(1, D_MODEL), pool_w_out[jj].astype(BF16), fg_l)

    prompt_done = None
    pending = None
    for i in range(depth):
        kind, j = i % 3, i // 3
        g = norm_g[i].reshape(1, D_MODEL)
        if kind == 0:
            g, w_in, mix, scale, w_out, fg_l = pool_layer_args(i)
            if prompt_done is None:
                xp, u_tail = _pool_prompt(xp, pending, g, w_in, mix, scale, w_out, fg_l, batch, seq, tm_p)
                pending = None
            else:
                xp, u_tail = prompt_done
                prompt_done = None
            pool_p.append(u_tail[:, HALO - POOL_STATE:])
            xs, st_new = _pool_sample(xs, jnp.transpose(state_pool[j], (1, 0, 2)), g, w_in, mix, scale, w_out, fg_l)
            pool_s.append(jnp.transpose(st_new, (1, 0, 2)))
        elif kind == 1:
            w_in = swa_w_in[j].astype(BF16)
            w_out = swa_w_out[j].astype(BF16)
            sinks = swa_sinks[j].astype(F32)
            qt_p, kh_p, kt_p, vt_p, vtb_p, gate_p = _swa_proj(xp, g, w_in, tm_p, batch, seq, True)
            xp = _swa_prompt(qt_p, kh_p, vtb_p, gate_p, w_out, xp, rb, sinks, batch, seq, tm_p)
            wk_p.append(_tokens_first(kt_p[:, :, seq - WINDOW:]))
            wv_p.append(_tokens_first(vt_p[:, :, seq - WINDOW:]))
            q_s, k_s, v_s, kt_s, vt_s, gate_s = _swa_proj(xs, g, w_in, tm_s, 1, db, False)
            o_s, wk, wv = _swa_sample(_head_rows_q(q_s), _tokens_last(cache_win_k[j]),
                                      _tokens_last(cache_win_v[j]), k_s[:, None, :], v_s[:, None, :],
                                      kt_s[0], vt_s[0], bias_keys, bias0, sinks.reshape(N_HEADS, 1),
                                      SWA_ROWS_PER_STEP)
            xs = _out_proj(o_s.reshape(db, D_MODEL), gate_s, w_out, xs, tm_s)
            wk_s.append(_tokens_first(wk))
            wv_s.append(_tokens_first(wv))
        else:
            w_full = fox_w_in[j]
            nqkv = D_MODEL + 2 * KV_WIDTH
            w_in = jnp.concatenate([w_full[:, :nqkv], w_full[:, nqkv + N_HEADS:]], axis=1).astype(BF16)
            wf = jnp.pad(w_full[:, nqkv:nqkv + N_HEADS], ((0, 0), (0, LANES - N_HEADS))).astype(BF16)
            fb = jnp.pad(fox_f_bias[j].astype(F32), (0, LANES - N_HEADS)).reshape(1, LANES)
            w_out = fox_w_out[j].astype(BF16)
            qt, kt_p, vt_p, kaug, vtb, logft_p, gate_p = _fox_proj(xp, g, w_in, wf, fb, FOX_TILE, batch, seq, True)
            kaug = _fox_decay(logft_p, kaug, FOX_TILE)
            fk_p.append(_tokens_first(kt_p))
            fv_p.append(_tokens_first(vt_p))
            fl_p.append(jnp.transpose(logft_p, (0, 2, 1)))
            q_s, k_s, v_s, kt_s, vt_s, logf_s, logft_s, gate_s = _fox_proj(
                xs, g, w_in, wf, fb, tm_s, 1, db, False)
            decode_args = (page_table, _head_rows_q(q_s), k_s[:, None, :], v_s[:, None, :], logf_s[:, :, None],
                           _tokens_last(cache_fox_k[j]), _tokens_last(cache_fox_v[j]),
                           jnp.transpose(cache_fox_logf[j], (0, 2, 1)))
            next_is_pool = i + 1 < depth and (i + 1) % 3 == 0
            if _fox_prompt_fits_decode(batch, seq, FOX_TILE, db):
                o_p, o_s = _fox_prompt(qt, kaug, vtb, batch, seq, FOX_TILE, decode=decode_args)
                if next_is_pool:
                    pending = (o_p, gate_p, w_out)
                else:
                    xp = _out_proj(o_p, gate_p, w_out, xp, tm_p)
            else:
                o_p = _fox_prompt(qt, kaug, vtb, batch, seq, FOX_TILE)
                if next_is_pool:
                    o_s, xp, u_tail = _fox_sample(
                        *decode_args, FOX_PAGES_PER_STEP,
                        pool_layer=(xp, (o_p, gate_p, w_out), *pool_layer_args(i + 1), batch, seq))
                    prompt_done = (xp, u_tail)
                else:
                    xp = _out_proj(o_p, gate_p, w_out, xp, tm_p)
                    o_s = _fox_sample(*decode_args, FOX_PAGES_PER_STEP)
            xs = _out_proj(o_s.reshape(db, D_MODEL), gate_s, w_out, xs, tm_s)
            fk_s.append(_tokens_first(kt_s[0])[:, None])
            fv_s.append(_tokens_first(vt_s[0])[:, None])
            fl_s.append(logft_s[0].T[:, None, :])

    if (depth - 1) % 3 != 0:
        xp, xs = _final_norm(xp, fg, tm_p), _final_norm(xs, fg, tm_s)
    y_prompt = xp.reshape(batch, seq, D_MODEL)
    y_sample = xs.reshape(db, 1, D_MODEL)
    return (y_prompt, y_sample, jnp.stack(pool_p), jnp.stack(pool_s), jnp.stack(wk_p), jnp.stack(wv_p),
            jnp.stack(wk_s), jnp.stack(wv_s), jnp.stack(fk_p), jnp.stack(fv_p), jnp.stack(fl_p),
            jnp.stack(fk_s), jnp.stack(fv_s), jnp.stack(fl_s))
```
